```python
import jax, jax.numpy as jnp
from jax import lax
import numpy as np

D_MODEL = 1024
BATCH = 8
SEQ = 2048
DEPTH = 1
DEC_BATCH = 32
DEC_SEQ = 1
PAST_LEN = 8192
PAGE_SIZE = 128

N_META = 16
H_A = 4
DK_A = 128
DV_A = 128
CONV_W = 4
CHUNK = 64
H_F = 8
HD_F = 64
Q_BLOCK = 128
N_EXP = 32
TOP_K = 4
D_EXP = D_MODEL
SWIGLU_LIMIT = 7.0
SWIGLU_ALPHA = 1.702
MOE_BLOCK = 128
EPS = 1e-6

QK_A = H_A * DK_A
VW_A = H_A * DV_A
QKV_A = 2 * QK_A + VW_A
W_F = H_F * HD_F
IN_COLS = (QKV_A, VW_A, H_A, H_A, W_F, W_F, W_F, H_F, D_MODEL, D_MODEL)
D_IN = sum(IN_COLS)

kernel_name = 'hybrid_gdn_fox_moe_step'


def _rmsnorm(x, g):
    xf = x.astype(jnp.float32)
    return xf * lax.rsqrt(jnp.mean(xf * xf, axis=-1, keepdims=True) + EPS) * g.astype(jnp.float32)


def _l2norm(x):
    return x * lax.rsqrt(jnp.sum(x * x, axis=-1, keepdims=True) + EPS)


def _split_cols(a):
    out, s = [], 0
    for w in IN_COLS:
        out.append(a[..., s:s + w])
        s += w
    return out


def _causal_conv(x_ext, w):
    c = x_ext.shape[-1]
    return lax.conv_general_dilated(x_ext, w.astype(jnp.float32)[:, None, :], (1,), 'VALID',
                                    dimension_numbers=('NWC', 'WIO', 'NWC'), feature_group_count=c)


def _delta_chunk(S, q, k, v, beta, g):
    C = q.shape[2]
    G = jnp.cumsum(g, axis=-1)
    incl = jnp.tril(jnp.ones((C, C), bool))
    strict = jnp.tril(jnp.ones((C, C), bool), -1)
    decay = jnp.exp(jnp.where(incl, G[..., :, None] - G[..., None, :], -jnp.inf))
    eG = jnp.exp(G)
    A = jnp.where(strict, beta[..., :, None] * decay * jnp.einsum('bhid,bhjd->bhij', k, k), 0.0)
    rhs = beta[..., None] * (v - eG[..., None] * jnp.einsum('bhid,bhdv->bhiv', k, S))
    delta = lax.linalg.triangular_solve(A + jnp.eye(C, dtype=A.dtype), rhs, left_side=True, lower=True)
    o = (eG[..., None] * jnp.einsum('bhid,bhdv->bhiv', q, S)
         + jnp.einsum('bhij,bhjv->bhiv', jnp.einsum('bhid,bhjd->bhij', q, k) * decay, delta))
    g_end = G[..., -1:]
    S_new = (jnp.exp(g_end)[..., None] * S
             + jnp.einsum('bhjd,bhjv->bhdv', k * jnp.exp(g_end - G)[..., None], delta))
    return S_new, o


def _delta_scan(S0, q, k, v, beta, g, chunk):
    B, H, T = beta.shape
    n = T // chunk

    def blocks(a):
        return jnp.moveaxis(a.reshape((B, H, n, chunk) + a.shape[3:]), 2, 0)

    S, o = lax.scan(lambda s, inp: _delta_chunk(s, *inp), S0,
                    tuple(blocks(a) for a in (q, k, v, beta, g)))
    return S, jnp.moveaxis(o, 0, 2).reshape(B, H, T, -1)


def _forgetting_attention(q, k, v, c_q, c_k, q_pos, k_pos):
    B, T, H, d = q.shape
    qb = min(Q_BLOCK, T)
    nb = -(-T // qb)
    pad = nb * qb - T
    scale = d ** -0.5
    qp = jnp.pad(q, ((0, 0), (0, pad), (0, 0), (0, 0))).reshape(B, nb, qb, H, d).swapaxes(0, 1)
    cp = jnp.pad(c_q, ((0, 0), (0, pad), (0, 0))).reshape(B, nb, qb, H).swapaxes(0, 1)
    pp = jnp.pad(q_pos, (0, pad), mode='edge').reshape(nb, qb)
    c_kT = jnp.swapaxes(c_k, 1, 2)

    def block(args):
        qi, ci, pi = args
        s = (jnp.einsum('bqhd,bkhd->bhqk', qi, k) * scale
             + (jnp.swapaxes(ci, 1, 2)[..., None] - c_kT[:, :, None, :]))
        s = jnp.where(k_pos[None, :] <= pi[:, None], s, -jnp.inf)
        p = jax.nn.softmax(s, axis=-1)
        return jnp.einsum('bhqk,bkhd->bqhd', p, v)

    o = lax.map(block, (qp, cp, pp))
    return o.swapaxes(0, 1).reshape(B, nb * qb, H, d)[:, :T]


def _moe(h, w_router, b_router, w_gate, b_gate, w_up, b_up, w_down, b_down):
    n_tok, dm = h.shape
    logits = (h @ w_router + b_router).astype(jnp.float32)
    top_val, top_idx = lax.top_k(logits, TOP_K)
    top_w = jax.nn.softmax(top_val, axis=-1)
    m = n_tok * TOP_K
    bm = min(MOE_BLOCK, -(-m // 8) * 8)
    n_blk = -(-(m + N_EXP * (bm - 1)) // bm)
    flat_e = top_idx.reshape(-1)
    order = jnp.argsort(flat_e)
    e_s = flat_e[order]
    tok_s = (order // TOP_K).astype(jnp.int32)
    w_s = top_w.reshape(-1)[order]
    counts = jnp.zeros((N_EXP,), jnp.int32).at[flat_e].add(1)
    padded = (counts + bm - 1) // bm * bm
    start = jnp.cumsum(counts) - counts
    pstart = jnp.cumsum(padded) - padded
    dest = pstart[e_s] + jnp.arange(m) - start[e_s]
    row_tok = jnp.full((n_blk * bm,), n_tok, jnp.int32).at[dest].set(tok_s)
    h_ext = jnp.concatenate([h, jnp.zeros((1, dm), h.dtype)], axis=0)
    xb = h_ext[row_tok].reshape(n_blk, bm, dm)
    blk_e = jnp.minimum(jnp.searchsorted(jnp.cumsum(padded), jnp.arange(n_blk) * bm, side='right'), N_EXP - 1)

    def expert(args):
        xi, e = args
        gate = jnp.minimum(xi @ w_gate[e] + b_gate[e], SWIGLU_LIMIT)
        up = jnp.clip(xi @ w_up[e] + b_up[e], -SWIGLU_LIMIT, SWIGLU_LIMIT)
        return ((up + 1.0) * gate * jax.nn.sigmoid(SWIGLU_ALPHA * gate)) @ w_down[e] + b_down[e]

    y = lax.map(expert, (xb, blk_e)).reshape(n_blk * bm, dm)
    return jax.ops.segment_sum(y[dest] * w_s[:, None], tok_s, num_segments=n_tok)


def _layer(x, conv_prev, S0, k_past, v_past, lf_past, lead_pad, chunk,
           g_mix, w_in, conv_w, a_log, dt_bias, norm_a, qn_g, kn_g, b_forget, p_a, p_f, w_o,
           g_ffn, w_router, b_router, w_gate, b_gate, w_up, b_up, w_down, b_down):
    f32 = jnp.float32
    B, T, _ = x.shape
    h = _rmsnorm(x, g_mix)
    (qkv_a, z_a, b_a, a_a, q_f, k_f, v_f, f_f, gate_a, gate_f) = _split_cols(
        jnp.einsum('btd,de->bte', h, w_in))

    x_ext = jnp.concatenate([conv_prev.astype(f32), qkv_a.astype(f32)], axis=1)
    conv_new = x_ext[:, -(CONV_W - 1):]
    qkv = jax.nn.silu(_causal_conv(x_ext, conv_w))
    qa = _l2norm(qkv[..., :QK_A].reshape(B, T, H_A, DK_A)) * DK_A ** -0.5
    ka = _l2norm(qkv[..., QK_A:2 * QK_A].reshape(B, T, H_A, DK_A))
    va = qkv[..., 2 * QK_A:].reshape(B, T, H_A, DV_A)
    beta = jax.nn.sigmoid(b_a.astype(f32))
    g = -jnp.exp(a_log.astype(f32)) * jax.nn.softplus(a_a.astype(f32) + dt_bias)

    def to_bh(a):
        a = jnp.moveaxis(a, 1, 2)
        return jnp.pad(a, [(0, 0), (0, 0), (lead_pad, 0)] + [(0, 0)] * (a.ndim - 3))

    S_new, o = _delta_scan(S0.astype(f32), *(to_bh(a) for a in (qa, ka, va, beta, g)), chunk)
    o = jnp.moveaxis(o[:, :, lead_pad:], 1, 2)
    o_a = (_rmsnorm(o, norm_a) * jax.nn.silu(z_a.astype(f32).reshape(B, T, H_A, DV_A))).reshape(B, T, VW_A)

    qf = _rmsnorm(q_f.reshape(B, T, H_F, HD_F), qn_g)
    kf = _rmsnorm(k_f.reshape(B, T, H_F, HD_F), kn_g)
    vf = v_f.astype(f32).reshape(B, T, H_F, HD_F)
    lf = jax.nn.log_sigmoid(f_f.astype(f32) + b_forget)
    if k_past is None:
        k_all, v_all, lf_all = kf, vf, lf
    else:
        k_all = jnp.concatenate([k_past.astype(f32), kf], axis=1)
        v_all = jnp.concatenate([v_past.astype(f32), vf], axis=1)
        lf_all = jnp.concatenate([lf_past.astype(f32), lf], axis=1)
    P = k_all.shape[1] - T
    c = jnp.cumsum(lf_all, axis=1)
    o_f = _forgetting_attention(qf, k_all, v_all, c[:, P:], c,
                                P + jnp.arange(T), jnp.arange(P + T)).reshape(B, T, W_F)

    mixed = jax.nn.sigmoid(gate_a) * (o_a @ p_a) + jax.nn.sigmoid(gate_f) * (o_f @ p_f)
    x = x + mixed @ w_o
    y = _moe(_rmsnorm(x, g_ffn).reshape(B * T, D_MODEL), w_router, b_router,
             w_gate, b_gate, w_up, b_up, w_down, b_down)
    x = x + y.reshape(B, T, D_MODEL)
    return x, (kf, vf, lf, S_new, conv_new)


def setup_inputs(seed: int = 0) -> dict:
    key = jax.random.key(seed)
    ks = iter(jax.random.split(key, 40))

    def nrm(shape, scale):
        return jax.random.normal(next(ks), shape, jnp.float32) * scale

    L = DEPTH
    n_pages = PAST_LEN // PAGE_SIZE
    n_used = DEC_BATCH * n_pages
    n_pool = n_used + n_used // 4
    page_table = jax.random.permutation(next(ks), n_pool)[:n_used].reshape(DEC_BATCH, n_pages).astype(jnp.int32)
    dt = jnp.exp(jax.random.uniform(next(ks), (L, H_A), jnp.float32, minval=np.log(1e-3), maxval=np.log(1e-1)))
    return dict(
        x_prompt=nrm((BATCH, SEQ, D_MODEL), 1.0),
        x_sample=nrm((DEC_BATCH, DEC_SEQ, D_MODEL), 1.0),
        cache_k=nrm((L, n_pool, PAGE_SIZE, H_F, HD_F), 1.0),
        cache_v=nrm((L, n_pool, PAGE_SIZE, H_F, HD_F), 1.0),
        cache_logf=jax.nn.log_sigmoid(2.0 + nrm((L, n_pool, PAGE_SIZE, H_F), 0.5)),
        state_delta=nrm((L, DEC_BATCH, H_A, DK_A, DV_A), DK_A ** -0.5),
        state_conv=nrm((L, DEC_BATCH, CONV_W - 1, QKV_A), 1.0),
        page_table=page_table,
        meta_tokens=nrm((N_META, D_MODEL), 1.0),
        g_mix=1.0 + nrm((L, D_MODEL), 0.02),
        w_in=nrm((L, D_MODEL, D_IN), D_MODEL ** -0.5),
        conv_w=nrm((L, CONV_W, QKV_A), CONV_W ** -0.5),
        a_log=jnp.log(jax.random.uniform(next(ks), (L, H_A), jnp.float32, minval=1.0, maxval=16.0)),
        dt_bias=dt + jnp.log(-jnp.expm1(-dt)),
        norm_a=1.0 + nrm((L, DV_A), 0.02),
        qn_g=1.0 + nrm((L, HD_F), 0.02),
        kn_g=1.0 + nrm((L, HD_F), 0.02),
        b_forget=2.0 + nrm((L, H_F), 0.5),
        p_a=nrm((L, VW_A, D_MODEL), VW_A ** -0.5),
        p_f=nrm((L, W_F, D_MODEL), W_F ** -0.5),
        w_o=nrm((L, D_MODEL, D_MODEL), D_MODEL ** -0.5),
        g_ffn=1.0 + nrm((L, D_MODEL), 0.02),
        w_router=nrm((L, D_MODEL, N_EXP), D_MODEL ** -0.5),
        b_router=nrm((L, N_EXP), 0.01),
        w_gate=nrm((L, N_EXP, D_MODEL, D_EXP), D_MODEL ** -0.5),
        b_gate=nrm((L, N_EXP, D_EXP), 0.01),
        w_up=nrm((L, N_EXP, D_MODEL, D_EXP), D_MODEL ** -0.5),
        b_up=nrm((L, N_EXP, D_EXP), 0.01),
        w_down=nrm((L, N_EXP, D_EXP, D_MODEL), D_EXP ** -0.5),
        b_down=nrm((L, N_EXP, D_MODEL), 0.01),
    )


def reference(x_prompt, x_sample, cache_k, cache_v, cache_logf, state_delta, state_conv, page_table,
              meta_tokens, g_mix, w_in, conv_w, a_log, dt_bias, norm_a, qn_g, kn_g, b_forget,
              p_a, p_f, w_o, g_ffn, w_router, b_router, w_gate, b_gate, w_up, b_up, w_down, b_down):
    B = x_prompt.shape[0]
    Bd = x_sample.shape[0]
    xp = jnp.concatenate([jnp.broadcast_to(meta_tokens.astype(x_prompt.dtype)[None], (B, N_META, D_MODEL)),
                          x_prompt], axis=1)
    xs = x_sample
    lead_pad = (-N_META) % CHUNK
    new_p = [[] for _ in range(5)]
    new_s = [[] for _ in range(5)]
    for l in range(DEPTH):
        w = (g_mix[l], w_in[l], conv_w[l], a_log[l], dt_bias[l], norm_a[l], qn_g[l], kn_g[l], b_forget[l],
             p_a[l], p_f[l], w_o[l], g_ffn[l], w_router[l], b_router[l], w_gate[l], b_gate[l],
             w_up[l], b_up[l], w_down[l], b_down[l])
        conv0 = jnp.zeros((B, CONV_W - 1, QKV_A), jnp.float32)
        S0 = jnp.zeros((B, H_A, DK_A, DV_A), jnp.float32)
        xp, st_p = _layer(xp, conv0, S0, None, None, None, lead_pad, CHUNK, *w)
        k_past = cache_k[l][page_table].reshape(Bd, -1, H_F, HD_F)
        v_past = cache_v[l][page_table].reshape(Bd, -1, H_F, HD_F)
        lf_past = cache_logf[l][page_table].reshape(Bd, -1, H_F)
        xs, st_s = _layer(xs, state_conv[l], state_delta[l], k_past, v_past, lf_past, 0, xs.shape[1], *w)
        for lst, a in zip(new_p, st_p):
            lst.append(a)
        for lst, a in zip(new_s, st_s):
            lst.append(a)
    k_p, v_p, lf_p, d_p, c_p = (jnp.stack(a) for a in new_p)
    k_s, v_s, lf_s, d_s, c_s = (jnp.stack(a) for a in new_s)
    return (xp[:, N_META:], xs, k_p, v_p, lf_p, k_s, v_s, lf_s, d_p, d_s, c_p, c_s)
```

```python
import functools

import jax
import jax.numpy as jnp
from jax import lax
from jax.experimental import pallas as pl
from jax.experimental.pallas import tpu as pltpu

F32 = jnp.float32
BF16 = jnp.bfloat16

N_META = 16
H_A = 4
DK_A = 128
DV_A = 128
CONV_W = 4
CHUNK = 64
H_F = 8
HD_F = 64
N_EXP = 32
TOP_K = 4
SWIGLU_LIMIT = 7.0
SWIGLU_ALPHA = 1.702
EPS = 1e-6

LANES = 128
F_OFF = 2 * H_A
HP = 16
ATT_BLOCK = 256
MOE_BLOCK = 256
PAGES_PER_STEP = 4
VMEM_LIMIT = 48 * 1024 * 1024


def _cparams(*sem):
    return pltpu.CompilerParams(dimension_semantics=sem, vmem_limit_bytes=VMEM_LIMIT)


def _row_tile(n, cap):
    if n <= cap:
        return n
    best = None
    for t in range(8, cap + 1, 8):
        if n % t == 0:
            best = t
    assert best is not None, n
    return best


def _dot(a, b):
    return jnp.dot(a, b, preferred_element_type=F32)


def _dot_nt(a, b):
    return lax.dot_general(a, b, (((1,), (1,)), ((), ())), preferred_element_type=F32)


def _dot_hi(a, b):
    return jnp.dot(a, b, preferred_element_type=F32, precision=lax.Precision.HIGHEST)


def _split2(x):
    hi = x.astype(BF16)
    lo = (x - hi.astype(F32)).astype(BF16)
    return hi, lo


def _split3(x):
    hi = x.astype(BF16)
    r = x - hi.astype(F32)
    mid = r.astype(BF16)
    lo = (r - mid.astype(F32)).astype(BF16)
    return hi, mid, lo


def _sigmoid(x):
    return 1.0 / (1.0 + jnp.exp(-x))


def _softplus(x):
    return jnp.maximum(x, 0.0) + jnp.log1p(jnp.exp(-jnp.abs(x)))


def _log_sigmoid(x):
    return jnp.minimum(x, 0.0) - jnp.log1p(jnp.exp(-jnp.abs(x)))


IN_WIDTHS = (2 * H_A * DK_A + H_A * DV_A, H_A * DV_A, H_F * HD_F, H_F * HD_F, H_F * HD_F)
COL_CHUNK = 512


def _in_proj_body(x_ref, g_ref, wm_ref, ws_ref, *out_refs, widths):
    x = x_ref[...]
    h = x * lax.rsqrt(jnp.mean(x * x, axis=-1, keepdims=True) + EPS) * g_ref[...]
    hb = h.astype(BF16)
    col = 0
    for o_ref, w in zip(out_refs[:-1], widths):
        for c0 in range(0, w, COL_CHUNK):
            o_ref[:, c0:c0 + COL_CHUNK] = _dot(hb, wm_ref[:, col + c0:col + c0 + COL_CHUNK])
        col += w
    out_refs[-1][...] = _dot(hb, ws_ref[...])


def _in_proj(x2d, g_row, w_main, w_small, widths):
    n, d = x2d.shape
    tm = _row_tile(n, 512)
    outs = [jax.ShapeDtypeStruct((n, w), F32) for w in widths] + [jax.ShapeDtypeStruct((n, LANES), F32)]
    out_specs = [pl.BlockSpec((tm, w), lambda i: (i, 0)) for w in widths] + [pl.BlockSpec((tm, LANES), lambda i: (i, 0))]
    return pl.pallas_call(
        functools.partial(_in_proj_body, widths=widths),
        grid=(n // tm,),
        in_specs=[pl.BlockSpec((tm, d), lambda i: (i, 0)),
                  pl.BlockSpec((1, d), lambda i: (0, 0)),
                  pl.BlockSpec(w_main.shape, lambda i: (0, 0), pipeline_mode=pl.Buffered(1)),
                  pl.BlockSpec(w_small.shape, lambda i: (0, 0), pipeline_mode=pl.Buffered(1))],
        out_specs=out_specs,
        out_shape=outs,
        compiler_params=_cparams("parallel"),
        name="in_proj",
    )(x2d, g_row, w_main, w_small)


def _conv_prep_body(x_ref, cw_ref, prev_ref, o_ref, buf, *, t_len, rows):
    j = pl.program_id(1)
    pad = 8
    buf[0:pad, :] = jnp.zeros((pad, LANES), F32)
    buf[pad - (CONV_W - 1):pad, :] = prev_ref[...]
    buf[pad:pad + t_len, :] = x_ref[...]
    cw = cw_ref[...]
    is_q = j < H_A
    is_qk = j < 2 * H_A
    for r0 in range(0, t_len, rows):
        y = jnp.zeros((rows, LANES), F32)
        for w in range(CONV_W):
            off = pad - (CONV_W - 1) + w + r0
            y = y + buf[off:off + rows, :] * cw[w:w + 1, :]
        y = y * _sigmoid(y)
        nrm = lax.rsqrt(jnp.sum(y * y, axis=-1, keepdims=True) + EPS)
        f = jnp.where(is_qk, nrm * jnp.where(is_q, DK_A ** -0.5, 1.0), 1.0)
        o_ref[r0:r0 + rows, :] = y * f


def _conv_prep(qkv3, conv_w, conv_prev):
    b, t, c = qkv3.shape
    rows = _row_tile(t, 512)
    return pl.pallas_call(
        functools.partial(_conv_prep_body, t_len=t, rows=rows),
        grid=(b, c // LANES),
        in_specs=[pl.BlockSpec((None, t, LANES), lambda i, j: (i, 0, j)),
                  pl.BlockSpec((CONV_W, LANES), lambda i, j: (0, j)),
                  pl.BlockSpec((None, CONV_W - 1, LANES), lambda i, j: (i, 0, j))],
        out_specs=pl.BlockSpec((None, t, LANES), lambda i, j: (i, 0, j)),
        out_shape=jax.ShapeDtypeStruct((b, t, c), F32),
        scratch_shapes=[pltpu.VMEM((8 + t, LANES), F32)],
        compiler_params=_cparams("parallel", "parallel"),
        name="conv_prep",
    )(qkv3, conv_w, conv_prev)


def _gates(sm, a_row, dt_row, h):
    lane = lax.broadcasted_iota(jnp.int32, sm.shape, 1)
    beta_all = _sigmoid(sm)
    g_all = -jnp.exp(a_row) * _softplus(sm + dt_row)
    beta = jnp.sum(jnp.where(lane == h, beta_all, 0.0), axis=-1, keepdims=True)
    g = jnp.sum(jnp.where(lane == H_A + h, g_all, 0.0), axis=-1, keepdims=True)
    return beta, g


def _delta_chunk(q, k, v, beta, g, s_ref):
    c = q.shape[0]
    ri = lax.broadcasted_iota(jnp.int32, (c, c), 0)
    ci = lax.broadcasted_iota(jnp.int32, (c, c), 1)
    eye = ri == ci
    incl = ci <= ri
    strict = ci < ri
    g_row = jnp.sum(jnp.where(eye, g, 0.0), axis=0, keepdims=True)
    g_cum = jnp.sum(jnp.where(incl, g_row, 0.0), axis=1, keepdims=True)
    g_cum_row = jnp.sum(jnp.where(ri <= ci, g, 0.0), axis=0, keepdims=True)
    decay = jnp.exp(jnp.where(incl, g_cum - g_cum_row, -jnp.inf))
    e_g = jnp.exp(g_cum)
    s = s_ref[...]
    sb = s.astype(BF16)
    qb = q.astype(BF16)
    kb = k.astype(BF16)
    kk = _dot_nt(kb, kb)
    qk = _dot_nt(qb, kb)
    k_s = _dot(kb, sb)
    q_s = _dot(qb, sb)
    a = jnp.where(strict, beta * decay * kk, 0.0)
    rhs = beta * (v - e_g * k_s)
    inv = jnp.where(eye, 1.0, 0.0) - a
    pw = a
    n_sq = max(1, (c - 1).bit_length() - 1)
    for _ in range(n_sq):
        pw = _dot_hi(pw, pw)
        inv = inv + _dot_hi(inv, pw)
    delta = _dot_hi(inv, rhs)
    db = delta.astype(BF16)
    o = e_g * q_s + _dot((qk * decay).astype(BF16), db)
    g_end = g_cum[c - 1:c, :]
    k_dec = (k * jnp.exp(g_end - g_cum)).T.astype(BF16)
    s_ref[...] = jnp.exp(g_end) * s + _dot(k_dec, db)
    return o


def _delta_scan_body(q_ref, k_ref, v_ref, sm_ref, a_ref, dt_ref, s0_ref, o_ref, s_out_ref, s_ref, *, t_len, lead):
    h = pl.program_id(1)
    a_row = a_ref[...]
    dt_row = dt_ref[...]
    s_ref[...] = s0_ref[...]
    first = CHUNK - lead
    n_chunks = (lead + t_len) // CHUNK

    def padded(ref):
        return jnp.concatenate([jnp.zeros((lead, LANES), F32), ref[0:first, :]], axis=0)

    if lead:
        beta, g = _gates(padded(sm_ref), a_row, dt_row, h)
        row = lax.broadcasted_iota(jnp.int32, (CHUNK, 1), 0)
        beta = jnp.where(row >= lead, beta, 0.0)
        g = jnp.where(row >= lead, g, 0.0)
        o = _delta_chunk(padded(q_ref), padded(k_ref), padded(v_ref), beta, g, s_ref)
        o_ref[0:first, :] = o[lead:, :]
        c_start = 1
    else:
        c_start = 0

    def body(c, carry):
        r0 = pl.multiple_of(c * CHUNK - lead, 8)
        sl = pl.ds(r0, CHUNK)
        beta, g = _gates(sm_ref[sl, :], a_row, dt_row, h)
        o_ref[sl, :] = _delta_chunk(q_ref[sl, :], k_ref[sl, :], v_ref[sl, :], beta, g, s_ref)
        return carry

    lax.fori_loop(c_start, n_chunks, body, 0)
    s_out_ref[...] = s_ref[...]


def _delta_scan(qkv3, small3, a_row, dt_row, s0, lead):
    b, t, _ = qkv3.shape
    assert (lead + t) % CHUNK == 0 and lead % 8 == 0
    blk = lambda off: pl.BlockSpec((None, t, LANES), lambda i, h: (i, 0, off + h))
    return pl.pallas_call(
        functools.partial(_delta_scan_body, t_len=t, lead=lead),
        grid=(b, H_A),
        in_specs=[blk(0), blk(H_A), blk(2 * H_A),
                  pl.BlockSpec((None, t, LANES), lambda i, h: (i, 0, 0)),
                  pl.BlockSpec((1, LANES), lambda i, h: (0, 0)),
                  pl.BlockSpec((1, LANES), lambda i, h: (0, 0)),
                  pl.BlockSpec((None, None, DK_A, DV_A), lambda i, h: (i, h, 0, 0))],
        out_specs=[pl.BlockSpec((None, t, LANES), lambda i, h: (i, 0, h)),
                   pl.BlockSpec((None, None, DK_A, DV_A), lambda i, h: (i, h, 0, 0))],
        out_shape=[jax.ShapeDtypeStruct((b, t, H_A * DV_A), F32),
                   jax.ShapeDtypeStruct((b, H_A, DK_A, DV_A), F32)],
        scratch_shapes=[pltpu.VMEM((DK_A, DV_A), F32)],
        compiler_params=_cparams("parallel", "parallel"),
        name="delta_scan",
    )(qkv3, qkv3, qkv3, small3, a_row, dt_row, s0)


def _group_rms(x, gain, seg):
    hi, lo = _split2(x * x)
    ss = _dot(hi, seg) + _dot(lo, seg)
    return x * lax.rsqrt(ss * (1.0 / HD_F) + EPS) * gain


def _seg_matrix(n):
    r = lax.broadcasted_iota(jnp.int32, (n, n), 0) // HD_F
    c = lax.broadcasted_iota(jnp.int32, (n, n), 1) // HD_F
    return jnp.where(r == c, 1.0, 0.0).astype(BF16)


def _fox_prep_body(q_ref, k_ref, v_ref, qg_ref, kg_ref, qb_ref, kb_ref, vb_ref, kn_ref, *, t_len, t_pad):
    seg = _seg_matrix(LANES)
    qn = _group_rms(q_ref[...], qg_ref[...], seg) * (HD_F ** -0.5)
    kn = _group_rms(k_ref[...], kg_ref[...], seg)
    kn_ref[...] = kn
    qb_ref[0:t_len, :] = qn.astype(BF16)
    kb_ref[0:t_len, :] = kn.astype(BF16)
    vb_ref[0:t_len, :] = v_ref[...].astype(BF16)
    if t_pad > t_len:
        z = jnp.zeros((t_pad - t_len, LANES), BF16)
        qb_ref[t_len:, :] = z
        kb_ref[t_len:, :] = z
        vb_ref[t_len:, :] = z


def _fox_prep(q3, k3, v3, qg_row, kg_row, t_pad):
    b, t, w = q3.shape
    blk = pl.BlockSpec((None, t, LANES), lambda i, p: (i, 0, p))
    pblk = pl.BlockSpec((None, t_pad, LANES), lambda i, p: (i, 0, p))
    gblk = pl.BlockSpec((1, LANES), lambda i, p: (0, 0))
    return pl.pallas_call(
        functools.partial(_fox_prep_body, t_len=t, t_pad=t_pad),
        grid=(b, w // LANES),
        in_specs=[blk, blk, blk, gblk, gblk],
        out_specs=[pblk, pblk, pblk, blk],
        out_shape=[jax.ShapeDtypeStruct((b, t_pad, w), BF16)] * 3 + [jax.ShapeDtypeStruct((b, t, w), F32)],
        compiler_params=_cparams("parallel", "parallel"),
        name="fox_prep",
    )(q3, k3, v3, qg_row, kg_row)


def _logf_cumsum_body(sm_ref, bf_ref, lf_ref, ccol_ref, crow_ref, buf, *, t_len, t_pad):
    lf = _log_sigmoid(sm_ref[...] + bf_ref[...])
    lf_ref[...] = lf
    buf[0:t_len, :] = lf
    if t_pad > t_len:
        buf[t_len:, :] = jnp.zeros((t_pad - t_len, LANES), F32)
    blk = ATT_BLOCK
    ri = lax.broadcasted_iota(jnp.int32, (blk, blk), 0)
    ci = lax.broadcasted_iota(jnp.int32, (blk, blk), 1)
    tri = jnp.where(ci <= ri, 1.0, 0.0).astype(BF16)
    carry = jnp.zeros((1, LANES), F32)
    for i in range(t_pad // blk):
        hi, mid, lo = _split3(buf[i * blk:(i + 1) * blk, :])
        c = _dot(tri, hi) + _dot(tri, mid) + _dot(tri, lo) + carry
        ccol_ref[i * blk:(i + 1) * blk, :] = c
        carry = c[blk - 1:blk, :]
    crow_ref[...] = ccol_ref[...].T[F_OFF:F_OFF + H_F, :]


def _logf_cumsum(small3, bf_row, t_pad):
    b, t, _ = small3.shape
    return pl.pallas_call(
        functools.partial(_logf_cumsum_body, t_len=t, t_pad=t_pad),
        grid=(b,),
        in_specs=[pl.BlockSpec((None, t, LANES), lambda i: (i, 0, 0)),
                  pl.BlockSpec((1, LANES), lambda i: (0, 0))],
        out_specs=[pl.BlockSpec((None, t, LANES), lambda i: (i, 0, 0)),
                   pl.BlockSpec((None, t_pad, LANES), lambda i: (i, 0, 0)),
                   pl.BlockSpec((None, H_F, t_pad), lambda i: (i, 0, 0))],
        out_shape=[jax.ShapeDtypeStruct((b, t, LANES), F32),
                   jax.ShapeDtypeStruct((b, t_pad, LANES), F32),
                   jax.ShapeDtypeStruct((b, H_F, t_pad), F32)],
        scratch_shapes=[pltpu.VMEM((t_pad, LANES), F32)],
        compiler_params=_cparams("parallel"),
        name="logf_cumsum",
    )(small3, bf_row)


def _fox_attn_body(q_ref, k_ref, v_ref, cq_ref, ck_ref, o_ref):
    p = pl.program_id(1)
    i = pl.program_id(2)
    blk = ATT_BLOCK
    heads = LANES // HD_F
    q = q_ref[...]
    lane = lax.broadcasted_iota(jnp.int32, (blk, LANES), 1)
    head_of_lane = lane // HD_F
    cq_all = cq_ref[...]
    qh, cq = [], []
    for hh in range(heads):
        qh.append(jnp.where(head_of_lane == hh, q, jnp.zeros_like(q)))
        sel = lane == F_OFF + p * heads + hh
        cq.append(jnp.sum(jnp.where(sel, cq_all, 0.0), axis=-1, keepdims=True))
    qpos = i * blk + lax.broadcasted_iota(jnp.int32, (blk, blk), 0)
    kiota = lax.broadcasted_iota(jnp.int32, (blk, blk), 1)

    def body(j, carry):
        ms, ls, acc = carry
        k0 = pl.multiple_of(j * blk, blk)
        k = k_ref[pl.ds(k0, blk), :]
        v = v_ref[pl.ds(k0, blk), :]
        ck_all = ck_ref[:, pl.ds(k0, blk)]
        causal = (k0 + kiota) <= qpos
        new_ms, new_ls, pvs, alphas = [], [], [], []
        for hh in range(heads):
            row = p * heads + hh
            rsel = lax.broadcasted_iota(jnp.int32, ck_all.shape, 0) == row
            ck = jnp.sum(jnp.where(rsel, ck_all, 0.0), axis=0, keepdims=True)
            s = _dot_nt(qh[hh], k) + (cq[hh] - ck)
            s = jnp.where(causal, s, -jnp.inf)
            m_new = jnp.maximum(ms[hh], jnp.max(s, axis=-1, keepdims=True))
            alpha = jnp.exp(ms[hh] - m_new)
            pr = jnp.exp(s - m_new)
            new_ls.append(alpha * ls[hh] + jnp.sum(pr, axis=-1, keepdims=True))
            new_ms.append(m_new)
            pvs.append(_dot(pr.astype(BF16), v))
            alphas.append(alpha)
        pv = pvs[heads - 1]
        al = alphas[heads - 1]
        for hh in range(heads - 2, -1, -1):
            pv = jnp.where(head_of_lane == hh, pvs[hh], pv)
            al = jnp.where(head_of_lane == hh, alphas[hh], al)
        return tuple(new_ms), tuple(new_ls), al * acc + pv

    init = (tuple(jnp.full((blk, 1), -jnp.inf, F32) for _ in range(heads)),
            tuple(jnp.zeros((blk, 1), F32) for _ in range(heads)),
            jnp.zeros((blk, LANES), F32))
    ms, ls, acc = lax.fori_loop(0, i + 1, body, init)
    inv = 1.0 / ls[heads - 1]
    for hh in range(heads - 2, -1, -1):
        inv = jnp.where(head_of_lane == hh, 1.0 / ls[hh], inv)
    o_ref[...] = (acc * inv).astype(o_ref.dtype)


def _fox_attn(qb, kb, vb, c_col, c_row):
    b, t_pad, w = qb.shape
    blk = ATT_BLOCK
    return pl.pallas_call(
        _fox_attn_body,
        grid=(b, w // LANES, t_pad // blk),
        in_specs=[pl.BlockSpec((None, blk, LANES), lambda n, p, i: (n, i, p)),
                  pl.BlockSpec((None, t_pad, LANES), lambda n, p, i: (n, 0, p)),
                  pl.BlockSpec((None, t_pad, LANES), lambda n, p, i: (n, 0, p)),
                  pl.BlockSpec((None, blk, LANES), lambda n, p, i: (n, i, 0)),
                  pl.BlockSpec((None, H_F, t_pad), lambda n, p, i: (n, 0, 0))],
        out_specs=pl.BlockSpec((None, blk, LANES), lambda n, p, i: (n, i, p)),
        out_shape=jax.ShapeDtypeStruct((b, t_pad, w), BF16),
        compiler_params=_cparams("parallel", "parallel", "arbitrary"),
        name="fox_attn",
    )(qb, kb, vb, c_col, c_row)


def _decode_delta_body(x_ref, sc_ref, sm_ref, cw_ref, a_ref, dt_ref, s_ref, o_ref, s_out_ref, conv_ref):
    x_new = x_ref[...]
    sc = sc_ref[...]
    cw = cw_ref[...]
    y = x_new * cw[CONV_W - 1:CONV_W, :]
    for w in range(CONV_W - 1):
        y = y + sc[w:w + 1, :] * cw[w:w + 1, :]
    y = y * _sigmoid(y)
    conv_ref[0:CONV_W - 2, :] = sc[1:, :]
    conv_ref[CONV_W - 2:CONV_W - 1, :] = x_new
    sm = sm_ref[...]
    lane = lax.broadcasted_iota(jnp.int32, (1, LANES), 1)
    beta_all = _sigmoid(sm)
    g_all = -jnp.exp(a_ref[...]) * _softplus(sm + dt_ref[...])
    qk_w = H_A * DK_A
    row0 = lax.broadcasted_iota(jnp.int32, (LANES, LANES), 0) == 0

    def in_row0(r):
        return jnp.where(row0, jnp.broadcast_to(r, (LANES, LANES)), 0.0)

    for h in range(H_A):
        q = y[:, h * DK_A:(h + 1) * DK_A]
        k = y[:, qk_w + h * DK_A:qk_w + (h + 1) * DK_A]
        v = y[:, 2 * qk_w + h * DV_A:2 * qk_w + (h + 1) * DV_A]
        q = q * lax.rsqrt(jnp.sum(q * q, axis=-1, keepdims=True) + EPS) * (DK_A ** -0.5)
        k = k * lax.rsqrt(jnp.sum(k * k, axis=-1, keepdims=True) + EPS)
        beta = jnp.sum(jnp.where(lane == h, beta_all, 0.0), axis=-1, keepdims=True)
        g = jnp.sum(jnp.where(lane == H_A + h, g_all, 0.0), axis=-1, keepdims=True)
        e_g = jnp.exp(g)
        s = s_ref[h]
        sb = s.astype(BF16)
        k_sq = in_row0(k)
        k_s = _dot(k_sq.astype(BF16), sb)[0:1, :]
        q_s = _dot(in_row0(q).astype(BF16), sb)[0:1, :]
        delta = beta * (v - e_g * k_s)
        qk = jnp.sum(q * k, axis=-1, keepdims=True)
        o_ref[:, h * DV_A:(h + 1) * DV_A] = e_g * q_s + qk * delta
        s_out_ref[h] = e_g * s + _dot(k_sq.T.astype(BF16), in_row0(delta).astype(BF16))


def _decode_delta(qkv3, state_conv, small3, conv_w, a_row, dt_row, s0):
    bd, _, c = qkv3.shape
    return pl.pallas_call(
        _decode_delta_body,
        grid=(bd,),
        in_specs=[pl.BlockSpec((None, 1, c), lambda i: (i, 0, 0)),
                  pl.BlockSpec((None, CONV_W - 1, c), lambda i: (i, 0, 0)),
                  pl.BlockSpec((None, 1, LANES), lambda i: (i, 0, 0)),
                  pl.BlockSpec((CONV_W, c), lambda i: (0, 0)),
                  pl.BlockSpec((1, LANES), lambda i: (0, 0)),
                  pl.BlockSpec((1, LANES), lambda i: (0, 0)),
                  pl.BlockSpec((None, H_A, DK_A, DV_A), lambda i: (i, 0, 0, 0))],
        out_specs=[pl.BlockSpec((None, 1, H_A * DV_A), lambda i: (i, 0, 0)),
                   pl.BlockSpec((None, H_A, DK_A, DV_A), lambda i: (i, 0, 0, 0)),
                   pl.BlockSpec((None, CONV_W - 1, c), lambda i: (i, 0, 0))],
        out_shape=[jax.ShapeDtypeStruct((bd, 1, H_A * DV_A), F32),
                   jax.ShapeDtypeStruct((bd, H_A, DK_A, DV_A), F32),
                   jax.ShapeDtypeStruct((bd, CONV_W - 1, c), F32)],
        compiler_params=_cparams("parallel"),
        name="decode_delta",
    )(qkv3, state_conv, small3, conv_w, a_row, dt_row, s0)


def _decode_attn_body(pt_ref, q_ref, k_ref, v_ref, sm_ref, qg_ref, kg_ref, bf_ref, *rest, n_pg):
    page_refs = rest[:3 * n_pg]
    o_ref, kn_ref, lf_ref, qrows, m_ref, l_ref, acc_ref, carry_ref = rest[3 * n_pg:]
    step = pl.program_id(1)
    w = H_F * HD_F
    row = lax.broadcasted_iota(jnp.int32, (HP, w), 0)
    head_of_lane = lax.broadcasted_iota(jnp.int32, (HP, w), 1) // HD_F

    @pl.when(step == 0)
    def _():
        seg = _seg_matrix(w)

        def rms_rows(x_row, gain):
            xr = jnp.broadcast_to(x_row, (HP, w))
            hi, lo = _split2(xr * xr)
            ss = _dot(hi, seg) + _dot(lo, seg)
            return xr * lax.rsqrt(ss * (1.0 / HD_F) + EPS) * gain

        qn = rms_rows(q_ref[...], qg_ref[...]) * (HD_F ** -0.5)
        kn = rms_rows(k_ref[...], kg_ref[...])
        kn_ref[...] = kn[0:1, :]
        lf = _log_sigmoid(sm_ref[...] + bf_ref[...])
        lf_ref[...] = lf
        q_m = jnp.where(head_of_lane == row, qn, 0.0)
        qrows[...] = q_m.astype(BF16)
        m_ref[...] = jnp.sum(q_m * kn, axis=-1, keepdims=True)
        l_ref[...] = jnp.ones((HP, 1), F32)
        acc_ref[...] = jnp.broadcast_to(v_ref[...], (HP, w))
        rr = lax.broadcasted_iota(jnp.int32, (HP, LANES), 0)
        ll = lax.broadcasted_iota(jnp.int32, (HP, LANES), 1)
        carry_ref[...] = jnp.sum(jnp.where(ll == rr + F_OFF, jnp.broadcast_to(lf, (HP, LANES)), 0.0),
                                 axis=-1, keepdims=True)

    pos = lax.broadcasted_iota(jnp.int32, (HP, LANES), 1)
    qb = qrows[...]
    for g in range(n_pg):
        kp = page_refs[3 * g][...].astype(BF16)
        vp = page_refs[3 * g + 1][...].astype(BF16)
        lfp = jnp.concatenate([page_refs[3 * g + 2][...], jnp.zeros((HP - H_F, LANES), F32)], axis=0)
        suf = lfp
        sh = 1
        while sh < LANES:
            rolled = pltpu.roll(suf, LANES - sh, axis=1)
            suf = suf + jnp.where(pos + sh < LANES, rolled, 0.0)
            sh *= 2
        carry = carry_ref[...]
        bias = carry + (suf - lfp)
        s = _dot_nt(qb, kp) + bias
        m_old = m_ref[...]
        m_new = jnp.maximum(m_old, jnp.max(s, axis=-1, keepdims=True))
        alpha = jnp.exp(m_old - m_new)
        pr = jnp.exp(s - m_new)
        l_ref[...] = alpha * l_ref[...] + jnp.sum(pr, axis=-1, keepdims=True)
        m_ref[...] = m_new
        acc_ref[...] = alpha * acc_ref[...] + _dot(pr.astype(BF16), vp)
        carry_ref[...] = carry + jnp.sum(lfp, axis=-1, keepdims=True)

    @pl.when(step == pl.num_programs(1) - 1)
    def _():
        out = jnp.where(head_of_lane == row, acc_ref[...] / l_ref[...], 0.0)
        o_ref[...] = jnp.sum(out, axis=0, keepdims=True)


def _decode_attn(page_table, q3, k3, v3, small3, qg_row, kg_row, bf_row, ck, cv, clf_t):
    bd, _, w = q3.shape
    n_pages = page_table.shape[1]
    page = ck.shape[1]
    assert page == LANES and n_pages % PAGES_PER_STEP == 0
    n_pg = PAGES_PER_STEP
    steps = n_pages // n_pg

    def page_idx(g):
        return lambda i, s, pt: (pt[i, n_pages - 1 - (s * n_pg + g)], 0, 0)

    row_spec = lambda width: pl.BlockSpec((None, 1, width), lambda i, s, pt: (i, 0, 0))
    const_spec = lambda width: pl.BlockSpec((1, width), lambda i, s, pt: (0, 0))
    in_specs = [row_spec(w), row_spec(w), row_spec(w), row_spec(LANES), const_spec(w), const_spec(w), const_spec(LANES)]
    args = [q3, k3, v3, small3, qg_row, kg_row, bf_row]
    for g in range(n_pg):
        in_specs += [pl.BlockSpec((None, page, w), page_idx(g)),
                     pl.BlockSpec((None, page, w), page_idx(g)),
                     pl.BlockSpec((None, H_F, page), page_idx(g))]
        args += [ck, cv, clf_t]
    grid_spec = pltpu.PrefetchScalarGridSpec(
        num_scalar_prefetch=1,
        grid=(bd, steps),
        in_specs=in_specs,
        out_specs=[row_spec(w), row_spec(w), row_spec(LANES)],
        scratch_shapes=[pltpu.VMEM((HP, w), BF16), pltpu.VMEM((HP, 1), F32), pltpu.VMEM((HP, 1), F32),
                        pltpu.VMEM((HP, w), F32), pltpu.VMEM((HP, 1), F32)],
    )
    return pl.pallas_call(
        functools.partial(_decode_attn_body, n_pg=n_pg),
        grid_spec=grid_spec,
        out_shape=[jax.ShapeDtypeStruct((bd, 1, w), F32), jax.ShapeDtypeStruct((bd, 1, w), F32),
                   jax.ShapeDtypeStruct((bd, 1, LANES), F32)],
        compiler_params=_cparams("parallel", "arbitrary"),
        name="decode_attn",
    )(page_table, *args)


def _merge_body(x_ref, oa_ref, z_ref, of_ref, ga_ref, gf_ref, na_ref, pa_ref, pf_ref, wo_ref, gffn_ref,
                wr_hi_ref, wr_lo_ref, br_ref, x1_ref, h2_ref, tw_ref, ti_ref):
    o = oa_ref[...]
    z = z_ref[...]
    na = na_ref[...]
    parts = []
    for h in range(H_A):
        oh = o[:, h * DV_A:(h + 1) * DV_A]
        zh = z[:, h * DV_A:(h + 1) * DV_A]
        on = oh * lax.rsqrt(jnp.mean(oh * oh, axis=-1, keepdims=True) + EPS) * na
        parts.append((on * (zh * _sigmoid(zh))).astype(BF16))
    o_a = jnp.concatenate(parts, axis=-1)
    ya = _dot(o_a, pa_ref[...])
    yf = _dot(of_ref[...].astype(BF16), pf_ref[...])
    mixed = _sigmoid(ga_ref[...]) * ya + _sigmoid(gf_ref[...]) * yf
    x1 = x_ref[...] + _dot(mixed.astype(BF16), wo_ref[...])
    x1_ref[...] = x1
    h2 = x1 * lax.rsqrt(jnp.mean(x1 * x1, axis=-1, keepdims=True) + EPS) * gffn_ref[...]
    h2_ref[...] = h2.astype(BF16)
    h_hi, h_lo = _split2(h2)
    logits = _dot(h_hi, wr_hi_ref[...]) + _dot(h_hi, wr_lo_ref[...]) + _dot(h_lo, wr_hi_ref[...]) + br_ref[...]
    lane = lax.broadcasted_iota(jnp.int32, logits.shape, 1)
    l = jnp.where(lane < N_EXP, logits, -jnp.inf)
    vals, idxs = [], []
    for _ in range(TOP_K):
        m = jnp.max(l, axis=-1, keepdims=True)
        idx = jnp.min(jnp.where(l == m, lane, LANES), axis=-1, keepdims=True)
        vals.append(m)
        idxs.append(idx)
        l = jnp.where(lane == idx, -jnp.inf, l)
    es = [jnp.exp(v - vals[0]) for v in vals]
    den = es[0]
    for e in es[1:]:
        den = den + e
    tw = jnp.zeros(logits.shape, F32)
    ti = jnp.zeros(logits.shape, jnp.int32)
    for kk in range(TOP_K):
        tw = jnp.where(lane == kk, es[kk] / den, tw)
        ti = jnp.where(lane == kk, idxs[kk], ti)
    tw_ref[...] = tw
    ti_ref[...] = ti


def _merge(x3, oa3, z3, of3, ga3, gf3, na_row, pa, pf, wo, gffn_row, wr_hi, wr_lo, br_row):
    g, t, d = x3.shape
    tm = _row_tile(t, 384)
    tok = lambda c: pl.BlockSpec((None, tm, c), lambda i, j: (i, j, 0))
    const = lambda a: pl.BlockSpec(a.shape, lambda i, j: (0,) * a.ndim, pipeline_mode=pl.Buffered(1))
    wa = H_A * DV_A
    wf = H_F * HD_F
    return pl.pallas_call(
        _merge_body,
        grid=(g, t // tm),
        in_specs=[tok(d), tok(wa), tok(wa), tok(wf), tok(d), tok(d),
                  const(na_row), const(pa), const(pf), const(wo), const(gffn_row),
                  const(wr_hi), const(wr_lo), const(br_row)],
        out_specs=[tok(d), tok(d), tok(LANES), tok(LANES)],
        out_shape=[jax.ShapeDtypeStruct((g, t, d), F32), jax.ShapeDtypeStruct((g, t, d), BF16),
                   jax.ShapeDtypeStruct((g, t, LANES), F32), jax.ShapeDtypeStruct((g, t, LANES), jnp.int32)],
        compiler_params=_cparams("parallel", "parallel"),
        name="merge",
    )(x3, oa3, z3, of3, ga3, gf3, na_row, pa, pf, wo, gffn_row, wr_hi, wr_lo, br_row)


def _moe_body(be_ref, nv_ref, x_ref, wg_ref, bg_ref, wu_ref, bu_ref, wd_ref, bd_ref, y_ref):
    i = pl.program_id(0)

    @pl.when(i < nv_ref[0])
    def _():
        x = x_ref[...]
        gate = jnp.minimum(_dot(x, wg_ref[...]) + bg_ref[...], SWIGLU_LIMIT)
        up = jnp.clip(_dot(x, wu_ref[...]) + bu_ref[...], -SWIGLU_LIMIT, SWIGLU_LIMIT)
        act = (up + 1.0) * gate * _sigmoid(SWIGLU_ALPHA * gate)
        y_ref[...] = _dot(act.astype(BF16), wd_ref[...]) + bd_ref[...]

    @pl.when(i >= nv_ref[0])
    def _():
        y_ref[...] = jnp.zeros(y_ref.shape, y_ref.dtype)


def _moe_experts(blk_e, n_valid, xs, wg, bg, wu, bu, wd, bd):
    n_rows, d = xs.shape
    bm = MOE_BLOCK
    n_blk = n_rows // bm
    de = wg.shape[-1]
    row_idx = lambda i, be, nv: (jnp.minimum(i, nv[0] - 1), 0)
    wspec = lambda a, b: pl.BlockSpec((None, a, b), lambda i, be, nv: (be[i], 0, 0))
    grid_spec = pltpu.PrefetchScalarGridSpec(
        num_scalar_prefetch=2,
        grid=(n_blk,),
        in_specs=[pl.BlockSpec((bm, d), row_idx),
                  wspec(d, de), wspec(1, de), wspec(d, de), wspec(1, de), wspec(de, d), wspec(1, d)],
        out_specs=pl.BlockSpec((bm, d), lambda i, be, nv: (i, 0)),
    )
    return pl.pallas_call(
        _moe_body,
        grid_spec=grid_spec,
        out_shape=jax.ShapeDtypeStruct((n_rows, d), F32),
        compiler_params=_cparams("arbitrary"),
        name="moe_experts",
    )(blk_e, n_valid, xs, wg, bg, wu, bu, wd, bd)


def _moe(h2, top_w, top_i, wg, bg, wu, bu, wd, bd):
    n, d = h2.shape
    bm = MOE_BLOCK
    m = n * TOP_K
    n_blk = -(-(m + N_EXP * (bm - 1)) // bm)
    flat_e = top_i.reshape(-1)
    onehot = (flat_e[:, None] == jnp.arange(N_EXP, dtype=jnp.int32)[None, :]).astype(jnp.int32)
    csum = jnp.cumsum(onehot, axis=0)
    rank = jnp.take_along_axis(csum, flat_e[:, None], axis=1)[:, 0] - 1
    counts = csum[-1]
    padded = (counts + bm - 1) // bm * bm
    pend = jnp.cumsum(padded)
    pstart = pend - padded
    dest = pstart[flat_e] + rank
    n_valid = (pend[-1] // bm).astype(jnp.int32)
    blk_first = jnp.minimum(jnp.arange(n_blk, dtype=jnp.int32), n_valid - 1) * bm
    blk_e = jnp.minimum(jnp.searchsorted(pend, blk_first, side='right'), N_EXP - 1).astype(jnp.int32)
    tok = jnp.arange(m, dtype=jnp.int32) // TOP_K
    row_tok = jnp.full((n_blk * bm,), n, jnp.int32).at[dest].set(tok)
    h_ext = jnp.concatenate([h2, jnp.zeros((1, d), h2.dtype)], axis=0)
    xs = h_ext[row_tok]
    y = _moe_experts(blk_e, n_valid.reshape(1), xs, wg, bg, wu, bu, wd, bd)
    yg = y[dest].reshape(n, TOP_K, d)
    return jnp.sum(yg * top_w[:, :, None], axis=1)


def _lane_row(vals, offset, width=LANES):
    return jnp.zeros((1, width), F32).at[0, offset:offset + vals.shape[0]].set(vals.astype(F32))


def kernel(x_prompt, x_sample, cache_k, cache_v, cache_logf, state_delta, state_conv, page_table,
           meta_tokens, g_mix, w_in, conv_w, a_log, dt_bias, norm_a, qn_g, kn_g, b_forget,
           p_a, p_f, w_o, g_ffn, w_router, b_router, w_gate, b_gate, w_up, b_up, w_down, b_down):
    depth = w_in.shape[0]
    b, seq, d = x_prompt.shape
    bd = x_sample.shape[0]
    assert x_sample.shape[1] == 1 and DK_A == LANES and DV_A == LANES
    t = N_META + seq
    lead = (-N_META) % CHUNK
    t_pad = -(-t // ATT_BLOCK) * ATT_BLOCK
    qkv_w, va_w, wf = 2 * H_A * DK_A + H_A * DV_A, H_A * DV_A, H_F * HD_F
    sizes = (qkv_w, va_w, H_A, H_A, wf, wf, wf, H_F, d, d)
    offs = [0]
    for s_ in sizes:
        offs.append(offs[-1] + s_)
    col = lambda i: slice(offs[i], offs[i + 1])
    widths = (qkv_w, va_w, wf, wf, wf, d, d)

    xp = jnp.concatenate([jnp.broadcast_to(meta_tokens.astype(x_prompt.dtype)[None], (b, N_META, d)), x_prompt], axis=1)
    xs = x_sample
    new_p = [[] for _ in range(5)]
    new_s = [[] for _ in range(5)]
    for l in range(depth):
        wl = w_in[l]
        w_main = jnp.concatenate([wl[:, col(0)], wl[:, col(1)], wl[:, col(4)], wl[:, col(5)], wl[:, col(6)],
                                  wl[:, col(8)], wl[:, col(9)]], axis=1).astype(BF16)
        w_small = jnp.concatenate([wl[:, col(2)], wl[:, col(3)], wl[:, col(7)],
                                   jnp.zeros((d, LANES - 2 * H_A - H_F), F32)], axis=1).astype(BF16)
        g_row = g_mix[l].reshape(1, d)
        a_row = _lane_row(a_log[l], H_A)
        dt_row = _lane_row(dt_bias[l], H_A)
        bf_row = _lane_row(b_forget[l], 2 * H_A)
        qg_pair = jnp.tile(qn_g[l], LANES // HD_F).reshape(1, LANES)
        kg_pair = jnp.tile(kn_g[l], LANES // HD_F).reshape(1, LANES)
        qg_full = jnp.tile(qn_g[l], H_F).reshape(1, wf)
        kg_full = jnp.tile(kn_g[l], H_F).reshape(1, wf)
        na_row = norm_a[l].reshape(1, DV_A)
        pa_b, pf_b, wo_b = p_a[l].astype(BF16), p_f[l].astype(BF16), w_o[l].astype(BF16)
        gffn_row = g_ffn[l].reshape(1, d)
        wr = jnp.pad(w_router[l], ((0, 0), (0, LANES - N_EXP)))
        wr_hi = wr.astype(BF16)
        wr_lo = (wr - wr_hi.astype(F32)).astype(BF16)
        br_row = _lane_row(b_router[l], 0)
        merge_w = (na_row, pa_b, pf_b, wo_b, gffn_row, wr_hi, wr_lo, br_row)

        qkv_p, z_p, qf_p, kf_p, vf_p, ga_p, gf_p, sm_p = _in_proj(xp.reshape(b * t, d), g_row, w_main, w_small, widths)
        r3 = lambda a: a.reshape(b, t, a.shape[-1])
        qkv3, sm3 = r3(qkv_p), r3(sm_p)
        conv_p = qkv3[:, t - (CONV_W - 1):, :]
        prep = _conv_prep(qkv3, conv_w[l], jnp.zeros((b, CONV_W - 1, qkv_w), F32))
        oa_p, s_p = _delta_scan(prep, sm3, a_row, dt_row, jnp.zeros((b, H_A, DK_A, DV_A), F32), lead)
        qb, kb, vb, kn_p = _fox_prep(r3(qf_p), r3(kf_p), r3(vf_p), qg_pair, kg_pair, t_pad)
        lf_p, c_col, c_row = _logf_cumsum(sm3, bf_row, t_pad)
        of_p = _fox_attn(qb, kb, vb, c_col, c_row)
        x1_p, h2_p, tw_p, ti_p = _merge(xp, oa_p, r3(z_p), of_p, r3(ga_p), r3(gf_p), *merge_w)

        qkv_s, z_s, qf_s, kf_s, vf_s, ga_s, gf_s, sm_s = _in_proj(xs.reshape(bd, d), g_row, w_main, w_small, widths)
        s3 = lambda a: a.reshape(bd, 1, a.shape[-1])
        oa_s, s_s, conv_s = _decode_delta(s3(qkv_s), state_conv[l], s3(sm_s), conv_w[l], a_row, dt_row, state_delta[l])
        n_pool, page = cache_k.shape[1], cache_k.shape[2]
        of_s, kn_s, lf_s = _decode_attn(page_table, s3(qf_s), s3(kf_s), s3(vf_s), s3(sm_s), qg_full, kg_full,
                                        _lane_row(b_forget[l], 2 * H_A),
                                        cache_k[l].reshape(n_pool, page, wf), cache_v[l].reshape(n_pool, page, wf),
                                        jnp.swapaxes(cache_logf[l], 1, 2))
        g1 = lambda a: a.reshape(1, bd, a.shape[-1])
        x1_s, h2_s, tw_s, ti_s = _merge(g1(xs), g1(oa_s), g1(z_s), g1(of_s), g1(ga_s), g1(gf_s), *merge_w)

        n_p = b * t
        h2_all = jnp.concatenate([h2_p.reshape(n_p, d), h2_s.reshape(bd, d)], axis=0)
        tw_all = jnp.concatenate([tw_p.reshape(n_p, LANES), tw_s.reshape(bd, LANES)], axis=0)[:, :TOP_K]
        ti_all = jnp.concatenate([ti_p.reshape(n_p, LANES), ti_s.reshape(bd, LANES)], axis=0)[:, :TOP_K]
        y_all = _moe(h2_all, tw_all, ti_all, w_gate[l].astype(BF16), b_gate[l][:, None, :],
                     w_up[l].astype(BF16), b_up[l][:, None, :], w_down[l].astype(BF16), b_down[l][:, None, :])
        xp = x1_p + y_all[:n_p].reshape(b, t, d)
        xs = x1_s.reshape(bd, 1, d) + y_all[n_p:].reshape(bd, 1, d)

        st_p = (kn_p.reshape(b, t, H_F, HD_F), vf_p.reshape(b, t, H_F, HD_F),
                lf_p[:, :, 2 * H_A:2 * H_A + H_F], s_p, conv_p)
        st_s = (kn_s.reshape(bd, 1, H_F, HD_F), vf_s.reshape(bd, 1, H_F, HD_F),
                lf_s[:, :, 2 * H_A:2 * H_A + H_F], s_s, conv_s)
        for lst, a in zip(new_p, st_p):
            lst.append(a)
        for lst, a in zip(new_s, st_s):
            lst.append(a)
    k_p, v_p, lf_pp, d_p, c_p = (jnp.stack(a) for a in new_p)
    k_s, v_s, lf_ss, d_s, c_s = (jnp.stack(a) for a in new_s)
    return (xp[:, N_META:], xs, k_p, v_p, lf_pp, k_s, v_s, lf_ss, d_p, d_s, c_p, c_s)
```

```python
import functools

import jax
import jax.numpy as jnp
from jax import lax
from jax.experimental import pallas as pl
from jax.experimental.pallas import tpu as pltpu

F32 = jnp.float32
BF16 = jnp.bfloat16

N_META = 16
H_A = 4
DK_A = 128
DV_A = 128
CONV_W = 4
CHUNK = 64
H_F = 8
HD_F = 64
N_EXP = 32
TOP_K = 4
SWIGLU_LIMIT = 7.0
SWIGLU_ALPHA = 1.702
EPS = 1e-6
LOG2E = 1.4426950408889634

LANES = 128
F_OFF = 2 * H_A
HP = 16
ATT_BLOCK = 256
MOE_BLOCK = 256
PAGES_PER_STEP = 4
VMEM_LIMIT = 48 * 1024 * 1024


def _cparams(*sem):
    return pltpu.CompilerParams(dimension_semantics=sem, vmem_limit_bytes=VMEM_LIMIT)


def _row_tile(n, cap):
    if n <= cap:
        return n
    best = None
    for t in range(8, cap + 1, 8):
        if n % t == 0:
            best = t
    assert best is not None, n
    return best


def _dot(a, b):
    return jnp.dot(a, b, preferred_element_type=F32)


def _dot_nt(a, b):
    return lax.dot_general(a, b, (((1,), (1,)), ((), ())), preferred_element_type=F32)


def _dot_hi(a, b):
    return jnp.dot(a, b, preferred_element_type=F32, precision=lax.Precision.HIGHEST)


def _split2(x):
    hi = x.astype(BF16)
    lo = (x - hi.astype(F32)).astype(BF16)
    return hi, lo


def _split3(x):
    hi = x.astype(BF16)
    r = x - hi.astype(F32)
    mid = r.astype(BF16)
    lo = (r - mid.astype(F32)).astype(BF16)
    return hi, mid, lo


def _sigmoid(x):
    return 1.0 / (1.0 + jnp.exp(-x))


def _softplus(x):
    return jnp.maximum(x, 0.0) + jnp.log1p(jnp.exp(-jnp.abs(x)))


def _log_sigmoid(x):
    return jnp.minimum(x, 0.0) - jnp.log1p(jnp.exp(-jnp.abs(x)))


IN_WIDTHS = (2 * H_A * DK_A + H_A * DV_A, H_A * DV_A, H_F * HD_F, H_F * HD_F, H_F * HD_F)
COL_CHUNK = 512


def _in_proj_body(x_ref, g_ref, wm_ref, ws_ref, *out_refs, widths):
    x = x_ref[...]
    h = x * lax.rsqrt(jnp.mean(x * x, axis=-1, keepdims=True) + EPS) * g_ref[...]
    hb = h.astype(BF16)
    col = 0
    for o_ref, w in zip(out_refs[:-1], widths):
        for c0 in range(0, w, COL_CHUNK):
            o_ref[:, c0:c0 + COL_CHUNK] = _dot(hb, wm_ref[:, col + c0:col + c0 + COL_CHUNK])
        col += w
    out_refs[-1][...] = _dot(hb, ws_ref[...])


def _in_proj(x2d, g_row, w_main, w_small, widths):
    n, d = x2d.shape
    tm = _row_tile(n, 512)
    outs = [jax.ShapeDtypeStruct((n, w), F32) for w in widths] + [jax.ShapeDtypeStruct((n, LANES), F32)]
    out_specs = [pl.BlockSpec((tm, w), lambda i: (i, 0)) for w in widths] + [pl.BlockSpec((tm, LANES), lambda i: (i, 0))]
    return pl.pallas_call(
        functools.partial(_in_proj_body, widths=widths),
        grid=(n // tm,),
        in_specs=[pl.BlockSpec((tm, d), lambda i: (i, 0)),
                  pl.BlockSpec((1, d), lambda i: (0, 0)),
                  pl.BlockSpec(w_main.shape, lambda i: (0, 0), pipeline_mode=pl.Buffered(1)),
                  pl.BlockSpec(w_small.shape, lambda i: (0, 0), pipeline_mode=pl.Buffered(1))],
        out_specs=out_specs,
        out_shape=outs,
        compiler_params=_cparams("parallel"),
        name="in_proj",
    )(x2d, g_row, w_main, w_small)


def _conv_prep_body(x_ref, cw_ref, prev_ref, o_ref, buf, *, t_len, rows):
    j = pl.program_id(1)
    pad = 8
    buf[0:pad, :] = jnp.zeros((pad, LANES), F32)
    buf[pad - (CONV_W - 1):pad, :] = prev_ref[...]
    buf[pad:pad + t_len, :] = x_ref[...]
    cw = cw_ref[...]
    is_q = j < H_A
    is_qk = j < 2 * H_A
    for r0 in range(0, t_len, rows):
        y = jnp.zeros((rows, LANES), F32)
        for w in range(CONV_W):
            off = pad - (CONV_W - 1) + w + r0
            y = y + buf[off:off + rows, :] * cw[w:w + 1, :]
        y = y * _sigmoid(y)
        nrm = lax.rsqrt(jnp.sum(y * y, axis=-1, keepdims=True) + EPS)
        f = jnp.where(is_qk, nrm * jnp.where(is_q, DK_A ** -0.5, 1.0), 1.0)
        o_ref[r0:r0 + rows, :] = y * f


def _conv_prep(qkv3, conv_w, conv_prev):
    b, t, c = qkv3.shape
    rows = _row_tile(t, 512)
    return pl.pallas_call(
        functools.partial(_conv_prep_body, t_len=t, rows=rows),
        grid=(b, c // LANES),
        in_specs=[pl.BlockSpec((None, t, LANES), lambda i, j: (i, 0, j)),
                  pl.BlockSpec((CONV_W, LANES), lambda i, j: (0, j)),
                  pl.BlockSpec((None, CONV_W - 1, LANES), lambda i, j: (i, 0, j))],
        out_specs=pl.BlockSpec((None, t, LANES), lambda i, j: (i, 0, j)),
        out_shape=jax.ShapeDtypeStruct((b, t, c), F32),
        scratch_shapes=[pltpu.VMEM((8 + t, LANES), F32)],
        compiler_params=_cparams("parallel", "parallel"),
        name="conv_prep",
    )(qkv3, conv_w, conv_prev)


def _gates(sm, a_row, dt_row, h):
    lane = lax.broadcasted_iota(jnp.int32, sm.shape, 1)
    beta_all = _sigmoid(sm)
    g_all = -jnp.exp(a_row) * _softplus(sm + dt_row)
    beta = jnp.sum(jnp.where(lane == h, beta_all, 0.0), axis=-1, keepdims=True)
    g = jnp.sum(jnp.where(lane == H_A + h, g_all, 0.0), axis=-1, keepdims=True)
    return beta, g


def _delta_chunk(qs, ks, vs, betas, gs, s_ref):
    n = len(qs)
    rng = range(n)
    c = qs[0].shape[0]
    ri = lax.broadcasted_iota(jnp.int32, (c, c), 0)
    ci = lax.broadcasted_iota(jnp.int32, (c, c), 1)
    eye = ri == ci
    incl = ci <= ri
    strict = ci < ri
    g_row = [jnp.sum(jnp.where(eye, gs[i], 0.0), axis=0, keepdims=True) for i in rng]
    g_cum = [jnp.sum(jnp.where(incl, g_row[i], 0.0), axis=1, keepdims=True) for i in rng]
    g_cum_row = [jnp.sum(jnp.where(ri <= ci, gs[i], 0.0), axis=0, keepdims=True) for i in rng]
    decay = [jnp.exp(jnp.where(incl, g_cum[i] - g_cum_row[i], -jnp.inf)) for i in rng]
    e_g = [jnp.exp(g_cum[i]) for i in rng]
    s = [s_ref[i] for i in rng]
    sb = [s[i].astype(BF16) for i in rng]
    qb = [qs[i].astype(BF16) for i in rng]
    kb = [ks[i].astype(BF16) for i in rng]
    kk = [_dot_nt(kb[i], kb[i]) for i in rng]
    k_s = [_dot(kb[i], sb[i]) for i in rng]
    a = [jnp.where(strict, betas[i] * decay[i] * kk[i], 0.0) for i in rng]
    rhs = [betas[i] * (vs[i] - e_g[i] * k_s[i]) for i in rng]
    inv = [jnp.where(eye, 1.0, 0.0) - a[i] for i in rng]
    pw = a
    n_sq = max(1, (c - 1).bit_length() - 1)
    for _ in range(n_sq):
        pw = [_dot_hi(pw[i], pw[i]) for i in rng]
        inv = [inv[i] + _dot_hi(inv[i], pw[i]) for i in rng]
    qk = [_dot_nt(qb[i], kb[i]) for i in rng]
    q_s = [_dot(qb[i], sb[i]) for i in rng]
    db = [_dot_hi(inv[i], rhs[i]).astype(BF16) for i in rng]
    o = [e_g[i] * q_s[i] + _dot((qk[i] * decay[i]).astype(BF16), db[i]) for i in rng]
    g_end = [g_cum[i][c - 1:c, :] for i in rng]
    k_dec = [(ks[i] * jnp.exp(g_end[i] - g_cum[i])).T.astype(BF16) for i in rng]
    for i in rng:
        s_ref[i] = jnp.exp(g_end[i]) * s[i] + _dot(k_dec[i], db[i])
    return o


def _delta_scan_body(qkv_ref, sm_ref, a_ref, dt_ref, s0_ref, o_ref, s_out_ref, s_ref, *, t_len, lead):
    a_row = a_ref[...]
    dt_row = dt_ref[...]
    s_ref[...] = s0_ref[...]
    first = CHUNK - lead
    n_chunks = (lead + t_len) // CHUNK

    def heads(load, store, mask_lead):
        sm = load(sm_ref, 0)
        betas, gs = [], []
        for h in range(H_A):
            beta, g = _gates(sm, a_row, dt_row, h)
            if mask_lead:
                row = lax.broadcasted_iota(jnp.int32, (CHUNK, 1), 0)
                beta = jnp.where(row >= lead, beta, 0.0)
                g = jnp.where(row >= lead, g, 0.0)
            betas.append(beta)
            gs.append(g)
        qs = [load(qkv_ref, h * DK_A) for h in range(H_A)]
        ks = [load(qkv_ref, (H_A + h) * DK_A) for h in range(H_A)]
        vs = [load(qkv_ref, 2 * H_A * DK_A + h * DV_A) for h in range(H_A)]
        for h, o in enumerate(_delta_chunk(qs, ks, vs, betas, gs, s_ref)):
            store(h, o)

    if lead:
        def load0(ref, col):
            return jnp.concatenate([jnp.zeros((lead, LANES), F32), ref[0:first, col:col + LANES]], axis=0)

        def store0(h, o):
            o_ref[0:first, h * DV_A:(h + 1) * DV_A] = o[lead:, :]

        heads(load0, store0, True)
        c_start = 1
    else:
        c_start = 0

    def body(c, carry):
        r0 = pl.multiple_of(c * CHUNK - lead, 8)
        sl = pl.ds(r0, CHUNK)

        def store(h, o):
            o_ref[sl, h * DV_A:(h + 1) * DV_A] = o

        heads(lambda ref, col: ref[sl, col:col + LANES], store, False)
        return carry

    lax.fori_loop(c_start, n_chunks, body, 0)
    s_out_ref[...] = s_ref[...]


def _delta_scan(qkv3, small3, a_row, dt_row, s0, lead):
    b, t, c = qkv3.shape
    assert (lead + t) % CHUNK == 0 and lead % 8 == 0
    return pl.pallas_call(
        functools.partial(_delta_scan_body, t_len=t, lead=lead),
        grid=(b,),
        in_specs=[pl.BlockSpec((None, t, c), lambda i: (i, 0, 0)),
                  pl.BlockSpec((None, t, LANES), lambda i: (i, 0, 0)),
                  pl.BlockSpec((1, LANES), lambda i: (0, 0)),
                  pl.BlockSpec((1, LANES), lambda i: (0, 0)),
                  pl.BlockSpec((None, H_A, DK_A, DV_A), lambda i: (i, 0, 0, 0))],
        out_specs=[pl.BlockSpec((None, t, H_A * DV_A), lambda i: (i, 0, 0)),
                   pl.BlockSpec((None, H_A, DK_A, DV_A), lambda i: (i, 0, 0, 0))],
        out_shape=[jax.ShapeDtypeStruct((b, t, H_A * DV_A), F32),
                   jax.ShapeDtypeStruct((b, H_A, DK_A, DV_A), F32)],
        scratch_shapes=[pltpu.VMEM((H_A, DK_A, DV_A), F32)],
        compiler_params=_cparams("parallel"),
        name="delta_scan",
    )(qkv3, small3, a_row, dt_row, s0)


def _group_rms(x, gain, seg):
    hi, lo = _split2(x * x)
    ss = _dot(hi, seg) + _dot(lo, seg)
    return x * lax.rsqrt(ss * (1.0 / HD_F) + EPS) * gain


def _seg_matrix(n):
    r = lax.broadcasted_iota(jnp.int32, (n, n), 0) // HD_F
    c = lax.broadcasted_iota(jnp.int32, (n, n), 1) // HD_F
    return jnp.where(r == c, 1.0, 0.0).astype(BF16)


def _fox_prep_body(q_ref, k_ref, v_ref, c_ref, qg_ref, kg_ref, qx_ref, kx_ref, vb_ref, kn_ref, *, t_len, t_pad):
    p = pl.program_id(1)
    heads = LANES // HD_F
    seg = _seg_matrix(LANES)
    qn = _group_rms(q_ref[...], qg_ref[...], seg) * (HD_F ** -0.5 * LOG2E)
    kn = _group_rms(k_ref[...], kg_ref[...], seg)
    kn_ref[...] = kn
    c_all = c_ref[...] * LOG2E
    lane = lax.broadcasted_iota(jnp.int32, (t_len, LANES), 1)
    for hh in range(heads):
        c = jnp.sum(jnp.where(lane == F_OFF + p * heads + hh, c_all, 0.0), axis=-1, keepdims=True)
        pieces = [x.astype(F32) for x in _split3(c)]
        own = (lane // HD_F) == hh
        f0 = ((hh + 1) % heads) * HD_F
        qx = jnp.where(own, qn, 0.0)
        kx = jnp.where(own, kn, 0.0)
        for n, piece in enumerate(pieces):
            qx = jnp.where(lane == f0 + n, piece, qx)
            kx = jnp.where(lane == f0 + n, 1.0, kx)
            qx = jnp.where(lane == f0 + 3 + n, 1.0, qx)
            kx = jnp.where(lane == f0 + 3 + n, -piece, kx)
        qx_ref[0:t_len, hh * LANES:(hh + 1) * LANES] = qx.astype(BF16)
        kx_ref[0:t_len, hh * LANES:(hh + 1) * LANES] = kx.astype(BF16)
    vb_ref[0:t_len, :] = v_ref[...].astype(BF16)
    if t_pad > t_len:
        qx_ref[t_len:, :] = jnp.zeros((t_pad - t_len, heads * LANES), BF16)
        kx_ref[t_len:, :] = jnp.zeros((t_pad - t_len, heads * LANES), BF16)
        vb_ref[t_len:, :] = jnp.zeros((t_pad - t_len, LANES), BF16)


def _fox_prep(q3, k3, v3, c_col, qg_row, kg_row, t_pad):
    b, t, w = q3.shape
    heads = LANES // HD_F
    blk = pl.BlockSpec((None, t, LANES), lambda i, p: (i, 0, p))
    xblk = pl.BlockSpec((None, t_pad, heads * LANES), lambda i, p: (i, 0, p))
    gblk = pl.BlockSpec((1, LANES), lambda i, p: (0, 0))
    return pl.pallas_call(
        functools.partial(_fox_prep_body, t_len=t, t_pad=t_pad),
        grid=(b, w // LANES),
        in_specs=[blk, blk, blk, pl.BlockSpec((None, t, LANES), lambda i, p: (i, 0, 0)), gblk, gblk],
        out_specs=[xblk, xblk, pl.BlockSpec((None, t_pad, LANES), lambda i, p: (i, 0, p)), blk],
        out_shape=[jax.ShapeDtypeStruct((b, t_pad, H_F * LANES), BF16), jax.ShapeDtypeStruct((b, t_pad, H_F * LANES), BF16),
                   jax.ShapeDtypeStruct((b, t_pad, w), BF16), jax.ShapeDtypeStruct((b, t, w), F32)],
        compiler_params=_cparams("parallel", "parallel"),
        name="fox_prep",
    )(q3, k3, v3, c_col, qg_row, kg_row)


def _logf_cumsum_body(sm_ref, bf_ref, lf_ref, ccol_ref, buf, *, t_len, t_pad):
    lf = _log_sigmoid(sm_ref[...] + bf_ref[...])
    lf_ref[...] = lf
    buf[0:t_len, :] = lf
    if t_pad > t_len:
        buf[t_len:, :] = jnp.zeros((t_pad - t_len, LANES), F32)
    blk = ATT_BLOCK
    ri = lax.broadcasted_iota(jnp.int32, (blk, blk), 0)
    ci = lax.broadcasted_iota(jnp.int32, (blk, blk), 1)
    tri = jnp.where(ci <= ri, 1.0, 0.0).astype(BF16)
    carry = jnp.zeros((1, LANES), F32)
    for i in range(t_pad // blk):
        hi, mid, lo = _split3(buf[i * blk:(i + 1) * blk, :])
        c = _dot(tri, hi) + _dot(tri, mid) + _dot(tri, lo) + carry
        ccol_ref[i * blk:(i + 1) * blk, :] = c
        carry = c[blk - 1:blk, :]


def _logf_cumsum(small3, bf_row, t_pad):
    b, t, _ = small3.shape
    return pl.pallas_call(
        functools.partial(_logf_cumsum_body, t_len=t, t_pad=t_pad),
        grid=(b,),
        in_specs=[pl.BlockSpec((None, t, LANES), lambda i: (i, 0, 0)),
                  pl.BlockSpec((1, LANES), lambda i: (0, 0))],
        out_specs=[pl.BlockSpec((None, t, LANES), lambda i: (i, 0, 0)),
                   pl.BlockSpec((None, t_pad, LANES), lambda i: (i, 0, 0))],
        out_shape=[jax.ShapeDtypeStruct((b, t, LANES), F32),
                   jax.ShapeDtypeStruct((b, t_pad, LANES), F32)],
        scratch_shapes=[pltpu.VMEM((t_pad, LANES), F32)],
        compiler_params=_cparams("parallel"),
        name="logf_cumsum",
    )(small3, bf_row)


def _fox_attn_body(q_ref, k_ref, v_ref, o_ref):
    i = pl.program_id(1)
    blk = ATT_BLOCK
    heads = LANES // HD_F
    n_pairs = H_F // heads
    first_half = lax.broadcasted_iota(jnp.int32, (blk, LANES), 1) < HD_F
    on_or_below = (lax.broadcasted_iota(jnp.int32, (blk, blk), 1)
                   <= lax.broadcasted_iota(jnp.int32, (blk, blk), 0))

    def step(j, carry, diagonal):
        ms, ls, accs = carry
        k0 = pl.multiple_of(j * blk, blk)
        new_ms, new_ls, new_accs = [], [], []
        for p in range(n_pairs):
            v = v_ref[pl.ds(k0, blk), p * LANES:(p + 1) * LANES]
            pvs, alphas = [], []
            for hh in range(heads):
                h = p * heads + hh
                s = _dot_nt(q_ref[:, h * LANES:(h + 1) * LANES], k_ref[pl.ds(k0, blk), h * LANES:(h + 1) * LANES])
                if diagonal:
                    s = jnp.where(on_or_below, s, -jnp.inf)
                m_new = jnp.maximum(ms[h], jnp.max(s, axis=-1, keepdims=True))
                alpha = jnp.exp2(ms[h] - m_new)
                pr = jnp.exp2(s - m_new)
                new_ls.append(alpha * ls[h] + jnp.sum(pr, axis=-1, keepdims=True))
                new_ms.append(m_new)
                pvs.append(_dot(pr.astype(BF16), v))
                alphas.append(alpha)
            pv = jnp.where(first_half, pvs[0], pvs[1])
            al = jnp.where(first_half, alphas[0], alphas[1])
            new_accs.append(al * accs[p] + pv)
        return tuple(new_ms), tuple(new_ls), tuple(new_accs)

    init = (tuple(jnp.full((blk, 1), -jnp.inf, F32) for _ in range(H_F)),
            tuple(jnp.zeros((blk, 1), F32) for _ in range(H_F)),
            tuple(jnp.zeros((blk, LANES), F32) for _ in range(n_pairs)))
    carry = lax.fori_loop(0, i, lambda j, c: step(j, c, False), init)
    ms, ls, accs = step(i, carry, True)
    for p in range(n_pairs):
        inv = jnp.where(first_half, 1.0 / ls[p * heads], 1.0 / ls[p * heads + 1])
        o_ref[:, p * LANES:(p + 1) * LANES] = (accs[p] * inv).astype(o_ref.dtype)


def _fox_attn(qx, kx, vb):
    b, t_pad, w = vb.shape
    blk = ATT_BLOCK
    assert LANES // HD_F == 2
    return pl.pallas_call(
        _fox_attn_body,
        grid=(b, t_pad // blk),
        in_specs=[pl.BlockSpec((None, blk, H_F * LANES), lambda n, i: (n, i, 0)),
                  pl.BlockSpec((None, t_pad, H_F * LANES), lambda n, i: (n, 0, 0)),
                  pl.BlockSpec((None, t_pad, w), lambda n, i: (n, 0, 0))],
        out_specs=pl.BlockSpec((None, blk, w), lambda n, i: (n, i, 0)),
        out_shape=jax.ShapeDtypeStruct((b, t_pad, w), BF16),
        compiler_params=_cparams("parallel", "arbitrary"),
        name="fox_attn",
    )(qx, kx, vb)


def _decode_delta_body(x_ref, sc_ref, sm_ref, cw_ref, a_ref, dt_ref, s_ref, o_ref, s_out_ref, conv_ref):
    x_new = x_ref[...]
    sc = sc_ref[...]
    cw = cw_ref[...]
    y = x_new * cw[CONV_W - 1:CONV_W, :]
    for w in range(CONV_W - 1):
        y = y + sc[w:w + 1, :] * cw[w:w + 1, :]
    y = y * _sigmoid(y)
    conv_ref[0:CONV_W - 2, :] = sc[1:, :]
    conv_ref[CONV_W - 2:CONV_W - 1, :] = x_new
    sm = sm_ref[...]
    lane = lax.broadcasted_iota(jnp.int32, (1, LANES), 1)
    beta_all = _sigmoid(sm)
    g_all = -jnp.exp(a_ref[...]) * _softplus(sm + dt_ref[...])
    qk_w = H_A * DK_A
    row0 = lax.broadcasted_iota(jnp.int32, (LANES, LANES), 0) == 0

    def in_row0(r):
        return jnp.where(row0, jnp.broadcast_to(r, (LANES, LANES)), 0.0)

    for h in range(H_A):
        q = y[:, h * DK_A:(h + 1) * DK_A]
        k = y[:, qk_w + h * DK_A:qk_w + (h + 1) * DK_A]
        v = y[:, 2 * qk_w + h * DV_A:2 * qk_w + (h + 1) * DV_A]
        q = q * lax.rsqrt(jnp.sum(q * q, axis=-1, keepdims=True) + EPS) * (DK_A ** -0.5)
        k = k * lax.rsqrt(jnp.sum(k * k, axis=-1, keepdims=True) + EPS)
        beta = jnp.sum(jnp.where(lane == h, beta_all, 0.0), axis=-1, keepdims=True)
        g = jnp.sum(jnp.where(lane == H_A + h, g_all, 0.0), axis=-1, keepdims=True)
        e_g = jnp.exp(g)
        s = s_ref[h]
        sb = s.astype(BF16)
        k_sq = in_row0(k)
        k_s = _dot(k_sq.astype(BF16), sb)[0:1, :]
        q_s = _dot(in_row0(q).astype(BF16), sb)[0:1, :]
        delta = beta * (v - e_g * k_s)
        qk = jnp.sum(q * k, axis=-1, keepdims=True)
        o_ref[:, h * DV_A:(h + 1) * DV_A] = e_g * q_s + qk * delta
        s_out_ref[h] = e_g * s + _dot(k_sq.T.astype(BF16), in_row0(delta).astype(BF16))


def _decode_delta(qkv3, state_conv, small3, conv_w, a_row, dt_row, s0):
    bd, _, c = qkv3.shape
    return pl.pallas_call(
        _decode_delta_body,
        grid=(bd,),
        in_specs=[pl.BlockSpec((None, 1, c), lambda i: (i, 0, 0)),
                  pl.BlockSpec((None, CONV_W - 1, c), lambda i: (i, 0, 0)),
                  pl.BlockSpec((None, 1, LANES), lambda i: (i, 0, 0)),
                  pl.BlockSpec((CONV_W, c), lambda i: (0, 0)),
                  pl.BlockSpec((1, LANES), lambda i: (0, 0)),
                  pl.BlockSpec((1, LANES), lambda i: (0, 0)),
                  pl.BlockSpec((None, H_A, DK_A, DV_A), lambda i: (i, 0, 0, 0))],
        out_specs=[pl.BlockSpec((None, 1, H_A * DV_A), lambda i: (i, 0, 0)),
                   pl.BlockSpec((None, H_A, DK_A, DV_A), lambda i: (i, 0, 0, 0)),
                   pl.BlockSpec((None, CONV_W - 1, c), lambda i: (i, 0, 0))],
        out_shape=[jax.ShapeDtypeStruct((bd, 1, H_A * DV_A), F32),
                   jax.ShapeDtypeStruct((bd, H_A, DK_A, DV_A), F32),
                   jax.ShapeDtypeStruct((bd, CONV_W - 1, c), F32)],
        compiler_params=_cparams("parallel"),
        name="decode_delta",
    )(qkv3, state_conv, small3, conv_w, a_row, dt_row, s0)


def _decode_attn_body(pt_ref, q_ref, k_ref, v_ref, sm_ref, qg_ref, kg_ref, bf_ref, *rest, n_pg):
    page_refs = rest[:3 * n_pg]
    o_ref, kn_ref, lf_ref, qrows, m_ref, l_ref, acc_ref, carry_ref = rest[3 * n_pg:]
    step = pl.program_id(1)
    w = H_F * HD_F
    row = lax.broadcasted_iota(jnp.int32, (HP, w), 0)
    head_of_lane = lax.broadcasted_iota(jnp.int32, (HP, w), 1) // HD_F

    @pl.when(step == 0)
    def _():
        seg = _seg_matrix(w)

        def rms_rows(x_row, gain):
            xr = jnp.broadcast_to(x_row, (HP, w))
            hi, lo = _split2(xr * xr)
            ss = _dot(hi, seg) + _dot(lo, seg)
            return xr * lax.rsqrt(ss * (1.0 / HD_F) + EPS) * gain

        qn = rms_rows(q_ref[...], qg_ref[...]) * (HD_F ** -0.5)
        kn = rms_rows(k_ref[...], kg_ref[...])
        kn_ref[...] = kn[0:1, :]
        lf = _log_sigmoid(sm_ref[...] + bf_ref[...])
        lf_ref[...] = lf
        q_m = jnp.where(head_of_lane == row, qn, 0.0)
        qrows[...] = q_m.astype(BF16)
        m_ref[...] = jnp.sum(q_m * kn, axis=-1, keepdims=True)
        l_ref[...] = jnp.ones((HP, 1), F32)
        acc_ref[...] = jnp.broadcast_to(v_ref[...], (HP, w))
        rr = lax.broadcasted_iota(jnp.int32, (HP, LANES), 0)
        ll = lax.broadcasted_iota(jnp.int32, (HP, LANES), 1)
        carry_ref[...] = jnp.sum(jnp.where(ll == rr + F_OFF, jnp.broadcast_to(lf, (HP, LANES)), 0.0),
                                 axis=-1, keepdims=True)

    pos = lax.broadcasted_iota(jnp.int32, (HP, LANES), 1)
    qb = qrows[...]
    for g in range(n_pg):
        kp = page_refs[3 * g][...].astype(BF16)
        vp = page_refs[3 * g + 1][...].astype(BF16)
        lfp = jnp.concatenate([page_refs[3 * g + 2][...], jnp.zeros((HP - H_F, LANES), F32)], axis=0)
        suf = lfp
        sh = 1
        while sh < LANES:
            rolled = pltpu.roll(suf, LANES - sh, axis=1)
            suf = suf + jnp.where(pos + sh < LANES, rolled, 0.0)
            sh *= 2
        carry = carry_ref[...]
        bias = carry + (suf - lfp)
        s = _dot_nt(qb, kp) + bias
        m_old = m_ref[...]
        m_new = jnp.maximum(m_old, jnp.max(s, axis=-1, keepdims=True))
        alpha = jnp.exp(m_old - m_new)
        pr = jnp.exp(s - m_new)
        l_ref[...] = alpha * l_ref[...] + jnp.sum(pr, axis=-1, keepdims=True)
        m_ref[...] = m_new
        acc_ref[...] = alpha * acc_ref[...] + _dot(pr.astype(BF16), vp)
        carry_ref[...] = carry + jnp.sum(lfp, axis=-1, keepdims=True)

    @pl.when(step == pl.num_programs(1) - 1)
    def _():
        out = jnp.where(head_of_lane == row, acc_ref[...] / l_ref[...], 0.0)
        o_ref[...] = jnp.sum(out, axis=0, keepdims=True)


def _decode_attn(page_table, q3, k3, v3, small3, qg_row, kg_row, bf_row, ck, cv, clf_t):
    bd, _, w = q3.shape
    n_pages = page_table.shape[1]
    page = ck.shape[1]
    assert page == LANES and n_pages % PAGES_PER_STEP == 0
    n_pg = PAGES_PER_STEP
    steps = n_pages // n_pg

    def page_idx(g):
        return lambda i, s, pt: (pt[i, n_pages - 1 - (s * n_pg + g)], 0, 0)

    row_spec = lambda width: pl.BlockSpec((None, 1, width), lambda i, s, pt: (i, 0, 0))
    const_spec = lambda width: pl.BlockSpec((1, width), lambda i, s, pt: (0, 0))
    in_specs = [row_spec(w), row_spec(w), row_spec(w), row_spec(LANES), const_spec(w), const_spec(w), const_spec(LANES)]
    args = [q3, k3, v3, small3, qg_row, kg_row, bf_row]
    for g in range(n_pg):
        in_specs += [pl.BlockSpec((None, page, w), page_idx(g)),
                     pl.BlockSpec((None, page, w), page_idx(g)),
                     pl.BlockSpec((None, H_F, page), page_idx(g))]
        args += [ck, cv, clf_t]
    grid_spec = pltpu.PrefetchScalarGridSpec(
        num_scalar_prefetch=1,
        grid=(bd, steps),
        in_specs=in_specs,
        out_specs=[row_spec(w), row_spec(w), row_spec(LANES)],
        scratch_shapes=[pltpu.VMEM((HP, w), BF16), pltpu.VMEM((HP, 1), F32), pltpu.VMEM((HP, 1), F32),
                        pltpu.VMEM((HP, w), F32), pltpu.VMEM((HP, 1), F32)],
    )
    return pl.pallas_call(
        functools.partial(_decode_attn_body, n_pg=n_pg),
        grid_spec=grid_spec,
        out_shape=[jax.ShapeDtypeStruct((bd, 1, w), F32), jax.ShapeDtypeStruct((bd, 1, w), F32),
                   jax.ShapeDtypeStruct((bd, 1, LANES), F32)],
        compiler_params=_cparams("parallel", "arbitrary"),
        name="decode_attn",
    )(page_table, *args)


def _merge_body(x_ref, oa_ref, z_ref, of_ref, ga_ref, gf_ref, na_ref, pa_ref, pf_ref, wo_ref, gffn_ref,
                wr_hi_ref, wr_lo_ref, br_ref, x1_ref, h2_ref, tw_ref, ti_ref):
    o = oa_ref[...]
    z = z_ref[...]
    na = na_ref[...]
    parts = []
    for h in range(H_A):
        oh = o[:, h * DV_A:(h + 1) * DV_A]
        zh = z[:, h * DV_A:(h + 1) * DV_A]
        on = oh * lax.rsqrt(jnp.mean(oh * oh, axis=-1, keepdims=True) + EPS) * na
        parts.append((on * (zh * _sigmoid(zh))).astype(BF16))
    o_a = jnp.concatenate(parts, axis=-1)
    ya = _dot(o_a, pa_ref[...])
    yf = _dot(of_ref[...].astype(BF16), pf_ref[...])
    mixed = _sigmoid(ga_ref[...]) * ya + _sigmoid(gf_ref[...]) * yf
    x1 = x_ref[...] + _dot(mixed.astype(BF16), wo_ref[...])
    x1_ref[...] = x1
    h2 = x1 * lax.rsqrt(jnp.mean(x1 * x1, axis=-1, keepdims=True) + EPS) * gffn_ref[...]
    h2_ref[...] = h2.astype(BF16)
    h_hi, h_lo = _split2(h2)
    logits = _dot(h_hi, wr_hi_ref[...]) + _dot(h_hi, wr_lo_ref[...]) + _dot(h_lo, wr_hi_ref[...]) + br_ref[...]
    lane = lax.broadcasted_iota(jnp.int32, logits.shape, 1)
    l = jnp.where(lane < N_EXP, logits, -jnp.inf)
    vals, idxs = [], []
    for _ in range(TOP_K):
        m = jnp.max(l, axis=-1, keepdims=True)
        idx = jnp.min(jnp.where(l == m, lane, LANES), axis=-1, keepdims=True)
        vals.append(m)
        idxs.append(idx)
        l = jnp.where(lane == idx, -jnp.inf, l)
    es = [jnp.exp(v - vals[0]) for v in vals]
    den = es[0]
    for e in es[1:]:
        den = den + e
    tw = jnp.zeros(logits.shape, F32)
    ti = jnp.zeros(logits.shape, jnp.int32)
    for kk in range(TOP_K):
        tw = jnp.where(lane == kk, es[kk] / den, tw)
        ti = jnp.where(lane == kk, idxs[kk], ti)
    tw_ref[...] = tw
    ti_ref[...] = ti


def _merge(x3, oa3, z3, of3, ga3, gf3, na_row, pa, pf, wo, gffn_row, wr_hi, wr_lo, br_row):
    g, t, d = x3.shape
    tm = _row_tile(t, 384)
    tok = lambda c: pl.BlockSpec((None, tm, c), lambda i, j: (i, j, 0))
    const = lambda a: pl.BlockSpec(a.shape, lambda i, j: (0,) * a.ndim, pipeline_mode=pl.Buffered(1))
    wa = H_A * DV_A
    wf = H_F * HD_F
    return pl.pallas_call(
        _merge_body,
        grid=(g, t // tm),
        in_specs=[tok(d), tok(wa), tok(wa), tok(wf), tok(d), tok(d),
                  const(na_row), const(pa), const(pf), const(wo), const(gffn_row),
                  const(wr_hi), const(wr_lo), const(br_row)],
        out_specs=[tok(d), tok(d), tok(LANES), tok(LANES)],
        out_shape=[jax.ShapeDtypeStruct((g, t, d), F32), jax.ShapeDtypeStruct((g, t, d), BF16),
                   jax.ShapeDtypeStruct((g, t, LANES), F32), jax.ShapeDtypeStruct((g, t, LANES), jnp.int32)],
        compiler_params=_cparams("parallel", "parallel"),
        name="merge",
    )(x3, oa3, z3, of3, ga3, gf3, na_row, pa, pf, wo, gffn_row, wr_hi, wr_lo, br_row)


def _moe_body(be_ref, nv_ref, x_ref, wg_ref, bg_ref, wu_ref, bu_ref, wd_ref, bd_ref, y_ref):
    i = pl.program_id(0)

    @pl.when(i < nv_ref[0])
    def _():
        x = x_ref[...]
        gate = jnp.minimum(_dot(x, wg_ref[...]) + bg_ref[...], SWIGLU_LIMIT)
        up = jnp.clip(_dot(x, wu_ref[...]) + bu_ref[...], -SWIGLU_LIMIT, SWIGLU_LIMIT)
        act = (up + 1.0) * gate * _sigmoid(SWIGLU_ALPHA * gate)
        y_ref[...] = _dot(act.astype(BF16), wd_ref[...]) + bd_ref[...]

    @pl.when(i >= nv_ref[0])
    def _():
        y_ref[...] = jnp.zeros(y_ref.shape, y_ref.dtype)


def _moe_experts(blk_e, n_valid, xs, wg, bg, wu, bu, wd, bd):
    n_rows, d = xs.shape
    bm = MOE_BLOCK
    n_blk = n_rows // bm
    de = wg.shape[-1]
    row_idx = lambda i, be, nv: (jnp.minimum(i, nv[0] - 1), 0)
    wspec = lambda a, b: pl.BlockSpec((None, a, b), lambda i, be, nv: (be[i], 0, 0))
    grid_spec = pltpu.PrefetchScalarGridSpec(
        num_scalar_prefetch=2,
        grid=(n_blk,),
        in_specs=[pl.BlockSpec((bm, d), row_idx),
                  wspec(d, de), wspec(1, de), wspec(d, de), wspec(1, de), wspec(de, d), wspec(1, d)],
        out_specs=pl.BlockSpec((bm, d), lambda i, be, nv: (i, 0)),
    )
    return pl.pallas_call(
        _moe_body,
        grid_spec=grid_spec,
        out_shape=jax.ShapeDtypeStruct((n_rows, d), F32),
        compiler_params=_cparams("arbitrary"),
        name="moe_experts",
    )(blk_e, n_valid, xs, wg, bg, wu, bu, wd, bd)


def _moe(h2, top_w, top_i, wg, bg, wu, bu, wd, bd):
    n, d = h2.shape
    bm = MOE_BLOCK
    m = n * TOP_K
    n_blk = -(-(m + N_EXP * (bm - 1)) // bm)
    flat_e = top_i.reshape(-1)
    onehot = (flat_e[:, None] == jnp.arange(N_EXP, dtype=jnp.int32)[None, :]).astype(jnp.int32)
    csum = jnp.cumsum(onehot, axis=0)
    rank = jnp.take_along_axis(csum, flat_e[:, None], axis=1)[:, 0] - 1
    counts = csum[-1]
    padded = (counts + bm - 1) // bm * bm
    pend = jnp.cumsum(padded)
    pstart = pend - padded
    dest = pstart[flat_e] + rank
    n_valid = (pend[-1] // bm).astype(jnp.int32)
    blk_first = jnp.minimum(jnp.arange(n_blk, dtype=jnp.int32), n_valid - 1) * bm
    blk_e = jnp.minimum(jnp.searchsorted(pend, blk_first, side='right'), N_EXP - 1).astype(jnp.int32)
    tok = jnp.arange(m, dtype=jnp.int32) // TOP_K
    row_tok = jnp.full((n_blk * bm,), n, jnp.int32).at[dest].set(tok)
    h_ext = jnp.concatenate([h2, jnp.zeros((1, d), h2.dtype)], axis=0)
    xs = h_ext[row_tok]
    y = _moe_experts(blk_e, n_valid.reshape(1), xs, wg, bg, wu, bu, wd, bd)
    yg = y[dest].reshape(n, TOP_K, d)
    return jnp.sum(yg * top_w[:, :, None], axis=1)


def _lane_row(vals, offset, width=LANES):
    return jnp.zeros((1, width), F32).at[0, offset:offset + vals.shape[0]].set(vals.astype(F32))


def kernel(x_prompt, x_sample, cache_k, cache_v, cache_logf, state_delta, state_conv, page_table,
           meta_tokens, g_mix, w_in, conv_w, a_log, dt_bias, norm_a, qn_g, kn_g, b_forget,
           p_a, p_f, w_o, g_ffn, w_router, b_router, w_gate, b_gate, w_up, b_up, w_down, b_down):
    depth = w_in.shape[0]
    b, seq, d = x_prompt.shape
    bd = x_sample.shape[0]
    assert x_sample.shape[1] == 1 and DK_A == LANES and DV_A == LANES
    t = N_META + seq
    lead = (-N_META) % CHUNK
    t_pad = -(-t // ATT_BLOCK) * ATT_BLOCK
    qkv_w, va_w, wf = 2 * H_A * DK_A + H_A * DV_A, H_A * DV_A, H_F * HD_F
    sizes = (qkv_w, va_w, H_A, H_A, wf, wf, wf, H_F, d, d)
    offs = [0]
    for s_ in sizes:
        offs.append(offs[-1] + s_)
    col = lambda i: slice(offs[i], offs[i + 1])
    widths = (qkv_w, va_w, wf, wf, wf, d, d)

    xp = jnp.concatenate([jnp.broadcast_to(meta_tokens.astype(x_prompt.dtype)[None], (b, N_META, d)), x_prompt], axis=1)
    xs = x_sample
    new_p = [[] for _ in range(5)]
    new_s = [[] for _ in range(5)]
    for l in range(depth):
        wl = w_in[l]
        w_main = jnp.concatenate([wl[:, col(0)], wl[:, col(1)], wl[:, col(4)], wl[:, col(5)], wl[:, col(6)],
                                  wl[:, col(8)], wl[:, col(9)]], axis=1).astype(BF16)
        w_small = jnp.concatenate([wl[:, col(2)], wl[:, col(3)], wl[:, col(7)],
                                   jnp.zeros((d, LANES - 2 * H_A - H_F), F32)], axis=1).astype(BF16)
        g_row = g_mix[l].reshape(1, d)
        a_row = _lane_row(a_log[l], H_A)
        dt_row = _lane_row(dt_bias[l], H_A)
        bf_row = _lane_row(b_forget[l], 2 * H_A)
        qg_pair = jnp.tile(qn_g[l], LANES // HD_F).reshape(1, LANES)
        kg_pair = jnp.tile(kn_g[l], LANES // HD_F).reshape(1, LANES)
        qg_full = jnp.tile(qn_g[l], H_F).reshape(1, wf)
        kg_full = jnp.tile(kn_g[l], H_F).reshape(1, wf)
        na_row = norm_a[l].reshape(1, DV_A)
        pa_b, pf_b, wo_b = p_a[l].astype(BF16), p_f[l].astype(BF16), w_o[l].astype(BF16)
        gffn_row = g_ffn[l].reshape(1, d)
        wr = jnp.pad(w_router[l], ((0, 0), (0, LANES - N_EXP)))
        wr_hi = wr.astype(BF16)
        wr_lo = (wr - wr_hi.astype(F32)).astype(BF16)
        br_row = _lane_row(b_router[l], 0)
        merge_w = (na_row, pa_b, pf_b, wo_b, gffn_row, wr_hi, wr_lo, br_row)

        qkv_p, z_p, qf_p, kf_p, vf_p, ga_p, gf_p, sm_p = _in_proj(xp.reshape(b * t, d), g_row, w_main, w_small, widths)
        r3 = lambda a: a.reshape(b, t, a.shape[-1])
        qkv3, sm3 = r3(qkv_p), r3(sm_p)
        conv_p = qkv3[:, t - (CONV_W - 1):, :]
        prep = _conv_prep(qkv3, conv_w[l], jnp.zeros((b, CONV_W - 1, qkv_w), F32))
        oa_p, s_p = _delta_scan(prep, sm3, a_row, dt_row, jnp.zeros((b, H_A, DK_A, DV_A), F32), lead)
        lf_p, c_col = _logf_cumsum(sm3, bf_row, t_pad)
        qx, kx, vb, kn_p = _fox_prep(r3(qf_p), r3(kf_p), r3(vf_p), c_col, qg_pair, kg_pair, t_pad)
        of_p = _fox_attn(qx, kx, vb)
        x1_p, h2_p, tw_p, ti_p = _merge(xp, oa_p, r3(z_p), of_p, r3(ga_p), r3(gf_p), *merge_w)

        qkv_s, z_s, qf_s, kf_s, vf_s, ga_s, gf_s, sm_s = _in_proj(xs.reshape(bd, d), g_row, w_main, w_small, widths)
        s3 = lambda a: a.reshape(bd, 1, a.shape[-1])
        oa_s, s_s, conv_s = _decode_delta(s3(qkv_s), state_conv[l], s3(sm_s), conv_w[l], a_row, dt_row, state_delta[l])
        n_pool, page = cache_k.shape[1], cache_k.shape[2]
        of_s, kn_s, lf_s = _decode_attn(page_table, s3(qf_s), s3(kf_s), s3(vf_s), s3(sm_s), qg_full, kg_full,
                                        _lane_row(b_forget[l], 2 * H_A),
                                        cache_k[l].reshape(n_pool, page, wf), cache_v[l].reshape(n_pool, page, wf),
                                        jnp.swapaxes(cache_logf[l], 1, 2))
        g1 = lambda a: a.reshape(1, bd, a.shape[-1])
        x1_s, h2_s, tw_s, ti_s = _merge(g1(xs), g1(oa_s), g1(z_s), g1(of_s), g1(ga_s), g1(gf_s), *merge_w)

        n_p = b * t
        h2_all = jnp.concatenate([h2_p.reshape(n_p, d), h2_s.reshape(bd, d)], axis=0)
        tw_all = jnp.concatenate([tw_p.reshape(n_p, LANES), tw_s.reshape(bd, LANES)], axis=0)[:, :TOP_K]
        ti_all = jnp.concatenate([ti_p.reshape(n_p, LANES), ti_s.reshape(bd, LANES)], axis=0)[:, :TOP_K]
        y_all = _moe(h2_all, tw_all, ti_all, w_gate[l].astype(BF16), b_gate[l][:, None, :],
                     w_up[l].astype(BF16), b_up[l][:, None, :], w_down[l].astype(BF16), b_down[l][:, None, :])
        xp = x1_p + y_all[:n_p].reshape(b, t, d)
        xs = x1_s.reshape(bd, 1, d) + y_all[n_p:].reshape(bd, 1, d)

        st_p = (kn_p.reshape(b, t, H_F, HD_F), vf_p.reshape(b, t, H_F, HD_F),
                lf_p[:, :, 2 * H_A:2 * H_A + H_F], s_p, conv_p)
        st_s = (kn_s.reshape(bd, 1, H_F, HD_F), vf_s.reshape(bd, 1, H_F, HD_F),
                lf_s[:, :, 2 * H_A:2 * H_A + H_F], s_s, conv_s)
        for lst, a in zip(new_p, st_p):
            lst.append(a)
        for lst, a in zip(new_s, st_s):
            lst.append(a)
    k_p, v_p, lf_pp, d_p, c_p = (jnp.stack(a) for a in new_p)
    k_s, v_s, lf_ss, d_s, c_s = (jnp.stack(a) for a in new_s)
    return (xp[:, N_META:], xs, k_p, v_p, lf_pp, k_s, v_s, lf_ss, d_p, d_s, c_p, c_s)
```

```python
import functools

import jax
import jax.numpy as jnp
from jax import lax
from jax.experimental import pallas as pl
from jax.experimental.pallas import tpu as pltpu

F32 = jnp.float32
BF16 = jnp.bfloat16

N_META = 16
H_A = 4
DK_A = 128
DV_A = 128
CONV_W = 4
CHUNK = 64
H_F = 8
HD_F = 64
N_EXP = 32
TOP_K = 4
SWIGLU_LIMIT = 7.0
SWIGLU_ALPHA = 1.702
EPS = 1e-6
LOG2E = 1.4426950408889634

LANES = 128
F_OFF = 2 * H_A
HP = 16
ATT_BLOCK = 256
MOE_BLOCK = 256
PAGES_PER_STEP = 8
VMEM_LIMIT = 48 * 1024 * 1024


def _cparams(*sem):
    return pltpu.CompilerParams(dimension_semantics=sem, vmem_limit_bytes=VMEM_LIMIT)


def _row_tile(n, cap):
    if n <= cap:
        return n
    best = None
    for t in range(8, cap + 1, 8):
        if n % t == 0:
            best = t
    assert best is not None, n
    return best


def _dot(a, b):
    return jnp.dot(a, b, preferred_element_type=F32)


def _dot_nt(a, b):
    return lax.dot_general(a, b, (((1,), (1,)), ((), ())), preferred_element_type=F32)


def _dot_hi(a, b):
    return jnp.dot(a, b, preferred_element_type=F32, precision=lax.Precision.HIGHEST)


def _split2(x):
    hi = x.astype(BF16)
    lo = (x - hi.astype(F32)).astype(BF16)
    return hi, lo


def _split3(x):
    hi = x.astype(BF16)
    r = x - hi.astype(F32)
    mid = r.astype(BF16)
    lo = (r - mid.astype(F32)).astype(BF16)
    return hi, mid, lo


def _sigmoid(x):
    return 1.0 / (1.0 + jnp.exp(-x))


def _softplus(x):
    return jnp.maximum(x, 0.0) + jnp.log1p(jnp.exp(-jnp.abs(x)))


def _log_sigmoid(x):
    return jnp.minimum(x, 0.0) - jnp.log1p(jnp.exp(-jnp.abs(x)))


IN_WIDTHS = (2 * H_A * DK_A + H_A * DV_A, H_A * DV_A, H_F * HD_F, H_F * HD_F, H_F * HD_F)
COL_CHUNK = 512


def _in_proj_body(x_ref, g_ref, wm_ref, ws_ref, *out_refs, widths):
    x = x_ref[...]
    h = x * lax.rsqrt(jnp.mean(x * x, axis=-1, keepdims=True) + EPS) * g_ref[...]
    hb = h.astype(BF16)
    col = 0
    for o_ref, w in zip(out_refs[:-1], widths):
        for c0 in range(0, w, COL_CHUNK):
            o_ref[:, c0:c0 + COL_CHUNK] = _dot(hb, wm_ref[:, col + c0:col + c0 + COL_CHUNK])
        col += w
    out_refs[-1][...] = _dot(hb, ws_ref[...])


def _in_proj(x2d, g_row, w_main, w_small, widths):
    n, d = x2d.shape
    tm = _row_tile(n, 512)
    outs = [jax.ShapeDtypeStruct((n, w), F32) for w in widths] + [jax.ShapeDtypeStruct((n, LANES), F32)]
    out_specs = [pl.BlockSpec((tm, w), lambda i: (i, 0)) for w in widths] + [pl.BlockSpec((tm, LANES), lambda i: (i, 0))]
    return pl.pallas_call(
        functools.partial(_in_proj_body, widths=widths),
        grid=(n // tm,),
        in_specs=[pl.BlockSpec((tm, d), lambda i: (i, 0)),
                  pl.BlockSpec((1, d), lambda i: (0, 0)),
                  pl.BlockSpec(w_main.shape, lambda i: (0, 0), pipeline_mode=pl.Buffered(1)),
                  pl.BlockSpec(w_small.shape, lambda i: (0, 0), pipeline_mode=pl.Buffered(1))],
        out_specs=out_specs,
        out_shape=outs,
        compiler_params=_cparams("parallel"),
        name="in_proj",
    )(x2d, g_row, w_main, w_small)


def _conv_prep_body(x_ref, cw_ref, prev_ref, o_ref, buf, *, t_len, rows):
    j = pl.program_id(1)
    pad = 8
    buf[0:pad, :] = jnp.zeros((pad, LANES), F32)
    buf[pad - (CONV_W - 1):pad, :] = prev_ref[...]
    buf[pad:pad + t_len, :] = x_ref[...]
    cw = cw_ref[...]
    is_q = j < H_A
    is_qk = j < 2 * H_A
    for r0 in range(0, t_len, rows):
        y = jnp.zeros((rows, LANES), F32)
        for w in range(CONV_W):
            off = pad - (CONV_W - 1) + w + r0
            y = y + buf[off:off + rows, :] * cw[w:w + 1, :]
        y = y * _sigmoid(y)
        nrm = lax.rsqrt(jnp.sum(y * y, axis=-1, keepdims=True) + EPS)
        f = jnp.where(is_qk, nrm * jnp.where(is_q, DK_A ** -0.5, 1.0), 1.0)
        o_ref[r0:r0 + rows, :] = y * f


def _conv_prep(qkv3, conv_w, conv_prev):
    b, t, c = qkv3.shape
    rows = _row_tile(t, 512)
    return pl.pallas_call(
        functools.partial(_conv_prep_body, t_len=t, rows=rows),
        grid=(b, c // LANES),
        in_specs=[pl.BlockSpec((None, t, LANES), lambda i, j: (i, 0, j)),
                  pl.BlockSpec((CONV_W, LANES), lambda i, j: (0, j)),
                  pl.BlockSpec((None, CONV_W - 1, LANES), lambda i, j: (i, 0, j))],
        out_specs=pl.BlockSpec((None, t, LANES), lambda i, j: (i, 0, j)),
        out_shape=jax.ShapeDtypeStruct((b, t, c), F32),
        scratch_shapes=[pltpu.VMEM((8 + t, LANES), F32)],
        compiler_params=_cparams("parallel", "parallel"),
        name="conv_prep",
    )(qkv3, conv_w, conv_prev)


def _gates(sm, a_row, dt_row, h):
    lane = lax.broadcasted_iota(jnp.int32, sm.shape, 1)
    beta_all = _sigmoid(sm)
    g_all = -jnp.exp(a_row) * _softplus(sm + dt_row)
    beta = jnp.sum(jnp.where(lane == h, beta_all, 0.0), axis=-1, keepdims=True)
    g = jnp.sum(jnp.where(lane == H_A + h, g_all, 0.0), axis=-1, keepdims=True)
    return beta, g


def _delta_chunk(qs, ks, vs, betas, gs, s_ref):
    n = len(qs)
    rng = range(n)
    c = qs[0].shape[0]
    ri = lax.broadcasted_iota(jnp.int32, (c, c), 0)
    ci = lax.broadcasted_iota(jnp.int32, (c, c), 1)
    eye = ri == ci
    incl = ci <= ri
    strict = ci < ri
    g_row = [jnp.sum(jnp.where(eye, gs[i], 0.0), axis=0, keepdims=True) for i in rng]
    g_cum = [jnp.sum(jnp.where(incl, g_row[i], 0.0), axis=1, keepdims=True) for i in rng]
    g_cum_row = [jnp.sum(jnp.where(ri <= ci, gs[i], 0.0), axis=0, keepdims=True) for i in rng]
    decay = [jnp.exp(jnp.where(incl, g_cum[i] - g_cum_row[i], -jnp.inf)) for i in rng]
    e_g = [jnp.exp(g_cum[i]) for i in rng]
    s = [s_ref[i] for i in rng]
    sb = [s[i].astype(BF16) for i in rng]
    qb = [qs[i].astype(BF16) for i in rng]
    kb = [ks[i].astype(BF16) for i in rng]
    kk = [_dot_nt(kb[i], kb[i]) for i in rng]
    k_s = [_dot(kb[i], sb[i]) for i in rng]
    a = [jnp.where(strict, betas[i] * decay[i] * kk[i], 0.0) for i in rng]
    rhs = [betas[i] * (vs[i] - e_g[i] * k_s[i]) for i in rng]
    inv = [jnp.where(eye, 1.0, 0.0) - a[i] for i in rng]
    pw = a
    n_sq = max(1, (c - 1).bit_length() - 1)
    for _ in range(n_sq):
        pw = [_dot_hi(pw[i], pw[i]) for i in rng]
        inv = [inv[i] + _dot_hi(inv[i], pw[i]) for i in rng]
    qk = [_dot_nt(qb[i], kb[i]) for i in rng]
    q_s = [_dot(qb[i], sb[i]) for i in rng]
    db = [_dot_hi(inv[i], rhs[i]).astype(BF16) for i in rng]
    o = [e_g[i] * q_s[i] + _dot((qk[i] * decay[i]).astype(BF16), db[i]) for i in rng]
    g_end = [g_cum[i][c - 1:c, :] for i in rng]
    k_dec = [(ks[i] * jnp.exp(g_end[i] - g_cum[i])).T.astype(BF16) for i in rng]
    for i in rng:
        s_ref[i] = jnp.exp(g_end[i]) * s[i] + _dot(k_dec[i], db[i])
    return o


def _delta_scan_body(qkv_ref, sm_ref, a_ref, dt_ref, s0_ref, o_ref, s_out_ref, s_ref, *, t_len, lead):
    a_row = a_ref[...]
    dt_row = dt_ref[...]
    s_ref[...] = s0_ref[...]
    first = CHUNK - lead
    n_chunks = (lead + t_len) // CHUNK

    def heads(load, store, mask_lead):
        sm = load(sm_ref, 0)
        betas, gs = [], []
        for h in range(H_A):
            beta, g = _gates(sm, a_row, dt_row, h)
            if mask_lead:
                row = lax.broadcasted_iota(jnp.int32, (CHUNK, 1), 0)
                beta = jnp.where(row >= lead, beta, 0.0)
                g = jnp.where(row >= lead, g, 0.0)
            betas.append(beta)
            gs.append(g)
        qs = [load(qkv_ref, h * DK_A) for h in range(H_A)]
        ks = [load(qkv_ref, (H_A + h) * DK_A) for h in range(H_A)]
        vs = [load(qkv_ref, 2 * H_A * DK_A + h * DV_A) for h in range(H_A)]
        for h, o in enumerate(_delta_chunk(qs, ks, vs, betas, gs, s_ref)):
            store(h, o)

    if lead:
        def load0(ref, col):
            return jnp.concatenate([jnp.zeros((lead, LANES), F32), ref[0:first, col:col + LANES]], axis=0)

        def store0(h, o):
            o_ref[0:first, h * DV_A:(h + 1) * DV_A] = o[lead:, :]

        heads(load0, store0, True)
        c_start = 1
    else:
        c_start = 0

    def body(c, carry):
        r0 = pl.multiple_of(c * CHUNK - lead, 8)
        sl = pl.ds(r0, CHUNK)

        def store(h, o):
            o_ref[sl, h * DV_A:(h + 1) * DV_A] = o

        heads(lambda ref, col: ref[sl, col:col + LANES], store, False)
        return carry

    lax.fori_loop(c_start, n_chunks, body, 0)
    s_out_ref[...] = s_ref[...]


def _delta_scan(qkv3, small3, a_row, dt_row, s0, lead):
    b, t, c = qkv3.shape
    assert (lead + t) % CHUNK == 0 and lead % 8 == 0
    return pl.pallas_call(
        functools.partial(_delta_scan_body, t_len=t, lead=lead),
        grid=(b,),
        in_specs=[pl.BlockSpec((None, t, c), lambda i: (i, 0, 0)),
                  pl.BlockSpec((None, t, LANES), lambda i: (i, 0, 0)),
                  pl.BlockSpec((1, LANES), lambda i: (0, 0)),
                  pl.BlockSpec((1, LANES), lambda i: (0, 0)),
                  pl.BlockSpec((None, H_A, DK_A, DV_A), lambda i: (i, 0, 0, 0))],
        out_specs=[pl.BlockSpec((None, t, H_A * DV_A), lambda i: (i, 0, 0)),
                   pl.BlockSpec((None, H_A, DK_A, DV_A), lambda i: (i, 0, 0, 0))],
        out_shape=[jax.ShapeDtypeStruct((b, t, H_A * DV_A), F32),
                   jax.ShapeDtypeStruct((b, H_A, DK_A, DV_A), F32)],
        scratch_shapes=[pltpu.VMEM((H_A, DK_A, DV_A), F32)],
        compiler_params=_cparams("parallel"),
        name="delta_scan",
    )(qkv3, small3, a_row, dt_row, s0)


def _group_rms(x, gain, seg):
    hi, lo = _split2(x * x)
    ss = _dot(hi, seg) + _dot(lo, seg)
    return x * lax.rsqrt(ss * (1.0 / HD_F) + EPS) * gain


def _seg_matrix(n):
    r = lax.broadcasted_iota(jnp.int32, (n, n), 0) // HD_F
    c = lax.broadcasted_iota(jnp.int32, (n, n), 1) // HD_F
    return jnp.where(r == c, 1.0, 0.0).astype(BF16)


def _fox_prep_body(q_ref, k_ref, v_ref, c_ref, qg_ref, kg_ref, qx_ref, kx_ref, vb_ref, kn_ref, *, t_len, t_pad):
    p = pl.program_id(1)
    heads = LANES // HD_F
    seg = _seg_matrix(LANES)
    qn = _group_rms(q_ref[...], qg_ref[...], seg) * (HD_F ** -0.5 * LOG2E)
    kn = _group_rms(k_ref[...], kg_ref[...], seg)
    kn_ref[...] = kn
    c_all = c_ref[...] * LOG2E
    lane = lax.broadcasted_iota(jnp.int32, (t_len, LANES), 1)
    for hh in range(heads):
        c = jnp.sum(jnp.where(lane == F_OFF + p * heads + hh, c_all, 0.0), axis=-1, keepdims=True)
        pieces = [x.astype(F32) for x in _split3(c)]
        own = (lane // HD_F) == hh
        f0 = ((hh + 1) % heads) * HD_F
        qx = jnp.where(own, qn, 0.0)
        kx = jnp.where(own, kn, 0.0)
        for n, piece in enumerate(pieces):
            qx = jnp.where(lane == f0 + n, piece, qx)
            kx = jnp.where(lane == f0 + n, 1.0, kx)
            qx = jnp.where(lane == f0 + 3 + n, 1.0, qx)
            kx = jnp.where(lane == f0 + 3 + n, -piece, kx)
        qx_ref[0:t_len, hh * LANES:(hh + 1) * LANES] = qx.astype(BF16)
        kx_ref[0:t_len, hh * LANES:(hh + 1) * LANES] = kx.astype(BF16)
    vb_ref[0:t_len, :] = v_ref[...].astype(BF16)
    if t_pad > t_len:
        qx_ref[t_len:, :] = jnp.zeros((t_pad - t_len, heads * LANES), BF16)
        kx_ref[t_len:, :] = jnp.zeros((t_pad - t_len, heads * LANES), BF16)
        vb_ref[t_len:, :] = jnp.zeros((t_pad - t_len, LANES), BF16)


def _fox_prep(q3, k3, v3, c_col, qg_row, kg_row, t_pad):
    b, t, w = q3.shape
    heads = LANES // HD_F
    blk = pl.BlockSpec((None, t, LANES), lambda i, p: (i, 0, p))
    xblk = pl.BlockSpec((None, t_pad, heads * LANES), lambda i, p: (i, 0, p))
    gblk = pl.BlockSpec((1, LANES), lambda i, p: (0, 0))
    return pl.pallas_call(
        functools.partial(_fox_prep_body, t_len=t, t_pad=t_pad),
        grid=(b, w // LANES),
        in_specs=[blk, blk, blk, pl.BlockSpec((None, t, LANES), lambda i, p: (i, 0, 0)), gblk, gblk],
        out_specs=[xblk, xblk, pl.BlockSpec((None, t_pad, LANES), lambda i, p: (i, 0, p)), blk],
        out_shape=[jax.ShapeDtypeStruct((b, t_pad, H_F * LANES), BF16), jax.ShapeDtypeStruct((b, t_pad, H_F * LANES), BF16),
                   jax.ShapeDtypeStruct((b, t_pad, w), BF16), jax.ShapeDtypeStruct((b, t, w), F32)],
        compiler_params=_cparams("parallel", "parallel"),
        name="fox_prep",
    )(q3, k3, v3, c_col, qg_row, kg_row)


def _logf_cumsum_body(sm_ref, bf_ref, lf_ref, ccol_ref, buf, *, t_len, t_pad):
    lf = _log_sigmoid(sm_ref[...] + bf_ref[...])
    lf_ref[...] = lf
    buf[0:t_len, :] = lf
    if t_pad > t_len:
        buf[t_len:, :] = jnp.zeros((t_pad - t_len, LANES), F32)
    blk = ATT_BLOCK
    ri = lax.broadcasted_iota(jnp.int32, (blk, blk), 0)
    ci = lax.broadcasted_iota(jnp.int32, (blk, blk), 1)
    tri = jnp.where(ci <= ri, 1.0, 0.0).astype(BF16)
    carry = jnp.zeros((1, LANES), F32)
    for i in range(t_pad // blk):
        hi, mid, lo = _split3(buf[i * blk:(i + 1) * blk, :])
        c = _dot(tri, hi) + _dot(tri, mid) + _dot(tri, lo) + carry
        ccol_ref[i * blk:(i + 1) * blk, :] = c
        carry = c[blk - 1:blk, :]


def _logf_cumsum(small3, bf_row, t_pad):
    b, t, _ = small3.shape
    return pl.pallas_call(
        functools.partial(_logf_cumsum_body, t_len=t, t_pad=t_pad),
        grid=(b,),
        in_specs=[pl.BlockSpec((None, t, LANES), lambda i: (i, 0, 0)),
                  pl.BlockSpec((1, LANES), lambda i: (0, 0))],
        out_specs=[pl.BlockSpec((None, t, LANES), lambda i: (i, 0, 0)),
                   pl.BlockSpec((None, t_pad, LANES), lambda i: (i, 0, 0))],
        out_shape=[jax.ShapeDtypeStruct((b, t, LANES), F32),
                   jax.ShapeDtypeStruct((b, t_pad, LANES), F32)],
        scratch_shapes=[pltpu.VMEM((t_pad, LANES), F32)],
        compiler_params=_cparams("parallel"),
        name="logf_cumsum",
    )(small3, bf_row)


def _fox_attn_body(q_ref, k_ref, v_ref, o_ref):
    i = pl.program_id(1)
    blk = ATT_BLOCK
    heads = LANES // HD_F
    n_pairs = H_F // heads
    first_half = lax.broadcasted_iota(jnp.int32, (blk, LANES), 1) < HD_F
    on_or_below = (lax.broadcasted_iota(jnp.int32, (blk, blk), 1)
                   <= lax.broadcasted_iota(jnp.int32, (blk, blk), 0))

    def step(j, carry, diagonal):
        ms, ls, accs = carry
        k0 = pl.multiple_of(j * blk, blk)
        new_ms, new_ls, new_accs = [], [], []
        for p in range(n_pairs):
            v = v_ref[pl.ds(k0, blk), p * LANES:(p + 1) * LANES]
            pvs, alphas = [], []
            for hh in range(heads):
                h = p * heads + hh
                s = _dot_nt(q_ref[:, h * LANES:(h + 1) * LANES], k_ref[pl.ds(k0, blk), h * LANES:(h + 1) * LANES])
                if diagonal:
                    s = jnp.where(on_or_below, s, -jnp.inf)
                m_new = jnp.maximum(ms[h], jnp.max(s, axis=-1, keepdims=True))
                alpha = jnp.exp2(ms[h] - m_new)
                pr = jnp.exp2(s - m_new)
                new_ls.append(alpha * ls[h] + jnp.sum(pr, axis=-1, keepdims=True))
                new_ms.append(m_new)
                pvs.append(_dot(pr.astype(BF16), v))
                alphas.append(alpha)
            pv = jnp.where(first_half, pvs[0], pvs[1])
            al = jnp.where(first_half, alphas[0], alphas[1])
            new_accs.append(al * accs[p] + pv)
        return tuple(new_ms), tuple(new_ls), tuple(new_accs)

    init = (tuple(jnp.full((blk, 1), -jnp.inf, F32) for _ in range(H_F)),
            tuple(jnp.zeros((blk, 1), F32) for _ in range(H_F)),
            tuple(jnp.zeros((blk, LANES), F32) for _ in range(n_pairs)))
    carry = lax.fori_loop(0, i, lambda j, c: step(j, c, False), init)
    ms, ls, accs = step(i, carry, True)
    for p in range(n_pairs):
        inv = jnp.where(first_half, 1.0 / ls[p * heads], 1.0 / ls[p * heads + 1])
        o_ref[:, p * LANES:(p + 1) * LANES] = (accs[p] * inv).astype(o_ref.dtype)


def _fox_attn(qx, kx, vb):
    b, t_pad, w = vb.shape
    blk = ATT_BLOCK
    assert LANES // HD_F == 2
    return pl.pallas_call(
        _fox_attn_body,
        grid=(b, t_pad // blk),
        in_specs=[pl.BlockSpec((None, blk, H_F * LANES), lambda n, i: (n, i, 0)),
                  pl.BlockSpec((None, t_pad, H_F * LANES), lambda n, i: (n, 0, 0)),
                  pl.BlockSpec((None, t_pad, w), lambda n, i: (n, 0, 0))],
        out_specs=pl.BlockSpec((None, blk, w), lambda n, i: (n, i, 0)),
        out_shape=jax.ShapeDtypeStruct((b, t_pad, w), BF16),
        compiler_params=_cparams("parallel", "arbitrary"),
        name="fox_attn",
    )(qx, kx, vb)


def _decode_delta_body(x_ref, sc_ref, sm_ref, cw_ref, a_ref, dt_ref, s_ref, o_ref, s_out_ref, conv_ref):
    x_new = x_ref[...]
    sc = sc_ref[...]
    cw = cw_ref[...]
    y = x_new * cw[CONV_W - 1:CONV_W, :]
    for w in range(CONV_W - 1):
        y = y + sc[w:w + 1, :] * cw[w:w + 1, :]
    y = y * _sigmoid(y)
    conv_ref[0:CONV_W - 2, :] = sc[1:, :]
    conv_ref[CONV_W - 2:CONV_W - 1, :] = x_new
    sm = sm_ref[...]
    lane = lax.broadcasted_iota(jnp.int32, (1, LANES), 1)
    beta_all = _sigmoid(sm)
    g_all = -jnp.exp(a_ref[...]) * _softplus(sm + dt_ref[...])
    qk_w = H_A * DK_A
    row0 = lax.broadcasted_iota(jnp.int32, (LANES, LANES), 0) == 0

    def in_row0(r):
        return jnp.where(row0, jnp.broadcast_to(r, (LANES, LANES)), 0.0)

    for h in range(H_A):
        q = y[:, h * DK_A:(h + 1) * DK_A]
        k = y[:, qk_w + h * DK_A:qk_w + (h + 1) * DK_A]
        v = y[:, 2 * qk_w + h * DV_A:2 * qk_w + (h + 1) * DV_A]
        q = q * lax.rsqrt(jnp.sum(q * q, axis=-1, keepdims=True) + EPS) * (DK_A ** -0.5)
        k = k * lax.rsqrt(jnp.sum(k * k, axis=-1, keepdims=True) + EPS)
        beta = jnp.sum(jnp.where(lane == h, beta_all, 0.0), axis=-1, keepdims=True)
        g = jnp.sum(jnp.where(lane == H_A + h, g_all, 0.0), axis=-1, keepdims=True)
        e_g = jnp.exp(g)
        s = s_ref[h]
        sb = s.astype(BF16)
        k_sq = in_row0(k)
        k_s = _dot(k_sq.astype(BF16), sb)[0:1, :]
        q_s = _dot(in_row0(q).astype(BF16), sb)[0:1, :]
        delta = beta * (v - e_g * k_s)
        qk = jnp.sum(q * k, axis=-1, keepdims=True)
        o_ref[:, h * DV_A:(h + 1) * DV_A] = e_g * q_s + qk * delta
        s_out_ref[h] = e_g * s + _dot(k_sq.T.astype(BF16), in_row0(delta).astype(BF16))


def _decode_delta(qkv3, state_conv, small3, conv_w, a_row, dt_row, s0):
    bd, _, c = qkv3.shape
    return pl.pallas_call(
        _decode_delta_body,
        grid=(bd,),
        in_specs=[pl.BlockSpec((None, 1, c), lambda i: (i, 0, 0)),
                  pl.BlockSpec((None, CONV_W - 1, c), lambda i: (i, 0, 0)),
                  pl.BlockSpec((None, 1, LANES), lambda i: (i, 0, 0)),
                  pl.BlockSpec((CONV_W, c), lambda i: (0, 0)),
                  pl.BlockSpec((1, LANES), lambda i: (0, 0)),
                  pl.BlockSpec((1, LANES), lambda i: (0, 0)),
                  pl.BlockSpec((None, H_A, DK_A, DV_A), lambda i: (i, 0, 0, 0))],
        out_specs=[pl.BlockSpec((None, 1, H_A * DV_A), lambda i: (i, 0, 0)),
                   pl.BlockSpec((None, H_A, DK_A, DV_A), lambda i: (i, 0, 0, 0)),
                   pl.BlockSpec((None, CONV_W - 1, c), lambda i: (i, 0, 0))],
        out_shape=[jax.ShapeDtypeStruct((bd, 1, H_A * DV_A), F32),
                   jax.ShapeDtypeStruct((bd, H_A, DK_A, DV_A), F32),
                   jax.ShapeDtypeStruct((bd, CONV_W - 1, c), F32)],
        compiler_params=_cparams("parallel"),
        name="decode_delta",
    )(qkv3, state_conv, small3, conv_w, a_row, dt_row, s0)


def _decode_attn_body(pt_ref, q_ref, k_ref, v_ref, sm_ref, qg_ref, kg_ref, bf_ref, *rest, n_pg):
    page_refs = rest[:3 * n_pg]
    o_ref, kn_ref, lf_ref, qcol, m_ref, l_ref, acc_ref, carry_ref = rest[3 * n_pg:]
    step = pl.program_id(1)
    w = H_F * HD_F
    row0 = lax.broadcasted_iota(jnp.int32, (LANES, LANES), 0) == 0

    def as_columns(x_row):
        cols = []
        for c in range(w // LANES):
            sq = jnp.where(row0, jnp.broadcast_to(x_row[:, c * LANES:(c + 1) * LANES], (LANES, LANES)), 0.0)
            cols.append(jnp.broadcast_to(sq.T[:, 0:1], (LANES, LANES)))
        return jnp.concatenate(cols, axis=0)

    @pl.when(step == 0)
    def _():
        seg = _seg_matrix(w)
        row = lax.broadcasted_iota(jnp.int32, (HP, w), 0)
        head_of_lane = lax.broadcasted_iota(jnp.int32, (HP, w), 1) // HD_F

        def rms_rows(x_row, gain):
            xr = jnp.broadcast_to(x_row, (HP, w))
            hi, lo = _split2(xr * xr)
            ss = _dot(hi, seg) + _dot(lo, seg)
            return xr * lax.rsqrt(ss * (1.0 / HD_F) + EPS) * gain

        qn = rms_rows(q_ref[...], qg_ref[...]) * (HD_F ** -0.5)
        kn = rms_rows(k_ref[...], kg_ref[...])
        kn_ref[...] = kn[0:1, :]
        lf = _log_sigmoid(sm_ref[...] + bf_ref[...])
        lf_ref[...] = lf
        qcol[...] = as_columns(qn[0:1, :])
        q_m = jnp.where(head_of_lane == row, qn, 0.0)
        m_ref[...] = jnp.sum(q_m * kn, axis=-1, keepdims=True)[0:H_F, :]
        l_ref[...] = jnp.ones((H_F, 1), F32)
        lane = lax.broadcasted_iota(jnp.int32, (w, LANES), 1)
        acc_ref[...] = jnp.where(lane == 0, as_columns(v_ref[...]), 0.0)
        rr = lax.broadcasted_iota(jnp.int32, (H_F, LANES), 0)
        ll = lax.broadcasted_iota(jnp.int32, (H_F, LANES), 1)
        carry_ref[...] = jnp.sum(jnp.where(ll == rr + F_OFF, jnp.broadcast_to(lf, (H_F, LANES)), 0.0),
                                 axis=-1, keepdims=True)

    pos = lax.broadcasted_iota(jnp.int32, (H_F, LANES), 1)
    carry = carry_ref[...]
    scores = []
    for g in range(n_pg):
        kt_ref = page_refs[3 * g]
        lfp = page_refs[3 * g + 2][...]
        suf = lfp
        sh = 1
        while sh < LANES:
            rolled = pltpu.roll(suf, LANES - sh, axis=1)
            suf = suf + jnp.where(pos + sh < LANES, rolled, 0.0)
            sh *= 2
        rows = [jnp.sum(kt_ref[h * HD_F:(h + 1) * HD_F, :] * qcol[h * HD_F:(h + 1) * HD_F, :], axis=0, keepdims=True)
                for h in range(H_F)]
        scores.append(jnp.concatenate(rows, axis=0) + (carry + (suf - lfp)))
        carry = carry + jnp.sum(lfp, axis=-1, keepdims=True)
    carry_ref[...] = carry
    m_old = m_ref[...]
    m_new = m_old
    for s in scores:
        m_new = jnp.maximum(m_new, jnp.max(s, axis=-1, keepdims=True))
    alpha = jnp.exp(m_old - m_new)
    probs = [jnp.exp(s - m_new) for s in scores]
    l_new = alpha * l_ref[...]
    for pr in probs:
        l_new = l_new + jnp.sum(pr, axis=-1, keepdims=True)
    l_ref[...] = l_new
    m_ref[...] = m_new
    for h in range(H_F):
        hs = slice(h * HD_F, (h + 1) * HD_F)
        acc = alpha[h:h + 1, :] * acc_ref[hs, :]
        for g in range(n_pg):
            acc = acc + probs[g][h:h + 1, :] * page_refs[3 * g + 1][hs, :]
        acc_ref[hs, :] = acc

    @pl.when(step == pl.num_programs(1) - 1)
    def _():
        l_all = l_ref[...]
        lane = lax.broadcasted_iota(jnp.int32, (1, LANES), 1)
        heads = LANES // HD_F
        for c in range(w // LANES):
            tot = jnp.sum(acc_ref[c * LANES:(c + 1) * LANES, :].T, axis=0, keepdims=True)
            inv = jnp.where(lane < HD_F, 1.0 / l_all[c * heads:c * heads + 1, :], 1.0 / l_all[c * heads + 1:c * heads + 2, :])
            o_ref[:, c * LANES:(c + 1) * LANES] = tot * inv


def _decode_attn(page_table, q3, k3, v3, small3, qg_row, kg_row, bf_row, ck, cv, clf_t):
    bd, _, w = q3.shape
    n_pages = page_table.shape[1]
    page = ck.shape[2]
    assert page == LANES and n_pages % PAGES_PER_STEP == 0 and LANES // HD_F == 2
    n_pg = PAGES_PER_STEP
    steps = n_pages // n_pg

    def page_idx(g):
        return lambda i, s, pt: (pt[i, n_pages - 1 - (s * n_pg + g)], 0, 0)

    row_spec = lambda width: pl.BlockSpec((None, 1, width), lambda i, s, pt: (i, 0, 0))
    const_spec = lambda width: pl.BlockSpec((1, width), lambda i, s, pt: (0, 0))
    in_specs = [row_spec(w), row_spec(w), row_spec(w), row_spec(LANES), const_spec(w), const_spec(w), const_spec(LANES)]
    args = [q3, k3, v3, small3, qg_row, kg_row, bf_row]
    for g in range(n_pg):
        in_specs += [pl.BlockSpec((None, w, page), page_idx(g)),
                     pl.BlockSpec((None, w, page), page_idx(g)),
                     pl.BlockSpec((None, H_F, page), page_idx(g))]
        args += [ck, cv, clf_t]
    grid_spec = pltpu.PrefetchScalarGridSpec(
        num_scalar_prefetch=1,
        grid=(bd, steps),
        in_specs=in_specs,
        out_specs=[row_spec(w), row_spec(w), row_spec(LANES)],
        scratch_shapes=[pltpu.VMEM((w, LANES), F32), pltpu.VMEM((H_F, 1), F32), pltpu.VMEM((H_F, 1), F32),
                        pltpu.VMEM((w, LANES), F32), pltpu.VMEM((H_F, 1), F32)],
    )
    return pl.pallas_call(
        functools.partial(_decode_attn_body, n_pg=n_pg),
        grid_spec=grid_spec,
        out_shape=[jax.ShapeDtypeStruct((bd, 1, w), F32), jax.ShapeDtypeStruct((bd, 1, w), F32),
                   jax.ShapeDtypeStruct((bd, 1, LANES), F32)],
        compiler_params=_cparams("parallel", "arbitrary"),
        name="decode_attn",
    )(page_table, *args)


def _merge_body(x_ref, oa_ref, z_ref, of_ref, ga_ref, gf_ref, na_ref, pa_ref, pf_ref, wo_ref, gffn_ref,
                wr_hi_ref, wr_lo_ref, br_ref, x1_ref, h2_ref, tw_ref, ti_ref):
    o = oa_ref[...]
    z = z_ref[...]
    na = na_ref[...]
    parts = []
    for h in range(H_A):
        oh = o[:, h * DV_A:(h + 1) * DV_A]
        zh = z[:, h * DV_A:(h + 1) * DV_A]
        on = oh * lax.rsqrt(jnp.mean(oh * oh, axis=-1, keepdims=True) + EPS) * na
        parts.append((on * (zh * _sigmoid(zh))).astype(BF16))
    o_a = jnp.concatenate(parts, axis=-1)
    ya = _dot(o_a, pa_ref[...])
    yf = _dot(of_ref[...].astype(BF16), pf_ref[...])
    mixed = _sigmoid(ga_ref[...]) * ya + _sigmoid(gf_ref[...]) * yf
    x1 = x_ref[...] + _dot(mixed.astype(BF16), wo_ref[...])
    x1_ref[...] = x1
    h2 = x1 * lax.rsqrt(jnp.mean(x1 * x1, axis=-1, keepdims=True) + EPS) * gffn_ref[...]
    h2_ref[...] = h2
    h_hi, h_lo = _split2(h2)
    logits = _dot(h_hi, wr_hi_ref[...]) + _dot(h_hi, wr_lo_ref[...]) + _dot(h_lo, wr_hi_ref[...]) + br_ref[...]
    lane = lax.broadcasted_iota(jnp.int32, logits.shape, 1)
    l = jnp.where(lane < N_EXP, logits, -jnp.inf)
    vals, idxs = [], []
    for _ in range(TOP_K):
        m = jnp.max(l, axis=-1, keepdims=True)
        idx = jnp.min(jnp.where(l == m, lane, LANES), axis=-1, keepdims=True)
        vals.append(m)
        idxs.append(idx)
        l = jnp.where(lane == idx, -jnp.inf, l)
    es = [jnp.exp(v - vals[0]) for v in vals]
    den = es[0]
    for e in es[1:]:
        den = den + e
    tw = jnp.zeros(logits.shape, F32)
    ti = jnp.zeros(logits.shape, jnp.int32)
    for kk in range(TOP_K):
        tw = jnp.where(lane == kk, es[kk] / den, tw)
        ti = jnp.where(lane == kk, idxs[kk], ti)
    tw_ref[...] = tw
    ti_ref[...] = ti


def _merge(x3, oa3, z3, of3, ga3, gf3, na_row, pa, pf, wo, gffn_row, wr_hi, wr_lo, br_row):
    g, t, d = x3.shape
    tm = _row_tile(t, 384)
    tok = lambda c: pl.BlockSpec((None, tm, c), lambda i, j: (i, j, 0))
    const = lambda a: pl.BlockSpec(a.shape, lambda i, j: (0,) * a.ndim, pipeline_mode=pl.Buffered(1))
    wa = H_A * DV_A
    wf = H_F * HD_F
    return pl.pallas_call(
        _merge_body,
        grid=(g, t // tm),
        in_specs=[tok(d), tok(wa), tok(wa), tok(wf), tok(d), tok(d),
                  const(na_row), const(pa), const(pf), const(wo), const(gffn_row),
                  const(wr_hi), const(wr_lo), const(br_row)],
        out_specs=[tok(d), tok(d), tok(LANES), tok(LANES)],
        out_shape=[jax.ShapeDtypeStruct((g, t, d), F32), jax.ShapeDtypeStruct((g, t, d), F32),
                   jax.ShapeDtypeStruct((g, t, LANES), F32), jax.ShapeDtypeStruct((g, t, LANES), jnp.int32)],
        compiler_params=_cparams("parallel", "parallel"),
        name="merge",
    )(x3, oa3, z3, of3, ga3, gf3, na_row, pa, pf, wo, gffn_row, wr_hi, wr_lo, br_row)


def _moe_body(be_ref, nv_ref, x_ref, wg_ref, bg_ref, wu_ref, bu_ref, wd_ref, bd_ref, y_ref):
    i = pl.program_id(0)

    @pl.when(i < nv_ref[0])
    def _():
        x = x_ref[...].astype(BF16)
        gate = jnp.minimum(_dot(x, wg_ref[...]) + bg_ref[...], SWIGLU_LIMIT)
        up = jnp.clip(_dot(x, wu_ref[...]) + bu_ref[...], -SWIGLU_LIMIT, SWIGLU_LIMIT)
        act = (up + 1.0) * gate * _sigmoid(SWIGLU_ALPHA * gate)
        y_ref[...] = _dot(act.astype(BF16), wd_ref[...]) + bd_ref[...]

    @pl.when(i >= nv_ref[0])
    def _():
        y_ref[...] = jnp.zeros(y_ref.shape, y_ref.dtype)


def _moe_experts(blk_e, n_valid, xs, wg, bg, wu, bu, wd, bd):
    n_rows, d = xs.shape
    bm = MOE_BLOCK
    n_blk = n_rows // bm
    de = wg.shape[-1]
    row_idx = lambda i, be, nv: (jnp.minimum(i, nv[0] - 1), 0)
    wspec = lambda a, b: pl.BlockSpec((None, a, b), lambda i, be, nv: (be[i], 0, 0))
    grid_spec = pltpu.PrefetchScalarGridSpec(
        num_scalar_prefetch=2,
        grid=(n_blk,),
        in_specs=[pl.BlockSpec((bm, d), row_idx),
                  wspec(d, de), wspec(1, de), wspec(d, de), wspec(1, de), wspec(de, d), wspec(1, d)],
        out_specs=pl.BlockSpec((bm, d), lambda i, be, nv: (i, 0)),
    )
    return pl.pallas_call(
        _moe_body,
        grid_spec=grid_spec,
        out_shape=jax.ShapeDtypeStruct((n_rows, d), F32),
        compiler_params=_cparams("arbitrary"),
        name="moe_experts",
    )(blk_e, n_valid, xs, wg, bg, wu, bu, wd, bd)


def _moe(h2, top_w, top_i, wg, bg, wu, bu, wd, bd):
    n, d = h2.shape
    bm = MOE_BLOCK
    m = n * TOP_K
    n_blk = -(-(m + N_EXP * (bm - 1)) // bm)
    flat_e = top_i.reshape(-1)
    onehot = (flat_e[:, None] == jnp.arange(N_EXP, dtype=jnp.int32)[None, :]).astype(jnp.int32)
    csum = jnp.cumsum(onehot, axis=0)
    rank = jnp.take_along_axis(csum, flat_e[:, None], axis=1)[:, 0] - 1
    counts = csum[-1]
    padded = (counts + bm - 1) // bm * bm
    pend = jnp.cumsum(padded)
    pstart = pend - padded
    dest = pstart[flat_e] + rank
    n_valid = (pend[-1] // bm).astype(jnp.int32)
    blk_first = jnp.minimum(jnp.arange(n_blk, dtype=jnp.int32), n_valid - 1) * bm
    blk_e = jnp.minimum(jnp.sum((pend[None, :] <= blk_first[:, None]).astype(jnp.int32), axis=1), N_EXP - 1)
    tok = jnp.arange(m, dtype=jnp.int32) // TOP_K
    row_tok = jnp.full((n_blk * bm,), n, jnp.int32).at[dest].set(tok, unique_indices=True)
    h_ext = jnp.concatenate([h2, jnp.zeros((1, d), h2.dtype)], axis=0)
    xs = h_ext[row_tok]
    y = _moe_experts(blk_e, n_valid.reshape(1), xs, wg, bg, wu, bu, wd, bd)
    yg = y[dest.reshape(n, TOP_K).T.reshape(-1)].reshape(TOP_K, n, d)
    return jnp.sum(yg * top_w.T[:, :, None], axis=0)


def _lane_row(vals, offset, width=LANES):
    return jnp.zeros((1, width), F32).at[0, offset:offset + vals.shape[0]].set(vals.astype(F32))


def kernel(x_prompt, x_sample, cache_k, cache_v, cache_logf, state_delta, state_conv, page_table,
           meta_tokens, g_mix, w_in, conv_w, a_log, dt_bias, norm_a, qn_g, kn_g, b_forget,
           p_a, p_f, w_o, g_ffn, w_router, b_router, w_gate, b_gate, w_up, b_up, w_down, b_down):
    depth = w_in.shape[0]
    b, seq, d = x_prompt.shape
    bd = x_sample.shape[0]
    assert x_sample.shape[1] == 1 and DK_A == LANES and DV_A == LANES
    t = N_META + seq
    lead = (-N_META) % CHUNK
    t_pad = -(-t // ATT_BLOCK) * ATT_BLOCK
    qkv_w, va_w, wf = 2 * H_A * DK_A + H_A * DV_A, H_A * DV_A, H_F * HD_F
    sizes = (qkv_w, va_w, H_A, H_A, wf, wf, wf, H_F, d, d)
    offs = [0]
    for s_ in sizes:
        offs.append(offs[-1] + s_)
    col = lambda i: slice(offs[i], offs[i + 1])
    widths = (qkv_w, va_w, wf, wf, wf, d, d)

    xp = jnp.concatenate([jnp.broadcast_to(meta_tokens.astype(x_prompt.dtype)[None], (b, N_META, d)), x_prompt], axis=1)
    xs = x_sample
    new_p = [[] for _ in range(5)]
    new_s = [[] for _ in range(5)]
    for l in range(depth):
        wl = w_in[l]
        w_main = jnp.concatenate([wl[:, col(0)], wl[:, col(1)], wl[:, col(4)], wl[:, col(5)], wl[:, col(6)],
                                  wl[:, col(8)], wl[:, col(9)]], axis=1).astype(BF16)
        w_small = jnp.concatenate([wl[:, col(2)], wl[:, col(3)], wl[:, col(7)],
                                   jnp.zeros((d, LANES - 2 * H_A - H_F), F32)], axis=1).astype(BF16)
        g_row = g_mix[l].reshape(1, d)
        a_row = _lane_row(a_log[l], H_A)
        dt_row = _lane_row(dt_bias[l], H_A)
        bf_row = _lane_row(b_forget[l], 2 * H_A)
        qg_pair = jnp.tile(qn_g[l], LANES // HD_F).reshape(1, LANES)
        kg_pair = jnp.tile(kn_g[l], LANES // HD_F).reshape(1, LANES)
        qg_full = jnp.tile(qn_g[l], H_F).reshape(1, wf)
        kg_full = jnp.tile(kn_g[l], H_F).reshape(1, wf)
        na_row = norm_a[l].reshape(1, DV_A)
        pa_b, pf_b, wo_b = p_a[l].astype(BF16), p_f[l].astype(BF16), w_o[l].astype(BF16)
        gffn_row = g_ffn[l].reshape(1, d)
        wr = jnp.pad(w_router[l], ((0, 0), (0, LANES - N_EXP)))
        wr_hi = wr.astype(BF16)
        wr_lo = (wr - wr_hi.astype(F32)).astype(BF16)
        br_row = _lane_row(b_router[l], 0)
        merge_w = (na_row, pa_b, pf_b, wo_b, gffn_row, wr_hi, wr_lo, br_row)

        qkv_p, z_p, qf_p, kf_p, vf_p, ga_p, gf_p, sm_p = _in_proj(xp.reshape(b * t, d), g_row, w_main, w_small, widths)
        r3 = lambda a: a.reshape(b, t, a.shape[-1])
        qkv3, sm3 = r3(qkv_p), r3(sm_p)
        conv_p = qkv3[:, t - (CONV_W - 1):, :]
        prep = _conv_prep(qkv3, conv_w[l], jnp.zeros((b, CONV_W - 1, qkv_w), F32))
        oa_p, s_p = _delta_scan(prep, sm3, a_row, dt_row, jnp.zeros((b, H_A, DK_A, DV_A), F32), lead)
        lf_p, c_col = _logf_cumsum(sm3, bf_row, t_pad)
        qx, kx, vb, kn_p = _fox_prep(r3(qf_p), r3(kf_p), r3(vf_p), c_col, qg_pair, kg_pair, t_pad)
        of_p = _fox_attn(qx, kx, vb)
        x1_p, h2_p, tw_p, ti_p = _merge(xp, oa_p, r3(z_p), of_p, r3(ga_p), r3(gf_p), *merge_w)

        qkv_s, z_s, qf_s, kf_s, vf_s, ga_s, gf_s, sm_s = _in_proj(xs.reshape(bd, d), g_row, w_main, w_small, widths)
        s3 = lambda a: a.reshape(bd, 1, a.shape[-1])
        oa_s, s_s, conv_s = _decode_delta(s3(qkv_s), state_conv[l], s3(sm_s), conv_w[l], a_row, dt_row, state_delta[l])
        n_pool, page = cache_k.shape[1], cache_k.shape[2]
        of_s, kn_s, lf_s = _decode_attn(page_table, s3(qf_s), s3(kf_s), s3(vf_s), s3(sm_s), qg_full, kg_full,
                                        _lane_row(b_forget[l], 2 * H_A),
                                        jnp.transpose(cache_k[l], (0, 2, 3, 1)).reshape(n_pool, wf, page),
                                        jnp.transpose(cache_v[l], (0, 2, 3, 1)).reshape(n_pool, wf, page),
                                        jnp.swapaxes(cache_logf[l], 1, 2))
        g1 = lambda a: a.reshape(1, bd, a.shape[-1])
        x1_s, h2_s, tw_s, ti_s = _merge(g1(xs), g1(oa_s), g1(z_s), g1(of_s), g1(ga_s), g1(gf_s), *merge_w)

        n_p = b * t
        h2_all = jnp.concatenate([h2_p.reshape(n_p, d), h2_s.reshape(bd, d)], axis=0)
        tw_all = jnp.concatenate([tw_p.reshape(n_p, LANES), tw_s.reshape(bd, LANES)], axis=0)[:, :TOP_K]
        ti_all = jnp.concatenate([ti_p.reshape(n_p, LANES), ti_s.reshape(bd, LANES)], axis=0)[:, :TOP_K]
        y_all = _moe(h2_all, tw_all, ti_all, w_gate[l].astype(BF16), b_gate[l][:, None, :],
                     w_up[l].astype(BF16), b_up[l][:, None, :], w_down[l].astype(BF16), b_down[l][:, None, :])
        xp = x1_p + y_all[:n_p].reshape(b, t, d)
        xs = x1_s.reshape(bd, 1, d) + y_all[n_p:].reshape(bd, 1, d)

        st_p = (kn_p.reshape(b, t, H_F, HD_F), vf_p.reshape(b, t, H_F, HD_F),
                lf_p[:, :, 2 * H_A:2 * H_A + H_F], s_p, conv_p)
        st_s = (kn_s.reshape(bd, 1, H_F, HD_F), vf_s.reshape(bd, 1, H_F, HD_F),
                lf_s[:, :, 2 * H_A:2 * H_A + H_F], s_s, conv_s)
        for lst, a in zip(new_p, st_p):
            lst.append(a)
        for lst, a in zip(new_s, st_s):
            lst.append(a)
    k_p, v_p, lf_pp, d_p, c_p = (jnp.stack(a) for a in new_p)
    k_s, v_s, lf_ss, d_s, c_s = (jnp.stack(a) for a in new_s)
    return (xp[:, N_META:], xs, k_p, v_p, lf_pp, k_s, v_s, lf_ss, d_p, d_s, c_p, c_s)
```

```python
import functools

import jax
import jax.numpy as jnp
from jax import lax
from jax.experimental import pallas as pl
from jax.experimental.pallas import tpu as pltpu

F32 = jnp.float32
BF16 = jnp.bfloat16

N_META = 16
H_A = 4
DK_A = 128
DV_A = 128
CONV_W = 4
CHUNK = 64
H_F = 8
HD_F = 64
N_EXP = 32
TOP_K = 4
SWIGLU_LIMIT = 7.0
SWIGLU_ALPHA = 1.702
EPS = 1e-6
LOG2E = 1.4426950408889634

LANES = 128
F_OFF = 2 * H_A
HP = 16
ATT_BLOCK = 256
MOE_BLOCK = 256
PAGES_PER_STEP = 8
VMEM_LIMIT = 48 * 1024 * 1024


def _cparams(*sem):
    return pltpu.CompilerParams(dimension_semantics=sem, vmem_limit_bytes=VMEM_LIMIT)


def _row_tile(n, cap):
    if n <= cap:
        return n
    best = None
    for t in range(8, cap + 1, 8):
        if n % t == 0:
            best = t
    assert best is not None, n
    return best


def _dot(a, b):
    return jnp.dot(a, b, preferred_element_type=F32)


def _dot_nt(a, b):
    return lax.dot_general(a, b, (((1,), (1,)), ((), ())), preferred_element_type=F32)


def _dot_hi(a, b):
    return jnp.dot(a, b, preferred_element_type=F32, precision=lax.Precision.HIGHEST)


def _split2(x):
    hi = x.astype(BF16)
    lo = (x - hi.astype(F32)).astype(BF16)
    return hi, lo


def _split3(x):
    hi = x.astype(BF16)
    r = x - hi.astype(F32)
    mid = r.astype(BF16)
    lo = (r - mid.astype(F32)).astype(BF16)
    return hi, mid, lo


def _sigmoid(x):
    return 1.0 / (1.0 + jnp.exp(-x))


def _softplus(x):
    return jnp.maximum(x, 0.0) + jnp.log1p(jnp.exp(-jnp.abs(x)))


def _log_sigmoid(x):
    return jnp.minimum(x, 0.0) - jnp.log1p(jnp.exp(-jnp.abs(x)))


IN_WIDTHS = (2 * H_A * DK_A + H_A * DV_A, H_A * DV_A, H_F * HD_F, H_F * HD_F, H_F * HD_F)
COL_CHUNK = 512


def _in_proj_body(x_ref, g_ref, wm_ref, ws_ref, *out_refs, widths):
    x = x_ref[...]
    h = x * lax.rsqrt(jnp.mean(x * x, axis=-1, keepdims=True) + EPS) * g_ref[...]
    hb = h.astype(BF16)
    col = 0
    for o_ref, w in zip(out_refs[:-1], widths):
        for c0 in range(0, w, COL_CHUNK):
            o_ref[:, c0:c0 + COL_CHUNK] = _dot(hb, wm_ref[:, col + c0:col + c0 + COL_CHUNK])
        col += w
    out_refs[-1][...] = _dot(hb, ws_ref[...])


def _in_proj(x2d, g_row, w_main, w_small, widths):
    n, d = x2d.shape
    tm = _row_tile(n, 512)
    outs = [jax.ShapeDtypeStruct((n, w), F32) for w in widths] + [jax.ShapeDtypeStruct((n, LANES), F32)]
    out_specs = [pl.BlockSpec((tm, w), lambda i: (i, 0)) for w in widths] + [pl.BlockSpec((tm, LANES), lambda i: (i, 0))]
    return pl.pallas_call(
        functools.partial(_in_proj_body, widths=widths),
        grid=(n // tm,),
        in_specs=[pl.BlockSpec((tm, d), lambda i: (i, 0)),
                  pl.BlockSpec((1, d), lambda i: (0, 0)),
                  pl.BlockSpec(w_main.shape, lambda i: (0, 0), pipeline_mode=pl.Buffered(1)),
                  pl.BlockSpec(w_small.shape, lambda i: (0, 0), pipeline_mode=pl.Buffered(1))],
        out_specs=out_specs,
        out_shape=outs,
        compiler_params=_cparams("parallel"),
        name="in_proj",
    )(x2d, g_row, w_main, w_small)


def _conv_prep_body(x_ref, cw_ref, prev_ref, o_ref, buf, *, t_len, rows):
    j = pl.program_id(1)
    pad = 8
    buf[0:pad, :] = jnp.zeros((pad, LANES), F32)
    buf[pad - (CONV_W - 1):pad, :] = prev_ref[...]
    buf[pad:pad + t_len, :] = x_ref[...]
    cw = cw_ref[...]
    is_q = j < H_A
    is_qk = j < 2 * H_A
    for r0 in range(0, t_len, rows):
        y = jnp.zeros((rows, LANES), F32)
        for w in range(CONV_W):
            off = pad - (CONV_W - 1) + w + r0
            y = y + buf[off:off + rows, :] * cw[w:w + 1, :]
        y = y * _sigmoid(y)
        nrm = lax.rsqrt(jnp.sum(y * y, axis=-1, keepdims=True) + EPS)
        f = jnp.where(is_qk, nrm * jnp.where(is_q, DK_A ** -0.5, 1.0), 1.0)
        o_ref[r0:r0 + rows, :] = y * f


def _conv_prep(qkv3, conv_w, conv_prev):
    b, t, c = qkv3.shape
    rows = _row_tile(t, 512)
    return pl.pallas_call(
        functools.partial(_conv_prep_body, t_len=t, rows=rows),
        grid=(b, c // LANES),
        in_specs=[pl.BlockSpec((None, t, LANES), lambda i, j: (i, 0, j)),
                  pl.BlockSpec((CONV_W, LANES), lambda i, j: (0, j)),
                  pl.BlockSpec((None, CONV_W - 1, LANES), lambda i, j: (i, 0, j))],
        out_specs=pl.BlockSpec((None, t, LANES), lambda i, j: (i, 0, j)),
        out_shape=jax.ShapeDtypeStruct((b, t, c), F32),
        scratch_shapes=[pltpu.VMEM((8 + t, LANES), F32)],
        compiler_params=_cparams("parallel", "parallel"),
        name="conv_prep",
    )(qkv3, conv_w, conv_prev)


def _gates(sm, a_row, dt_row, h):
    lane = lax.broadcasted_iota(jnp.int32, sm.shape, 1)
    beta_all = _sigmoid(sm)
    g_all = -jnp.exp(a_row) * _softplus(sm + dt_row)
    beta = jnp.sum(jnp.where(lane == h, beta_all, 0.0), axis=-1, keepdims=True)
    g = jnp.sum(jnp.where(lane == H_A + h, g_all, 0.0), axis=-1, keepdims=True)
    return beta, g


def _delta_chunk(qs, ks, vs, betas, gs, s_ref):
    n = len(qs)
    rng = range(n)
    c = qs[0].shape[0]
    ri = lax.broadcasted_iota(jnp.int32, (c, c), 0)
    ci = lax.broadcasted_iota(jnp.int32, (c, c), 1)
    eye = ri == ci
    incl = ci <= ri
    strict = ci < ri
    g_row = [jnp.sum(jnp.where(eye, gs[i], 0.0), axis=0, keepdims=True) for i in rng]
    g_cum = [jnp.sum(jnp.where(incl, g_row[i], 0.0), axis=1, keepdims=True) for i in rng]
    g_cum_row = [jnp.sum(jnp.where(ri <= ci, gs[i], 0.0), axis=0, keepdims=True) for i in rng]
    decay = [jnp.exp(jnp.where(incl, g_cum[i] - g_cum_row[i], -jnp.inf)) for i in rng]
    e_g = [jnp.exp(g_cum[i]) for i in rng]
    s = [s_ref[i] for i in rng]
    sb = [s[i].astype(BF16) for i in rng]
    qb = [qs[i].astype(BF16) for i in rng]
    kb = [ks[i].astype(BF16) for i in rng]
    kk = [_dot_nt(kb[i], kb[i]) for i in rng]
    k_s = [_dot(kb[i], sb[i]) for i in rng]
    a = [jnp.where(strict, betas[i] * decay[i] * kk[i], 0.0) for i in rng]
    rhs = [betas[i] * (vs[i] - e_g[i] * k_s[i]) for i in rng]
    inv = [jnp.where(eye, 1.0, 0.0) - a[i] for i in rng]
    pw = a
    n_sq = max(1, (c - 1).bit_length() - 1)
    for _ in range(n_sq):
        pw = [_dot_hi(pw[i], pw[i]) for i in rng]
        inv = [inv[i] + _dot_hi(inv[i], pw[i]) for i in rng]
    qk = [_dot_nt(qb[i], kb[i]) for i in rng]
    q_s = [_dot(qb[i], sb[i]) for i in rng]
    db = [_dot_hi(inv[i], rhs[i]).astype(BF16) for i in rng]
    o = [e_g[i] * q_s[i] + _dot((qk[i] * decay[i]).astype(BF16), db[i]) for i in rng]
    g_end = [g_cum[i][c - 1:c, :] for i in rng]
    k_dec = [(ks[i] * jnp.exp(g_end[i] - g_cum[i])).T.astype(BF16) for i in rng]
    for i in rng:
        s_ref[i] = jnp.exp(g_end[i]) * s[i] + _dot(k_dec[i], db[i])
    return o


def _delta_scan_body(qkv_ref, sm_ref, a_ref, dt_ref, s0_ref, o_ref, s_out_ref, s_ref, *, t_len, lead):
    a_row = a_ref[...]
    dt_row = dt_ref[...]
    s_ref[...] = s0_ref[...]
    first = CHUNK - lead
    n_chunks = (lead + t_len) // CHUNK

    def heads(load, store, mask_lead):
        sm = load(sm_ref, 0)
        betas, gs = [], []
        for h in range(H_A):
            beta, g = _gates(sm, a_row, dt_row, h)
            if mask_lead:
                row = lax.broadcasted_iota(jnp.int32, (CHUNK, 1), 0)
                beta = jnp.where(row >= lead, beta, 0.0)
                g = jnp.where(row >= lead, g, 0.0)
            betas.append(beta)
            gs.append(g)
        qs = [load(qkv_ref, h * DK_A) for h in range(H_A)]
        ks = [load(qkv_ref, (H_A + h) * DK_A) for h in range(H_A)]
        vs = [load(qkv_ref, 2 * H_A * DK_A + h * DV_A) for h in range(H_A)]
        for h, o in enumerate(_delta_chunk(qs, ks, vs, betas, gs, s_ref)):
            store(h, o)

    if lead:
        def load0(ref, col):
            return jnp.concatenate([jnp.zeros((lead, LANES), F32), ref[0:first, col:col + LANES]], axis=0)

        def store0(h, o):
            o_ref[0:first, h * DV_A:(h + 1) * DV_A] = o[lead:, :]

        heads(load0, store0, True)
        c_start = 1
    else:
        c_start = 0

    def body(c, carry):
        r0 = pl.multiple_of(c * CHUNK - lead, 8)
        sl = pl.ds(r0, CHUNK)

        def store(h, o):
            o_ref[sl, h * DV_A:(h + 1) * DV_A] = o

        heads(lambda ref, col: ref[sl, col:col + LANES], store, False)
        return carry

    lax.fori_loop(c_start, n_chunks, body, 0)
    s_out_ref[...] = s_ref[...]


def _delta_scan(qkv3, small3, a_row, dt_row, s0, lead):
    b, t, c = qkv3.shape
    assert (lead + t) % CHUNK == 0 and lead % 8 == 0
    return pl.pallas_call(
        functools.partial(_delta_scan_body, t_len=t, lead=lead),
        grid=(b,),
        in_specs=[pl.BlockSpec((None, t, c), lambda i: (i, 0, 0)),
                  pl.BlockSpec((None, t, LANES), lambda i: (i, 0, 0)),
                  pl.BlockSpec((1, LANES), lambda i: (0, 0)),
                  pl.BlockSpec((1, LANES), lambda i: (0, 0)),
                  pl.BlockSpec((None, H_A, DK_A, DV_A), lambda i: (i, 0, 0, 0))],
        out_specs=[pl.BlockSpec((None, t, H_A * DV_A), lambda i: (i, 0, 0)),
                   pl.BlockSpec((None, H_A, DK_A, DV_A), lambda i: (i, 0, 0, 0))],
        out_shape=[jax.ShapeDtypeStruct((b, t, H_A * DV_A), F32),
                   jax.ShapeDtypeStruct((b, H_A, DK_A, DV_A), F32)],
        scratch_shapes=[pltpu.VMEM((H_A, DK_A, DV_A), F32)],
        compiler_params=_cparams("parallel"),
        name="delta_scan",
    )(qkv3, small3, a_row, dt_row, s0)


def _group_rms(x, gain, seg):
    hi, lo = _split2(x * x)
    ss = _dot(hi, seg) + _dot(lo, seg)
    return x * lax.rsqrt(ss * (1.0 / HD_F) + EPS) * gain


def _seg_matrix(n):
    r = lax.broadcasted_iota(jnp.int32, (n, n), 0) // HD_F
    c = lax.broadcasted_iota(jnp.int32, (n, n), 1) // HD_F
    return jnp.where(r == c, 1.0, 0.0).astype(BF16)


def _fox_prep_body(q_ref, k_ref, v_ref, c_ref, qg_ref, kg_ref, qx_ref, kx_ref, vt_ref, kn_ref, *, t_len, t_pad):
    p = pl.program_id(1)
    heads = LANES // HD_F

    def feature_major(x):
        if t_pad > t_len:
            x = jnp.concatenate([x, jnp.zeros((t_pad - t_len, LANES), F32)], axis=0)
        return x.T.astype(BF16)

    seg = _seg_matrix(LANES)
    qn = _group_rms(q_ref[...], qg_ref[...], seg) * (HD_F ** -0.5 * LOG2E)
    kn = _group_rms(k_ref[...], kg_ref[...], seg)
    kn_ref[...] = kn
    c_all = c_ref[...] * LOG2E
    lane = lax.broadcasted_iota(jnp.int32, (t_len, LANES), 1)
    for hh in range(heads):
        c = jnp.sum(jnp.where(lane == F_OFF + p * heads + hh, c_all, 0.0), axis=-1, keepdims=True)
        pieces = [x.astype(F32) for x in _split3(c)]
        own = (lane // HD_F) == hh
        f0 = ((hh + 1) % heads) * HD_F
        qx = jnp.where(own, qn, 0.0)
        kx = jnp.where(own, kn, 0.0)
        for n, piece in enumerate(pieces):
            qx = jnp.where(lane == f0 + n, piece, qx)
            kx = jnp.where(lane == f0 + n, 1.0, kx)
            qx = jnp.where(lane == f0 + 3 + n, 1.0, qx)
            kx = jnp.where(lane == f0 + 3 + n, -piece, kx)
        kx_ref[0:t_len, hh * LANES:(hh + 1) * LANES] = kx.astype(BF16)
        qx_ref[hh * LANES:(hh + 1) * LANES, :] = feature_major(qx)
    vt_ref[...] = feature_major(v_ref[...])
    if t_pad > t_len:
        kx_ref[t_len:, :] = jnp.zeros((t_pad - t_len, heads * LANES), BF16)


def _fox_prep(q3, k3, v3, c_col, qg_row, kg_row, t_pad):
    b, t, w = q3.shape
    heads = LANES // HD_F
    blk = pl.BlockSpec((None, t, LANES), lambda i, p: (i, 0, p))
    gblk = pl.BlockSpec((1, LANES), lambda i, p: (0, 0))
    return pl.pallas_call(
        functools.partial(_fox_prep_body, t_len=t, t_pad=t_pad),
        grid=(b, w // LANES),
        in_specs=[blk, blk, blk, pl.BlockSpec((None, t, LANES), lambda i, p: (i, 0, 0)), gblk, gblk],
        out_specs=[pl.BlockSpec((None, heads * LANES, t_pad), lambda i, p: (i, p, 0)),
                   pl.BlockSpec((None, t_pad, heads * LANES), lambda i, p: (i, 0, p)),
                   pl.BlockSpec((None, LANES, t_pad), lambda i, p: (i, p, 0)), blk],
        out_shape=[jax.ShapeDtypeStruct((b, H_F * LANES, t_pad), BF16), jax.ShapeDtypeStruct((b, t_pad, H_F * LANES), BF16),
                   jax.ShapeDtypeStruct((b, w, t_pad), BF16), jax.ShapeDtypeStruct((b, t, w), F32)],
        compiler_params=_cparams("parallel", "parallel"),
        name="fox_prep",
    )(q3, k3, v3, c_col, qg_row, kg_row)


def _logf_cumsum_body(sm_ref, bf_ref, lf_ref, ccol_ref, buf, *, t_len, t_pad):
    lf = _log_sigmoid(sm_ref[...] + bf_ref[...])
    lf_ref[...] = lf
    buf[0:t_len, :] = lf
    if t_pad > t_len:
        buf[t_len:, :] = jnp.zeros((t_pad - t_len, LANES), F32)
    blk = ATT_BLOCK
    ri = lax.broadcasted_iota(jnp.int32, (blk, blk), 0)
    ci = lax.broadcasted_iota(jnp.int32, (blk, blk), 1)
    tri = jnp.where(ci <= ri, 1.0, 0.0).astype(BF16)
    carry = jnp.zeros((1, LANES), F32)
    for i in range(t_pad // blk):
        hi, mid, lo = _split3(buf[i * blk:(i + 1) * blk, :])
        c = _dot(tri, hi) + _dot(tri, mid) + _dot(tri, lo) + carry
        ccol_ref[i * blk:(i + 1) * blk, :] = c
        carry = c[blk - 1:blk, :]


def _logf_cumsum(small3, bf_row, t_pad):
    b, t, _ = small3.shape
    return pl.pallas_call(
        functools.partial(_logf_cumsum_body, t_len=t, t_pad=t_pad),
        grid=(b,),
        in_specs=[pl.BlockSpec((None, t, LANES), lambda i: (i, 0, 0)),
                  pl.BlockSpec((1, LANES), lambda i: (0, 0))],
        out_specs=[pl.BlockSpec((None, t, LANES), lambda i: (i, 0, 0)),
                   pl.BlockSpec((None, t_pad, LANES), lambda i: (i, 0, 0))],
        out_shape=[jax.ShapeDtypeStruct((b, t, LANES), F32),
                   jax.ShapeDtypeStruct((b, t_pad, LANES), F32)],
        scratch_shapes=[pltpu.VMEM((t_pad, LANES), F32)],
        compiler_params=_cparams("parallel"),
        name="logf_cumsum",
    )(small3, bf_row)


def _fox_attn_body(qt_ref, k_ref, vt_ref, o_ref):
    i = pl.program_id(1)
    blk = ATT_BLOCK
    key_le_query = (lax.broadcasted_iota(jnp.int32, (blk, blk), 0)
                    <= lax.broadcasted_iota(jnp.int32, (blk, blk), 1))

    def step(j, carry, diagonal):
        ms, ls, accs = carry
        k0 = pl.multiple_of(j * blk, blk)
        st = []
        for h in range(H_F):
            s = _dot(k_ref[pl.ds(k0, blk), h * LANES:(h + 1) * LANES], qt_ref[h * LANES:(h + 1) * LANES, :])
            st.append(jnp.where(key_le_query, s, -jnp.inf) if diagonal else s)
        new_ms, new_ls, alphas, pts = [], [], [], []
        for h in range(H_F):
            m_new = jnp.maximum(ms[h], jnp.max(st[h], axis=0, keepdims=True))
            alpha = jnp.exp2(ms[h] - m_new)
            pt = jnp.exp2(st[h] - m_new)
            new_ls.append(alpha * ls[h] + jnp.sum(pt, axis=0, keepdims=True))
            new_ms.append(m_new)
            alphas.append(alpha)
            pts.append(pt.astype(BF16))
        new_accs = [alphas[h] * accs[h] + _dot(vt_ref[h * HD_F:(h + 1) * HD_F, pl.ds(k0, blk)], pts[h])
                    for h in range(H_F)]
        return tuple(new_ms), tuple(new_ls), tuple(new_accs)

    init = (tuple(jnp.full((1, blk), -jnp.inf, F32) for _ in range(H_F)),
            tuple(jnp.zeros((1, blk), F32) for _ in range(H_F)),
            tuple(jnp.zeros((HD_F, blk), F32) for _ in range(H_F)))
    carry = lax.fori_loop(0, i, lambda j, c: step(j, c, False), init)
    ms, ls, accs = step(i, carry, True)
    heads = LANES // HD_F
    for p in range(H_F // heads):
        ot = jnp.concatenate([accs[p * heads + hh] * (1.0 / ls[p * heads + hh]) for hh in range(heads)], axis=0)
        o_ref[:, p * LANES:(p + 1) * LANES] = ot.T.astype(o_ref.dtype)


def _fox_attn(qt, kx, vt):
    b, w, t_pad = vt.shape
    blk = ATT_BLOCK
    return pl.pallas_call(
        _fox_attn_body,
        grid=(b, t_pad // blk),
        in_specs=[pl.BlockSpec((None, H_F * LANES, blk), lambda n, i: (n, 0, i)),
                  pl.BlockSpec((None, t_pad, H_F * LANES), lambda n, i: (n, 0, 0)),
                  pl.BlockSpec((None, w, t_pad), lambda n, i: (n, 0, 0))],
        out_specs=pl.BlockSpec((None, blk, w), lambda n, i: (n, i, 0)),
        out_shape=jax.ShapeDtypeStruct((b, t_pad, w), BF16),
        compiler_params=_cparams("parallel", "arbitrary"),
        name="fox_attn",
    )(qt, kx, vt)


def _decode_delta_body(x_ref, sc_ref, sm_ref, cw_ref, a_ref, dt_ref, s_ref, o_ref, s_out_ref, conv_ref):
    x_new = x_ref[...]
    sc = sc_ref[...]
    cw = cw_ref[...]
    y = x_new * cw[CONV_W - 1:CONV_W, :]
    for w in range(CONV_W - 1):
        y = y + sc[w:w + 1, :] * cw[w:w + 1, :]
    y = y * _sigmoid(y)
    conv_ref[0:CONV_W - 2, :] = sc[1:, :]
    conv_ref[CONV_W - 2:CONV_W - 1, :] = x_new
    sm = sm_ref[...]
    lane = lax.broadcasted_iota(jnp.int32, (1, LANES), 1)
    beta_all = _sigmoid(sm)
    g_all = -jnp.exp(a_ref[...]) * _softplus(sm + dt_ref[...])
    qk_w = H_A * DK_A
    row0 = lax.broadcasted_iota(jnp.int32, (LANES, LANES), 0) == 0

    def in_row0(r):
        return jnp.where(row0, jnp.broadcast_to(r, (LANES, LANES)), 0.0)

    for h in range(H_A):
        q = y[:, h * DK_A:(h + 1) * DK_A]
        k = y[:, qk_w + h * DK_A:qk_w + (h + 1) * DK_A]
        v = y[:, 2 * qk_w + h * DV_A:2 * qk_w + (h + 1) * DV_A]
        q = q * lax.rsqrt(jnp.sum(q * q, axis=-1, keepdims=True) + EPS) * (DK_A ** -0.5)
        k = k * lax.rsqrt(jnp.sum(k * k, axis=-1, keepdims=True) + EPS)
        beta = jnp.sum(jnp.where(lane == h, beta_all, 0.0), axis=-1, keepdims=True)
        g = jnp.sum(jnp.where(lane == H_A + h, g_all, 0.0), axis=-1, keepdims=True)
        e_g = jnp.exp(g)
        s = s_ref[h]
        sb = s.astype(BF16)
        k_sq = in_row0(k)
        k_s = _dot(k_sq.astype(BF16), sb)[0:1, :]
        q_s = _dot(in_row0(q).astype(BF16), sb)[0:1, :]
        delta = beta * (v - e_g * k_s)
        qk = jnp.sum(q * k, axis=-1, keepdims=True)
        o_ref[:, h * DV_A:(h + 1) * DV_A] = e_g * q_s + qk * delta
        s_out_ref[h] = e_g * s + _dot(k_sq.T.astype(BF16), in_row0(delta).astype(BF16))


def _decode_delta(qkv3, state_conv, small3, conv_w, a_row, dt_row, s0):
    bd, _, c = qkv3.shape
    return pl.pallas_call(
        _decode_delta_body,
        grid=(bd,),
        in_specs=[pl.BlockSpec((None, 1, c), lambda i: (i, 0, 0)),
                  pl.BlockSpec((None, CONV_W - 1, c), lambda i: (i, 0, 0)),
                  pl.BlockSpec((None, 1, LANES), lambda i: (i, 0, 0)),
                  pl.BlockSpec((CONV_W, c), lambda i: (0, 0)),
                  pl.BlockSpec((1, LANES), lambda i: (0, 0)),
                  pl.BlockSpec((1, LANES), lambda i: (0, 0)),
                  pl.BlockSpec((None, H_A, DK_A, DV_A), lambda i: (i, 0, 0, 0))],
        out_specs=[pl.BlockSpec((None, 1, H_A * DV_A), lambda i: (i, 0, 0)),
                   pl.BlockSpec((None, H_A, DK_A, DV_A), lambda i: (i, 0, 0, 0)),
                   pl.BlockSpec((None, CONV_W - 1, c), lambda i: (i, 0, 0))],
        out_shape=[jax.ShapeDtypeStruct((bd, 1, H_A * DV_A), F32),
                   jax.ShapeDtypeStruct((bd, H_A, DK_A, DV_A), F32),
                   jax.ShapeDtypeStruct((bd, CONV_W - 1, c), F32)],
        compiler_params=_cparams("parallel"),
        name="decode_delta",
    )(qkv3, state_conv, small3, conv_w, a_row, dt_row, s0)


def _decode_attn_body(pt_ref, q_ref, k_ref, v_ref, sm_ref, qg_ref, kg_ref, bf_ref, *rest, n_pg):
    page_refs = rest[:3 * n_pg]
    o_ref, kn_ref, lf_ref, qcol, m_ref, l_ref, acc_ref, carry_ref = rest[3 * n_pg:]
    step = pl.program_id(1)
    w = H_F * HD_F
    row0 = lax.broadcasted_iota(jnp.int32, (LANES, LANES), 0) == 0

    def as_columns(x_row):
        cols = []
        for c in range(w // LANES):
            sq = jnp.where(row0, jnp.broadcast_to(x_row[:, c * LANES:(c + 1) * LANES], (LANES, LANES)), 0.0)
            cols.append(jnp.broadcast_to(sq.T[:, 0:1], (LANES, LANES)))
        return jnp.concatenate(cols, axis=0)

    @pl.when(step == 0)
    def _():
        seg = _seg_matrix(w)
        row = lax.broadcasted_iota(jnp.int32, (HP, w), 0)
        head_of_lane = lax.broadcasted_iota(jnp.int32, (HP, w), 1) // HD_F

        def rms_rows(x_row, gain):
            xr = jnp.broadcast_to(x_row, (HP, w))
            hi, lo = _split2(xr * xr)
            ss = _dot(hi, seg) + _dot(lo, seg)
            return xr * lax.rsqrt(ss * (1.0 / HD_F) + EPS) * gain

        qn = rms_rows(q_ref[...], qg_ref[...]) * (HD_F ** -0.5)
        kn = rms_rows(k_ref[...], kg_ref[...])
        kn_ref[...] = kn[0:1, :]
        lf = _log_sigmoid(sm_ref[...] + bf_ref[...])
        lf_ref[...] = lf
        qcol[...] = as_columns(qn[0:1, :])
        q_m = jnp.where(head_of_lane == row, qn, 0.0)
        m_ref[...] = jnp.sum(q_m * kn, axis=-1, keepdims=True)[0:H_F, :]
        l_ref[...] = jnp.ones((H_F, 1), F32)
        lane = lax.broadcasted_iota(jnp.int32, (w, LANES), 1)
        acc_ref[...] = jnp.where(lane == 0, as_columns(v_ref[...]), 0.0)
        rr = lax.broadcasted_iota(jnp.int32, (H_F, LANES), 0)
        ll = lax.broadcasted_iota(jnp.int32, (H_F, LANES), 1)
        carry_ref[...] = jnp.sum(jnp.where(ll == rr + F_OFF, jnp.broadcast_to(lf, (H_F, LANES)), 0.0),
                                 axis=-1, keepdims=True)

    pos = lax.broadcasted_iota(jnp.int32, (H_F, LANES), 1)
    carry = carry_ref[...]
    scores = []
    for g in range(n_pg):
        kt_ref = page_refs[3 * g]
        lfp = page_refs[3 * g + 2][...]
        suf = lfp
        sh = 1
        while sh < LANES:
            rolled = pltpu.roll(suf, LANES - sh, axis=1)
            suf = suf + jnp.where(pos + sh < LANES, rolled, 0.0)
            sh *= 2
        rows = [jnp.sum(kt_ref[h * HD_F:(h + 1) * HD_F, :] * qcol[h * HD_F:(h + 1) * HD_F, :], axis=0, keepdims=True)
                for h in range(H_F)]
        scores.append(jnp.concatenate(rows, axis=0) + (carry + (suf - lfp)))
        carry = carry + jnp.sum(lfp, axis=-1, keepdims=True)
    carry_ref[...] = carry
    m_old = m_ref[...]
    m_new = m_old
    for s in scores:
        m_new = jnp.maximum(m_new, jnp.max(s, axis=-1, keepdims=True))
    alpha = jnp.exp(m_old - m_new)
    probs = [jnp.exp(s - m_new) for s in scores]
    l_new = alpha * l_ref[...]
    for pr in probs:
        l_new = l_new + jnp.sum(pr, axis=-1, keepdims=True)
    l_ref[...] = l_new
    m_ref[...] = m_new
    for h in range(H_F):
        hs = slice(h * HD_F, (h + 1) * HD_F)
        acc = alpha[h:h + 1, :] * acc_ref[hs, :]
        for g in range(n_pg):
            acc = acc + probs[g][h:h + 1, :] * page_refs[3 * g + 1][hs, :]
        acc_ref[hs, :] = acc

    @pl.when(step == pl.num_programs(1) - 1)
    def _():
        l_all = l_ref[...]
        lane = lax.broadcasted_iota(jnp.int32, (1, LANES), 1)
        heads = LANES // HD_F
        for c in range(w // LANES):
            tot = jnp.sum(acc_ref[c * LANES:(c + 1) * LANES, :].T, axis=0, keepdims=True)
            inv = jnp.where(lane < HD_F, 1.0 / l_all[c * heads:c * heads + 1, :], 1.0 / l_all[c * heads + 1:c * heads + 2, :])
            o_ref[:, c * LANES:(c + 1) * LANES] = tot * inv


def _decode_attn(page_table, q3, k3, v3, small3, qg_row, kg_row, bf_row, ck, cv, clf_t):
    bd, _, w = q3.shape
    n_pages = page_table.shape[1]
    page = ck.shape[2]
    assert page == LANES and n_pages % PAGES_PER_STEP == 0 and LANES // HD_F == 2
    n_pg = PAGES_PER_STEP
    steps = n_pages // n_pg

    def page_idx(g):
        return lambda i, s, pt: (pt[i, n_pages - 1 - (s * n_pg + g)], 0, 0)

    row_spec = lambda width: pl.BlockSpec((None, 1, width), lambda i, s, pt: (i, 0, 0))
    const_spec = lambda width: pl.BlockSpec((1, width), lambda i, s, pt: (0, 0))
    in_specs = [row_spec(w), row_spec(w), row_spec(w), row_spec(LANES), const_spec(w), const_spec(w), const_spec(LANES)]
    args = [q3, k3, v3, small3, qg_row, kg_row, bf_row]
    for g in range(n_pg):
        in_specs += [pl.BlockSpec((None, w, page), page_idx(g)),
                     pl.BlockSpec((None, w, page), page_idx(g)),
                     pl.BlockSpec((None, H_F, page), page_idx(g))]
        args += [ck, cv, clf_t]
    grid_spec = pltpu.PrefetchScalarGridSpec(
        num_scalar_prefetch=1,
        grid=(bd, steps),
        in_specs=in_specs,
        out_specs=[row_spec(w), row_spec(w), row_spec(LANES)],
        scratch_shapes=[pltpu.VMEM((w, LANES), F32), pltpu.VMEM((H_F, 1), F32), pltpu.VMEM((H_F, 1), F32),
                        pltpu.VMEM((w, LANES), F32), pltpu.VMEM((H_F, 1), F32)],
    )
    return pl.pallas_call(
        functools.partial(_decode_attn_body, n_pg=n_pg),
        grid_spec=grid_spec,
        out_shape=[jax.ShapeDtypeStruct((bd, 1, w), F32), jax.ShapeDtypeStruct((bd, 1, w), F32),
                   jax.ShapeDtypeStruct((bd, 1, LANES), F32)],
        compiler_params=_cparams("parallel", "arbitrary"),
        name="decode_attn",
    )(page_table, *args)


def _merge_body(x_ref, oa_ref, z_ref, of_ref, ga_ref, gf_ref, na_ref, pa_ref, pf_ref, wo_ref, gffn_ref,
                wr_hi_ref, wr_lo_ref, br_ref, x1_ref, h2_ref, tw_ref, ti_ref):
    o = oa_ref[...]
    z = z_ref[...]
    na = na_ref[...]
    parts = []
    for h in range(H_A):
        oh = o[:, h * DV_A:(h + 1) * DV_A]
        zh = z[:, h * DV_A:(h + 1) * DV_A]
        on = oh * lax.rsqrt(jnp.mean(oh * oh, axis=-1, keepdims=True) + EPS) * na
        parts.append((on * (zh * _sigmoid(zh))).astype(BF16))
    o_a = jnp.concatenate(parts, axis=-1)
    ya = _dot(o_a, pa_ref[...])
    yf = _dot(of_ref[...].astype(BF16), pf_ref[...])
    mixed = _sigmoid(ga_ref[...]) * ya + _sigmoid(gf_ref[...]) * yf
    x1 = x_ref[...] + _dot(mixed.astype(BF16), wo_ref[...])
    x1_ref[...] = x1
    h2 = x1 * lax.rsqrt(jnp.mean(x1 * x1, axis=-1, keepdims=True) + EPS) * gffn_ref[...]
    h2_ref[...] = h2
    h_hi, h_lo = _split2(h2)
    logits = _dot(h_hi, wr_hi_ref[...]) + _dot(h_hi, wr_lo_ref[...]) + _dot(h_lo, wr_hi_ref[...]) + br_ref[...]
    lane = lax.broadcasted_iota(jnp.int32, logits.shape, 1)
    l = jnp.where(lane < N_EXP, logits, -jnp.inf)
    vals, idxs = [], []
    for _ in range(TOP_K):
        m = jnp.max(l, axis=-1, keepdims=True)
        idx = jnp.min(jnp.where(l == m, lane, LANES), axis=-1, keepdims=True)
        vals.append(m)
        idxs.append(idx)
        l = jnp.where(lane == idx, -jnp.inf, l)
    es = [jnp.exp(v - vals[0]) for v in vals]
    den = es[0]
    for e in es[1:]:
        den = den + e
    tw = jnp.zeros(logits.shape, F32)
    ti = jnp.zeros(logits.shape, jnp.int32)
    for kk in range(TOP_K):
        tw = jnp.where(lane == kk, es[kk] / den, tw)
        ti = jnp.where(lane == kk, idxs[kk], ti)
    tw_ref[...] = tw
    ti_ref[...] = ti


def _merge(x3, oa3, z3, of3, ga3, gf3, na_row, pa, pf, wo, gffn_row, wr_hi, wr_lo, br_row):
    g, t, d = x3.shape
    tm = _row_tile(t, 384)
    tok = lambda c: pl.BlockSpec((None, tm, c), lambda i, j: (i, j, 0))
    const = lambda a: pl.BlockSpec(a.shape, lambda i, j: (0,) * a.ndim, pipeline_mode=pl.Buffered(1))
    wa = H_A * DV_A
    wf = H_F * HD_F
    return pl.pallas_call(
        _merge_body,
        grid=(g, t // tm),
        in_specs=[tok(d), tok(wa), tok(wa), tok(wf), tok(d), tok(d),
                  const(na_row), const(pa), const(pf), const(wo), const(gffn_row),
                  const(wr_hi), const(wr_lo), const(br_row)],
        out_specs=[tok(d), tok(d), tok(LANES), tok(LANES)],
        out_shape=[jax.ShapeDtypeStruct((g, t, d), F32), jax.ShapeDtypeStruct((g, t, d), F32),
                   jax.ShapeDtypeStruct((g, t, LANES), F32), jax.ShapeDtypeStruct((g, t, LANES), jnp.int32)],
        compiler_params=_cparams("parallel", "parallel"),
        name="merge",
    )(x3, oa3, z3, of3, ga3, gf3, na_row, pa, pf, wo, gffn_row, wr_hi, wr_lo, br_row)


def _moe_body(be_ref, nv_ref, x_ref, wg_ref, bg_ref, wu_ref, bu_ref, wd_ref, bd_ref, y_ref, wg_b, wu_b, wd_b):
    i = pl.program_id(0)

    @pl.when(jnp.logical_or(i == 0, be_ref[i] != be_ref[jnp.maximum(i - 1, 0)]))
    def _():
        wg_b[...] = wg_ref[...].astype(BF16)
        wu_b[...] = wu_ref[...].astype(BF16)
        wd_b[...] = wd_ref[...].astype(BF16)

    @pl.when(i < nv_ref[0])
    def _():
        x = x_ref[...].astype(BF16)
        gate = jnp.minimum(_dot(x, wg_b[...]) + bg_ref[...], SWIGLU_LIMIT)
        up = jnp.clip(_dot(x, wu_b[...]) + bu_ref[...], -SWIGLU_LIMIT, SWIGLU_LIMIT)
        act = (up + 1.0) * gate * _sigmoid(SWIGLU_ALPHA * gate)
        y_ref[...] = _dot(act.astype(BF16), wd_b[...]) + bd_ref[...]

    @pl.when(i >= nv_ref[0])
    def _():
        y_ref[...] = jnp.zeros(y_ref.shape, y_ref.dtype)


def _moe_experts(blk_e, n_valid, xs, wg, bg, wu, bu, wd, bd):
    n_rows, d = xs.shape
    bm = MOE_BLOCK
    n_blk = n_rows // bm
    de = wg.shape[-1]
    row_idx = lambda i, be, nv: (jnp.minimum(i, nv[0] - 1), 0)
    wspec = lambda a, b: pl.BlockSpec((None, a, b), lambda i, be, nv: (be[i], 0, 0))
    grid_spec = pltpu.PrefetchScalarGridSpec(
        num_scalar_prefetch=2,
        grid=(n_blk,),
        in_specs=[pl.BlockSpec((bm, d), row_idx),
                  wspec(d, de), wspec(1, de), wspec(d, de), wspec(1, de), wspec(de, d), wspec(1, d)],
        out_specs=pl.BlockSpec((bm, d), lambda i, be, nv: (i, 0)),
        scratch_shapes=[pltpu.VMEM((d, de), BF16), pltpu.VMEM((d, de), BF16), pltpu.VMEM((de, d), BF16)],
    )
    return pl.pallas_call(
        _moe_body,
        grid_spec=grid_spec,
        out_shape=jax.ShapeDtypeStruct((n_rows, d), F32),
        compiler_params=_cparams("arbitrary"),
        name="moe_experts",
    )(blk_e, n_valid, xs, wg, bg, wu, bu, wd, bd)


def _moe(h2, top_w, top_i, wg, bg, wu, bu, wd, bd):
    n, d = h2.shape
    bm = MOE_BLOCK
    m = n * TOP_K
    n_blk = -(-(m + N_EXP * (bm - 1)) // bm)
    flat_e = top_i.reshape(-1)
    onehot = (flat_e[:, None] == jnp.arange(N_EXP, dtype=jnp.int32)[None, :]).astype(jnp.int32)
    csum = jnp.cumsum(onehot, axis=0)
    rank = jnp.take_along_axis(csum, flat_e[:, None], axis=1)[:, 0] - 1
    counts = csum[-1]
    padded = (counts + bm - 1) // bm * bm
    pend = jnp.cumsum(padded)
    pstart = pend - padded
    dest = pstart[flat_e] + rank
    n_valid = (pend[-1] // bm).astype(jnp.int32)
    blk_first = jnp.minimum(jnp.arange(n_blk, dtype=jnp.int32), n_valid - 1) * bm
    blk_e = jnp.minimum(jnp.sum((pend[None, :] <= blk_first[:, None]).astype(jnp.int32), axis=1), N_EXP - 1)
    tok = jnp.arange(m, dtype=jnp.int32) // TOP_K
    row_tok = jnp.full((n_blk * bm,), n, jnp.int32).at[dest].set(tok, unique_indices=True)
    h_ext = jnp.concatenate([h2, jnp.zeros((1, d), h2.dtype)], axis=0)
    xs = h_ext[row_tok]
    y = _moe_experts(blk_e, n_valid.reshape(1), xs, wg, bg, wu, bu, wd, bd)
    yg = y[dest.reshape(n, TOP_K).T.reshape(-1)].reshape(TOP_K, n, d)
    return jnp.sum(yg * top_w.T[:, :, None], axis=0)


def _lane_row(vals, offset, width=LANES):
    return jnp.zeros((1, width), F32).at[0, offset:offset + vals.shape[0]].set(vals.astype(F32))


def kernel(x_prompt, x_sample, cache_k, cache_v, cache_logf, state_delta, state_conv, page_table,
           meta_tokens, g_mix, w_in, conv_w, a_log, dt_bias, norm_a, qn_g, kn_g, b_forget,
           p_a, p_f, w_o, g_ffn, w_router, b_router, w_gate, b_gate, w_up, b_up, w_down, b_down):
    depth = w_in.shape[0]
    b, seq, d = x_prompt.shape
    bd = x_sample.shape[0]
    assert x_sample.shape[1] == 1 and DK_A == LANES and DV_A == LANES
    t = N_META + seq
    lead = (-N_META) % CHUNK
    t_pad = -(-t // ATT_BLOCK) * ATT_BLOCK
    qkv_w, va_w, wf = 2 * H_A * DK_A + H_A * DV_A, H_A * DV_A, H_F * HD_F
    sizes = (qkv_w, va_w, H_A, H_A, wf, wf, wf, H_F, d, d)
    offs = [0]
    for s_ in sizes:
        offs.append(offs[-1] + s_)
    col = lambda i: slice(offs[i], offs[i + 1])
    widths = (qkv_w, va_w, wf, wf, wf, d, d)

    xp = jnp.concatenate([jnp.broadcast_to(meta_tokens.astype(x_prompt.dtype)[None], (b, N_META, d)), x_prompt], axis=1)
    xs = x_sample
    new_p = [[] for _ in range(5)]
    new_s = [[] for _ in range(5)]
    for l in range(depth):
        wl = w_in[l]
        w_main = jnp.concatenate([wl[:, col(0)], wl[:, col(1)], wl[:, col(4)], wl[:, col(5)], wl[:, col(6)],
                                  wl[:, col(8)], wl[:, col(9)]], axis=1).astype(BF16)
        w_small = jnp.concatenate([wl[:, col(2)], wl[:, col(3)], wl[:, col(7)],
                                   jnp.zeros((d, LANES - 2 * H_A - H_F), F32)], axis=1).astype(BF16)
        g_row = g_mix[l].reshape(1, d)
        a_row = _lane_row(a_log[l], H_A)
        dt_row = _lane_row(dt_bias[l], H_A)
        bf_row = _lane_row(b_forget[l], 2 * H_A)
        qg_pair = jnp.tile(qn_g[l], LANES // HD_F).reshape(1, LANES)
        kg_pair = jnp.tile(kn_g[l], LANES // HD_F).reshape(1, LANES)
        qg_full = jnp.tile(qn_g[l], H_F).reshape(1, wf)
        kg_full = jnp.tile(kn_g[l], H_F).reshape(1, wf)
        na_row = norm_a[l].reshape(1, DV_A)
        pa_b, pf_b, wo_b = p_a[l].astype(BF16), p_f[l].astype(BF16), w_o[l].astype(BF16)
        gffn_row = g_ffn[l].reshape(1, d)
        wr = jnp.pad(w_router[l], ((0, 0), (0, LANES - N_EXP)))
        wr_hi = wr.astype(BF16)
        wr_lo = (wr - wr_hi.astype(F32)).astype(BF16)
        br_row = _lane_row(b_router[l], 0)
        merge_w = (na_row, pa_b, pf_b, wo_b, gffn_row, wr_hi, wr_lo, br_row)

        qkv_p, z_p, qf_p, kf_p, vf_p, ga_p, gf_p, sm_p = _in_proj(xp.reshape(b * t, d), g_row, w_main, w_small, widths)
        r3 = lambda a: a.reshape(b, t, a.shape[-1])
        qkv3, sm3 = r3(qkv_p), r3(sm_p)
        conv_p = qkv3[:, t - (CONV_W - 1):, :]
        prep = _conv_prep(qkv3, conv_w[l], jnp.zeros((b, CONV_W - 1, qkv_w), F32))
        oa_p, s_p = _delta_scan(prep, sm3, a_row, dt_row, jnp.zeros((b, H_A, DK_A, DV_A), F32), lead)
        lf_p, c_col = _logf_cumsum(sm3, bf_row, t_pad)
        qx, kx, vb, kn_p = _fox_prep(r3(qf_p), r3(kf_p), r3(vf_p), c_col, qg_pair, kg_pair, t_pad)
        of_p = _fox_attn(qx, kx, vb)
        x1_p, h2_p, tw_p, ti_p = _merge(xp, oa_p, r3(z_p), of_p, r3(ga_p), r3(gf_p), *merge_w)

        qkv_s, z_s, qf_s, kf_s, vf_s, ga_s, gf_s, sm_s = _in_proj(xs.reshape(bd, d), g_row, w_main, w_small, widths)
        s3 = lambda a: a.reshape(bd, 1, a.shape[-1])
        oa_s, s_s, conv_s = _decode_delta(s3(qkv_s), state_conv[l], s3(sm_s), conv_w[l], a_row, dt_row, state_delta[l])
        n_pool, page = cache_k.shape[1], cache_k.shape[2]
        of_s, kn_s, lf_s = _decode_attn(page_table, s3(qf_s), s3(kf_s), s3(vf_s), s3(sm_s), qg_full, kg_full,
                                        _lane_row(b_forget[l], 2 * H_A),
                                        jnp.transpose(cache_k[l], (0, 2, 3, 1)).reshape(n_pool, wf, page),
                                        jnp.transpose(cache_v[l], (0, 2, 3, 1)).reshape(n_pool, wf, page),
                                        jnp.swapaxes(cache_logf[l], 1, 2))
        g1 = lambda a: a.reshape(1, bd, a.shape[-1])
        x1_s, h2_s, tw_s, ti_s = _merge(g1(xs), g1(oa_s), g1(z_s), g1(of_s), g1(ga_s), g1(gf_s), *merge_w)

        n_p = b * t
        h2_all = jnp.concatenate([h2_p.reshape(n_p, d), h2_s.reshape(bd, d)], axis=0)
        tw_all = jnp.concatenate([tw_p.reshape(n_p, LANES), tw_s.reshape(bd, LANES)], axis=0)[:, :TOP_K]
        ti_all = jnp.concatenate([ti_p.reshape(n_p, LANES), ti_s.reshape(bd, LANES)], axis=0)[:, :TOP_K]
        y_all = _moe(h2_all, tw_all, ti_all, w_gate[l], b_gate[l][:, None, :],
                     w_up[l], b_up[l][:, None, :], w_down[l], b_down[l][:, None, :])
        xp = x1_p + y_all[:n_p].reshape(b, t, d)
        xs = x1_s.reshape(bd, 1, d) + y_all[n_p:].reshape(bd, 1, d)

        st_p = (kn_p.reshape(b, t, H_F, HD_F), vf_p.reshape(b, t, H_F, HD_F),
                lf_p[:, :, 2 * H_A:2 * H_A + H_F], s_p, conv_p)
        st_s = (kn_s.reshape(bd, 1, H_F, HD_F), vf_s.reshape(bd, 1, H_F, HD_F),
                lf_s[:, :, 2 * H_A:2 * H_A + H_F], s_s, conv_s)
        for lst, a in zip(new_p, st_p):
            lst.append(a)
        for lst, a in zip(new_s, st_s):
            lst.append(a)
    k_p, v_p, lf_pp, d_p, c_p = (jnp.stack(a) for a in new_p)
    k_s, v_s, lf_ss, d_s, c_s = (jnp.stack(a) for a in new_s)
    return (xp[:, N_META:], xs, k_p, v_p, lf_pp, k_s, v_s, lf_ss, d_p, d_s, c_p, c_s)
```

```python
import functools

import jax
import jax.numpy as jnp
from jax import lax
from jax.experimental import pallas as pl
from jax.experimental.pallas import tpu as pltpu

F32 = jnp.float32
BF16 = jnp.bfloat16

N_META = 16
H_A = 4
DK_A = 128
DV_A = 128
CONV_W = 4
CHUNK = 64
H_F = 8
HD_F = 64
N_EXP = 32
TOP_K = 4
SWIGLU_LIMIT = 7.0
SWIGLU_ALPHA = 1.702
EPS = 1e-6
LOG2E = 1.4426950408889634

LANES = 128
F_OFF = 2 * H_A
HP = 16
ATT_BLOCK = 256
MOE_BLOCK = 256
PAGES_PER_STEP = 8
VMEM_LIMIT = 48 * 1024 * 1024


def _cparams(*sem):
    return pltpu.CompilerParams(dimension_semantics=sem, vmem_limit_bytes=VMEM_LIMIT)


def _row_tile(n, cap):
    if n <= cap:
        return n
    best = None
    for t in range(8, cap + 1, 8):
        if n % t == 0:
            best = t
    assert best is not None, n
    return best


def _dot(a, b):
    return jnp.dot(a, b, preferred_element_type=F32)


def _dot_nt(a, b):
    return lax.dot_general(a, b, (((1,), (1,)), ((), ())), preferred_element_type=F32)


def _split2(x):
    hi = x.astype(BF16)
    lo = (x - hi.astype(F32)).astype(BF16)
    return hi, lo


def _dot_x3(a, b):
    a_hi, a_lo = _split2(a)
    b_hi, b_lo = _split2(b)
    return _dot(a_hi, b_hi) + (_dot(a_hi, b_lo) + _dot(a_lo, b_hi))


def _split3(x):
    hi = x.astype(BF16)
    r = x - hi.astype(F32)
    mid = r.astype(BF16)
    lo = (r - mid.astype(F32)).astype(BF16)
    return hi, mid, lo


def _sigmoid(x):
    return 1.0 / (1.0 + jnp.exp(-x))


def _softplus(x):
    return jnp.maximum(x, 0.0) + jnp.log1p(jnp.exp(-jnp.abs(x)))


def _log_sigmoid(x):
    return jnp.minimum(x, 0.0) - jnp.log1p(jnp.exp(-jnp.abs(x)))


IN_WIDTHS = (2 * H_A * DK_A + H_A * DV_A, H_A * DV_A, H_F * HD_F, H_F * HD_F, H_F * HD_F)
COL_CHUNK = 512


def _in_proj_body(x_ref, g_ref, wm_ref, ws_ref, *out_refs, widths):
    x = x_ref[...]
    h = x * lax.rsqrt(jnp.mean(x * x, axis=-1, keepdims=True) + EPS) * g_ref[...]
    hb = h.astype(BF16)
    col = 0
    for o_ref, w in zip(out_refs[:-1], widths):
        for c0 in range(0, w, COL_CHUNK):
            o_ref[:, c0:c0 + COL_CHUNK] = _dot(hb, wm_ref[:, col + c0:col + c0 + COL_CHUNK])
        col += w
    out_refs[-1][...] = _dot(hb, ws_ref[...])


def _in_proj(x2d, g_row, w_main, w_small, widths):
    n, d = x2d.shape
    tm = _row_tile(n, 512)
    outs = [jax.ShapeDtypeStruct((n, w), F32) for w in widths] + [jax.ShapeDtypeStruct((n, LANES), F32)]
    out_specs = [pl.BlockSpec((tm, w), lambda i: (i, 0)) for w in widths] + [pl.BlockSpec((tm, LANES), lambda i: (i, 0))]
    return pl.pallas_call(
        functools.partial(_in_proj_body, widths=widths),
        grid=(n // tm,),
        in_specs=[pl.BlockSpec((tm, d), lambda i: (i, 0)),
                  pl.BlockSpec((1, d), lambda i: (0, 0)),
                  pl.BlockSpec(w_main.shape, lambda i: (0, 0), pipeline_mode=pl.Buffered(1)),
                  pl.BlockSpec(w_small.shape, lambda i: (0, 0), pipeline_mode=pl.Buffered(1))],
        out_specs=out_specs,
        out_shape=outs,
        compiler_params=_cparams("parallel"),
        name="in_proj",
    )(x2d, g_row, w_main, w_small)


def _conv_prep_body(x_ref, cw_ref, prev_ref, o_ref, buf, *, t_len, rows):
    j = pl.program_id(1)
    pad = 8
    buf[0:pad, :] = jnp.zeros((pad, LANES), F32)
    buf[pad - (CONV_W - 1):pad, :] = prev_ref[...]
    buf[pad:pad + t_len, :] = x_ref[...]
    cw = cw_ref[...]
    is_q = j < H_A
    is_qk = j < 2 * H_A
    for r0 in range(0, t_len, rows):
        y = jnp.zeros((rows, LANES), F32)
        for w in range(CONV_W):
            off = pad - (CONV_W - 1) + w + r0
            y = y + buf[off:off + rows, :] * cw[w:w + 1, :]
        y = y * _sigmoid(y)
        nrm = lax.rsqrt(jnp.sum(y * y, axis=-1, keepdims=True) + EPS)
        f = jnp.where(is_qk, nrm * jnp.where(is_q, DK_A ** -0.5, 1.0), 1.0)
        o_ref[r0:r0 + rows, :] = y * f


def _conv_prep(qkv3, conv_w, conv_prev):
    b, t, c = qkv3.shape
    rows = _row_tile(t, 512)
    return pl.pallas_call(
        functools.partial(_conv_prep_body, t_len=t, rows=rows),
        grid=(b, c // LANES),
        in_specs=[pl.BlockSpec((None, t, LANES), lambda i, j: (i, 0, j)),
                  pl.BlockSpec((CONV_W, LANES), lambda i, j: (0, j)),
                  pl.BlockSpec((None, CONV_W - 1, LANES), lambda i, j: (i, 0, j))],
        out_specs=pl.BlockSpec((None, t, LANES), lambda i, j: (i, 0, j)),
        out_shape=jax.ShapeDtypeStruct((b, t, c), F32),
        scratch_shapes=[pltpu.VMEM((8 + t, LANES), F32)],
        compiler_params=_cparams("parallel", "parallel"),
        name="conv_prep",
    )(qkv3, conv_w, conv_prev)


def _gates(sm, a_row, dt_row, h):
    lane = lax.broadcasted_iota(jnp.int32, sm.shape, 1)
    beta_all = _sigmoid(sm)
    g_all = -jnp.exp(a_row) * _softplus(sm + dt_row)
    beta = jnp.sum(jnp.where(lane == h, beta_all, 0.0), axis=-1, keepdims=True)
    g = jnp.sum(jnp.where(lane == H_A + h, g_all, 0.0), axis=-1, keepdims=True)
    return beta, g


def _decay_terms(gs):
    c = gs[0].shape[0]
    ri = lax.broadcasted_iota(jnp.int32, (c, c), 0)
    ci = lax.broadcasted_iota(jnp.int32, (c, c), 1)
    g_cum, decay = [], []
    for g in gs:
        g_row = jnp.sum(jnp.where(ri == ci, g, 0.0), axis=0, keepdims=True)
        gc = jnp.sum(jnp.where(ci <= ri, g_row, 0.0), axis=1, keepdims=True)
        gc_row = jnp.sum(jnp.where(ri <= ci, g, 0.0), axis=0, keepdims=True)
        g_cum.append(gc)
        decay.append(jnp.exp(jnp.where(ci <= ri, gc - gc_row, -jnp.inf)))
    return g_cum, decay


def _delta_step(cur, nxt, s_ref):
    o = inv_n = None
    stages = []
    if cur is not None:
        qs, ks, vs, betas, gs, inv = cur
        rng = range(len(qs))
        c = qs[0].shape[0]
        g_cum, decay = _decay_terms(gs)
        e_g = [jnp.exp(g_cum[i]) for i in rng]
        s = [s_ref[i] for i in rng]
        sb = [s[i].astype(BF16) for i in rng]
        qb = [qs[i].astype(BF16) for i in rng]
        kb = [ks[i].astype(BF16) for i in rng]
        k_s = [_dot(kb[i], sb[i]) for i in rng]
        val = {}

        def st_delta():
            rhs = [betas[i] * (vs[i] - e_g[i] * k_s[i]) for i in rng]
            val["db"] = [_dot_x3(inv[i], rhs[i]).astype(BF16) for i in rng]
            val["qk"] = [_dot_nt(qb[i], kb[i]) for i in rng]
            val["q_s"] = [_dot(qb[i], sb[i]) for i in rng]

        def st_out():
            val["o"] = [e_g[i] * val["q_s"][i] + _dot((val["qk"][i] * decay[i]).astype(BF16), val["db"][i]) for i in rng]

        def st_state():
            g_end = [g_cum[i][c - 1:c, :] for i in rng]
            k_dec = [(ks[i] * jnp.exp(g_end[i] - g_cum[i])).T.astype(BF16) for i in rng]
            for i in rng:
                s_ref[i] = jnp.exp(g_end[i]) * s[i] + _dot(k_dec[i], val["db"][i])

        stages = [st_delta, st_out, st_state]
    if nxt is not None:
        ks_n, betas_n, gs_n = nxt
        rng_n = range(len(ks_n))
        c = ks_n[0].shape[0]
        ri = lax.broadcasted_iota(jnp.int32, (c, c), 0)
        ci = lax.broadcasted_iota(jnp.int32, (c, c), 1)
        _, decay_n = _decay_terms(gs_n)
        kb_n = [ks_n[i].astype(BF16) for i in rng_n]
        kk = [_dot_nt(kb_n[i], kb_n[i]) for i in rng_n]
        a = [jnp.where(ci < ri, betas_n[i] * decay_n[i] * kk[i], 0.0) for i in rng_n]
        inv_n = [jnp.where(ri == ci, 1.0, 0.0) - a[i] for i in rng_n]
        pw = a
        for _ in range(max(1, (c - 1).bit_length() - 1)):
            if stages:
                stages.pop(0)()
            pw = [_dot_x3(pw[i], pw[i]) for i in rng_n]
            inv_n = [inv_n[i] + _dot_x3(inv_n[i], pw[i]) for i in rng_n]
    for st in stages:
        st()
    if cur is not None:
        o = val["o"]
    return o, inv_n


def _delta_scan_body(qkv_ref, sm_ref, a_ref, dt_ref, s0_ref, o_ref, s_out_ref, s_ref, *, t_len, lead):
    a_row = a_ref[...]
    dt_row = dt_ref[...]
    s_ref[...] = s0_ref[...]
    first = CHUNK - lead
    n_chunks = (lead + t_len) // CHUNK

    def gates(load, mask_lead):
        sm = load(sm_ref, 0)
        betas, gs = [], []
        for h in range(H_A):
            beta, g = _gates(sm, a_row, dt_row, h)
            if mask_lead:
                row = lax.broadcasted_iota(jnp.int32, (CHUNK, 1), 0)
                beta = jnp.where(row >= lead, beta, 0.0)
                g = jnp.where(row >= lead, g, 0.0)
            betas.append(beta)
            gs.append(g)
        return betas, gs

    def nxt_args(load, mask_lead):
        betas, gs = gates(load, mask_lead)
        return [load(qkv_ref, (H_A + h) * DK_A) for h in range(H_A)], betas, gs

    def cur_args(load, mask_lead, inv):
        betas, gs = gates(load, mask_lead)
        qs = [load(qkv_ref, h * DK_A) for h in range(H_A)]
        ks = [load(qkv_ref, (H_A + h) * DK_A) for h in range(H_A)]
        vs = [load(qkv_ref, 2 * H_A * DK_A + h * DV_A) for h in range(H_A)]
        return qs, ks, vs, betas, gs, inv

    def chunk_rows(c):
        return pl.ds(pl.multiple_of(c * CHUNK - lead, 8), CHUNK)

    def loader(c):
        sl = chunk_rows(c)
        return lambda ref, col: ref[sl, col:col + LANES]

    if lead:
        def load0(ref, col):
            return jnp.concatenate([jnp.zeros((lead, LANES), F32), ref[0:first, col:col + LANES]], axis=0)

        def store0(h, o):
            o_ref[0:first, h * DV_A:(h + 1) * DV_A] = o[lead:, :]

        _, inv0 = _delta_step(None, nxt_args(load0, True), s_ref)
        has_next = n_chunks > 1
        outs, inv1 = _delta_step(cur_args(load0, True, inv0), nxt_args(loader(1), False) if has_next else None, s_ref)
        for h, o in enumerate(outs):
            store0(h, o)
        c_start = 1
    else:
        has_next = n_chunks > 0
        _, inv1 = _delta_step(None, nxt_args(loader(0), False), s_ref) if has_next else (None, None)
        c_start = 0

    def body(c, inv):
        sl = chunk_rows(c)
        outs, inv_n = _delta_step(cur_args(loader(c), False, inv),
                                  nxt_args(loader(jnp.minimum(c + 1, n_chunks - 1)), False), s_ref)
        for h, o in enumerate(outs):
            o_ref[sl, h * DV_A:(h + 1) * DV_A] = o
        return tuple(inv_n)

    if has_next:
        lax.fori_loop(c_start, n_chunks, body, tuple(inv1))
    s_out_ref[...] = s_ref[...]


def _delta_scan(qkv3, small3, a_row, dt_row, s0, lead):
    b, t, c = qkv3.shape
    assert (lead + t) % CHUNK == 0 and lead % 8 == 0
    return pl.pallas_call(
        functools.partial(_delta_scan_body, t_len=t, lead=lead),
        grid=(b,),
        in_specs=[pl.BlockSpec((None, t, c), lambda i: (i, 0, 0)),
                  pl.BlockSpec((None, t, LANES), lambda i: (i, 0, 0)),
                  pl.BlockSpec((1, LANES), lambda i: (0, 0)),
                  pl.BlockSpec((1, LANES), lambda i: (0, 0)),
                  pl.BlockSpec((None, H_A, DK_A, DV_A), lambda i: (i, 0, 0, 0))],
        out_specs=[pl.BlockSpec((None, t, H_A * DV_A), lambda i: (i, 0, 0)),
                   pl.BlockSpec((None, H_A, DK_A, DV_A), lambda i: (i, 0, 0, 0))],
        out_shape=[jax.ShapeDtypeStruct((b, t, H_A * DV_A), F32),
                   jax.ShapeDtypeStruct((b, H_A, DK_A, DV_A), F32)],
        scratch_shapes=[pltpu.VMEM((H_A, DK_A, DV_A), F32)],
        compiler_params=_cparams("parallel"),
        name="delta_scan",
    )(qkv3, small3, a_row, dt_row, s0)


def _group_rms(x, gain, seg):
    hi, lo = _split2(x * x)
    ss = _dot(hi, seg) + _dot(lo, seg)
    return x * lax.rsqrt(ss * (1.0 / HD_F) + EPS) * gain


def _seg_matrix(n):
    r = lax.broadcasted_iota(jnp.int32, (n, n), 0) // HD_F
    c = lax.broadcasted_iota(jnp.int32, (n, n), 1) // HD_F
    return jnp.where(r == c, 1.0, 0.0).astype(BF16)


def _fox_prep_body(q_ref, k_ref, v_ref, c_ref, qg_ref, kg_ref, qx_ref, kx_ref, vt_ref, kn_ref, *, t_len, t_pad):
    p = pl.program_id(1)
    heads = LANES // HD_F

    def feature_major(x):
        if t_pad > t_len:
            x = jnp.concatenate([x, jnp.zeros((t_pad - t_len, LANES), F32)], axis=0)
        return x.T.astype(BF16)

    seg = _seg_matrix(LANES)
    qn = _group_rms(q_ref[...], qg_ref[...], seg) * (HD_F ** -0.5 * LOG2E)
    kn = _group_rms(k_ref[...], kg_ref[...], seg)
    kn_ref[...] = kn
    c_all = c_ref[...] * LOG2E
    lane = lax.broadcasted_iota(jnp.int32, (t_len, LANES), 1)
    for hh in range(heads):
        c = jnp.sum(jnp.where(lane == F_OFF + p * heads + hh, c_all, 0.0), axis=-1, keepdims=True)
        pieces = [x.astype(F32) for x in _split3(c)]
        own = (lane // HD_F) == hh
        f0 = ((hh + 1) % heads) * HD_F
        qx = jnp.where(own, qn, 0.0)
        kx = jnp.where(own, kn, 0.0)
        for n, piece in enumerate(pieces):
            qx = jnp.where(lane == f0 + n, piece, qx)
            kx = jnp.where(lane == f0 + n, 1.0, kx)
            qx = jnp.where(lane == f0 + 3 + n, 1.0, qx)
            kx = jnp.where(lane == f0 + 3 + n, -piece, kx)
        kx_ref[0:t_len, hh * LANES:(hh + 1) * LANES] = kx.astype(BF16)
        qx_ref[hh * LANES:(hh + 1) * LANES, :] = feature_major(qx)
    vt_ref[...] = feature_major(v_ref[...])
    if t_pad > t_len:
        kx_ref[t_len:, :] = jnp.zeros((t_pad - t_len, heads * LANES), BF16)


def _fox_prep(q3, k3, v3, c_col, qg_row, kg_row, t_pad):
    b, t, w = q3.shape
    heads = LANES // HD_F
    blk = pl.BlockSpec((None, t, LANES), lambda i, p: (i, 0, p))
    gblk = pl.BlockSpec((1, LANES), lambda i, p: (0, 0))
    return pl.pallas_call(
        functools.partial(_fox_prep_body, t_len=t, t_pad=t_pad),
        grid=(b, w // LANES),
        in_specs=[blk, blk, blk, pl.BlockSpec((None, t, LANES), lambda i, p: (i, 0, 0)), gblk, gblk],
        out_specs=[pl.BlockSpec((None, heads * LANES, t_pad), lambda i, p: (i, p, 0)),
                   pl.BlockSpec((None, t_pad, heads * LANES), lambda i, p: (i, 0, p)),
                   pl.BlockSpec((None, LANES, t_pad), lambda i, p: (i, p, 0)), blk],
        out_shape=[jax.ShapeDtypeStruct((b, H_F * LANES, t_pad), BF16), jax.ShapeDtypeStruct((b, t_pad, H_F * LANES), BF16),
                   jax.ShapeDtypeStruct((b, w, t_pad), BF16), jax.ShapeDtypeStruct((b, t, w), F32)],
        compiler_params=_cparams("parallel", "parallel"),
        name="fox_prep",
    )(q3, k3, v3, c_col, qg_row, kg_row)


def _logf_cumsum_body(sm_ref, bf_ref, lf_ref, ccol_ref, buf, *, t_len, t_pad):
    lf = _log_sigmoid(sm_ref[...] + bf_ref[...])
    lf_ref[...] = lf
    buf[0:t_len, :] = lf
    if t_pad > t_len:
        buf[t_len:, :] = jnp.zeros((t_pad - t_len, LANES), F32)
    blk = ATT_BLOCK
    ri = lax.broadcasted_iota(jnp.int32, (blk, blk), 0)
    ci = lax.broadcasted_iota(jnp.int32, (blk, blk), 1)
    tri = jnp.where(ci <= ri, 1.0, 0.0).astype(BF16)
    carry = jnp.zeros((1, LANES), F32)
    for i in range(t_pad // blk):
        hi, mid, lo = _split3(buf[i * blk:(i + 1) * blk, :])
        c = _dot(tri, hi) + _dot(tri, mid) + _dot(tri, lo) + carry
        ccol_ref[i * blk:(i + 1) * blk, :] = c
        carry = c[blk - 1:blk, :]


def _logf_cumsum(small3, bf_row, t_pad):
    b, t, _ = small3.shape
    return pl.pallas_call(
        functools.partial(_logf_cumsum_body, t_len=t, t_pad=t_pad),
        grid=(b,),
        in_specs=[pl.BlockSpec((None, t, LANES), lambda i: (i, 0, 0)),
                  pl.BlockSpec((1, LANES), lambda i: (0, 0))],
        out_specs=[pl.BlockSpec((None, t, LANES), lambda i: (i, 0, 0)),
                   pl.BlockSpec((None, t_pad, LANES), lambda i: (i, 0, 0))],
        out_shape=[jax.ShapeDtypeStruct((b, t, LANES), F32),
                   jax.ShapeDtypeStruct((b, t_pad, LANES), F32)],
        scratch_shapes=[pltpu.VMEM((t_pad, LANES), F32)],
        compiler_params=_cparams("parallel"),
        name="logf_cumsum",
    )(small3, bf_row)


def _fox_attn_body(qt_ref, k_ref, vt_ref, o_ref):
    i = pl.program_id(1)
    blk = ATT_BLOCK
    key_le_query = (lax.broadcasted_iota(jnp.int32, (blk, blk), 0)
                    <= lax.broadcasted_iota(jnp.int32, (blk, blk), 1))

    def step(j, carry, diagonal):
        ms, ls, accs = carry
        k0 = pl.multiple_of(j * blk, blk)
        st = []
        for h in range(H_F):
            s = _dot(k_ref[pl.ds(k0, blk), h * LANES:(h + 1) * LANES], qt_ref[h * LANES:(h + 1) * LANES, :])
            st.append(jnp.where(key_le_query, s, -jnp.inf) if diagonal else s)
        new_ms, new_ls, alphas, pts = [], [], [], []
        for h in range(H_F):
            m_new = jnp.maximum(ms[h], jnp.max(st[h], axis=0, keepdims=True))
            alpha = jnp.exp2(ms[h] - m_new)
            pt = jnp.exp2(st[h] - m_new)
            new_ls.append(alpha * ls[h] + jnp.sum(pt, axis=0, keepdims=True))
            new_ms.append(m_new)
            alphas.append(alpha)
            pts.append(pt.astype(BF16))
        new_accs = [alphas[h] * accs[h] + _dot(vt_ref[h * HD_F:(h + 1) * HD_F, pl.ds(k0, blk)], pts[h])
                    for h in range(H_F)]
        return tuple(new_ms), tuple(new_ls), tuple(new_accs)

    init = (tuple(jnp.full((1, blk), -jnp.inf, F32) for _ in range(H_F)),
            tuple(jnp.zeros((1, blk), F32) for _ in range(H_F)),
            tuple(jnp.zeros((HD_F, blk), F32) for _ in range(H_F)))
    carry = lax.fori_loop(0, i, lambda j, c: step(j, c, False), init)
    ms, ls, accs = step(i, carry, True)
    heads = LANES // HD_F
    for p in range(H_F // heads):
        ot = jnp.concatenate([accs[p * heads + hh] * (1.0 / ls[p * heads + hh]) for hh in range(heads)], axis=0)
        o_ref[:, p * LANES:(p + 1) * LANES] = ot.T.astype(o_ref.dtype)


def _fox_attn(qt, kx, vt):
    b, w, t_pad = vt.shape
    blk = ATT_BLOCK
    return pl.pallas_call(
        _fox_attn_body,
        grid=(b, t_pad // blk),
        in_specs=[pl.BlockSpec((None, H_F * LANES, blk), lambda n, i: (n, 0, i)),
                  pl.BlockSpec((None, t_pad, H_F * LANES), lambda n, i: (n, 0, 0)),
                  pl.BlockSpec((None, w, t_pad), lambda n, i: (n, 0, 0))],
        out_specs=pl.BlockSpec((None, blk, w), lambda n, i: (n, i, 0)),
        out_shape=jax.ShapeDtypeStruct((b, t_pad, w), BF16),
        compiler_params=_cparams("parallel", "arbitrary"),
        name="fox_attn",
    )(qt, kx, vt)


def _decode_delta_body(x_ref, sc_ref, sm_ref, cw_ref, a_ref, dt_ref, s_ref, o_ref, s_out_ref, conv_ref):
    x_new = x_ref[...]
    sc = sc_ref[...]
    cw = cw_ref[...]
    y = x_new * cw[CONV_W - 1:CONV_W, :]
    for w in range(CONV_W - 1):
        y = y + sc[w:w + 1, :] * cw[w:w + 1, :]
    y = y * _sigmoid(y)
    conv_ref[0:CONV_W - 2, :] = sc[1:, :]
    conv_ref[CONV_W - 2:CONV_W - 1, :] = x_new
    sm = sm_ref[...]
    lane = lax.broadcasted_iota(jnp.int32, (1, LANES), 1)
    beta_all = _sigmoid(sm)
    g_all = -jnp.exp(a_ref[...]) * _softplus(sm + dt_ref[...])
    qk_w = H_A * DK_A
    row0 = lax.broadcasted_iota(jnp.int32, (LANES, LANES), 0) == 0

    def in_row0(r):
        return jnp.where(row0, jnp.broadcast_to(r, (LANES, LANES)), 0.0)

    for h in range(H_A):
        q = y[:, h * DK_A:(h + 1) * DK_A]
        k = y[:, qk_w + h * DK_A:qk_w + (h + 1) * DK_A]
        v = y[:, 2 * qk_w + h * DV_A:2 * qk_w + (h + 1) * DV_A]
        q = q * lax.rsqrt(jnp.sum(q * q, axis=-1, keepdims=True) + EPS) * (DK_A ** -0.5)
        k = k * lax.rsqrt(jnp.sum(k * k, axis=-1, keepdims=True) + EPS)
        beta = jnp.sum(jnp.where(lane == h, beta_all, 0.0), axis=-1, keepdims=True)
        g = jnp.sum(jnp.where(lane == H_A + h, g_all, 0.0), axis=-1, keepdims=True)
        e_g = jnp.exp(g)
        s = s_ref[h]
        sb = s.astype(BF16)
        k_sq = in_row0(k)
        k_s = _dot(k_sq.astype(BF16), sb)[0:1, :]
        q_s = _dot(in_row0(q).astype(BF16), sb)[0:1, :]
        delta = beta * (v - e_g * k_s)
        qk = jnp.sum(q * k, axis=-1, keepdims=True)
        o_ref[:, h * DV_A:(h + 1) * DV_A] = e_g * q_s + qk * delta
        s_out_ref[h] = e_g * s + _dot_x3(k_sq.T, in_row0(delta))


def _decode_delta(qkv3, state_conv, small3, conv_w, a_row, dt_row, s0):
    bd, _, c = qkv3.shape
    return pl.pallas_call(
        _decode_delta_body,
        grid=(bd,),
        in_specs=[pl.BlockSpec((None, 1, c), lambda i: (i, 0, 0)),
                  pl.BlockSpec((None, CONV_W - 1, c), lambda i: (i, 0, 0)),
                  pl.BlockSpec((None, 1, LANES), lambda i: (i, 0, 0)),
                  pl.BlockSpec((CONV_W, c), lambda i: (0, 0)),
                  pl.BlockSpec((1, LANES), lambda i: (0, 0)),
                  pl.BlockSpec((1, LANES), lambda i: (0, 0)),
                  pl.BlockSpec((None, H_A, DK_A, DV_A), lambda i: (i, 0, 0, 0))],
        out_specs=[pl.BlockSpec((None, 1, H_A * DV_A), lambda i: (i, 0, 0)),
                   pl.BlockSpec((None, H_A, DK_A, DV_A), lambda i: (i, 0, 0, 0)),
                   pl.BlockSpec((None, CONV_W - 1, c), lambda i: (i, 0, 0))],
        out_shape=[jax.ShapeDtypeStruct((bd, 1, H_A * DV_A), F32),
                   jax.ShapeDtypeStruct((bd, H_A, DK_A, DV_A), F32),
                   jax.ShapeDtypeStruct((bd, CONV_W - 1, c), F32)],
        compiler_params=_cparams("parallel"),
        name="decode_delta",
    )(qkv3, state_conv, small3, conv_w, a_row, dt_row, s0)


def _decode_attn_body(pt_ref, q_ref, k_ref, v_ref, sm_ref, qg_ref, kg_ref, bf_ref, *rest, n_pg):
    page_refs = rest[:3 * n_pg]
    o_ref, kn_ref, lf_ref, qrows, vnew, snew, m_ref, l_ref, acc_ref, carry_ref, s_scr = rest[3 * n_pg:]
    step = pl.program_id(1)
    n_steps = pl.num_programs(1) // 2
    w = H_F * HD_F
    row = lax.broadcasted_iota(jnp.int32, (HP, w), 0)
    head_of_lane = lax.broadcasted_iota(jnp.int32, (HP, w), 1) // HD_F

    @pl.when(step == 0)
    def _():
        seg = _seg_matrix(w)

        def rms_rows(x_row, gain):
            xr = jnp.broadcast_to(x_row, (HP, w))
            hi, lo = _split2(xr * xr)
            ss = _dot(hi, seg) + _dot(lo, seg)
            return xr * lax.rsqrt(ss * (1.0 / HD_F) + EPS) * gain

        qn = rms_rows(q_ref[...], qg_ref[...]) * (HD_F ** -0.5)
        kn = rms_rows(k_ref[...], kg_ref[...])
        kn_ref[...] = kn[0:1, :]
        lf = _log_sigmoid(sm_ref[...] + bf_ref[...])
        lf_ref[...] = lf
        q_m = jnp.where(head_of_lane == row, qn, 0.0).astype(BF16)
        qrows[...] = q_m
        vnew[...] = jnp.broadcast_to(v_ref[...], (HP, w)).astype(BF16)
        s_new = jnp.sum(q_m.astype(F32) * kn.astype(BF16).astype(F32), axis=-1, keepdims=True)
        snew[...] = s_new
        m_ref[...] = s_new
        rr = lax.broadcasted_iota(jnp.int32, (HP, LANES), 0)
        ll = lax.broadcasted_iota(jnp.int32, (HP, LANES), 1)
        carry_ref[...] = jnp.sum(jnp.where(ll == rr + F_OFF, jnp.broadcast_to(lf, (HP, LANES)), 0.0),
                                 axis=-1, keepdims=True)

    @pl.when(step < n_steps)
    def _():
        pos = lax.broadcasted_iota(jnp.int32, (HP, LANES), 1)
        qb = qrows[...]
        carry = carry_ref[...]
        m_new = m_ref[...]
        for g in range(n_pg):
            lfp = jnp.concatenate([page_refs[3 * g + 2][...], jnp.zeros((HP - H_F, LANES), F32)], axis=0)
            suf = lfp
            sh = 1
            while sh < LANES:
                rolled = pltpu.roll(suf, LANES - sh, axis=1)
                suf = suf + jnp.where(pos + sh < LANES, rolled, 0.0)
                sh *= 2
            s = _dot(qb, page_refs[3 * g][...].astype(BF16)) + (carry + (suf - lfp))
            s_scr[step * n_pg + g] = s
            m_new = jnp.maximum(m_new, jnp.max(s, axis=-1, keepdims=True))
            carry = carry + jnp.sum(lfp, axis=-1, keepdims=True)
        carry_ref[...] = carry
        m_ref[...] = m_new

    @pl.when(step == n_steps)
    def _():
        m = m_ref[...]
        p_new = jnp.exp(snew[...] - m)

        def add(i, tot):
            return tot + jnp.sum(jnp.exp(s_scr[i] - m), axis=-1, keepdims=True)

        l = lax.fori_loop(0, n_steps * n_pg, add, p_new)
        l_ref[...] = l
        acc_ref[...] = (p_new / l).astype(BF16).astype(F32) * vnew[...].astype(F32)

    @pl.when(step >= n_steps)
    def _():
        m = m_ref[...]
        l = l_ref[...]
        acc = acc_ref[...]
        for g in range(n_pg):
            pr = (jnp.exp(s_scr[(step - n_steps) * n_pg + g] - m) / l).astype(BF16)
            acc = acc + _dot_nt(pr, page_refs[3 * g + 1][...].astype(BF16))
        acc_ref[...] = acc

    @pl.when(step == pl.num_programs(1) - 1)
    def _():
        o_ref[...] = jnp.sum(jnp.where(head_of_lane == row, acc_ref[...], 0.0), axis=0, keepdims=True)


def _decode_attn(page_table, q3, k3, v3, small3, qg_row, kg_row, bf_row, ck, cv, clf_t):
    bd, _, w = q3.shape
    n_pages = page_table.shape[1]
    page = ck.shape[2]
    assert page == LANES and n_pages % PAGES_PER_STEP == 0 and LANES // HD_F == 2
    n_pg = PAGES_PER_STEP
    steps = n_pages // n_pg

    def page_of(i, s, g, pt):
        return pt[i, n_pages - 1 - (s * n_pg + g)]

    def key_idx(g):
        return lambda i, s, pt: (page_of(i, jnp.minimum(s, steps - 1), g, pt), 0, 0)

    def val_idx(g):
        return lambda i, s, pt: (page_of(i, jnp.maximum(s - steps, 0), g, pt), 0, 0)

    row_spec = lambda width: pl.BlockSpec((None, 1, width), lambda i, s, pt: (i, 0, 0))
    const_spec = lambda width: pl.BlockSpec((1, width), lambda i, s, pt: (0, 0))
    in_specs = [row_spec(w), row_spec(w), row_spec(w), row_spec(LANES), const_spec(w), const_spec(w), const_spec(LANES)]
    args = [q3, k3, v3, small3, qg_row, kg_row, bf_row]
    for g in range(n_pg):
        in_specs += [pl.BlockSpec((None, w, page), key_idx(g)),
                     pl.BlockSpec((None, w, page), val_idx(g)),
                     pl.BlockSpec((None, H_F, page), key_idx(g))]
        args += [ck, cv, clf_t]
    col = lambda: pltpu.VMEM((HP, 1), F32)
    grid_spec = pltpu.PrefetchScalarGridSpec(
        num_scalar_prefetch=1,
        grid=(bd, 2 * steps),
        in_specs=in_specs,
        out_specs=[row_spec(w), row_spec(w), row_spec(LANES)],
        scratch_shapes=[pltpu.VMEM((HP, w), BF16), pltpu.VMEM((HP, w), BF16), col(), col(), col(),
                        pltpu.VMEM((HP, w), F32), col(), pltpu.VMEM((n_pages, HP, LANES), F32)],
    )
    return pl.pallas_call(
        functools.partial(_decode_attn_body, n_pg=n_pg),
        grid_spec=grid_spec,
        out_shape=[jax.ShapeDtypeStruct((bd, 1, w), F32), jax.ShapeDtypeStruct((bd, 1, w), F32),
                   jax.ShapeDtypeStruct((bd, 1, LANES), F32)],
        compiler_params=_cparams("parallel", "arbitrary"),
        name="decode_attn",
    )(page_table, *args)


def _merge_body(x_ref, oa_ref, z_ref, of_ref, ga_ref, gf_ref, na_ref, pa_ref, pf_ref, wo_ref, gffn_ref,
                wr_ref, br_ref, x1_ref, h2_ref, tw_ref, ti_ref):
    o = oa_ref[...]
    z = z_ref[...]
    na = na_ref[...]
    parts = []
    for h in range(H_A):
        oh = o[:, h * DV_A:(h + 1) * DV_A]
        zh = z[:, h * DV_A:(h + 1) * DV_A]
        on = oh * lax.rsqrt(jnp.mean(oh * oh, axis=-1, keepdims=True) + EPS) * na
        parts.append((on * (zh * _sigmoid(zh))).astype(BF16))
    o_a = jnp.concatenate(parts, axis=-1)
    ya = _dot(o_a, pa_ref[...])
    yf = _dot(of_ref[...].astype(BF16), pf_ref[...])
    mixed = _sigmoid(ga_ref[...]) * ya + _sigmoid(gf_ref[...]) * yf
    x1 = x_ref[...] + _dot(mixed.astype(BF16), wo_ref[...])
    x1_ref[...] = x1
    h2 = x1 * lax.rsqrt(jnp.mean(x1 * x1, axis=-1, keepdims=True) + EPS) * gffn_ref[...]
    h2_ref[...] = h2
    logits = _dot(h2.astype(BF16), wr_ref[...]) + br_ref[...]
    lane = lax.broadcasted_iota(jnp.int32, logits.shape, 1)
    l = jnp.where(lane < N_EXP, logits, -jnp.inf)
    vals, idxs = [], []
    for _ in range(TOP_K):
        m = jnp.max(l, axis=-1, keepdims=True)
        idx = jnp.min(jnp.where(l == m, lane, LANES), axis=-1, keepdims=True)
        vals.append(m)
        idxs.append(idx)
        l = jnp.where(lane == idx, -jnp.inf, l)
    es = [jnp.exp(v - vals[0]) for v in vals]
    den = es[0]
    for e in es[1:]:
        den = den + e
    tw = jnp.zeros(logits.shape, F32)
    ti = jnp.zeros(logits.shape, jnp.int32)
    for kk in range(TOP_K):
        tw = jnp.where(lane == kk, es[kk] / den, tw)
        ti = jnp.where(lane == kk, idxs[kk], ti)
    tw_ref[...] = tw
    ti_ref[...] = ti


def _merge(x3, oa3, z3, of3, ga3, gf3, na_row, pa, pf, wo, gffn_row, wr, br_row):
    g, t, d = x3.shape
    tm = _row_tile(t, 384)
    tok = lambda c: pl.BlockSpec((None, tm, c), lambda i, j: (i, j, 0))
    const = lambda a: pl.BlockSpec(a.shape, lambda i, j: (0,) * a.ndim, pipeline_mode=pl.Buffered(1))
    wa = H_A * DV_A
    wf = H_F * HD_F
    return pl.pallas_call(
        _merge_body,
        grid=(g, t // tm),
        in_specs=[tok(d), tok(wa), tok(wa), tok(wf), tok(d), tok(d),
                  const(na_row), const(pa), const(pf), const(wo), const(gffn_row),
                  const(wr), const(br_row)],
        out_specs=[tok(d), tok(d), tok(LANES), tok(LANES)],
        out_shape=[jax.ShapeDtypeStruct((g, t, d), F32), jax.ShapeDtypeStruct((g, t, d), F32),
                   jax.ShapeDtypeStruct((g, t, LANES), F32), jax.ShapeDtypeStruct((g, t, LANES), jnp.int32)],
        compiler_params=_cparams("parallel", "parallel"),
        name="merge",
    )(x3, oa3, z3, of3, ga3, gf3, na_row, pa, pf, wo, gffn_row, wr, br_row)


def _moe_body(be_ref, nv_ref, x_ref, wg_ref, bg_ref, wu_ref, bu_ref, wd_ref, bd_ref, y_ref, wg_b, wu_b, wd_b):
    i = pl.program_id(0)

    @pl.when(jnp.logical_or(i == 0, be_ref[i] != be_ref[jnp.maximum(i - 1, 0)]))
    def _():
        wg_b[...] = wg_ref[...].astype(BF16)
        wu_b[...] = wu_ref[...].astype(BF16)
        wd_b[...] = wd_ref[...].astype(BF16)

    @pl.when(i < nv_ref[0])
    def _():
        x = x_ref[...].astype(BF16)
        gate = jnp.minimum(_dot(x, wg_b[...]) + bg_ref[...], SWIGLU_LIMIT)
        up = jnp.clip(_dot(x, wu_b[...]) + bu_ref[...], -SWIGLU_LIMIT, SWIGLU_LIMIT)
        act = (up + 1.0) * gate * _sigmoid(SWIGLU_ALPHA * gate)
        y_ref[...] = _dot(act.astype(BF16), wd_b[...]) + bd_ref[...]

    @pl.when(i >= nv_ref[0])
    def _():
        y_ref[...] = jnp.zeros(y_ref.shape, y_ref.dtype)


def _moe_experts(blk_e, n_valid, xs, wg, bg, wu, bu, wd, bd):
    n_rows, d = xs.shape
    bm = MOE_BLOCK
    n_blk = n_rows // bm
    de = wg.shape[-1]
    row_idx = lambda i, be, nv: (jnp.minimum(i, nv[0] - 1), 0)
    wspec = lambda a, b: pl.BlockSpec((None, a, b), lambda i, be, nv: (be[i], 0, 0))
    grid_spec = pltpu.PrefetchScalarGridSpec(
        num_scalar_prefetch=2,
        grid=(n_blk,),
        in_specs=[pl.BlockSpec((bm, d), row_idx),
                  wspec(d, de), wspec(1, de), wspec(d, de), wspec(1, de), wspec(de, d), wspec(1, d)],
        out_specs=pl.BlockSpec((bm, d), lambda i, be, nv: (i, 0)),
        scratch_shapes=[pltpu.VMEM((d, de), BF16), pltpu.VMEM((d, de), BF16), pltpu.VMEM((de, d), BF16)],
    )
    return pl.pallas_call(
        _moe_body,
        grid_spec=grid_spec,
        out_shape=jax.ShapeDtypeStruct((n_rows, d), F32),
        compiler_params=_cparams("arbitrary"),
        name="moe_experts",
    )(blk_e, n_valid, xs, wg, bg, wu, bu, wd, bd)


def _moe(h2, top_w, top_i, wg, bg, wu, bu, wd, bd):
    n, d = h2.shape
    bm = MOE_BLOCK
    m = n * TOP_K
    n_blk = -(-(m + N_EXP * (bm - 1)) // bm)
    flat_e = top_i.reshape(-1)
    onehot = (flat_e[:, None] == jnp.arange(N_EXP, dtype=jnp.int32)[None, :]).astype(jnp.int32)
    csum = jnp.cumsum(onehot, axis=0)
    rank = jnp.take_along_axis(csum, flat_e[:, None], axis=1)[:, 0] - 1
    counts = csum[-1]
    padded = (counts + bm - 1) // bm * bm
    pend = jnp.cumsum(padded)
    pstart = pend - padded
    dest = pstart[flat_e] + rank
    n_valid = (pend[-1] // bm).astype(jnp.int32)
    blk_ids = jnp.arange(n_blk, dtype=jnp.int32)
    expert_of = lambda blk: jnp.minimum(jnp.sum((pend[None, :] <= (blk * bm)[:, None]).astype(jnp.int32), axis=1), N_EXP - 1)
    blk_e = expert_of(jnp.minimum(blk_ids, n_valid - 1))
    shift = (m - 1).bit_length()
    assert N_EXP << shift < 2 ** 31
    pair_sorted = jnp.sort((flat_e << shift) | jnp.arange(m, dtype=jnp.int32)) & ((1 << shift) - 1)
    e_of_blk = expert_of(blk_ids)
    start = jnp.cumsum(counts) - counts
    rank = (blk_ids * bm - pstart[e_of_blk])[:, None] + jnp.arange(bm, dtype=jnp.int32)[None, :]
    src = jnp.clip(start[e_of_blk][:, None] + rank, 0, m - 1)
    row_tok = jnp.where(rank < counts[e_of_blk][:, None], pair_sorted[src.reshape(-1)].reshape(n_blk, bm) // TOP_K, n)
    row_tok = row_tok.reshape(-1)
    h_ext = jnp.concatenate([h2, jnp.zeros((1, d), h2.dtype)], axis=0)
    xs = h_ext[row_tok]
    y = _moe_experts(blk_e, n_valid.reshape(1), xs, wg, bg, wu, bu, wd, bd)
    yg = y[dest.reshape(n, TOP_K).T.reshape(-1)].reshape(TOP_K, n, d)
    return jnp.sum(yg * top_w.T[:, :, None], axis=0)


def _lane_row(vals, offset, width=LANES):
    return jnp.zeros((1, width), F32).at[0, offset:offset + vals.shape[0]].set(vals.astype(F32))


def kernel(x_prompt, x_sample, cache_k, cache_v, cache_logf, state_delta, state_conv, page_table,
           meta_tokens, g_mix, w_in, conv_w, a_log, dt_bias, norm_a, qn_g, kn_g, b_forget,
           p_a, p_f, w_o, g_ffn, w_router, b_router, w_gate, b_gate, w_up, b_up, w_down, b_down):
    depth = w_in.shape[0]
    b, seq, d = x_prompt.shape
    bd = x_sample.shape[0]
    assert x_sample.shape[1] == 1 and DK_A == LANES and DV_A == LANES
    t = N_META + seq
    lead = (-N_META) % CHUNK
    t_pad = -(-t // ATT_BLOCK) * ATT_BLOCK
    qkv_w, va_w, wf = 2 * H_A * DK_A + H_A * DV_A, H_A * DV_A, H_F * HD_F
    sizes = (qkv_w, va_w, H_A, H_A, wf, wf, wf, H_F, d, d)
    offs = [0]
    for s_ in sizes:
        offs.append(offs[-1] + s_)
    col = lambda i: slice(offs[i], offs[i + 1])
    widths = (qkv_w, va_w, wf, wf, wf, d, d)

    xp = jnp.concatenate([jnp.broadcast_to(meta_tokens.astype(x_prompt.dtype)[None], (b, N_META, d)), x_prompt], axis=1)
    xs = x_sample
    new_p = [[] for _ in range(5)]
    new_s = [[] for _ in range(5)]
    for l in range(depth):
        wl = w_in[l]
        w_main = jnp.concatenate([wl[:, col(0)], wl[:, col(1)], wl[:, col(4)], wl[:, col(5)], wl[:, col(6)],
                                  wl[:, col(8)], wl[:, col(9)]], axis=1).astype(BF16)
        w_small = jnp.concatenate([wl[:, col(2)], wl[:, col(3)], wl[:, col(7)],
                                   jnp.zeros((d, LANES - 2 * H_A - H_F), F32)], axis=1).astype(BF16)
        g_row = g_mix[l].reshape(1, d)
        a_row = _lane_row(a_log[l], H_A)
        dt_row = _lane_row(dt_bias[l], H_A)
        bf_row = _lane_row(b_forget[l], 2 * H_A)
        qg_pair = jnp.tile(qn_g[l], LANES // HD_F).reshape(1, LANES)
        kg_pair = jnp.tile(kn_g[l], LANES // HD_F).reshape(1, LANES)
        qg_full = jnp.tile(qn_g[l], H_F).reshape(1, wf)
        kg_full = jnp.tile(kn_g[l], H_F).reshape(1, wf)
        na_row = norm_a[l].reshape(1, DV_A)
        pa_b, pf_b, wo_b = p_a[l].astype(BF16), p_f[l].astype(BF16), w_o[l].astype(BF16)
        gffn_row = g_ffn[l].reshape(1, d)
        wr = jnp.pad(w_router[l], ((0, 0), (0, LANES - N_EXP))).astype(BF16)
        br_row = _lane_row(b_router[l], 0)
        merge_w = (na_row, pa_b, pf_b, wo_b, gffn_row, wr, br_row)

        qkv_p, z_p, qf_p, kf_p, vf_p, ga_p, gf_p, sm_p = _in_proj(xp.reshape(b * t, d), g_row, w_main, w_small, widths)
        r3 = lambda a: a.reshape(b, t, a.shape[-1])
        qkv3, sm3 = r3(qkv_p), r3(sm_p)
        conv_p = qkv3[:, t - (CONV_W - 1):, :]
        prep = _conv_prep(qkv3, conv_w[l], jnp.zeros((b, CONV_W - 1, qkv_w), F32))
        oa_p, s_p = _delta_scan(prep, sm3, a_row, dt_row, jnp.zeros((b, H_A, DK_A, DV_A), F32), lead)
        lf_p, c_col = _logf_cumsum(sm3, bf_row, t_pad)
        qx, kx, vb, kn_p = _fox_prep(r3(qf_p), r3(kf_p), r3(vf_p), c_col, qg_pair, kg_pair, t_pad)
        of_p = _fox_attn(qx, kx, vb)
        x1_p, h2_p, tw_p, ti_p = _merge(xp, oa_p, r3(z_p), of_p, r3(ga_p), r3(gf_p), *merge_w)

        qkv_s, z_s, qf_s, kf_s, vf_s, ga_s, gf_s, sm_s = _in_proj(xs.reshape(bd, d), g_row, w_main, w_small, widths)
        s3 = lambda a: a.reshape(bd, 1, a.shape[-1])
        oa_s, s_s, conv_s = _decode_delta(s3(qkv_s), state_conv[l], s3(sm_s), conv_w[l], a_row, dt_row, state_delta[l])
        n_pool, page = cache_k.shape[1], cache_k.shape[2]
        of_s, kn_s, lf_s = _decode_attn(page_table, s3(qf_s), s3(kf_s), s3(vf_s), s3(sm_s), qg_full, kg_full,
                                        _lane_row(b_forget[l], 2 * H_A),
                                        jnp.transpose(cache_k[l], (0, 2, 3, 1)).reshape(n_pool, wf, page),
                                        jnp.transpose(cache_v[l], (0, 2, 3, 1)).reshape(n_pool, wf, page),
                                        jnp.swapaxes(cache_logf[l], 1, 2))
        g1 = lambda a: a.reshape(1, bd, a.shape[-1])
        x1_s, h2_s, tw_s, ti_s = _merge(g1(xs), g1(oa_s), g1(z_s), g1(of_s), g1(ga_s), g1(gf_s), *merge_w)

        n_p = b * t
        h2_all = jnp.concatenate([h2_p.reshape(n_p, d), h2_s.reshape(bd, d)], axis=0)
        tw_all = jnp.concatenate([tw_p.reshape(n_p, LANES), tw_s.reshape(bd, LANES)], axis=0)[:, :TOP_K]
        ti_all = jnp.concatenate([ti_p.reshape(n_p, LANES), ti_s.reshape(bd, LANES)], axis=0)[:, :TOP_K]
        y_all = _moe(h2_all, tw_all, ti_all, w_gate[l], b_gate[l][:, None, :],
                     w_up[l], b_up[l][:, None, :], w_down[l], b_down[l][:, None, :])
        xp = x1_p + y_all[:n_p].reshape(b, t, d)
        xs = x1_s.reshape(bd, 1, d) + y_all[n_p:].reshape(bd, 1, d)

        st_p = (kn_p.reshape(b, t, H_F, HD_F), vf_p.reshape(b, t, H_F, HD_F),
                lf_p[:, :, 2 * H_A:2 * H_A + H_F], s_p, conv_p)
        st_s = (kn_s.reshape(bd, 1, H_F, HD_F), vf_s.reshape(bd, 1, H_F, HD_F),
                lf_s[:, :, 2 * H_A:2 * H_A + H_F], s_s, conv_s)
        for lst, a in zip(new_p, st_p):
            lst.append(a)
        for lst, a in zip(new_s, st_s):
            lst.append(a)
    k_p, v_p, lf_pp, d_p, c_p = (jnp.stack(a) for a in new_p)
    k_s, v_s, lf_ss, d_s, c_s = (jnp.stack(a) for a in new_s)
    return (xp[:, N_META:], xs, k_p, v_p, lf_pp, k_s, v_s, lf_ss, d_p, d_s, c_p, c_s)
```

```python
import functools

import jax
import jax.numpy as jnp
from jax import lax
from jax.experimental import pallas as pl
from jax.experimental.pallas import tpu as pltpu

F32 = jnp.float32
BF16 = jnp.bfloat16

N_META = 16
H_A = 4
DK_A = 128
DV_A = 128
CONV_W = 4
CHUNK = 64
H_F = 8
HD_F = 64
N_EXP = 32
TOP_K = 4
SWIGLU_LIMIT = 7.0
SWIGLU_ALPHA = 1.702
EPS = 1e-6
LOG2E = 1.4426950408889634

LANES = 128
F_OFF = 2 * H_A
HP = 16
ATT_BLOCK = 256
MOE_BLOCK = 256
PAGES_PER_STEP = 16
VMEM_LIMIT = 48 * 1024 * 1024


def _cparams(*sem):
    return pltpu.CompilerParams(dimension_semantics=sem, vmem_limit_bytes=VMEM_LIMIT)


def _row_tile(n, cap):
    if n <= cap:
        return n
    best = None
    for t in range(8, cap + 1, 8):
        if n % t == 0:
            best = t
    assert best is not None, n
    return best


def _dot(a, b):
    return jnp.dot(a, b, preferred_element_type=F32)


def _dot_nt(a, b):
    return lax.dot_general(a, b, (((1,), (1,)), ((), ())), preferred_element_type=F32)


def _split2(x):
    hi = x.astype(BF16)
    lo = (x - hi.astype(F32)).astype(BF16)
    return hi, lo


def _dot_x3(a, b):
    a_hi, a_lo = _split2(a)
    b_hi, b_lo = _split2(b)
    return _dot(a_hi, b_hi) + (_dot(a_hi, b_lo) + _dot(a_lo, b_hi))


def _split3(x):
    hi = x.astype(BF16)
    r = x - hi.astype(F32)
    mid = r.astype(BF16)
    lo = (r - mid.astype(F32)).astype(BF16)
    return hi, mid, lo


def _sigmoid(x):
    return 1.0 / (1.0 + jnp.exp(-x))


def _softplus(x):
    return jnp.maximum(x, 0.0) + jnp.log1p(jnp.exp(-jnp.abs(x)))


def _log_sigmoid(x):
    return jnp.minimum(x, 0.0) - jnp.log1p(jnp.exp(-jnp.abs(x)))


IN_WIDTHS = (2 * H_A * DK_A + H_A * DV_A, H_A * DV_A, H_F * HD_F, H_F * HD_F, H_F * HD_F)
COL_CHUNK = 512


def _in_proj_body(x_ref, g_ref, wm_ref, ws_ref, *out_refs, widths):
    x = x_ref[...]
    h = x * lax.rsqrt(jnp.mean(x * x, axis=-1, keepdims=True) + EPS) * g_ref[...]
    hb = h.astype(BF16)
    col = 0
    for o_ref, w in zip(out_refs[:-1], widths):
        for c0 in range(0, w, COL_CHUNK):
            o_ref[:, c0:c0 + COL_CHUNK] = _dot(hb, wm_ref[:, col + c0:col + c0 + COL_CHUNK])
        col += w
    out_refs[-1][...] = _dot(hb, ws_ref[...])


def _in_proj(x2d, g_row, w_main, w_small, widths):
    n, d = x2d.shape
    tm = _row_tile(n, 512)
    outs = [jax.ShapeDtypeStruct((n, w), F32) for w in widths] + [jax.ShapeDtypeStruct((n, LANES), F32)]
    out_specs = [pl.BlockSpec((tm, w), lambda i: (i, 0)) for w in widths] + [pl.BlockSpec((tm, LANES), lambda i: (i, 0))]
    return pl.pallas_call(
        functools.partial(_in_proj_body, widths=widths),
        grid=(n // tm,),
        in_specs=[pl.BlockSpec((tm, d), lambda i: (i, 0)),
                  pl.BlockSpec((1, d), lambda i: (0, 0)),
                  pl.BlockSpec(w_main.shape, lambda i: (0, 0), pipeline_mode=pl.Buffered(1)),
                  pl.BlockSpec(w_small.shape, lambda i: (0, 0), pipeline_mode=pl.Buffered(1))],
        out_specs=out_specs,
        out_shape=outs,
        compiler_params=_cparams("parallel"),
        name="in_proj",
    )(x2d, g_row, w_main, w_small)


def _conv_prep_body(x_ref, cw_ref, prev_ref, o_ref, buf, *, t_len, rows):
    j = pl.program_id(1)
    pad = 8
    buf[0:pad, :] = jnp.zeros((pad, LANES), F32)
    buf[pad - (CONV_W - 1):pad, :] = prev_ref[...]
    buf[pad:pad + t_len, :] = x_ref[...]
    cw = cw_ref[...]
    is_q = j < H_A
    is_qk = j < 2 * H_A
    for r0 in range(0, t_len, rows):
        y = jnp.zeros((rows, LANES), F32)
        for w in range(CONV_W):
            off = pad - (CONV_W - 1) + w + r0
            y = y + buf[off:off + rows, :] * cw[w:w + 1, :]
        y = y * _sigmoid(y)
        nrm = lax.rsqrt(jnp.sum(y * y, axis=-1, keepdims=True) + EPS)
        f = jnp.where(is_qk, nrm * jnp.where(is_q, DK_A ** -0.5, 1.0), 1.0)
        o_ref[r0:r0 + rows, :] = y * f


def _conv_prep(qkv3, conv_w, conv_prev):
    b, t, c = qkv3.shape
    rows = _row_tile(t, 512)
    return pl.pallas_call(
        functools.partial(_conv_prep_body, t_len=t, rows=rows),
        grid=(b, c // LANES),
        in_specs=[pl.BlockSpec((None, t, LANES), lambda i, j: (i, 0, j)),
                  pl.BlockSpec((CONV_W, LANES), lambda i, j: (0, j)),
                  pl.BlockSpec((None, CONV_W - 1, LANES), lambda i, j: (i, 0, j))],
        out_specs=pl.BlockSpec((None, t, LANES), lambda i, j: (i, 0, j)),
        out_shape=jax.ShapeDtypeStruct((b, t, c), F32),
        scratch_shapes=[pltpu.VMEM((8 + t, LANES), F32)],
        compiler_params=_cparams("parallel", "parallel"),
        name="conv_prep",
    )(qkv3, conv_w, conv_prev)


def _gates(sm, a_row, dt_row, h):
    lane = lax.broadcasted_iota(jnp.int32, sm.shape, 1)
    beta_all = _sigmoid(sm)
    g_all = -jnp.exp(a_row) * _softplus(sm + dt_row)
    beta = jnp.sum(jnp.where(lane == h, beta_all, 0.0), axis=-1, keepdims=True)
    g = jnp.sum(jnp.where(lane == H_A + h, g_all, 0.0), axis=-1, keepdims=True)
    return beta, g


def _decay_terms(gs):
    c = gs[0].shape[0]
    ri = lax.broadcasted_iota(jnp.int32, (c, c), 0)
    ci = lax.broadcasted_iota(jnp.int32, (c, c), 1)
    g_cum, decay = [], []
    for g in gs:
        g_row = jnp.sum(jnp.where(ri == ci, g, 0.0), axis=0, keepdims=True)
        gc = jnp.sum(jnp.where(ci <= ri, g_row, 0.0), axis=1, keepdims=True)
        gc_row = jnp.sum(jnp.where(ri <= ci, g, 0.0), axis=0, keepdims=True)
        g_cum.append(gc)
        decay.append(jnp.exp(jnp.where(ci <= ri, gc - gc_row, -jnp.inf)))
    return g_cum, decay


def _delta_step(cur, nxt, s_ref):
    o = inv_n = None
    stages = []
    if cur is not None:
        qs, ks, vs, betas, gs, inv = cur
        rng = range(len(qs))
        c = qs[0].shape[0]
        g_cum, decay = _decay_terms(gs)
        e_g = [jnp.exp(g_cum[i]) for i in rng]
        s = [s_ref[i] for i in rng]
        sb = [s[i].astype(BF16) for i in rng]
        qb = [qs[i].astype(BF16) for i in rng]
        kb = [ks[i].astype(BF16) for i in rng]
        k_s = [_dot(kb[i], sb[i]) for i in rng]
        val = {}

        def st_delta():
            rhs = [betas[i] * (vs[i] - e_g[i] * k_s[i]) for i in rng]
            val["db"] = [_dot_x3(inv[i], rhs[i]).astype(BF16) for i in rng]
            val["qk"] = [_dot_nt(qb[i], kb[i]) for i in rng]
            val["q_s"] = [_dot(qb[i], sb[i]) for i in rng]

        def st_out():
            val["o"] = [e_g[i] * val["q_s"][i] + _dot((val["qk"][i] * decay[i]).astype(BF16), val["db"][i]) for i in rng]

        def st_state():
            g_end = [g_cum[i][c - 1:c, :] for i in rng]
            k_dec = [(ks[i] * jnp.exp(g_end[i] - g_cum[i])).T.astype(BF16) for i in rng]
            for i in rng:
                s_ref[i] = jnp.exp(g_end[i]) * s[i] + _dot(k_dec[i], val["db"][i])

        stages = [st_delta, st_out, st_state]
    if nxt is not None:
        ks_n, betas_n, gs_n = nxt
        rng_n = range(len(ks_n))
        c = ks_n[0].shape[0]
        ri = lax.broadcasted_iota(jnp.int32, (c, c), 0)
        ci = lax.broadcasted_iota(jnp.int32, (c, c), 1)
        _, decay_n = _decay_terms(gs_n)
        kb_n = [ks_n[i].astype(BF16) for i in rng_n]
        kk = [_dot_nt(kb_n[i], kb_n[i]) for i in rng_n]
        a = [jnp.where(ci < ri, betas_n[i] * decay_n[i] * kk[i], 0.0) for i in rng_n]
        inv_n = [jnp.where(ri == ci, 1.0, 0.0) - a[i] for i in rng_n]
        pw = a
        for _ in range(max(1, (c - 1).bit_length() - 1)):
            if stages:
                stages.pop(0)()
            pw = [_dot_x3(pw[i], pw[i]) for i in rng_n]
            inv_n = [inv_n[i] + _dot_x3(inv_n[i], pw[i]) for i in rng_n]
    for st in stages:
        st()
    if cur is not None:
        o = val["o"]
    return o, inv_n


def _delta_scan_body(qkv_ref, sm_ref, a_ref, dt_ref, s0_ref, o_ref, s_out_ref, s_ref, *, t_len, lead):
    a_row = a_ref[...]
    dt_row = dt_ref[...]
    s_ref[...] = s0_ref[...]
    first = CHUNK - lead
    n_chunks = (lead + t_len) // CHUNK

    def gates(load, mask_lead):
        sm = load(sm_ref, 0)
        betas, gs = [], []
        for h in range(H_A):
            beta, g = _gates(sm, a_row, dt_row, h)
            if mask_lead:
                row = lax.broadcasted_iota(jnp.int32, (CHUNK, 1), 0)
                beta = jnp.where(row >= lead, beta, 0.0)
                g = jnp.where(row >= lead, g, 0.0)
            betas.append(beta)
            gs.append(g)
        return betas, gs

    def nxt_args(load, mask_lead):
        betas, gs = gates(load, mask_lead)
        return [load(qkv_ref, (H_A + h) * DK_A) for h in range(H_A)], betas, gs

    def cur_args(load, mask_lead, inv):
        betas, gs = gates(load, mask_lead)
        qs = [load(qkv_ref, h * DK_A) for h in range(H_A)]
        ks = [load(qkv_ref, (H_A + h) * DK_A) for h in range(H_A)]
        vs = [load(qkv_ref, 2 * H_A * DK_A + h * DV_A) for h in range(H_A)]
        return qs, ks, vs, betas, gs, inv

    def chunk_rows(c):
        return pl.ds(pl.multiple_of(c * CHUNK - lead, 8), CHUNK)

    def loader(c):
        sl = chunk_rows(c)
        return lambda ref, col: ref[sl, col:col + LANES]

    if lead:
        def load0(ref, col):
            return jnp.concatenate([jnp.zeros((lead, LANES), F32), ref[0:first, col:col + LANES]], axis=0)

        def store0(h, o):
            o_ref[0:first, h * DV_A:(h + 1) * DV_A] = o[lead:, :]

        _, inv0 = _delta_step(None, nxt_args(load0, True), s_ref)
        has_next = n_chunks > 1
        outs, inv1 = _delta_step(cur_args(load0, True, inv0), nxt_args(loader(1), False) if has_next else None, s_ref)
        for h, o in enumerate(outs):
            store0(h, o)
        c_start = 1
    else:
        has_next = n_chunks > 0
        _, inv1 = _delta_step(None, nxt_args(loader(0), False), s_ref) if has_next else (None, None)
        c_start = 0

    def body(c, inv):
        sl = chunk_rows(c)
        outs, inv_n = _delta_step(cur_args(loader(c), False, inv),
                                  nxt_args(loader(jnp.minimum(c + 1, n_chunks - 1)), False), s_ref)
        for h, o in enumerate(outs):
            o_ref[sl, h * DV_A:(h + 1) * DV_A] = o
        return tuple(inv_n)

    if has_next:
        lax.fori_loop(c_start, n_chunks, body, tuple(inv1))
    s_out_ref[...] = s_ref[...]


def _delta_scan(qkv3, small3, a_row, dt_row, s0, lead):
    b, t, c = qkv3.shape
    assert (lead + t) % CHUNK == 0 and lead % 8 == 0
    return pl.pallas_call(
        functools.partial(_delta_scan_body, t_len=t, lead=lead),
        grid=(b,),
        in_specs=[pl.BlockSpec((None, t, c), lambda i: (i, 0, 0)),
                  pl.BlockSpec((None, t, LANES), lambda i: (i, 0, 0)),
                  pl.BlockSpec((1, LANES), lambda i: (0, 0)),
                  pl.BlockSpec((1, LANES), lambda i: (0, 0)),
                  pl.BlockSpec((None, H_A, DK_A, DV_A), lambda i: (i, 0, 0, 0))],
        out_specs=[pl.BlockSpec((None, t, H_A * DV_A), lambda i: (i, 0, 0)),
                   pl.BlockSpec((None, H_A, DK_A, DV_A), lambda i: (i, 0, 0, 0))],
        out_shape=[jax.ShapeDtypeStruct((b, t, H_A * DV_A), F32),
                   jax.ShapeDtypeStruct((b, H_A, DK_A, DV_A), F32)],
        scratch_shapes=[pltpu.VMEM((H_A, DK_A, DV_A), F32)],
        compiler_params=_cparams("parallel"),
        name="delta_scan",
    )(qkv3, small3, a_row, dt_row, s0)


def _group_rms(x, gain, seg):
    hi, lo = _split2(x * x)
    ss = _dot(hi, seg) + _dot(lo, seg)
    return x * lax.rsqrt(ss * (1.0 / HD_F) + EPS) * gain


def _seg_matrix(n):
    r = lax.broadcasted_iota(jnp.int32, (n, n), 0) // HD_F
    c = lax.broadcasted_iota(jnp.int32, (n, n), 1) // HD_F
    return jnp.where(r == c, 1.0, 0.0).astype(BF16)


def _fox_prep_body(q_ref, k_ref, v_ref, c_ref, qg_ref, kg_ref, qx_ref, kx_ref, vt_ref, kn_ref, *, t_len, t_pad):
    p = pl.program_id(1)
    heads = LANES // HD_F

    def feature_major(x):
        if t_pad > t_len:
            x = jnp.concatenate([x, jnp.zeros((t_pad - t_len, LANES), F32)], axis=0)
        return x.T.astype(BF16)

    seg = _seg_matrix(LANES)
    qn = _group_rms(q_ref[...], qg_ref[...], seg) * (HD_F ** -0.5 * LOG2E)
    kn = _group_rms(k_ref[...], kg_ref[...], seg)
    kn_ref[...] = kn
    c_all = c_ref[...] * LOG2E
    lane = lax.broadcasted_iota(jnp.int32, (t_len, LANES), 1)
    for hh in range(heads):
        c = jnp.sum(jnp.where(lane == F_OFF + p * heads + hh, c_all, 0.0), axis=-1, keepdims=True)
        pieces = [x.astype(F32) for x in _split3(c)]
        own = (lane // HD_F) == hh
        f0 = ((hh + 1) % heads) * HD_F
        qx = jnp.where(own, qn, 0.0)
        kx = jnp.where(own, kn, 0.0)
        for n, piece in enumerate(pieces):
            qx = jnp.where(lane == f0 + n, piece, qx)
            kx = jnp.where(lane == f0 + n, 1.0, kx)
            qx = jnp.where(lane == f0 + 3 + n, 1.0, qx)
            kx = jnp.where(lane == f0 + 3 + n, -piece, kx)
        kx_ref[0:t_len, hh * LANES:(hh + 1) * LANES] = kx.astype(BF16)
        qx_ref[hh * LANES:(hh + 1) * LANES, :] = feature_major(qx)
    vt_ref[...] = feature_major(v_ref[...])
    if t_pad > t_len:
        kx_ref[t_len:, :] = jnp.zeros((t_pad - t_len, heads * LANES), BF16)


def _fox_prep(q3, k3, v3, c_col, qg_row, kg_row, t_pad):
    b, t, w = q3.shape
    heads = LANES // HD_F
    blk = pl.BlockSpec((None, t, LANES), lambda i, p: (i, 0, p))
    gblk = pl.BlockSpec((1, LANES), lambda i, p: (0, 0))
    return pl.pallas_call(
        functools.partial(_fox_prep_body, t_len=t, t_pad=t_pad),
        grid=(b, w // LANES),
        in_specs=[blk, blk, blk, pl.BlockSpec((None, t, LANES), lambda i, p: (i, 0, 0)), gblk, gblk],
        out_specs=[pl.BlockSpec((None, heads * LANES, t_pad), lambda i, p: (i, p, 0)),
                   pl.BlockSpec((None, t_pad, heads * LANES), lambda i, p: (i, 0, p)),
                   pl.BlockSpec((None, LANES, t_pad), lambda i, p: (i, p, 0)), blk],
        out_shape=[jax.ShapeDtypeStruct((b, H_F * LANES, t_pad), BF16), jax.ShapeDtypeStruct((b, t_pad, H_F * LANES), BF16),
                   jax.ShapeDtypeStruct((b, w, t_pad), BF16), jax.ShapeDtypeStruct((b, t, w), F32)],
        compiler_params=_cparams("parallel", "parallel"),
        name="fox_prep",
    )(q3, k3, v3, c_col, qg_row, kg_row)


def _logf_cumsum_body(sm_ref, bf_ref, lf_ref, ccol_ref, buf, *, t_len, t_pad):
    lf = _log_sigmoid(sm_ref[...] + bf_ref[...])
    lf_ref[...] = lf
    buf[0:t_len, :] = lf
    if t_pad > t_len:
        buf[t_len:, :] = jnp.zeros((t_pad - t_len, LANES), F32)
    blk = ATT_BLOCK
    ri = lax.broadcasted_iota(jnp.int32, (blk, blk), 0)
    ci = lax.broadcasted_iota(jnp.int32, (blk, blk), 1)
    tri = jnp.where(ci <= ri, 1.0, 0.0).astype(BF16)
    carry = jnp.zeros((1, LANES), F32)
    for i in range(t_pad // blk):
        hi, mid, lo = _split3(buf[i * blk:(i + 1) * blk, :])
        c = _dot(tri, hi) + _dot(tri, mid) + _dot(tri, lo) + carry
        ccol_ref[i * blk:(i + 1) * blk, :] = c
        carry = c[blk - 1:blk, :]


def _logf_cumsum(small3, bf_row, t_pad):
    b, t, _ = small3.shape
    return pl.pallas_call(
        functools.partial(_logf_cumsum_body, t_len=t, t_pad=t_pad),
        grid=(b,),
        in_specs=[pl.BlockSpec((None, t, LANES), lambda i: (i, 0, 0)),
                  pl.BlockSpec((1, LANES), lambda i: (0, 0))],
        out_specs=[pl.BlockSpec((None, t, LANES), lambda i: (i, 0, 0)),
                   pl.BlockSpec((None, t_pad, LANES), lambda i: (i, 0, 0))],
        out_shape=[jax.ShapeDtypeStruct((b, t, LANES), F32),
                   jax.ShapeDtypeStruct((b, t_pad, LANES), F32)],
        scratch_shapes=[pltpu.VMEM((t_pad, LANES), F32)],
        compiler_params=_cparams("parallel"),
        name="logf_cumsum",
    )(small3, bf_row)


def _fox_attn_body(qt_ref, k_ref, vt_ref, o_ref):
    i = pl.program_id(1)
    blk = ATT_BLOCK
    key_le_query = (lax.broadcasted_iota(jnp.int32, (blk, blk), 0)
                    <= lax.broadcasted_iota(jnp.int32, (blk, blk), 1))

    def step(j, carry, diagonal):
        ms, ls, accs = carry
        k0 = pl.multiple_of(j * blk, blk)
        st = []
        for h in range(H_F):
            s = _dot(k_ref[pl.ds(k0, blk), h * LANES:(h + 1) * LANES], qt_ref[h * LANES:(h + 1) * LANES, :])
            st.append(jnp.where(key_le_query, s, -jnp.inf) if diagonal else s)
        new_ms, new_ls, alphas, pts = [], [], [], []
        for h in range(H_F):
            m_new = jnp.maximum(ms[h], jnp.max(st[h], axis=0, keepdims=True))
            alpha = jnp.exp2(ms[h] - m_new)
            pt = jnp.exp2(st[h] - m_new)
            new_ls.append(alpha * ls[h] + jnp.sum(pt, axis=0, keepdims=True))
            new_ms.append(m_new)
            alphas.append(alpha)
            pts.append(pt.astype(BF16))
        new_accs = [alphas[h] * accs[h] + _dot(vt_ref[h * HD_F:(h + 1) * HD_F, pl.ds(k0, blk)], pts[h])
                    for h in range(H_F)]
        return tuple(new_ms), tuple(new_ls), tuple(new_accs)

    init = (tuple(jnp.full((1, blk), -jnp.inf, F32) for _ in range(H_F)),
            tuple(jnp.zeros((1, blk), F32) for _ in range(H_F)),
            tuple(jnp.zeros((HD_F, blk), F32) for _ in range(H_F)))
    carry = lax.fori_loop(0, i, lambda j, c: step(j, c, False), init)
    ms, ls, accs = step(i, carry, True)
    heads = LANES // HD_F
    for p in range(H_F // heads):
        ot = jnp.concatenate([accs[p * heads + hh] * (1.0 / ls[p * heads + hh]) for hh in range(heads)], axis=0)
        o_ref[:, p * LANES:(p + 1) * LANES] = ot.T.astype(o_ref.dtype)


def _fox_attn(qt, kx, vt):
    b, w, t_pad = vt.shape
    blk = ATT_BLOCK
    return pl.pallas_call(
        _fox_attn_body,
        grid=(b, t_pad // blk),
        in_specs=[pl.BlockSpec((None, H_F * LANES, blk), lambda n, i: (n, 0, i)),
                  pl.BlockSpec((None, t_pad, H_F * LANES), lambda n, i: (n, 0, 0)),
                  pl.BlockSpec((None, w, t_pad), lambda n, i: (n, 0, 0))],
        out_specs=pl.BlockSpec((None, blk, w), lambda n, i: (n, i, 0)),
        out_shape=jax.ShapeDtypeStruct((b, t_pad, w), BF16),
        compiler_params=_cparams("parallel", "arbitrary"),
        name="fox_attn",
    )(qt, kx, vt)


def _decode_delta_body(x_ref, sc_ref, sm_ref, cw_ref, a_ref, dt_ref, s_ref, o_ref, s_out_ref, conv_ref):
    x_new = x_ref[...]
    sc = sc_ref[...]
    cw = cw_ref[...]
    y = x_new * cw[CONV_W - 1:CONV_W, :]
    for w in range(CONV_W - 1):
        y = y + sc[w:w + 1, :] * cw[w:w + 1, :]
    y = y * _sigmoid(y)
    conv_ref[0:CONV_W - 2, :] = sc[1:, :]
    conv_ref[CONV_W - 2:CONV_W - 1, :] = x_new
    sm = sm_ref[...]
    lane = lax.broadcasted_iota(jnp.int32, (1, LANES), 1)
    beta_all = _sigmoid(sm)
    g_all = -jnp.exp(a_ref[...]) * _softplus(sm + dt_ref[...])
    qk_w = H_A * DK_A
    row0 = lax.broadcasted_iota(jnp.int32, (LANES, LANES), 0) == 0

    def in_row0(r):
        return jnp.where(row0, jnp.broadcast_to(r, (LANES, LANES)), 0.0)

    for h in range(H_A):
        q = y[:, h * DK_A:(h + 1) * DK_A]
        k = y[:, qk_w + h * DK_A:qk_w + (h + 1) * DK_A]
        v = y[:, 2 * qk_w + h * DV_A:2 * qk_w + (h + 1) * DV_A]
        q = q * lax.rsqrt(jnp.sum(q * q, axis=-1, keepdims=True) + EPS) * (DK_A ** -0.5)
        k = k * lax.rsqrt(jnp.sum(k * k, axis=-1, keepdims=True) + EPS)
        beta = jnp.sum(jnp.where(lane == h, beta_all, 0.0), axis=-1, keepdims=True)
        g = jnp.sum(jnp.where(lane == H_A + h, g_all, 0.0), axis=-1, keepdims=True)
        e_g = jnp.exp(g)
        s = s_ref[h]
        sb = s.astype(BF16)
        k_sq = in_row0(k)
        k_s = _dot(k_sq.astype(BF16), sb)[0:1, :]
        q_s = _dot(in_row0(q).astype(BF16), sb)[0:1, :]
        delta = beta * (v - e_g * k_s)
        qk = jnp.sum(q * k, axis=-1, keepdims=True)
        o_ref[:, h * DV_A:(h + 1) * DV_A] = e_g * q_s + qk * delta
        s_out_ref[h] = e_g * s + _dot_x3(k_sq.T, in_row0(delta))


def _decode_delta(qkv3, state_conv, small3, conv_w, a_row, dt_row, s0):
    bd, _, c = qkv3.shape
    return pl.pallas_call(
        _decode_delta_body,
        grid=(bd,),
        in_specs=[pl.BlockSpec((None, 1, c), lambda i: (i, 0, 0)),
                  pl.BlockSpec((None, CONV_W - 1, c), lambda i: (i, 0, 0)),
                  pl.BlockSpec((None, 1, LANES), lambda i: (i, 0, 0)),
                  pl.BlockSpec((CONV_W, c), lambda i: (0, 0)),
                  pl.BlockSpec((1, LANES), lambda i: (0, 0)),
                  pl.BlockSpec((1, LANES), lambda i: (0, 0)),
                  pl.BlockSpec((None, H_A, DK_A, DV_A), lambda i: (i, 0, 0, 0))],
        out_specs=[pl.BlockSpec((None, 1, H_A * DV_A), lambda i: (i, 0, 0)),
                   pl.BlockSpec((None, H_A, DK_A, DV_A), lambda i: (i, 0, 0, 0)),
                   pl.BlockSpec((None, CONV_W - 1, c), lambda i: (i, 0, 0))],
        out_shape=[jax.ShapeDtypeStruct((bd, 1, H_A * DV_A), F32),
                   jax.ShapeDtypeStruct((bd, H_A, DK_A, DV_A), F32),
                   jax.ShapeDtypeStruct((bd, CONV_W - 1, c), F32)],
        compiler_params=_cparams("parallel"),
        name="decode_delta",
    )(qkv3, state_conv, small3, conv_w, a_row, dt_row, s0)


def _decode_attn_body(pt_ref, q_ref, k_ref, v_ref, sm_ref, qg_ref, kg_ref, bf_ref, *rest, n_pg):
    page_refs = rest[:3 * n_pg]
    o_ref, kn_ref, lf_ref, qrows, vnew, snew, m_ref, l_ref, acc_ref, carry_ref, s_scr = rest[3 * n_pg:]
    step = pl.program_id(1)
    n_steps = pl.num_programs(1) // 2
    w = H_F * HD_F
    row = lax.broadcasted_iota(jnp.int32, (HP, w), 0)
    head_of_lane = lax.broadcasted_iota(jnp.int32, (HP, w), 1) // HD_F

    @pl.when(step == 0)
    def _():
        seg = _seg_matrix(w)

        def rms_rows(x_row, gain):
            xr = jnp.broadcast_to(x_row, (HP, w))
            hi, lo = _split2(xr * xr)
            ss = _dot(hi, seg) + _dot(lo, seg)
            return xr * lax.rsqrt(ss * (1.0 / HD_F) + EPS) * gain

        qn = rms_rows(q_ref[...], qg_ref[...]) * (HD_F ** -0.5)
        kn = rms_rows(k_ref[...], kg_ref[...])
        kn_ref[...] = kn[0:1, :]
        lf = _log_sigmoid(sm_ref[...] + bf_ref[...])
        lf_ref[...] = lf
        q_m = jnp.where(head_of_lane == row, qn, 0.0).astype(BF16)
        qrows[...] = q_m
        vnew[...] = jnp.broadcast_to(v_ref[...], (HP, w)).astype(BF16)
        s_new = jnp.sum(q_m.astype(F32) * kn.astype(BF16).astype(F32), axis=-1, keepdims=True)
        snew[...] = s_new
        m_ref[...] = s_new
        rr = lax.broadcasted_iota(jnp.int32, (HP, LANES), 0)
        ll = lax.broadcasted_iota(jnp.int32, (HP, LANES), 1)
        carry_ref[...] = jnp.sum(jnp.where(ll == rr + F_OFF, jnp.broadcast_to(lf, (HP, LANES)), 0.0),
                                 axis=-1, keepdims=True)

    @pl.when(step < n_steps)
    def _():
        qb = qrows[...]
        lf_all = jnp.concatenate(
            [jnp.concatenate([page_refs[3 * g + 2][...], jnp.zeros((HP - H_F, LANES), F32)], axis=0)
             for g in range(n_pg)], axis=0)
        later = (lax.broadcasted_iota(jnp.int32, (LANES, LANES), 0)
                 > lax.broadcasted_iota(jnp.int32, (LANES, LANES), 1)).astype(BF16)
        hi, mid, lo = _split3(lf_all)
        after = _dot(hi, later) + (_dot(mid, later) + _dot(lo, later))
        totals = jnp.sum(lf_all, axis=-1, keepdims=True)
        raw = [_dot(qb, page_refs[3 * g][...].astype(BF16)) for g in range(n_pg)]
        carry = carry_ref[...]
        m_new = m_ref[...]
        for g in range(n_pg):
            rows = slice(g * HP, (g + 1) * HP)
            s = raw[g] + (carry + after[rows, :])
            s_scr[step * n_pg + g] = s
            m_new = jnp.maximum(m_new, jnp.max(s, axis=-1, keepdims=True))
            carry = carry + totals[rows, :]
        carry_ref[...] = carry
        m_ref[...] = m_new

    @pl.when(step == n_steps)
    def _():
        m = m_ref[...]
        p_new = jnp.exp(snew[...] - m)

        def add(i, tot):
            return tot + jnp.sum(jnp.exp(s_scr[i] - m), axis=-1, keepdims=True)

        l = lax.fori_loop(0, n_steps * n_pg, add, p_new)
        l_ref[...] = l
        acc_ref[...] = (p_new / l).astype(BF16).astype(F32) * vnew[...].astype(F32)

    @pl.when(step >= n_steps)
    def _():
        m = m_ref[...]
        l = l_ref[...]
        probs = [(jnp.exp(s_scr[(step - n_steps) * n_pg + g] - m) / l).astype(BF16) for g in range(n_pg)]
        parts = [_dot_nt(probs[g], page_refs[3 * g + 1][...].astype(BF16)) for g in range(n_pg)]
        acc = acc_ref[...]
        for part in parts:
            acc = acc + part
        acc_ref[...] = acc

    @pl.when(step == pl.num_programs(1) - 1)
    def _():
        o_ref[...] = jnp.sum(jnp.where(head_of_lane == row, acc_ref[...], 0.0), axis=0, keepdims=True)


def _decode_attn(page_table, q3, k3, v3, small3, qg_row, kg_row, bf_row, ck, cv, clf_t):
    bd, _, w = q3.shape
    n_pages = page_table.shape[1]
    page = ck.shape[2]
    assert page == LANES and n_pages % PAGES_PER_STEP == 0 and LANES // HD_F == 2
    n_pg = PAGES_PER_STEP
    steps = n_pages // n_pg

    def page_of(i, s, g, pt):
        return pt[i, n_pages - 1 - (s * n_pg + g)]

    def key_idx(g):
        return lambda i, s, pt: (page_of(i, jnp.minimum(s, steps - 1), g, pt), 0, 0)

    def val_idx(g):
        return lambda i, s, pt: (page_of(i, jnp.maximum(s - steps, 0), g, pt), 0, 0)

    row_spec = lambda width: pl.BlockSpec((None, 1, width), lambda i, s, pt: (i, 0, 0))
    const_spec = lambda width: pl.BlockSpec((1, width), lambda i, s, pt: (0, 0))
    in_specs = [row_spec(w), row_spec(w), row_spec(w), row_spec(LANES), const_spec(w), const_spec(w), const_spec(LANES)]
    args = [q3, k3, v3, small3, qg_row, kg_row, bf_row]
    for g in range(n_pg):
        in_specs += [pl.BlockSpec((None, w, page), key_idx(g)),
                     pl.BlockSpec((None, w, page), val_idx(g)),
                     pl.BlockSpec((None, H_F, page), key_idx(g))]
        args += [ck, cv, clf_t]
    col = lambda: pltpu.VMEM((HP, 1), F32)
    grid_spec = pltpu.PrefetchScalarGridSpec(
        num_scalar_prefetch=1,
        grid=(bd, 2 * steps),
        in_specs=in_specs,
        out_specs=[row_spec(w), row_spec(w), row_spec(LANES)],
        scratch_shapes=[pltpu.VMEM((HP, w), BF16), pltpu.VMEM((HP, w), BF16), col(), col(), col(),
                        pltpu.VMEM((HP, w), F32), col(), pltpu.VMEM((n_pages, HP, LANES), F32)],
    )
    return pl.pallas_call(
        functools.partial(_decode_attn_body, n_pg=n_pg),
        grid_spec=grid_spec,
        out_shape=[jax.ShapeDtypeStruct((bd, 1, w), F32), jax.ShapeDtypeStruct((bd, 1, w), F32),
                   jax.ShapeDtypeStruct((bd, 1, LANES), F32)],
        compiler_params=_cparams("parallel", "arbitrary"),
        name="decode_attn",
    )(page_table, *args)


def _merge_body(x_ref, oa_ref, z_ref, of_ref, ga_ref, gf_ref, na_ref, pa_ref, pf_ref, wo_ref, gffn_ref,
                wr_ref, br_ref, x1_ref, h2_ref, tw_ref, ti_ref):
    o = oa_ref[...]
    z = z_ref[...]
    na = na_ref[...]
    parts = []
    for h in range(H_A):
        oh = o[:, h * DV_A:(h + 1) * DV_A]
        zh = z[:, h * DV_A:(h + 1) * DV_A]
        on = oh * lax.rsqrt(jnp.mean(oh * oh, axis=-1, keepdims=True) + EPS) * na
        parts.append((on * (zh * _sigmoid(zh))).astype(BF16))
    o_a = jnp.concatenate(parts, axis=-1)
    ya = _dot(o_a, pa_ref[...])
    yf = _dot(of_ref[...].astype(BF16), pf_ref[...])
    mixed = _sigmoid(ga_ref[...]) * ya + _sigmoid(gf_ref[...]) * yf
    x1 = x_ref[...] + _dot(mixed.astype(BF16), wo_ref[...])
    x1_ref[...] = x1
    h2 = x1 * lax.rsqrt(jnp.mean(x1 * x1, axis=-1, keepdims=True) + EPS) * gffn_ref[...]
    h2_ref[...] = h2
    logits = _dot(h2.astype(BF16), wr_ref[...]) + br_ref[...]
    lane = lax.broadcasted_iota(jnp.int32, logits.shape, 1)
    l = jnp.where(lane < N_EXP, logits, -jnp.inf)
    vals, idxs = [], []
    for _ in range(TOP_K):
        m = jnp.max(l, axis=-1, keepdims=True)
        idx = jnp.min(jnp.where(l == m, lane, LANES), axis=-1, keepdims=True)
        vals.append(m)
        idxs.append(idx)
        l = jnp.where(lane == idx, -jnp.inf, l)
    es = [jnp.exp(v - vals[0]) for v in vals]
    den = es[0]
    for e in es[1:]:
        den = den + e
    tw = jnp.zeros(logits.shape, F32)
    ti = jnp.zeros(logits.shape, jnp.int32)
    for kk in range(TOP_K):
        tw = jnp.where(lane == kk, es[kk] / den, tw)
        ti = jnp.where(lane == kk, idxs[kk], ti)
    tw_ref[...] = tw
    ti_ref[...] = ti


def _merge(x3, oa3, z3, of3, ga3, gf3, na_row, pa, pf, wo, gffn_row, wr, br_row):
    g, t, d = x3.shape
    tm = _row_tile(t, 384)
    tok = lambda c: pl.BlockSpec((None, tm, c), lambda i, j: (i, j, 0))
    const = lambda a: pl.BlockSpec(a.shape, lambda i, j: (0,) * a.ndim, pipeline_mode=pl.Buffered(1))
    wa = H_A * DV_A
    wf = H_F * HD_F
    return pl.pallas_call(
        _merge_body,
        grid=(g, t // tm),
        in_specs=[tok(d), tok(wa), tok(wa), tok(wf), tok(d), tok(d),
                  const(na_row), const(pa), const(pf), const(wo), const(gffn_row),
                  const(wr), const(br_row)],
        out_specs=[tok(d), tok(d), tok(LANES), tok(LANES)],
        out_shape=[jax.ShapeDtypeStruct((g, t, d), F32), jax.ShapeDtypeStruct((g, t, d), F32),
                   jax.ShapeDtypeStruct((g, t, LANES), F32), jax.ShapeDtypeStruct((g, t, LANES), jnp.int32)],
        compiler_params=_cparams("parallel", "parallel"),
        name="merge",
    )(x3, oa3, z3, of3, ga3, gf3, na_row, pa, pf, wo, gffn_row, wr, br_row)


def _moe_body(be_ref, nv_ref, x_ref, wg_ref, bg_ref, wu_ref, bu_ref, wd_ref, bd_ref, y_ref, wg_b, wu_b, wd_b):
    i = pl.program_id(0)

    @pl.when(jnp.logical_or(i == 0, be_ref[i] != be_ref[jnp.maximum(i - 1, 0)]))
    def _():
        wg_b[...] = wg_ref[...].astype(BF16)
        wu_b[...] = wu_ref[...].astype(BF16)
        wd_b[...] = wd_ref[...].astype(BF16)

    @pl.when(i < nv_ref[0])
    def _():
        x = x_ref[...].astype(BF16)
        gate = jnp.minimum(_dot(x, wg_b[...]) + bg_ref[...], SWIGLU_LIMIT)
        up = jnp.clip(_dot(x, wu_b[...]) + bu_ref[...], -SWIGLU_LIMIT, SWIGLU_LIMIT)
        act = (up + 1.0) * gate * _sigmoid(SWIGLU_ALPHA * gate)
        y_ref[...] = _dot(act.astype(BF16), wd_b[...]) + bd_ref[...]

    @pl.when(i >= nv_ref[0])
    def _():
        y_ref[...] = jnp.zeros(y_ref.shape, y_ref.dtype)


def _moe_experts(blk_e, n_valid, xs, wg, bg, wu, bu, wd, bd):
    n_rows, d = xs.shape
    bm = MOE_BLOCK
    n_blk = n_rows // bm
    de = wg.shape[-1]
    row_idx = lambda i, be, nv: (jnp.minimum(i, nv[0] - 1), 0)
    wspec = lambda a, b: pl.BlockSpec((None, a, b), lambda i, be, nv: (be[i], 0, 0))
    grid_spec = pltpu.PrefetchScalarGridSpec(
        num_scalar_prefetch=2,
        grid=(n_blk,),
        in_specs=[pl.BlockSpec((bm, d), row_idx),
                  wspec(d, de), wspec(1, de), wspec(d, de), wspec(1, de), wspec(de, d), wspec(1, d)],
        out_specs=pl.BlockSpec((bm, d), lambda i, be, nv: (i, 0)),
        scratch_shapes=[pltpu.VMEM((d, de), BF16), pltpu.VMEM((d, de), BF16), pltpu.VMEM((de, d), BF16)],
    )
    return pl.pallas_call(
        _moe_body,
        grid_spec=grid_spec,
        out_shape=jax.ShapeDtypeStruct((n_rows, d), F32),
        compiler_params=_cparams("arbitrary"),
        name="moe_experts",
    )(blk_e, n_valid, xs, wg, bg, wu, bu, wd, bd)


def _moe(h2, top_w, top_i, wg, bg, wu, bu, wd, bd):
    n, d = h2.shape
    bm = MOE_BLOCK
    m = n * TOP_K
    n_blk = -(-(m + N_EXP * (bm - 1)) // bm)
    flat_e = top_i.reshape(-1)
    onehot = (flat_e[:, None] == jnp.arange(N_EXP, dtype=jnp.int32)[None, :]).astype(jnp.int32)
    csum = jnp.cumsum(onehot, axis=0)
    rank = jnp.take_along_axis(csum, flat_e[:, None], axis=1)[:, 0] - 1
    counts = csum[-1]
    padded = (counts + bm - 1) // bm * bm
    pend = jnp.cumsum(padded)
    pstart = pend - padded
    dest = pstart[flat_e] + rank
    n_valid = (pend[-1] // bm).astype(jnp.int32)
    blk_ids = jnp.arange(n_blk, dtype=jnp.int32)
    expert_of = lambda blk: jnp.minimum(jnp.sum((pend[None, :] <= (blk * bm)[:, None]).astype(jnp.int32), axis=1), N_EXP - 1)
    blk_e = expert_of(jnp.minimum(blk_ids, n_valid - 1))
    shift = (m - 1).bit_length()
    assert N_EXP << shift < 2 ** 31
    pair_sorted = jnp.sort((flat_e << shift) | jnp.arange(m, dtype=jnp.int32)) & ((1 << shift) - 1)
    e_of_blk = expert_of(blk_ids)
    start = jnp.cumsum(counts) - counts
    rank = (blk_ids * bm - pstart[e_of_blk])[:, None] + jnp.arange(bm, dtype=jnp.int32)[None, :]
    src = jnp.clip(start[e_of_blk][:, None] + rank, 0, m - 1)
    row_tok = jnp.where(rank < counts[e_of_blk][:, None], pair_sorted[src.reshape(-1)].reshape(n_blk, bm) // TOP_K, n)
    row_tok = row_tok.reshape(-1)
    h_ext = jnp.concatenate([h2, jnp.zeros((1, d), h2.dtype)], axis=0)
    xs = h_ext[row_tok]
    y = _moe_experts(blk_e, n_valid.reshape(1), xs, wg, bg, wu, bu, wd, bd)
    yg = y[dest.reshape(n, TOP_K).T.reshape(-1)].reshape(TOP_K, n, d)
    return jnp.sum(yg * top_w.T[:, :, None], axis=0)


def _lane_row(vals, offset, width=LANES):
    return jnp.zeros((1, width), F32).at[0, offset:offset + vals.shape[0]].set(vals.astype(F32))


def kernel(x_prompt, x_sample, cache_k, cache_v, cache_logf, state_delta, state_conv, page_table,
           meta_tokens, g_mix, w_in, conv_w, a_log, dt_bias, norm_a, qn_g, kn_g, b_forget,
           p_a, p_f, w_o, g_ffn, w_router, b_router, w_gate, b_gate, w_up, b_up, w_down, b_down):
    depth = w_in.shape[0]
    b, seq, d = x_prompt.shape
    bd = x_sample.shape[0]
    assert x_sample.shape[1] == 1 and DK_A == LANES and DV_A == LANES
    t = N_META + seq
    lead = (-N_META) % CHUNK
    t_pad = -(-t // ATT_BLOCK) * ATT_BLOCK
    qkv_w, va_w, wf = 2 * H_A * DK_A + H_A * DV_A, H_A * DV_A, H_F * HD_F
    sizes = (qkv_w, va_w, H_A, H_A, wf, wf, wf, H_F, d, d)
    offs = [0]
    for s_ in sizes:
        offs.append(offs[-1] + s_)
    col = lambda i: slice(offs[i], offs[i + 1])
    widths = (qkv_w, va_w, wf, wf, wf, d, d)

    xp = jnp.concatenate([jnp.broadcast_to(meta_tokens.astype(x_prompt.dtype)[None], (b, N_META, d)), x_prompt], axis=1)
    xs = x_sample
    new_p = [[] for _ in range(5)]
    new_s = [[] for _ in range(5)]
    for l in range(depth):
        wl = w_in[l]
        w_main = jnp.concatenate([wl[:, col(0)], wl[:, col(1)], wl[:, col(4)], wl[:, col(5)], wl[:, col(6)],
                                  wl[:, col(8)], wl[:, col(9)]], axis=1).astype(BF16)
        w_small = jnp.concatenate([wl[:, col(2)], wl[:, col(3)], wl[:, col(7)],
                                   jnp.zeros((d, LANES - 2 * H_A - H_F), F32)], axis=1).astype(BF16)
        g_row = g_mix[l].reshape(1, d)
        a_row = _lane_row(a_log[l], H_A)
        dt_row = _lane_row(dt_bias[l], H_A)
        bf_row = _lane_row(b_forget[l], 2 * H_A)
        qg_pair = jnp.tile(qn_g[l], LANES // HD_F).reshape(1, LANES)
        kg_pair = jnp.tile(kn_g[l], LANES // HD_F).reshape(1, LANES)
        qg_full = jnp.tile(qn_g[l], H_F).reshape(1, wf)
        kg_full = jnp.tile(kn_g[l], H_F).reshape(1, wf)
        na_row = norm_a[l].reshape(1, DV_A)
        pa_b, pf_b, wo_b = p_a[l].astype(BF16), p_f[l].astype(BF16), w_o[l].astype(BF16)
        gffn_row = g_ffn[l].reshape(1, d)
        wr = jnp.pad(w_router[l], ((0, 0), (0, LANES - N_EXP))).astype(BF16)
        br_row = _lane_row(b_router[l], 0)
        merge_w = (na_row, pa_b, pf_b, wo_b, gffn_row, wr, br_row)

        qkv_p, z_p, qf_p, kf_p, vf_p, ga_p, gf_p, sm_p = _in_proj(xp.reshape(b * t, d), g_row, w_main, w_small, widths)
        r3 = lambda a: a.reshape(b, t, a.shape[-1])
        qkv3, sm3 = r3(qkv_p), r3(sm_p)
        conv_p = qkv3[:, t - (CONV_W - 1):, :]
        prep = _conv_prep(qkv3, conv_w[l], jnp.zeros((b, CONV_W - 1, qkv_w), F32))
        oa_p, s_p = _delta_scan(prep, sm3, a_row, dt_row, jnp.zeros((b, H_A, DK_A, DV_A), F32), lead)
        lf_p, c_col = _logf_cumsum(sm3, bf_row, t_pad)
        qx, kx, vb, kn_p = _fox_prep(r3(qf_p), r3(kf_p), r3(vf_p), c_col, qg_pair, kg_pair, t_pad)
        of_p = _fox_attn(qx, kx, vb)
        x1_p, h2_p, tw_p, ti_p = _merge(xp, oa_p, r3(z_p), of_p, r3(ga_p), r3(gf_p), *merge_w)

        qkv_s, z_s, qf_s, kf_s, vf_s, ga_s, gf_s, sm_s = _in_proj(xs.reshape(bd, d), g_row, w_main, w_small, widths)
        s3 = lambda a: a.reshape(bd, 1, a.shape[-1])
        oa_s, s_s, conv_s = _decode_delta(s3(qkv_s), state_conv[l], s3(sm_s), conv_w[l], a_row, dt_row, state_delta[l])
        n_pool, page = cache_k.shape[1], cache_k.shape[2]
        of_s, kn_s, lf_s = _decode_attn(page_table, s3(qf_s), s3(kf_s), s3(vf_s), s3(sm_s), qg_full, kg_full,
                                        _lane_row(b_forget[l], 2 * H_A),
                                        jnp.transpose(cache_k[l], (0, 2, 3, 1)).reshape(n_pool, wf, page),
                                        jnp.transpose(cache_v[l], (0, 2, 3, 1)).reshape(n_pool, wf, page),
                                        jnp.swapaxes(cache_logf[l], 1, 2))
        g1 = lambda a: a.reshape(1, bd, a.shape[-1])
        x1_s, h2_s, tw_s, ti_s = _merge(g1(xs), g1(oa_s), g1(z_s), g1(of_s), g1(ga_s), g1(gf_s), *merge_w)

        n_p = b * t
        h2_all = jnp.concatenate([h2_p.reshape(n_p, d), h2_s.reshape(bd, d)], axis=0)
        tw_all = jnp.concatenate([tw_p.reshape(n_p, LANES), tw_s.reshape(bd, LANES)], axis=0)[:, :TOP_K]
        ti_all = jnp.concatenate([ti_p.reshape(n_p, LANES), ti_s.reshape(bd, LANES)], axis=0)[:, :TOP_K]
        y_all = _moe(h2_all, tw_all, ti_all, w_gate[l], b_gate[l][:, None, :],
                     w_up[l], b_up[l][:, None, :], w_down[l], b_down[l][:, None, :])
        xp = x1_p + y_all[:n_p].reshape(b, t, d)
        xs = x1_s.reshape(bd, 1, d) + y_all[n_p:].reshape(bd, 1, d)

        st_p = (kn_p.reshape(b, t, H_F, HD_F), vf_p.reshape(b, t, H_F, HD_F),
                lf_p[:, :, 2 * H_A:2 * H_A + H_F], s_p, conv_p)
        st_s = (kn_s.reshape(bd, 1, H_F, HD_F), vf_s.reshape(bd, 1, H_F, HD_F),
                lf_s[:, :, 2 * H_A:2 * H_A + H_F], s_s, conv_s)
        for lst, a in zip(new_p, st_p):
            lst.append(a)
        for lst, a in zip(new_s, st_s):
            lst.append(a)
    k_p, v_p, lf_pp, d_p, c_p = (jnp.stack(a) for a in new_p)
    k_s, v_s, lf_ss, d_s, c_s = (jnp.stack(a) for a in new_s)
    return (xp[:, N_META:], xs, k_p, v_p, lf_pp, k_s, v_s, lf_ss, d_p, d_s, c_p, c_s)
```

```python
import functools

import jax
import jax.numpy as jnp
from jax import lax
from jax.experimental import pallas as pl
from jax.experimental.pallas import tpu as pltpu

F32 = jnp.float32
BF16 = jnp.bfloat16

N_META = 16
H_A = 4
DK_A = 128
DV_A = 128
CONV_W = 4
CHUNK = 64
H_F = 8
HD_F = 64
N_EXP = 32
TOP_K = 4
SWIGLU_LIMIT = 7.0
SWIGLU_ALPHA = 1.702
EPS = 1e-6
LOG2E = 1.4426950408889634

LANES = 128
F_OFF = 2 * H_A
HP = 16
ATT_BLOCK = 256
MOE_BLOCK = 256
PAGES_PER_STEP = 16
VMEM_LIMIT = 48 * 1024 * 1024


def _cparams(*sem):
    return pltpu.CompilerParams(dimension_semantics=sem, vmem_limit_bytes=VMEM_LIMIT)


def _row_tile(n, cap):
    if n <= cap:
        return n
    best = None
    for t in range(8, cap + 1, 8):
        if n % t == 0:
            best = t
    assert best is not None, n
    return best


def _dot(a, b):
    return jnp.dot(a, b, preferred_element_type=F32)


def _dot_nt(a, b):
    return lax.dot_general(a, b, (((1,), (1,)), ((), ())), preferred_element_type=F32)


def _split2(x):
    hi = x.astype(BF16)
    lo = (x - hi.astype(F32)).astype(BF16)
    return hi, lo


def _dot_x3(a, b):
    a_hi, a_lo = _split2(a)
    b_hi, b_lo = _split2(b)
    return _dot(a_hi, b_hi) + (_dot(a_hi, b_lo) + _dot(a_lo, b_hi))


def _split3(x):
    hi = x.astype(BF16)
    r = x - hi.astype(F32)
    mid = r.astype(BF16)
    lo = (r - mid.astype(F32)).astype(BF16)
    return hi, mid, lo


def _sigmoid(x):
    return 1.0 / (1.0 + jnp.exp(-x))


def _softplus(x):
    return jnp.maximum(x, 0.0) + jnp.log1p(jnp.exp(-jnp.abs(x)))


def _log_sigmoid(x):
    return jnp.minimum(x, 0.0) - jnp.log1p(jnp.exp(-jnp.abs(x)))


IN_WIDTHS = (2 * H_A * DK_A + H_A * DV_A, H_A * DV_A, H_F * HD_F, H_F * HD_F, H_F * HD_F)
COL_CHUNK = 512


def _in_proj_body(x_ref, g_ref, wm_ref, ws_ref, *out_refs, widths):
    x = x_ref[...]
    h = x * lax.rsqrt(jnp.mean(x * x, axis=-1, keepdims=True) + EPS) * g_ref[...]
    hb = h.astype(BF16)
    col = 0
    for o_ref, w in zip(out_refs[:-1], widths):
        for c0 in range(0, w, COL_CHUNK):
            o_ref[:, c0:c0 + COL_CHUNK] = _dot(hb, wm_ref[:, col + c0:col + c0 + COL_CHUNK])
        col += w
    out_refs[-1][...] = _dot(hb, ws_ref[...])


def _in_proj(x2d, g_row, w_main, w_small, widths):
    n, d = x2d.shape
    tm = _row_tile(n, 512)
    outs = [jax.ShapeDtypeStruct((n, w), F32) for w in widths] + [jax.ShapeDtypeStruct((n, LANES), F32)]
    out_specs = [pl.BlockSpec((tm, w), lambda i: (i, 0)) for w in widths] + [pl.BlockSpec((tm, LANES), lambda i: (i, 0))]
    return pl.pallas_call(
        functools.partial(_in_proj_body, widths=widths),
        grid=(n // tm,),
        in_specs=[pl.BlockSpec((tm, d), lambda i: (i, 0)),
                  pl.BlockSpec((1, d), lambda i: (0, 0)),
                  pl.BlockSpec(w_main.shape, lambda i: (0, 0), pipeline_mode=pl.Buffered(1)),
                  pl.BlockSpec(w_small.shape, lambda i: (0, 0), pipeline_mode=pl.Buffered(1))],
        out_specs=out_specs,
        out_shape=outs,
        compiler_params=_cparams("parallel"),
        name="in_proj",
    )(x2d, g_row, w_main, w_small)


def _conv_prep_body(x_ref, cw_ref, prev_ref, o_ref, buf, *, t_len, rows):
    j = pl.program_id(1)
    pad = 8
    buf[0:pad, :] = jnp.zeros((pad, LANES), F32)
    buf[pad - (CONV_W - 1):pad, :] = prev_ref[...]
    buf[pad:pad + t_len, :] = x_ref[...]
    cw = cw_ref[...]
    is_q = j < H_A
    is_qk = j < 2 * H_A
    for r0 in range(0, t_len, rows):
        y = jnp.zeros((rows, LANES), F32)
        for w in range(CONV_W):
            off = pad - (CONV_W - 1) + w + r0
            y = y + buf[off:off + rows, :] * cw[w:w + 1, :]
        y = y * _sigmoid(y)
        nrm = lax.rsqrt(jnp.sum(y * y, axis=-1, keepdims=True) + EPS)
        f = jnp.where(is_qk, nrm * jnp.where(is_q, DK_A ** -0.5, 1.0), 1.0)
        o_ref[r0:r0 + rows, :] = y * f


def _conv_prep(qkv3, conv_w, conv_prev):
    b, t, c = qkv3.shape
    rows = _row_tile(t, 512)
    return pl.pallas_call(
        functools.partial(_conv_prep_body, t_len=t, rows=rows),
        grid=(b, c // LANES),
        in_specs=[pl.BlockSpec((None, t, LANES), lambda i, j: (i, 0, j)),
                  pl.BlockSpec((CONV_W, LANES), lambda i, j: (0, j)),
                  pl.BlockSpec((None, CONV_W - 1, LANES), lambda i, j: (i, 0, j))],
        out_specs=pl.BlockSpec((None, t, LANES), lambda i, j: (i, 0, j)),
        out_shape=jax.ShapeDtypeStruct((b, t, c), F32),
        scratch_shapes=[pltpu.VMEM((8 + t, LANES), F32)],
        compiler_params=_cparams("parallel", "parallel"),
        name="conv_prep",
    )(qkv3, conv_w, conv_prev)


def _gates(sm, a_row, dt_row, h):
    lane = lax.broadcasted_iota(jnp.int32, sm.shape, 1)
    beta_all = _sigmoid(sm)
    g_all = -jnp.exp(a_row) * _softplus(sm + dt_row)
    beta = jnp.sum(jnp.where(lane == h, beta_all, 0.0), axis=-1, keepdims=True)
    g = jnp.sum(jnp.where(lane == H_A + h, g_all, 0.0), axis=-1, keepdims=True)
    return beta, g


def _decay_terms(gs):
    c = gs[0].shape[0]
    ri = lax.broadcasted_iota(jnp.int32, (c, c), 0)
    ci = lax.broadcasted_iota(jnp.int32, (c, c), 1)
    g_cum, decay = [], []
    for g in gs:
        g_row = jnp.sum(jnp.where(ri == ci, g, 0.0), axis=0, keepdims=True)
        gc = jnp.sum(jnp.where(ci <= ri, g_row, 0.0), axis=1, keepdims=True)
        gc_row = jnp.sum(jnp.where(ri <= ci, g, 0.0), axis=0, keepdims=True)
        g_cum.append(gc)
        decay.append(jnp.exp(jnp.where(ci <= ri, gc - gc_row, -jnp.inf)))
    return g_cum, decay


def _delta_step(cur, nxt, s_ref):
    o = inv_n = None
    stages = []
    if cur is not None:
        qs, ks, vs, betas, gs, inv = cur
        rng = range(len(qs))
        c = qs[0].shape[0]
        g_cum, decay = _decay_terms(gs)
        e_g = [jnp.exp(g_cum[i]) for i in rng]
        s = [s_ref[i] for i in rng]
        sb = [s[i].astype(BF16) for i in rng]
        qb = [qs[i].astype(BF16) for i in rng]
        kb = [ks[i].astype(BF16) for i in rng]
        k_s = [_dot(kb[i], sb[i]) for i in rng]
        val = {}

        def st_delta():
            rhs = [betas[i] * (vs[i] - e_g[i] * k_s[i]) for i in rng]
            val["db"] = [_dot_x3(inv[i], rhs[i]).astype(BF16) for i in rng]
            val["qk"] = [_dot_nt(qb[i], kb[i]) for i in rng]
            val["q_s"] = [_dot(qb[i], sb[i]) for i in rng]

        def st_out():
            val["o"] = [e_g[i] * val["q_s"][i] + _dot((val["qk"][i] * decay[i]).astype(BF16), val["db"][i]) for i in rng]

        def st_state():
            g_end = [g_cum[i][c - 1:c, :] for i in rng]
            k_dec = [(ks[i] * jnp.exp(g_end[i] - g_cum[i])).T.astype(BF16) for i in rng]
            for i in rng:
                s_ref[i] = jnp.exp(g_end[i]) * s[i] + _dot(k_dec[i], val["db"][i])

        stages = [st_delta, st_out, st_state]
    if nxt is not None:
        ks_n, betas_n, gs_n = nxt
        rng_n = range(len(ks_n))
        c = ks_n[0].shape[0]
        ri = lax.broadcasted_iota(jnp.int32, (c, c), 0)
        ci = lax.broadcasted_iota(jnp.int32, (c, c), 1)
        _, decay_n = _decay_terms(gs_n)
        kb_n = [ks_n[i].astype(BF16) for i in rng_n]
        kk = [_dot_nt(kb_n[i], kb_n[i]) for i in rng_n]
        a = [jnp.where(ci < ri, betas_n[i] * decay_n[i] * kk[i], 0.0) for i in rng_n]
        inv_n = [jnp.where(ri == ci, 1.0, 0.0) - a[i] for i in rng_n]
        pw = a
        for _ in range(max(1, (c - 1).bit_length() - 1)):
            if stages:
                stages.pop(0)()
            pw = [_dot_x3(pw[i], pw[i]) for i in rng_n]
            inv_n = [inv_n[i] + _dot_x3(inv_n[i], pw[i]) for i in rng_n]
    for st in stages:
        st()
    if cur is not None:
        o = val["o"]
    return o, inv_n


def _delta_scan_body(qkv_ref, sm_ref, a_ref, dt_ref, s0_ref, o_ref, s_out_ref, s_ref, *, t_len, lead):
    a_row = a_ref[...]
    dt_row = dt_ref[...]
    s_ref[...] = s0_ref[...]
    first = CHUNK - lead
    n_chunks = (lead + t_len) // CHUNK

    def gates(load, mask_lead):
        sm = load(sm_ref, 0)
        betas, gs = [], []
        for h in range(H_A):
            beta, g = _gates(sm, a_row, dt_row, h)
            if mask_lead:
                row = lax.broadcasted_iota(jnp.int32, (CHUNK, 1), 0)
                beta = jnp.where(row >= lead, beta, 0.0)
                g = jnp.where(row >= lead, g, 0.0)
            betas.append(beta)
            gs.append(g)
        return betas, gs

    def nxt_args(load, mask_lead):
        betas, gs = gates(load, mask_lead)
        return [load(qkv_ref, (H_A + h) * DK_A) for h in range(H_A)], betas, gs

    def cur_args(load, mask_lead, inv):
        betas, gs = gates(load, mask_lead)
        qs = [load(qkv_ref, h * DK_A) for h in range(H_A)]
        ks = [load(qkv_ref, (H_A + h) * DK_A) for h in range(H_A)]
        vs = [load(qkv_ref, 2 * H_A * DK_A + h * DV_A) for h in range(H_A)]
        return qs, ks, vs, betas, gs, inv

    def chunk_rows(c):
        return pl.ds(pl.multiple_of(c * CHUNK - lead, 8), CHUNK)

    def loader(c):
        sl = chunk_rows(c)
        return lambda ref, col: ref[sl, col:col + LANES]

    if lead:
        def load0(ref, col):
            return jnp.concatenate([jnp.zeros((lead, LANES), F32), ref[0:first, col:col + LANES]], axis=0)

        def store0(h, o):
            o_ref[0:first, h * DV_A:(h + 1) * DV_A] = o[lead:, :]

        _, inv0 = _delta_step(None, nxt_args(load0, True), s_ref)
        has_next = n_chunks > 1
        outs, inv1 = _delta_step(cur_args(load0, True, inv0), nxt_args(loader(1), False) if has_next else None, s_ref)
        for h, o in enumerate(outs):
            store0(h, o)
        c_start = 1
    else:
        has_next = n_chunks > 0
        _, inv1 = _delta_step(None, nxt_args(loader(0), False), s_ref) if has_next else (None, None)
        c_start = 0

    def body(c, inv):
        sl = chunk_rows(c)
        outs, inv_n = _delta_step(cur_args(loader(c), False, inv),
                                  nxt_args(loader(jnp.minimum(c + 1, n_chunks - 1)), False), s_ref)
        for h, o in enumerate(outs):
            o_ref[sl, h * DV_A:(h + 1) * DV_A] = o
        return tuple(inv_n)

    if has_next:
        lax.fori_loop(c_start, n_chunks, body, tuple(inv1))
    s_out_ref[...] = s_ref[...]


def _delta_scan(qkv3, small3, a_row, dt_row, s0, lead):
    b, t, c = qkv3.shape
    assert (lead + t) % CHUNK == 0 and lead % 8 == 0
    return pl.pallas_call(
        functools.partial(_delta_scan_body, t_len=t, lead=lead),
        grid=(b,),
        in_specs=[pl.BlockSpec((None, t, c), lambda i: (i, 0, 0)),
                  pl.BlockSpec((None, t, LANES), lambda i: (i, 0, 0)),
                  pl.BlockSpec((1, LANES), lambda i: (0, 0)),
                  pl.BlockSpec((1, LANES), lambda i: (0, 0)),
                  pl.BlockSpec((None, H_A, DK_A, DV_A), lambda i: (i, 0, 0, 0))],
        out_specs=[pl.BlockSpec((None, t, H_A * DV_A), lambda i: (i, 0, 0)),
                   pl.BlockSpec((None, H_A, DK_A, DV_A), lambda i: (i, 0, 0, 0))],
        out_shape=[jax.ShapeDtypeStruct((b, t, H_A * DV_A), F32),
                   jax.ShapeDtypeStruct((b, H_A, DK_A, DV_A), F32)],
        scratch_shapes=[pltpu.VMEM((H_A, DK_A, DV_A), F32)],
        compiler_params=_cparams("parallel"),
        name="delta_scan",
    )(qkv3, small3, a_row, dt_row, s0)


def _group_rms(x, gain, seg):
    hi, lo = _split2(x * x)
    ss = _dot(hi, seg) + _dot(lo, seg)
    return x * lax.rsqrt(ss * (1.0 / HD_F) + EPS) * gain


def _seg_matrix(n):
    r = lax.broadcasted_iota(jnp.int32, (n, n), 0) // HD_F
    c = lax.broadcasted_iota(jnp.int32, (n, n), 1) // HD_F
    return jnp.where(r == c, 1.0, 0.0).astype(BF16)


def _fox_prep_body(q_ref, k_ref, v_ref, c_ref, qg_ref, kg_ref, qx_ref, kx_ref, vt_ref, kn_ref, *, t_len, t_pad):
    p = pl.program_id(1)
    heads = LANES // HD_F

    def feature_major(x):
        if t_pad > t_len:
            x = jnp.concatenate([x, jnp.zeros((t_pad - t_len, LANES), F32)], axis=0)
        return x.T.astype(BF16)

    seg = _seg_matrix(LANES)
    qn = _group_rms(q_ref[...], qg_ref[...], seg) * (HD_F ** -0.5 * LOG2E)
    kn = _group_rms(k_ref[...], kg_ref[...], seg)
    kn_ref[...] = kn
    c_all = c_ref[...] * LOG2E
    lane = lax.broadcasted_iota(jnp.int32, (t_len, LANES), 1)
    for hh in range(heads):
        c = jnp.sum(jnp.where(lane == F_OFF + p * heads + hh, c_all, 0.0), axis=-1, keepdims=True)
        pieces = [x.astype(F32) for x in _split3(c)]
        own = (lane // HD_F) == hh
        f0 = ((hh + 1) % heads) * HD_F
        qx = jnp.where(own, qn, 0.0)
        kx = jnp.where(own, kn, 0.0)
        for n, piece in enumerate(pieces):
            qx = jnp.where(lane == f0 + n, piece, qx)
            kx = jnp.where(lane == f0 + n, 1.0, kx)
            qx = jnp.where(lane == f0 + 3 + n, 1.0, qx)
            kx = jnp.where(lane == f0 + 3 + n, -piece, kx)
        kx_ref[0:t_len, hh * LANES:(hh + 1) * LANES] = kx.astype(BF16)
        qx_ref[hh * LANES:(hh + 1) * LANES, :] = feature_major(qx)
    vt_ref[...] = feature_major(v_ref[...])
    if t_pad > t_len:
        kx_ref[t_len:, :] = jnp.zeros((t_pad - t_len, heads * LANES), BF16)


def _fox_prep(q3, k3, v3, c_col, qg_row, kg_row, t_pad):
    b, t, w = q3.shape
    heads = LANES // HD_F
    blk = pl.BlockSpec((None, t, LANES), lambda i, p: (i, 0, p))
    gblk = pl.BlockSpec((1, LANES), lambda i, p: (0, 0))
    return pl.pallas_call(
        functools.partial(_fox_prep_body, t_len=t, t_pad=t_pad),
        grid=(b, w // LANES),
        in_specs=[blk, blk, blk, pl.BlockSpec((None, t, LANES), lambda i, p: (i, 0, 0)), gblk, gblk],
        out_specs=[pl.BlockSpec((None, heads * LANES, t_pad), lambda i, p: (i, p, 0)),
                   pl.BlockSpec((None, t_pad, heads * LANES), lambda i, p: (i, 0, p)),
                   pl.BlockSpec((None, LANES, t_pad), lambda i, p: (i, p, 0)), blk],
        out_shape=[jax.ShapeDtypeStruct((b, H_F * LANES, t_pad), BF16), jax.ShapeDtypeStruct((b, t_pad, H_F * LANES), BF16),
                   jax.ShapeDtypeStruct((b, w, t_pad), BF16), jax.ShapeDtypeStruct((b, t, w), F32)],
        compiler_params=_cparams("parallel", "parallel"),
        name="fox_prep",
    )(q3, k3, v3, c_col, qg_row, kg_row)


def _logf_cumsum_body(sm_ref, bf_ref, lf_ref, ccol_ref, buf, *, t_len, t_pad):
    lf = _log_sigmoid(sm_ref[...] + bf_ref[...])
    lf_ref[...] = lf
    buf[0:t_len, :] = lf
    if t_pad > t_len:
        buf[t_len:, :] = jnp.zeros((t_pad - t_len, LANES), F32)
    blk = ATT_BLOCK
    ri = lax.broadcasted_iota(jnp.int32, (blk, blk), 0)
    ci = lax.broadcasted_iota(jnp.int32, (blk, blk), 1)
    tri = jnp.where(ci <= ri, 1.0, 0.0).astype(BF16)
    carry = jnp.zeros((1, LANES), F32)
    for i in range(t_pad // blk):
        hi, mid, lo = _split3(buf[i * blk:(i + 1) * blk, :])
        c = _dot(tri, hi) + _dot(tri, mid) + _dot(tri, lo) + carry
        ccol_ref[i * blk:(i + 1) * blk, :] = c
        carry = c[blk - 1:blk, :]


def _logf_cumsum(small3, bf_row, t_pad):
    b, t, _ = small3.shape
    return pl.pallas_call(
        functools.partial(_logf_cumsum_body, t_len=t, t_pad=t_pad),
        grid=(b,),
        in_specs=[pl.BlockSpec((None, t, LANES), lambda i: (i, 0, 0)),
                  pl.BlockSpec((1, LANES), lambda i: (0, 0))],
        out_specs=[pl.BlockSpec((None, t, LANES), lambda i: (i, 0, 0)),
                   pl.BlockSpec((None, t_pad, LANES), lambda i: (i, 0, 0))],
        out_shape=[jax.ShapeDtypeStruct((b, t, LANES), F32),
                   jax.ShapeDtypeStruct((b, t_pad, LANES), F32)],
        scratch_shapes=[pltpu.VMEM((t_pad, LANES), F32)],
        compiler_params=_cparams("parallel"),
        name="logf_cumsum",
    )(small3, bf_row)


def _fox_attn_body(qt_ref, k_ref, vt_ref, o_ref):
    i = pl.program_id(1)
    blk = ATT_BLOCK
    key_le_query = (lax.broadcasted_iota(jnp.int32, (blk, blk), 0)
                    <= lax.broadcasted_iota(jnp.int32, (blk, blk), 1))

    def step(j, carry, diagonal):
        ms, ls, accs = carry
        k0 = pl.multiple_of(j * blk, blk)
        st = []
        for h in range(H_F):
            s = _dot(k_ref[pl.ds(k0, blk), h * LANES:(h + 1) * LANES], qt_ref[h * LANES:(h + 1) * LANES, :])
            st.append(jnp.where(key_le_query, s, -jnp.inf) if diagonal else s)
        new_ms, new_ls, alphas, pts = [], [], [], []
        for h in range(H_F):
            m_new = jnp.maximum(ms[h], jnp.max(st[h], axis=0, keepdims=True))
            alpha = jnp.exp2(ms[h] - m_new)
            pt = jnp.exp2(st[h] - m_new)
            new_ls.append(alpha * ls[h] + jnp.sum(pt, axis=0, keepdims=True))
            new_ms.append(m_new)
            alphas.append(alpha)
            pts.append(pt.astype(BF16))
        new_accs = [alphas[h] * accs[h] + _dot(vt_ref[h * HD_F:(h + 1) * HD_F, pl.ds(k0, blk)], pts[h])
                    for h in range(H_F)]
        return tuple(new_ms), tuple(new_ls), tuple(new_accs)

    init = (tuple(jnp.full((1, blk), -jnp.inf, F32) for _ in range(H_F)),
            tuple(jnp.zeros((1, blk), F32) for _ in range(H_F)),
            tuple(jnp.zeros((HD_F, blk), F32) for _ in range(H_F)))
    carry = lax.fori_loop(0, i, lambda j, c: step(j, c, False), init)
    ms, ls, accs = step(i, carry, True)
    heads = LANES // HD_F
    for p in range(H_F // heads):
        ot = jnp.concatenate([accs[p * heads + hh] * (1.0 / ls[p * heads + hh]) for hh in range(heads)], axis=0)
        o_ref[:, p * LANES:(p + 1) * LANES] = ot.T.astype(o_ref.dtype)


def _fox_attn(qt, kx, vt):
    b, w, t_pad = vt.shape
    blk = ATT_BLOCK
    return pl.pallas_call(
        _fox_attn_body,
        grid=(b, t_pad // blk),
        in_specs=[pl.BlockSpec((None, H_F * LANES, blk), lambda n, i: (n, 0, i)),
                  pl.BlockSpec((None, t_pad, H_F * LANES), lambda n, i: (n, 0, 0)),
                  pl.BlockSpec((None, w, t_pad), lambda n, i: (n, 0, 0))],
        out_specs=pl.BlockSpec((None, blk, w), lambda n, i: (n, i, 0)),
        out_shape=jax.ShapeDtypeStruct((b, t_pad, w), BF16),
        compiler_params=_cparams("parallel", "arbitrary"),
        name="fox_attn",
    )(qt, kx, vt)


def _decode_delta_body(x_ref, sc_ref, sm_ref, cw_ref, a_ref, dt_ref, s_ref, o_ref, s_out_ref, conv_ref):
    x_new = x_ref[...]
    sc = sc_ref[...]
    cw = cw_ref[...]
    y = x_new * cw[CONV_W - 1:CONV_W, :]
    for w in range(CONV_W - 1):
        y = y + sc[w:w + 1, :] * cw[w:w + 1, :]
    y = y * _sigmoid(y)
    conv_ref[0:CONV_W - 2, :] = sc[1:, :]
    conv_ref[CONV_W - 2:CONV_W - 1, :] = x_new
    sm = sm_ref[...]
    lane = lax.broadcasted_iota(jnp.int32, (1, LANES), 1)
    beta_all = _sigmoid(sm)
    g_all = -jnp.exp(a_ref[...]) * _softplus(sm + dt_ref[...])
    qk_w = H_A * DK_A
    row0 = lax.broadcasted_iota(jnp.int32, (LANES, LANES), 0) == 0

    def in_row0(r):
        return jnp.where(row0, jnp.broadcast_to(r, (LANES, LANES)), 0.0)

    for h in range(H_A):
        q = y[:, h * DK_A:(h + 1) * DK_A]
        k = y[:, qk_w + h * DK_A:qk_w + (h + 1) * DK_A]
        v = y[:, 2 * qk_w + h * DV_A:2 * qk_w + (h + 1) * DV_A]
        q = q * lax.rsqrt(jnp.sum(q * q, axis=-1, keepdims=True) + EPS) * (DK_A ** -0.5)
        k = k * lax.rsqrt(jnp.sum(k * k, axis=-1, keepdims=True) + EPS)
        beta = jnp.sum(jnp.where(lane == h, beta_all, 0.0), axis=-1, keepdims=True)
        g = jnp.sum(jnp.where(lane == H_A + h, g_all, 0.0), axis=-1, keepdims=True)
        e_g = jnp.exp(g)
        s = s_ref[h]
        sb = s.astype(BF16)
        k_sq = in_row0(k)
        k_s = _dot(k_sq.astype(BF16), sb)[0:1, :]
        q_s = _dot(in_row0(q).astype(BF16), sb)[0:1, :]
        delta = beta * (v - e_g * k_s)
        qk = jnp.sum(q * k, axis=-1, keepdims=True)
        o_ref[:, h * DV_A:(h + 1) * DV_A] = e_g * q_s + qk * delta
        s_out_ref[h] = e_g * s + _dot_x3(k_sq.T, in_row0(delta))


def _decode_delta(qkv3, state_conv, small3, conv_w, a_row, dt_row, s0):
    bd, _, c = qkv3.shape
    return pl.pallas_call(
        _decode_delta_body,
        grid=(bd,),
        in_specs=[pl.BlockSpec((None, 1, c), lambda i: (i, 0, 0)),
                  pl.BlockSpec((None, CONV_W - 1, c), lambda i: (i, 0, 0)),
                  pl.BlockSpec((None, 1, LANES), lambda i: (i, 0, 0)),
                  pl.BlockSpec((CONV_W, c), lambda i: (0, 0)),
                  pl.BlockSpec((1, LANES), lambda i: (0, 0)),
                  pl.BlockSpec((1, LANES), lambda i: (0, 0)),
                  pl.BlockSpec((None, H_A, DK_A, DV_A), lambda i: (i, 0, 0, 0))],
        out_specs=[pl.BlockSpec((None, 1, H_A * DV_A), lambda i: (i, 0, 0)),
                   pl.BlockSpec((None, H_A, DK_A, DV_A), lambda i: (i, 0, 0, 0)),
                   pl.BlockSpec((None, CONV_W - 1, c), lambda i: (i, 0, 0))],
        out_shape=[jax.ShapeDtypeStruct((bd, 1, H_A * DV_A), F32),
                   jax.ShapeDtypeStruct((bd, H_A, DK_A, DV_A), F32),
                   jax.ShapeDtypeStruct((bd, CONV_W - 1, c), F32)],
        compiler_params=_cparams("parallel"),
        name="decode_delta",
    )(qkv3, state_conv, small3, conv_w, a_row, dt_row, s0)


def _decode_attn_body(pt_ref, q_ref, k_ref, v_ref, sm_ref, qg_ref, kg_ref, bf_ref, *rest, n_pg):
    page_refs = rest[:3 * n_pg]
    o_ref, kn_ref, lf_ref, qrows, vnew, snew, m_ref, l_ref, acc_ref, carry_ref, s_scr = rest[3 * n_pg:]
    step = pl.program_id(1)
    n_steps = pl.num_programs(1) // 2
    w = H_F * HD_F
    row = lax.broadcasted_iota(jnp.int32, (HP, w), 0)
    head_of_lane = lax.broadcasted_iota(jnp.int32, (HP, w), 1) // HD_F

    @pl.when(step == 0)
    def _():
        seg = _seg_matrix(w)

        def rms_rows(x_row, gain):
            xr = jnp.broadcast_to(x_row, (HP, w))
            hi, lo = _split2(xr * xr)
            ss = _dot(hi, seg) + _dot(lo, seg)
            return xr * lax.rsqrt(ss * (1.0 / HD_F) + EPS) * gain

        qn = rms_rows(q_ref[...], qg_ref[...]) * (HD_F ** -0.5)
        kn = rms_rows(k_ref[...], kg_ref[...])
        kn_ref[...] = kn[0:1, :]
        lf = _log_sigmoid(sm_ref[...] + bf_ref[...])
        lf_ref[...] = lf
        q_m = jnp.where(head_of_lane == row, qn, 0.0).astype(BF16)
        qrows[...] = q_m
        vnew[...] = jnp.broadcast_to(v_ref[...], (HP, w)).astype(BF16)
        s_new = jnp.sum(q_m.astype(F32) * kn.astype(BF16).astype(F32), axis=-1, keepdims=True)
        snew[...] = s_new
        m_ref[...] = s_new
        rr = lax.broadcasted_iota(jnp.int32, (HP, LANES), 0)
        ll = lax.broadcasted_iota(jnp.int32, (HP, LANES), 1)
        carry_ref[...] = jnp.sum(jnp.where(ll == rr + F_OFF, jnp.broadcast_to(lf, (HP, LANES)), 0.0),
                                 axis=-1, keepdims=True)

    @pl.when(step < n_steps)
    def _():
        qb = qrows[...]
        lf_all = jnp.concatenate(
            [jnp.concatenate([page_refs[3 * g + 2][...], jnp.zeros((HP - H_F, LANES), F32)], axis=0)
             for g in range(n_pg)], axis=0)
        later = (lax.broadcasted_iota(jnp.int32, (LANES, LANES), 0)
                 > lax.broadcasted_iota(jnp.int32, (LANES, LANES), 1)).astype(BF16)
        hi, mid, lo = _split3(lf_all)
        after = _dot(hi, later) + (_dot(mid, later) + _dot(lo, later))
        totals = jnp.sum(lf_all, axis=-1, keepdims=True)
        raw = [_dot(qb, page_refs[3 * g][...].astype(BF16)) for g in range(n_pg)]
        carry = carry_ref[...]
        m_new = m_ref[...]
        for g in range(n_pg):
            rows = slice(g * HP, (g + 1) * HP)
            s = raw[g] + (carry + after[rows, :])
            s_scr[step * n_pg + g] = s
            m_new = jnp.maximum(m_new, jnp.max(s, axis=-1, keepdims=True))
            carry = carry + totals[rows, :]
        carry_ref[...] = carry
        m_ref[...] = m_new

    @pl.when(step == n_steps)
    def _():
        m = m_ref[...]
        p_new = jnp.exp(snew[...] - m)

        def add(i, tot):
            return tot + jnp.exp(s_scr[i] - m)

        tot = lax.fori_loop(0, n_steps * n_pg, add, jnp.zeros((HP, LANES), F32), unroll=8)
        l = p_new + jnp.sum(tot, axis=-1, keepdims=True)
        l_ref[...] = l
        acc_ref[...] = (p_new / l).astype(BF16).astype(F32) * vnew[...].astype(F32)

    @pl.when(step >= n_steps)
    def _():
        m = m_ref[...]
        l = l_ref[...]
        probs = [(jnp.exp(s_scr[(step - n_steps) * n_pg + g] - m) / l).astype(BF16) for g in range(n_pg)]
        parts = [_dot_nt(probs[g], page_refs[3 * g + 1][...].astype(BF16)) for g in range(n_pg)]
        acc = acc_ref[...]
        for part in parts:
            acc = acc + part
        acc_ref[...] = acc

    @pl.when(step == pl.num_programs(1) - 1)
    def _():
        o_ref[...] = jnp.sum(jnp.where(head_of_lane == row, acc_ref[...], 0.0), axis=0, keepdims=True)


def _decode_attn(page_table, q3, k3, v3, small3, qg_row, kg_row, bf_row, ck, cv, clf_t):
    bd, _, w = q3.shape
    n_pages = page_table.shape[1]
    page = ck.shape[2]
    assert page == LANES and n_pages % PAGES_PER_STEP == 0 and LANES // HD_F == 2
    n_pg = PAGES_PER_STEP
    steps = n_pages // n_pg

    def page_of(i, s, g, pt):
        return pt[i, n_pages - 1 - (s * n_pg + g)]

    def key_idx(g):
        return lambda i, s, pt: (page_of(i, jnp.minimum(s, steps - 1), g, pt), 0, 0)

    def val_idx(g):
        return lambda i, s, pt: (page_of(i, jnp.maximum(s - steps, 0), g, pt), 0, 0)

    row_spec = lambda width: pl.BlockSpec((None, 1, width), lambda i, s, pt: (i, 0, 0))
    const_spec = lambda width: pl.BlockSpec((1, width), lambda i, s, pt: (0, 0))
    in_specs = [row_spec(w), row_spec(w), row_spec(w), row_spec(LANES), const_spec(w), const_spec(w), const_spec(LANES)]
    args = [q3, k3, v3, small3, qg_row, kg_row, bf_row]
    for g in range(n_pg):
        in_specs += [pl.BlockSpec((None, w, page), key_idx(g)),
                     pl.BlockSpec((None, w, page), val_idx(g)),
                     pl.BlockSpec((None, H_F, page), key_idx(g))]
        args += [ck, cv, clf_t]
    col = lambda: pltpu.VMEM((HP, 1), F32)
    grid_spec = pltpu.PrefetchScalarGridSpec(
        num_scalar_prefetch=1,
        grid=(bd, 2 * steps),
        in_specs=in_specs,
        out_specs=[row_spec(w), row_spec(w), row_spec(LANES)],
        scratch_shapes=[pltpu.VMEM((HP, w), BF16), pltpu.VMEM((HP, w), BF16), col(), col(), col(),
                        pltpu.VMEM((HP, w), F32), col(), pltpu.VMEM((n_pages, HP, LANES), F32)],
    )
    return pl.pallas_call(
        functools.partial(_decode_attn_body, n_pg=n_pg),
        grid_spec=grid_spec,
        out_shape=[jax.ShapeDtypeStruct((bd, 1, w), F32), jax.ShapeDtypeStruct((bd, 1, w), F32),
                   jax.ShapeDtypeStruct((bd, 1, LANES), F32)],
        compiler_params=_cparams("parallel", "arbitrary"),
        name="decode_attn",
    )(page_table, *args)


def _merge_body(x_ref, oa_ref, z_ref, of_ref, ga_ref, gf_ref, na_ref, pa_ref, pf_ref, wo_ref, gffn_ref,
                wr_ref, br_ref, x1_ref, h2_ref, tw_ref, ti_ref):
    o = oa_ref[...]
    z = z_ref[...]
    na = na_ref[...]
    parts = []
    for h in range(H_A):
        oh = o[:, h * DV_A:(h + 1) * DV_A]
        zh = z[:, h * DV_A:(h + 1) * DV_A]
        on = oh * lax.rsqrt(jnp.mean(oh * oh, axis=-1, keepdims=True) + EPS) * na
        parts.append((on * (zh * _sigmoid(zh))).astype(BF16))
    o_a = jnp.concatenate(parts, axis=-1)
    ya = _dot(o_a, pa_ref[...])
    yf = _dot(of_ref[...].astype(BF16), pf_ref[...])
    mixed = _sigmoid(ga_ref[...]) * ya + _sigmoid(gf_ref[...]) * yf
    x1 = x_ref[...] + _dot(mixed.astype(BF16), wo_ref[...])
    x1_ref[...] = x1
    h2 = x1 * lax.rsqrt(jnp.mean(x1 * x1, axis=-1, keepdims=True) + EPS) * gffn_ref[...]
    h2_ref[...] = h2
    logits = _dot(h2.astype(BF16), wr_ref[...]) + br_ref[...]
    lane = lax.broadcasted_iota(jnp.int32, logits.shape, 1)
    l = jnp.where(lane < N_EXP, logits, -jnp.inf)
    vals, idxs = [], []
    for _ in range(TOP_K):
        m = jnp.max(l, axis=-1, keepdims=True)
        idx = jnp.min(jnp.where(l == m, lane, LANES), axis=-1, keepdims=True)
        vals.append(m)
        idxs.append(idx)
        l = jnp.where(lane == idx, -jnp.inf, l)
    es = [jnp.exp(v - vals[0]) for v in vals]
    den = es[0]
    for e in es[1:]:
        den = den + e
    tw = jnp.zeros(logits.shape, F32)
    ti = jnp.zeros(logits.shape, jnp.int32)
    for kk in range(TOP_K):
        tw = jnp.where(lane == kk, es[kk] / den, tw)
        ti = jnp.where(lane == kk, idxs[kk], ti)
    tw_ref[...] = tw
    ti_ref[...] = ti


def _merge(x3, oa3, z3, of3, ga3, gf3, na_row, pa, pf, wo, gffn_row, wr, br_row):
    g, t, d = x3.shape
    tm = _row_tile(t, 384)
    tok = lambda c: pl.BlockSpec((None, tm, c), lambda i, j: (i, j, 0))
    const = lambda a: pl.BlockSpec(a.shape, lambda i, j: (0,) * a.ndim, pipeline_mode=pl.Buffered(1))
    wa = H_A * DV_A
    wf = H_F * HD_F
    return pl.pallas_call(
        _merge_body,
        grid=(g, t // tm),
        in_specs=[tok(d), tok(wa), tok(wa), tok(wf), tok(d), tok(d),
                  const(na_row), const(pa), const(pf), const(wo), const(gffn_row),
                  const(wr), const(br_row)],
        out_specs=[tok(d), tok(d), tok(LANES), tok(LANES)],
        out_shape=[jax.ShapeDtypeStruct((g, t, d), F32), jax.ShapeDtypeStruct((g, t, d), F32),
                   jax.ShapeDtypeStruct((g, t, LANES), F32), jax.ShapeDtypeStruct((g, t, LANES), jnp.int32)],
        compiler_params=_cparams("parallel", "parallel"),
        name="merge",
    )(x3, oa3, z3, of3, ga3, gf3, na_row, pa, pf, wo, gffn_row, wr, br_row)


def _moe_body(be_ref, nv_ref, x_ref, wg_ref, bg_ref, wu_ref, bu_ref, wd_ref, bd_ref, y_ref, wg_b, wu_b, wd_b):
    i = pl.program_id(0)

    @pl.when(jnp.logical_or(i == 0, be_ref[i] != be_ref[jnp.maximum(i - 1, 0)]))
    def _():
        wg_b[...] = wg_ref[...].astype(BF16)
        wu_b[...] = wu_ref[...].astype(BF16)
        wd_b[...] = wd_ref[...].astype(BF16)

    @pl.when(i < nv_ref[0])
    def _():
        x = x_ref[...].astype(BF16)
        gate = jnp.minimum(_dot(x, wg_b[...]) + bg_ref[...], SWIGLU_LIMIT)
        up = jnp.clip(_dot(x, wu_b[...]) + bu_ref[...], -SWIGLU_LIMIT, SWIGLU_LIMIT)
        act = (up + 1.0) * gate * _sigmoid(SWIGLU_ALPHA * gate)
        y_ref[...] = _dot(act.astype(BF16), wd_b[...]) + bd_ref[...]

    @pl.when(i >= nv_ref[0])
    def _():
        y_ref[...] = jnp.zeros(y_ref.shape, y_ref.dtype)


def _moe_experts(blk_e, n_valid, xs, wg, bg, wu, bu, wd, bd):
    n_rows, d = xs.shape
    bm = MOE_BLOCK
    n_blk = n_rows // bm
    de = wg.shape[-1]
    row_idx = lambda i, be, nv: (jnp.minimum(i, nv[0] - 1), 0)
    wspec = lambda a, b: pl.BlockSpec((None, a, b), lambda i, be, nv: (be[i], 0, 0))
    grid_spec = pltpu.PrefetchScalarGridSpec(
        num_scalar_prefetch=2,
        grid=(n_blk,),
        in_specs=[pl.BlockSpec((bm, d), row_idx),
                  wspec(d, de), wspec(1, de), wspec(d, de), wspec(1, de), wspec(de, d), wspec(1, d)],
        out_specs=pl.BlockSpec((bm, d), lambda i, be, nv: (i, 0)),
        scratch_shapes=[pltpu.VMEM((d, de), BF16), pltpu.VMEM((d, de), BF16), pltpu.VMEM((de, d), BF16)],
    )
    return pl.pallas_call(
        _moe_body,
        grid_spec=grid_spec,
        out_shape=jax.ShapeDtypeStruct((n_rows, d), F32),
        compiler_params=_cparams("arbitrary"),
        name="moe_experts",
    )(blk_e, n_valid, xs, wg, bg, wu, bu, wd, bd)


def _moe(h2, top_w, top_i, wg, bg, wu, bu, wd, bd):
    n, d = h2.shape
    bm = MOE_BLOCK
    m = n * TOP_K
    n_blk = -(-(m + N_EXP * (bm - 1)) // bm)
    flat_e = top_i.reshape(-1)
    onehot = (flat_e[:, None] == jnp.arange(N_EXP, dtype=jnp.int32)[None, :]).astype(jnp.int32)
    csum = jnp.cumsum(onehot, axis=0)
    rank = jnp.take_along_axis(csum, flat_e[:, None], axis=1)[:, 0] - 1
    counts = csum[-1]
    padded = (counts + bm - 1) // bm * bm
    pend = jnp.cumsum(padded)
    pstart = pend - padded
    dest = pstart[flat_e] + rank
    n_valid = (pend[-1] // bm).astype(jnp.int32)
    blk_ids = jnp.arange(n_blk, dtype=jnp.int32)
    expert_of = lambda blk: jnp.minimum(jnp.sum((pend[None, :] <= (blk * bm)[:, None]).astype(jnp.int32), axis=1), N_EXP - 1)
    blk_e = expert_of(jnp.minimum(blk_ids, n_valid - 1))
    shift = (m - 1).bit_length()
    assert N_EXP << shift < 2 ** 31
    pair_sorted = jnp.sort((flat_e << shift) | jnp.arange(m, dtype=jnp.int32)) & ((1 << shift) - 1)
    e_of_blk = expert_of(blk_ids)
    start = jnp.cumsum(counts) - counts
    rank = (blk_ids * bm - pstart[e_of_blk])[:, None] + jnp.arange(bm, dtype=jnp.int32)[None, :]
    src = jnp.clip(start[e_of_blk][:, None] + rank, 0, m - 1)
    row_tok = jnp.where(rank < counts[e_of_blk][:, None], pair_sorted[src.reshape(-1)].reshape(n_blk, bm) // TOP_K, n)
    row_tok = row_tok.reshape(-1)
    h_ext = jnp.concatenate([h2, jnp.zeros((1, d), h2.dtype)], axis=0)
    xs = h_ext[row_tok]
    y = _moe_experts(blk_e, n_valid.reshape(1), xs, wg, bg, wu, bu, wd, bd)
    yg = y[dest.reshape(n, TOP_K).T.reshape(-1)].reshape(TOP_K, n, d)
    return yg, top_w.T[:, :, None]


def _mix(yg, w, lo, hi):
    out = yg[0, lo:hi] * w[0, lo:hi]
    for k in range(1, TOP_K):
        out = out + yg[k, lo:hi] * w[k, lo:hi]
    return out


def _lane_row(vals, offset, width=LANES):
    return jnp.zeros((1, width), F32).at[0, offset:offset + vals.shape[0]].set(vals.astype(F32))


def kernel(x_prompt, x_sample, cache_k, cache_v, cache_logf, state_delta, state_conv, page_table,
           meta_tokens, g_mix, w_in, conv_w, a_log, dt_bias, norm_a, qn_g, kn_g, b_forget,
           p_a, p_f, w_o, g_ffn, w_router, b_router, w_gate, b_gate, w_up, b_up, w_down, b_down):
    depth = w_in.shape[0]
    b, seq, d = x_prompt.shape
    bd = x_sample.shape[0]
    assert x_sample.shape[1] == 1 and DK_A == LANES and DV_A == LANES
    t = N_META + seq
    lead = (-N_META) % CHUNK
    t_pad = -(-t // ATT_BLOCK) * ATT_BLOCK
    qkv_w, va_w, wf = 2 * H_A * DK_A + H_A * DV_A, H_A * DV_A, H_F * HD_F
    sizes = (qkv_w, va_w, H_A, H_A, wf, wf, wf, H_F, d, d)
    offs = [0]
    for s_ in sizes:
        offs.append(offs[-1] + s_)
    col = lambda i: slice(offs[i], offs[i + 1])
    widths = (qkv_w, va_w, wf, wf, wf, d, d)

    xp = jnp.concatenate([jnp.broadcast_to(meta_tokens.astype(x_prompt.dtype)[None], (b, N_META, d)), x_prompt], axis=1)
    xs = x_sample
    new_p = [[] for _ in range(5)]
    new_s = [[] for _ in range(5)]
    for l in range(depth):
        wl = w_in[l]
        w_main = jnp.concatenate([wl[:, col(0)], wl[:, col(1)], wl[:, col(4)], wl[:, col(5)], wl[:, col(6)],
                                  wl[:, col(8)], wl[:, col(9)]], axis=1).astype(BF16)
        w_small = jnp.concatenate([wl[:, col(2)], wl[:, col(3)], wl[:, col(7)],
                                   jnp.zeros((d, LANES - 2 * H_A - H_F), F32)], axis=1).astype(BF16)
        g_row = g_mix[l].reshape(1, d)
        a_row = _lane_row(a_log[l], H_A)
        dt_row = _lane_row(dt_bias[l], H_A)
        bf_row = _lane_row(b_forget[l], 2 * H_A)
        qg_pair = jnp.tile(qn_g[l], LANES // HD_F).reshape(1, LANES)
        kg_pair = jnp.tile(kn_g[l], LANES // HD_F).reshape(1, LANES)
        qg_full = jnp.tile(qn_g[l], H_F).reshape(1, wf)
        kg_full = jnp.tile(kn_g[l], H_F).reshape(1, wf)
        na_row = norm_a[l].reshape(1, DV_A)
        pa_b, pf_b, wo_b = p_a[l].astype(BF16), p_f[l].astype(BF16), w_o[l].astype(BF16)
        gffn_row = g_ffn[l].reshape(1, d)
        wr = jnp.pad(w_router[l], ((0, 0), (0, LANES - N_EXP))).astype(BF16)
        br_row = _lane_row(b_router[l], 0)
        merge_w = (na_row, pa_b, pf_b, wo_b, gffn_row, wr, br_row)

        qkv_p, z_p, qf_p, kf_p, vf_p, ga_p, gf_p, sm_p = _in_proj(xp.reshape(b * t, d), g_row, w_main, w_small, widths)
        r3 = lambda a: a.reshape(b, t, a.shape[-1])
        qkv3, sm3 = r3(qkv_p), r3(sm_p)
        conv_p = qkv3[:, t - (CONV_W - 1):, :]
        prep = _conv_prep(qkv3, conv_w[l], jnp.zeros((b, CONV_W - 1, qkv_w), F32))
        oa_p, s_p = _delta_scan(prep, sm3, a_row, dt_row, jnp.zeros((b, H_A, DK_A, DV_A), F32), lead)
        lf_p, c_col = _logf_cumsum(sm3, bf_row, t_pad)
        qx, kx, vb, kn_p = _fox_prep(r3(qf_p), r3(kf_p), r3(vf_p), c_col, qg_pair, kg_pair, t_pad)
        of_p = _fox_attn(qx, kx, vb)
        x1_p, h2_p, tw_p, ti_p = _merge(xp, oa_p, r3(z_p), of_p, r3(ga_p), r3(gf_p), *merge_w)

        qkv_s, z_s, qf_s, kf_s, vf_s, ga_s, gf_s, sm_s = _in_proj(xs.reshape(bd, d), g_row, w_main, w_small, widths)
        s3 = lambda a: a.reshape(bd, 1, a.shape[-1])
        oa_s, s_s, conv_s = _decode_delta(s3(qkv_s), state_conv[l], s3(sm_s), conv_w[l], a_row, dt_row, state_delta[l])
        n_pool, page = cache_k.shape[1], cache_k.shape[2]
        of_s, kn_s, lf_s = _decode_attn(page_table, s3(qf_s), s3(kf_s), s3(vf_s), s3(sm_s), qg_full, kg_full,
                                        _lane_row(b_forget[l], 2 * H_A),
                                        jnp.transpose(cache_k[l], (0, 2, 3, 1)).reshape(n_pool, wf, page),
                                        jnp.transpose(cache_v[l], (0, 2, 3, 1)).reshape(n_pool, wf, page),
                                        jnp.swapaxes(cache_logf[l], 1, 2))
        g1 = lambda a: a.reshape(1, bd, a.shape[-1])
        x1_s, h2_s, tw_s, ti_s = _merge(g1(xs), g1(oa_s), g1(z_s), g1(of_s), g1(ga_s), g1(gf_s), *merge_w)

        n_p = b * t
        h2_all = jnp.concatenate([h2_p.reshape(n_p, d), h2_s.reshape(bd, d)], axis=0)
        tw_all = jnp.concatenate([tw_p.reshape(n_p, LANES), tw_s.reshape(bd, LANES)], axis=0)[:, :TOP_K]
        ti_all = jnp.concatenate([ti_p.reshape(n_p, LANES), ti_s.reshape(bd, LANES)], axis=0)[:, :TOP_K]
        yg, yw = _moe(h2_all, tw_all, ti_all, w_gate[l], b_gate[l][:, None, :],
                      w_up[l], b_up[l][:, None, :], w_down[l], b_down[l][:, None, :])
        xp = x1_p + _mix(yg, yw, 0, n_p).reshape(b, t, d)
        xs = x1_s.reshape(bd, 1, d) + _mix(yg, yw, n_p, n_p + bd).reshape(bd, 1, d)

        st_p = (kn_p.reshape(b, t, H_F, HD_F), vf_p.reshape(b, t, H_F, HD_F),
                lf_p[:, :, 2 * H_A:2 * H_A + H_F], s_p, conv_p)
        st_s = (kn_s.reshape(bd, 1, H_F, HD_F), vf_s.reshape(bd, 1, H_F, HD_F),
                lf_s[:, :, 2 * H_A:2 * H_A + H_F], s_s, conv_s)
        for lst, a in zip(new_p, st_p):
            lst.append(a)
        for lst, a in zip(new_s, st_s):
            lst.append(a)
    k_p, v_p, lf_pp, d_p, c_p = (jnp.stack(a) for a in new_p)
    k_s, v_s, lf_ss, d_s, c_s = (jnp.stack(a) for a in new_s)
    return (xp[:, N_META:], xs, k_p, v_p, lf_pp, k_s, v_s, lf_ss, d_p, d_s, c_p, c_s)
```

```python
import functools

import jax
import jax.numpy as jnp
from jax import lax
from jax.experimental import pallas as pl
from jax.experimental.pallas import tpu as pltpu

F32 = jnp.float32
BF16 = jnp.bfloat16

N_META = 16
H_A = 4
DK_A = 128
DV_A = 128
CONV_W = 4
CHUNK = 64
H_F = 8
HD_F = 64
N_EXP = 32
TOP_K = 4
SWIGLU_LIMIT = 7.0
SWIGLU_ALPHA = 1.702
EPS = 1e-6
LOG2E = 1.4426950408889634

LANES = 128
F_OFF = 2 * H_A
HP = 16
ATT_BLOCK = 256
MOE_BLOCK = 512
PAGES_PER_STEP = 16
VMEM_LIMIT = 48 * 1024 * 1024
MOE_VMEM_LIMIT = 58 * 1024 * 1024


def _cparams(*sem):
    return pltpu.CompilerParams(dimension_semantics=sem, vmem_limit_bytes=VMEM_LIMIT)


def _row_tile(n, cap):
    if n <= cap:
        return n
    best = None
    for t in range(8, cap + 1, 8):
        if n % t == 0:
            best = t
    assert best is not None, n
    return best


def _dot(a, b):
    return jnp.dot(a, b, preferred_element_type=F32)


def _dot_nt(a, b):
    return lax.dot_general(a, b, (((1,), (1,)), ((), ())), preferred_element_type=F32)


def _split2(x):
    hi = x.astype(BF16)
    lo = (x - hi.astype(F32)).astype(BF16)
    return hi, lo


def _dot_x3(a, b):
    a_hi, a_lo = _split2(a)
    b_hi, b_lo = _split2(b)
    return _dot(a_hi, b_hi) + (_dot(a_hi, b_lo) + _dot(a_lo, b_hi))


def _split3(x):
    hi = x.astype(BF16)
    r = x - hi.astype(F32)
    mid = r.astype(BF16)
    lo = (r - mid.astype(F32)).astype(BF16)
    return hi, mid, lo


def _sigmoid(x):
    return 1.0 / (1.0 + jnp.exp(-x))


def _softplus(x):
    return jnp.maximum(x, 0.0) + jnp.log1p(jnp.exp(-jnp.abs(x)))


def _log_sigmoid(x):
    return jnp.minimum(x, 0.0) - jnp.log1p(jnp.exp(-jnp.abs(x)))


IN_WIDTHS = (2 * H_A * DK_A + H_A * DV_A, H_A * DV_A, H_F * HD_F, H_F * HD_F, H_F * HD_F)
COL_CHUNK = 512


def _in_proj_body(x_ref, g_ref, wm_ref, ws_ref, *out_refs, widths):
    x = x_ref[...]
    h = x * lax.rsqrt(jnp.mean(x * x, axis=-1, keepdims=True) + EPS) * g_ref[...]
    hb = h.astype(BF16)
    col = 0
    for o_ref, w in zip(out_refs[:-1], widths):
        for c0 in range(0, w, COL_CHUNK):
            o_ref[:, c0:c0 + COL_CHUNK] = _dot(hb, wm_ref[:, col + c0:col + c0 + COL_CHUNK])
        col += w
    out_refs[-1][...] = _dot(hb, ws_ref[...])


def _in_proj(x2d, g_row, w_main, w_small, widths):
    n, d = x2d.shape
    tm = _row_tile(n, 512)
    outs = [jax.ShapeDtypeStruct((n, w), F32) for w in widths] + [jax.ShapeDtypeStruct((n, LANES), F32)]
    out_specs = [pl.BlockSpec((tm, w), lambda i: (i, 0)) for w in widths] + [pl.BlockSpec((tm, LANES), lambda i: (i, 0))]
    return pl.pallas_call(
        functools.partial(_in_proj_body, widths=widths),
        grid=(n // tm,),
        in_specs=[pl.BlockSpec((tm, d), lambda i: (i, 0)),
                  pl.BlockSpec((1, d), lambda i: (0, 0)),
                  pl.BlockSpec(w_main.shape, lambda i: (0, 0), pipeline_mode=pl.Buffered(1)),
                  pl.BlockSpec(w_small.shape, lambda i: (0, 0), pipeline_mode=pl.Buffered(1))],
        out_specs=out_specs,
        out_shape=outs,
        compiler_params=_cparams("parallel"),
        name="in_proj",
    )(x2d, g_row, w_main, w_small)


def _conv_prep_body(x_ref, cw_ref, prev_ref, o_ref, buf, *, t_len, rows):
    j = pl.program_id(1)
    pad = 8
    buf[0:pad, :] = jnp.zeros((pad, LANES), F32)
    buf[pad - (CONV_W - 1):pad, :] = prev_ref[...]
    buf[pad:pad + t_len, :] = x_ref[...]
    cw = cw_ref[...]
    is_q = j < H_A
    is_qk = j < 2 * H_A
    for r0 in range(0, t_len, rows):
        y = jnp.zeros((rows, LANES), F32)
        for w in range(CONV_W):
            off = pad - (CONV_W - 1) + w + r0
            y = y + buf[off:off + rows, :] * cw[w:w + 1, :]
        y = y * _sigmoid(y)
        nrm = lax.rsqrt(jnp.sum(y * y, axis=-1, keepdims=True) + EPS)
        f = jnp.where(is_qk, nrm * jnp.where(is_q, DK_A ** -0.5, 1.0), 1.0)
        o_ref[r0:r0 + rows, :] = y * f


def _conv_prep(qkv3, conv_w, conv_prev):
    b, t, c = qkv3.shape
    rows = _row_tile(t, 512)
    return pl.pallas_call(
        functools.partial(_conv_prep_body, t_len=t, rows=rows),
        grid=(b, c // LANES),
        in_specs=[pl.BlockSpec((None, t, LANES), lambda i, j: (i, 0, j)),
                  pl.BlockSpec((CONV_W, LANES), lambda i, j: (0, j)),
                  pl.BlockSpec((None, CONV_W - 1, LANES), lambda i, j: (i, 0, j))],
        out_specs=pl.BlockSpec((None, t, LANES), lambda i, j: (i, 0, j)),
        out_shape=jax.ShapeDtypeStruct((b, t, c), F32),
        scratch_shapes=[pltpu.VMEM((8 + t, LANES), F32)],
        compiler_params=_cparams("parallel", "parallel"),
        name="conv_prep",
    )(qkv3, conv_w, conv_prev)


def _gates(sm, a_row, dt_row, h):
    lane = lax.broadcasted_iota(jnp.int32, sm.shape, 1)
    beta_all = _sigmoid(sm)
    g_all = -jnp.exp(a_row) * _softplus(sm + dt_row)
    beta = jnp.sum(jnp.where(lane == h, beta_all, 0.0), axis=-1, keepdims=True)
    g = jnp.sum(jnp.where(lane == H_A + h, g_all, 0.0), axis=-1, keepdims=True)
    return beta, g


def _decay_terms(gs):
    c = gs[0].shape[0]
    ri = lax.broadcasted_iota(jnp.int32, (c, c), 0)
    ci = lax.broadcasted_iota(jnp.int32, (c, c), 1)
    g_cum, decay = [], []
    for g in gs:
        g_row = jnp.sum(jnp.where(ri == ci, g, 0.0), axis=0, keepdims=True)
        gc = jnp.sum(jnp.where(ci <= ri, g_row, 0.0), axis=1, keepdims=True)
        gc_row = jnp.sum(jnp.where(ri <= ci, g, 0.0), axis=0, keepdims=True)
        g_cum.append(gc)
        decay.append(jnp.exp(jnp.where(ci <= ri, gc - gc_row, -jnp.inf)))
    return g_cum, decay


def _delta_step(cur, nxt, s_ref):
    o = inv_n = None
    stages = []
    if cur is not None:
        qs, ks, vs, betas, gs, inv = cur
        rng = range(len(qs))
        c = qs[0].shape[0]
        g_cum, decay = _decay_terms(gs)
        e_g = [jnp.exp(g_cum[i]) for i in rng]
        s = [s_ref[i] for i in rng]
        sb = [s[i].astype(BF16) for i in rng]
        qb = [qs[i].astype(BF16) for i in rng]
        kb = [ks[i].astype(BF16) for i in rng]
        k_s = [_dot(kb[i], sb[i]) for i in rng]
        val = {}

        def st_delta():
            rhs = [betas[i] * (vs[i] - e_g[i] * k_s[i]) for i in rng]
            val["db"] = [_dot_x3(inv[i], rhs[i]).astype(BF16) for i in rng]
            val["qk"] = [_dot_nt(qb[i], kb[i]) for i in rng]
            val["q_s"] = [_dot(qb[i], sb[i]) for i in rng]

        def st_out():
            val["o"] = [e_g[i] * val["q_s"][i] + _dot((val["qk"][i] * decay[i]).astype(BF16), val["db"][i]) for i in rng]

        def st_state():
            g_end = [g_cum[i][c - 1:c, :] for i in rng]
            k_dec = [(ks[i] * jnp.exp(g_end[i] - g_cum[i])).T.astype(BF16) for i in rng]
            for i in rng:
                s_ref[i] = jnp.exp(g_end[i]) * s[i] + _dot(k_dec[i], val["db"][i])

        stages = [st_delta, st_out, st_state]
    if nxt is not None:
        ks_n, betas_n, gs_n = nxt
        rng_n = range(len(ks_n))
        c = ks_n[0].shape[0]
        ri = lax.broadcasted_iota(jnp.int32, (c, c), 0)
        ci = lax.broadcasted_iota(jnp.int32, (c, c), 1)
        _, decay_n = _decay_terms(gs_n)
        kb_n = [ks_n[i].astype(BF16) for i in rng_n]
        kk = [_dot_nt(kb_n[i], kb_n[i]) for i in rng_n]
        a = [jnp.where(ci < ri, betas_n[i] * decay_n[i] * kk[i], 0.0) for i in rng_n]
        inv_n = [jnp.where(ri == ci, 1.0, 0.0) - a[i] for i in rng_n]
        pw = a
        for _ in range(max(1, (c - 1).bit_length() - 1)):
            if stages:
                stages.pop(0)()
            pw = [_dot_x3(pw[i], pw[i]) for i in rng_n]
            inv_n = [inv_n[i] + _dot_x3(inv_n[i], pw[i]) for i in rng_n]
    for st in stages:
        st()
    if cur is not None:
        o = val["o"]
    return o, inv_n


def _delta_scan_body(qkv_ref, sm_ref, a_ref, dt_ref, s0_ref, o_ref, s_out_ref, s_ref, *, t_len, lead):
    a_row = a_ref[...]
    dt_row = dt_ref[...]
    s_ref[...] = s0_ref[...]
    first = CHUNK - lead
    n_chunks = (lead + t_len) // CHUNK

    def gates(load, mask_lead):
        sm = load(sm_ref, 0)
        betas, gs = [], []
        for h in range(H_A):
            beta, g = _gates(sm, a_row, dt_row, h)
            if mask_lead:
                row = lax.broadcasted_iota(jnp.int32, (CHUNK, 1), 0)
                beta = jnp.where(row >= lead, beta, 0.0)
                g = jnp.where(row >= lead, g, 0.0)
            betas.append(beta)
            gs.append(g)
        return betas, gs

    def nxt_args(load, mask_lead):
        betas, gs = gates(load, mask_lead)
        return [load(qkv_ref, (H_A + h) * DK_A) for h in range(H_A)], betas, gs

    def cur_args(load, mask_lead, inv):
        betas, gs = gates(load, mask_lead)
        qs = [load(qkv_ref, h * DK_A) for h in range(H_A)]
        ks = [load(qkv_ref, (H_A + h) * DK_A) for h in range(H_A)]
        vs = [load(qkv_ref, 2 * H_A * DK_A + h * DV_A) for h in range(H_A)]
        return qs, ks, vs, betas, gs, inv

    def chunk_rows(c):
        return pl.ds(pl.multiple_of(c * CHUNK - lead, 8), CHUNK)

    def loader(c):
        sl = chunk_rows(c)
        return lambda ref, col: ref[sl, col:col + LANES]

    if lead:
        def load0(ref, col):
            return jnp.concatenate([jnp.zeros((lead, LANES), F32), ref[0:first, col:col + LANES]], axis=0)

        def store0(h, o):
            o_ref[0:first, h * DV_A:(h + 1) * DV_A] = o[lead:, :]

        _, inv0 = _delta_step(None, nxt_args(load0, True), s_ref)
        has_next = n_chunks > 1
        outs, inv1 = _delta_step(cur_args(load0, True, inv0), nxt_args(loader(1), False) if has_next else None, s_ref)
        for h, o in enumerate(outs):
            store0(h, o)
        c_start = 1
    else:
        has_next = n_chunks > 0
        _, inv1 = _delta_step(None, nxt_args(loader(0), False), s_ref) if has_next else (None, None)
        c_start = 0

    def body(c, inv):
        sl = chunk_rows(c)
        outs, inv_n = _delta_step(cur_args(loader(c), False, inv),
                                  nxt_args(loader(jnp.minimum(c + 1, n_chunks - 1)), False), s_ref)
        for h, o in enumerate(outs):
            o_ref[sl, h * DV_A:(h + 1) * DV_A] = o
        return tuple(inv_n)

    if has_next:
        lax.fori_loop(c_start, n_chunks, body, tuple(inv1))
    s_out_ref[...] = s_ref[...]


def _delta_scan(qkv3, small3, a_row, dt_row, s0, lead):
    b, t, c = qkv3.shape
    assert (lead + t) % CHUNK == 0 and lead % 8 == 0
    return pl.pallas_call(
        functools.partial(_delta_scan_body, t_len=t, lead=lead),
        grid=(b,),
        in_specs=[pl.BlockSpec((None, t, c), lambda i: (i, 0, 0)),
                  pl.BlockSpec((None, t, LANES), lambda i: (i, 0, 0)),
                  pl.BlockSpec((1, LANES), lambda i: (0, 0)),
                  pl.BlockSpec((1, LANES), lambda i: (0, 0)),
                  pl.BlockSpec((None, H_A, DK_A, DV_A), lambda i: (i, 0, 0, 0))],
        out_specs=[pl.BlockSpec((None, t, H_A * DV_A), lambda i: (i, 0, 0)),
                   pl.BlockSpec((None, H_A, DK_A, DV_A), lambda i: (i, 0, 0, 0))],
        out_shape=[jax.ShapeDtypeStruct((b, t, H_A * DV_A), F32),
                   jax.ShapeDtypeStruct((b, H_A, DK_A, DV_A), F32)],
        scratch_shapes=[pltpu.VMEM((H_A, DK_A, DV_A), F32)],
        compiler_params=_cparams("parallel"),
        name="delta_scan",
    )(qkv3, small3, a_row, dt_row, s0)


def _group_rms(x, gain, seg):
    hi, lo = _split2(x * x)
    ss = _dot(hi, seg) + _dot(lo, seg)
    return x * lax.rsqrt(ss * (1.0 / HD_F) + EPS) * gain


def _seg_matrix(n):
    r = lax.broadcasted_iota(jnp.int32, (n, n), 0) // HD_F
    c = lax.broadcasted_iota(jnp.int32, (n, n), 1) // HD_F
    return jnp.where(r == c, 1.0, 0.0).astype(BF16)


def _fox_prep_body(q_ref, k_ref, v_ref, c_ref, qg_ref, kg_ref, qx_ref, kx_ref, vt_ref, kn_ref, *, t_len, t_pad):
    p = pl.program_id(1)
    heads = LANES // HD_F

    def feature_major(x):
        if t_pad > t_len:
            x = jnp.concatenate([x, jnp.zeros((t_pad - t_len, LANES), F32)], axis=0)
        return x.T.astype(BF16)

    seg = _seg_matrix(LANES)
    qn = _group_rms(q_ref[...], qg_ref[...], seg) * (HD_F ** -0.5 * LOG2E)
    kn = _group_rms(k_ref[...], kg_ref[...], seg)
    kn_ref[...] = kn
    c_all = c_ref[...] * LOG2E
    lane = lax.broadcasted_iota(jnp.int32, (t_len, LANES), 1)
    for hh in range(heads):
        c = jnp.sum(jnp.where(lane == F_OFF + p * heads + hh, c_all, 0.0), axis=-1, keepdims=True)
        pieces = [x.astype(F32) for x in _split3(c)]
        own = (lane // HD_F) == hh
        f0 = ((hh + 1) % heads) * HD_F
        qx = jnp.where(own, qn, 0.0)
        kx = jnp.where(own, kn, 0.0)
        for n, piece in enumerate(pieces):
            qx = jnp.where(lane == f0 + n, piece, qx)
            kx = jnp.where(lane == f0 + n, 1.0, kx)
            qx = jnp.where(lane == f0 + 3 + n, 1.0, qx)
            kx = jnp.where(lane == f0 + 3 + n, -piece, kx)
        kx_ref[0:t_len, hh * LANES:(hh + 1) * LANES] = kx.astype(BF16)
        qx_ref[hh * LANES:(hh + 1) * LANES, :] = feature_major(qx)
    vt_ref[...] = feature_major(v_ref[...])
    if t_pad > t_len:
        kx_ref[t_len:, :] = jnp.zeros((t_pad - t_len, heads * LANES), BF16)


def _fox_prep(q3, k3, v3, c_col, qg_row, kg_row, t_pad):
    b, t, w = q3.shape
    heads = LANES // HD_F
    blk = pl.BlockSpec((None, t, LANES), lambda i, p: (i, 0, p))
    gblk = pl.BlockSpec((1, LANES), lambda i, p: (0, 0))
    return pl.pallas_call(
        functools.partial(_fox_prep_body, t_len=t, t_pad=t_pad),
        grid=(b, w // LANES),
        in_specs=[blk, blk, blk, pl.BlockSpec((None, t, LANES), lambda i, p: (i, 0, 0)), gblk, gblk],
        out_specs=[pl.BlockSpec((None, heads * LANES, t_pad), lambda i, p: (i, p, 0)),
                   pl.BlockSpec((None, t_pad, heads * LANES), lambda i, p: (i, 0, p)),
                   pl.BlockSpec((None, LANES, t_pad), lambda i, p: (i, p, 0)), blk],
        out_shape=[jax.ShapeDtypeStruct((b, H_F * LANES, t_pad), BF16), jax.ShapeDtypeStruct((b, t_pad, H_F * LANES), BF16),
                   jax.ShapeDtypeStruct((b, w, t_pad), BF16), jax.ShapeDtypeStruct((b, t, w), F32)],
        compiler_params=_cparams("parallel", "parallel"),
        name="fox_prep",
    )(q3, k3, v3, c_col, qg_row, kg_row)


def _logf_cumsum_body(sm_ref, bf_ref, lf_ref, ccol_ref, buf, *, t_len, t_pad):
    lf = _log_sigmoid(sm_ref[...] + bf_ref[...])
    lf_ref[...] = lf
    buf[0:t_len, :] = lf
    if t_pad > t_len:
        buf[t_len:, :] = jnp.zeros((t_pad - t_len, LANES), F32)
    blk = ATT_BLOCK
    ri = lax.broadcasted_iota(jnp.int32, (blk, blk), 0)
    ci = lax.broadcasted_iota(jnp.int32, (blk, blk), 1)
    tri = jnp.where(ci <= ri, 1.0, 0.0).astype(BF16)
    carry = jnp.zeros((1, LANES), F32)
    for i in range(t_pad // blk):
        hi, mid, lo = _split3(buf[i * blk:(i + 1) * blk, :])
        c = _dot(tri, hi) + _dot(tri, mid) + _dot(tri, lo) + carry
        ccol_ref[i * blk:(i + 1) * blk, :] = c
        carry = c[blk - 1:blk, :]


def _logf_cumsum(small3, bf_row, t_pad):
    b, t, _ = small3.shape
    return pl.pallas_call(
        functools.partial(_logf_cumsum_body, t_len=t, t_pad=t_pad),
        grid=(b,),
        in_specs=[pl.BlockSpec((None, t, LANES), lambda i: (i, 0, 0)),
                  pl.BlockSpec((1, LANES), lambda i: (0, 0))],
        out_specs=[pl.BlockSpec((None, t, LANES), lambda i: (i, 0, 0)),
                   pl.BlockSpec((None, t_pad, LANES), lambda i: (i, 0, 0))],
        out_shape=[jax.ShapeDtypeStruct((b, t, LANES), F32),
                   jax.ShapeDtypeStruct((b, t_pad, LANES), F32)],
        scratch_shapes=[pltpu.VMEM((t_pad, LANES), F32)],
        compiler_params=_cparams("parallel"),
        name="logf_cumsum",
    )(small3, bf_row)


def _fox_attn_body(qt_ref, k_ref, vt_ref, o_ref):
    i = pl.program_id(1)
    blk = ATT_BLOCK
    key_le_query = (lax.broadcasted_iota(jnp.int32, (blk, blk), 0)
                    <= lax.broadcasted_iota(jnp.int32, (blk, blk), 1))

    def step(j, carry, diagonal):
        ms, ls, accs = carry
        k0 = pl.multiple_of(j * blk, blk)
        st = []
        for h in range(H_F):
            s = _dot(k_ref[pl.ds(k0, blk), h * LANES:(h + 1) * LANES], qt_ref[h * LANES:(h + 1) * LANES, :])
            st.append(jnp.where(key_le_query, s, -jnp.inf) if diagonal else s)
        new_ms, new_ls, alphas, pts = [], [], [], []
        for h in range(H_F):
            m_new = jnp.maximum(ms[h], jnp.max(st[h], axis=0, keepdims=True))
            alpha = jnp.exp2(ms[h] - m_new)
            pt = jnp.exp2(st[h] - m_new)
            new_ls.append(alpha * ls[h] + jnp.sum(pt, axis=0, keepdims=True))
            new_ms.append(m_new)
            alphas.append(alpha)
            pts.append(pt.astype(BF16))
        new_accs = [alphas[h] * accs[h] + _dot(vt_ref[h * HD_F:(h + 1) * HD_F, pl.ds(k0, blk)], pts[h])
                    for h in range(H_F)]
        return tuple(new_ms), tuple(new_ls), tuple(new_accs)

    init = (tuple(jnp.full((1, blk), -jnp.inf, F32) for _ in range(H_F)),
            tuple(jnp.zeros((1, blk), F32) for _ in range(H_F)),
            tuple(jnp.zeros((HD_F, blk), F32) for _ in range(H_F)))
    carry = lax.fori_loop(0, i, lambda j, c: step(j, c, False), init)
    ms, ls, accs = step(i, carry, True)
    heads = LANES // HD_F
    for p in range(H_F // heads):
        ot = jnp.concatenate([accs[p * heads + hh] * (1.0 / ls[p * heads + hh]) for hh in range(heads)], axis=0)
        o_ref[:, p * LANES:(p + 1) * LANES] = ot.T.astype(o_ref.dtype)


def _fox_attn(qt, kx, vt):
    b, w, t_pad = vt.shape
    blk = ATT_BLOCK
    return pl.pallas_call(
        _fox_attn_body,
        grid=(b, t_pad // blk),
        in_specs=[pl.BlockSpec((None, H_F * LANES, blk), lambda n, i: (n, 0, i)),
                  pl.BlockSpec((None, t_pad, H_F * LANES), lambda n, i: (n, 0, 0)),
                  pl.BlockSpec((None, w, t_pad), lambda n, i: (n, 0, 0))],
        out_specs=pl.BlockSpec((None, blk, w), lambda n, i: (n, i, 0)),
        out_shape=jax.ShapeDtypeStruct((b, t_pad, w), BF16),
        compiler_params=_cparams("parallel", "arbitrary"),
        name="fox_attn",
    )(qt, kx, vt)


def _decode_delta_body(x_ref, sc_ref, sm_ref, cw_ref, a_ref, dt_ref, s_ref, o_ref, s_out_ref, conv_ref):
    x_new = x_ref[...]
    sc = sc_ref[...]
    cw = cw_ref[...]
    y = x_new * cw[CONV_W - 1:CONV_W, :]
    for w in range(CONV_W - 1):
        y = y + sc[w:w + 1, :] * cw[w:w + 1, :]
    y = y * _sigmoid(y)
    conv_ref[0:CONV_W - 2, :] = sc[1:, :]
    conv_ref[CONV_W - 2:CONV_W - 1, :] = x_new
    sm = sm_ref[...]
    lane = lax.broadcasted_iota(jnp.int32, (1, LANES), 1)
    beta_all = _sigmoid(sm)
    g_all = -jnp.exp(a_ref[...]) * _softplus(sm + dt_ref[...])
    qk_w = H_A * DK_A
    row0 = lax.broadcasted_iota(jnp.int32, (LANES, LANES), 0) == 0

    def in_row0(r):
        return jnp.where(row0, jnp.broadcast_to(r, (LANES, LANES)), 0.0)

    for h in range(H_A):
        q = y[:, h * DK_A:(h + 1) * DK_A]
        k = y[:, qk_w + h * DK_A:qk_w + (h + 1) * DK_A]
        v = y[:, 2 * qk_w + h * DV_A:2 * qk_w + (h + 1) * DV_A]
        q = q * lax.rsqrt(jnp.sum(q * q, axis=-1, keepdims=True) + EPS) * (DK_A ** -0.5)
        k = k * lax.rsqrt(jnp.sum(k * k, axis=-1, keepdims=True) + EPS)
        beta = jnp.sum(jnp.where(lane == h, beta_all, 0.0), axis=-1, keepdims=True)
        g = jnp.sum(jnp.where(lane == H_A + h, g_all, 0.0), axis=-1, keepdims=True)
        e_g = jnp.exp(g)
        s = s_ref[h]
        sb = s.astype(BF16)
        k_sq = in_row0(k)
        k_s = _dot(k_sq.astype(BF16), sb)[0:1, :]
        q_s = _dot(in_row0(q).astype(BF16), sb)[0:1, :]
        delta = beta * (v - e_g * k_s)
        qk = jnp.sum(q * k, axis=-1, keepdims=True)
        o_ref[:, h * DV_A:(h + 1) * DV_A] = e_g * q_s + qk * delta
        s_out_ref[h] = e_g * s + _dot_x3(k_sq.T, in_row0(delta))


def _decode_delta(qkv3, state_conv, small3, conv_w, a_row, dt_row, s0):
    bd, _, c = qkv3.shape
    return pl.pallas_call(
        _decode_delta_body,
        grid=(bd,),
        in_specs=[pl.BlockSpec((None, 1, c), lambda i: (i, 0, 0)),
                  pl.BlockSpec((None, CONV_W - 1, c), lambda i: (i, 0, 0)),
                  pl.BlockSpec((None, 1, LANES), lambda i: (i, 0, 0)),
                  pl.BlockSpec((CONV_W, c), lambda i: (0, 0)),
                  pl.BlockSpec((1, LANES), lambda i: (0, 0)),
                  pl.BlockSpec((1, LANES), lambda i: (0, 0)),
                  pl.BlockSpec((None, H_A, DK_A, DV_A), lambda i: (i, 0, 0, 0))],
        out_specs=[pl.BlockSpec((None, 1, H_A * DV_A), lambda i: (i, 0, 0)),
                   pl.BlockSpec((None, H_A, DK_A, DV_A), lambda i: (i, 0, 0, 0)),
                   pl.BlockSpec((None, CONV_W - 1, c), lambda i: (i, 0, 0))],
        out_shape=[jax.ShapeDtypeStruct((bd, 1, H_A * DV_A), F32),
                   jax.ShapeDtypeStruct((bd, H_A, DK_A, DV_A), F32),
                   jax.ShapeDtypeStruct((bd, CONV_W - 1, c), F32)],
        compiler_params=_cparams("parallel"),
        name="decode_delta",
    )(qkv3, state_conv, small3, conv_w, a_row, dt_row, s0)


def _decode_attn_body(pt_ref, q_ref, k_ref, v_ref, sm_ref, qg_ref, kg_ref, bf_ref, *rest, n_pg):
    page_refs = rest[:3 * n_pg]
    o_ref, kn_ref, lf_ref, qrows, vnew, snew, m_ref, l_ref, acc_ref, carry_ref, s_scr = rest[3 * n_pg:]
    step = pl.program_id(1)
    n_steps = pl.num_programs(1) // 2
    w = H_F * HD_F
    row = lax.broadcasted_iota(jnp.int32, (HP, w), 0)
    head_of_lane = lax.broadcasted_iota(jnp.int32, (HP, w), 1) // HD_F

    @pl.when(step == 0)
    def _():
        seg = _seg_matrix(w)

        def rms_rows(x_row, gain):
            xr = jnp.broadcast_to(x_row, (HP, w))
            hi, lo = _split2(xr * xr)
            ss = _dot(hi, seg) + _dot(lo, seg)
            return xr * lax.rsqrt(ss * (1.0 / HD_F) + EPS) * gain

        qn = rms_rows(q_ref[...], qg_ref[...]) * (HD_F ** -0.5)
        kn = rms_rows(k_ref[...], kg_ref[...])
        kn_ref[...] = kn[0:1, :]
        lf = _log_sigmoid(sm_ref[...] + bf_ref[...])
        lf_ref[...] = lf
        q_m = jnp.where(head_of_lane == row, qn, 0.0).astype(BF16)
        qrows[...] = q_m
        vnew[...] = jnp.broadcast_to(v_ref[...], (HP, w)).astype(BF16)
        s_new = jnp.sum(q_m.astype(F32) * kn.astype(BF16).astype(F32), axis=-1, keepdims=True)
        snew[...] = s_new
        m_ref[...] = s_new
        rr = lax.broadcasted_iota(jnp.int32, (HP, LANES), 0)
        ll = lax.broadcasted_iota(jnp.int32, (HP, LANES), 1)
        carry_ref[...] = jnp.sum(jnp.where(ll == rr + F_OFF, jnp.broadcast_to(lf, (HP, LANES)), 0.0),
                                 axis=-1, keepdims=True)

    @pl.when(step < n_steps)
    def _():
        qb = qrows[...]
        lf_all = jnp.concatenate(
            [jnp.concatenate([page_refs[3 * g + 2][...], jnp.zeros((HP - H_F, LANES), F32)], axis=0)
             for g in range(n_pg)], axis=0)
        later = (lax.broadcasted_iota(jnp.int32, (LANES, LANES), 0)
                 > lax.broadcasted_iota(jnp.int32, (LANES, LANES), 1)).astype(BF16)
        hi, mid, lo = _split3(lf_all)
        after = _dot(hi, later) + (_dot(mid, later) + _dot(lo, later))
        totals = jnp.sum(lf_all, axis=-1, keepdims=True)
        raw = [_dot(qb, page_refs[3 * g][...].astype(BF16)) for g in range(n_pg)]
        carry = carry_ref[...]
        m_new = m_ref[...]
        for g in range(n_pg):
            rows = slice(g * HP, (g + 1) * HP)
            s = raw[g] + (carry + after[rows, :])
            s_scr[step * n_pg + g] = s
            m_new = jnp.maximum(m_new, jnp.max(s, axis=-1, keepdims=True))
            carry = carry + totals[rows, :]
        carry_ref[...] = carry
        m_ref[...] = m_new

    @pl.when(step == n_steps)
    def _():
        m = m_ref[...]
        p_new = jnp.exp(snew[...] - m)

        def add(i, tot):
            return tot + jnp.exp(s_scr[i] - m)

        tot = lax.fori_loop(0, n_steps * n_pg, add, jnp.zeros((HP, LANES), F32), unroll=8)
        l = p_new + jnp.sum(tot, axis=-1, keepdims=True)
        l_ref[...] = l
        acc_ref[...] = (p_new / l).astype(BF16).astype(F32) * vnew[...].astype(F32)

    @pl.when(step >= n_steps)
    def _():
        m = m_ref[...]
        l = l_ref[...]
        probs = [(jnp.exp(s_scr[(step - n_steps) * n_pg + g] - m) / l).astype(BF16) for g in range(n_pg)]
        parts = [_dot_nt(probs[g], page_refs[3 * g + 1][...].astype(BF16)) for g in range(n_pg)]
        acc = acc_ref[...]
        for part in parts:
            acc = acc + part
        acc_ref[...] = acc

    @pl.when(step == pl.num_programs(1) - 1)
    def _():
        o_ref[...] = jnp.sum(jnp.where(head_of_lane == row, acc_ref[...], 0.0), axis=0, keepdims=True)


def _decode_attn(page_table, q3, k3, v3, small3, qg_row, kg_row, bf_row, ck, cv, clf_t):
    bd, _, w = q3.shape
    n_pages = page_table.shape[1]
    page = ck.shape[2]
    assert page == LANES and n_pages % PAGES_PER_STEP == 0 and LANES // HD_F == 2
    n_pg = PAGES_PER_STEP
    steps = n_pages // n_pg

    def page_of(i, s, g, pt):
        return pt[i, n_pages - 1 - (s * n_pg + g)]

    def key_idx(g):
        return lambda i, s, pt: (page_of(i, jnp.minimum(s, steps - 1), g, pt), 0, 0)

    def val_idx(g):
        return lambda i, s, pt: (page_of(i, jnp.maximum(s - steps, 0), g, pt), 0, 0)

    row_spec = lambda width: pl.BlockSpec((None, 1, width), lambda i, s, pt: (i, 0, 0))
    const_spec = lambda width: pl.BlockSpec((1, width), lambda i, s, pt: (0, 0))
    in_specs = [row_spec(w), row_spec(w), row_spec(w), row_spec(LANES), const_spec(w), const_spec(w), const_spec(LANES)]
    args = [q3, k3, v3, small3, qg_row, kg_row, bf_row]
    for g in range(n_pg):
        in_specs += [pl.BlockSpec((None, w, page), key_idx(g)),
                     pl.BlockSpec((None, w, page), val_idx(g)),
                     pl.BlockSpec((None, H_F, page), key_idx(g))]
        args += [ck, cv, clf_t]
    col = lambda: pltpu.VMEM((HP, 1), F32)
    grid_spec = pltpu.PrefetchScalarGridSpec(
        num_scalar_prefetch=1,
        grid=(bd, 2 * steps),
        in_specs=in_specs,
        out_specs=[row_spec(w), row_spec(w), row_spec(LANES)],
        scratch_shapes=[pltpu.VMEM((HP, w), BF16), pltpu.VMEM((HP, w), BF16), col(), col(), col(),
                        pltpu.VMEM((HP, w), F32), col(), pltpu.VMEM((n_pages, HP, LANES), F32)],
    )
    return pl.pallas_call(
        functools.partial(_decode_attn_body, n_pg=n_pg),
        grid_spec=grid_spec,
        out_shape=[jax.ShapeDtypeStruct((bd, 1, w), F32), jax.ShapeDtypeStruct((bd, 1, w), F32),
                   jax.ShapeDtypeStruct((bd, 1, LANES), F32)],
        compiler_params=_cparams("parallel", "arbitrary"),
        name="decode_attn",
    )(page_table, *args)


def _merge_body(x_ref, oa_ref, z_ref, of_ref, ga_ref, gf_ref, na_ref, pa_ref, pf_ref, wo_ref, gffn_ref,
                wr_ref, br_ref, x1_ref, h2_ref, tw_ref, ti_ref):
    o = oa_ref[...]
    z = z_ref[...]
    na = na_ref[...]
    parts = []
    for h in range(H_A):
        oh = o[:, h * DV_A:(h + 1) * DV_A]
        zh = z[:, h * DV_A:(h + 1) * DV_A]
        on = oh * lax.rsqrt(jnp.mean(oh * oh, axis=-1, keepdims=True) + EPS) * na
        parts.append((on * (zh * _sigmoid(zh))).astype(BF16))
    o_a = jnp.concatenate(parts, axis=-1)
    ya = _dot(o_a, pa_ref[...])
    yf = _dot(of_ref[...].astype(BF16), pf_ref[...])
    mixed = _sigmoid(ga_ref[...]) * ya + _sigmoid(gf_ref[...]) * yf
    x1 = x_ref[...] + _dot(mixed.astype(BF16), wo_ref[...])
    x1_ref[...] = x1
    h2 = x1 * lax.rsqrt(jnp.mean(x1 * x1, axis=-1, keepdims=True) + EPS) * gffn_ref[...]
    h2_ref[...] = h2
    logits = _dot(h2.astype(BF16), wr_ref[...]) + br_ref[...]
    lane = lax.broadcasted_iota(jnp.int32, logits.shape, 1)
    l = jnp.where(lane < N_EXP, logits, -jnp.inf)
    vals, idxs = [], []
    for _ in range(TOP_K):
        m = jnp.max(l, axis=-1, keepdims=True)
        idx = jnp.min(jnp.where(l == m, lane, LANES), axis=-1, keepdims=True)
        vals.append(m)
        idxs.append(idx)
        l = jnp.where(lane == idx, -jnp.inf, l)
    es = [jnp.exp(v - vals[0]) for v in vals]
    den = es[0]
    for e in es[1:]:
        den = den + e
    tw = jnp.zeros(logits.shape, F32)
    ti = jnp.zeros(logits.shape, jnp.int32)
    for kk in range(TOP_K):
        tw = jnp.where(lane == kk, es[kk] / den, tw)
        ti = jnp.where(lane == kk, idxs[kk], ti)
    tw_ref[...] = tw
    ti_ref[...] = ti


def _merge(x3, oa3, z3, of3, ga3, gf3, na_row, pa, pf, wo, gffn_row, wr, br_row):
    g, t, d = x3.shape
    tm = _row_tile(t, 384)
    tok = lambda c: pl.BlockSpec((None, tm, c), lambda i, j: (i, j, 0))
    const = lambda a: pl.BlockSpec(a.shape, lambda i, j: (0,) * a.ndim, pipeline_mode=pl.Buffered(1))
    wa = H_A * DV_A
    wf = H_F * HD_F
    return pl.pallas_call(
        _merge_body,
        grid=(g, t // tm),
        in_specs=[tok(d), tok(wa), tok(wa), tok(wf), tok(d), tok(d),
                  const(na_row), const(pa), const(pf), const(wo), const(gffn_row),
                  const(wr), const(br_row)],
        out_specs=[tok(d), tok(d), tok(LANES), tok(LANES)],
        out_shape=[jax.ShapeDtypeStruct((g, t, d), F32), jax.ShapeDtypeStruct((g, t, d), F32),
                   jax.ShapeDtypeStruct((g, t, LANES), F32), jax.ShapeDtypeStruct((g, t, LANES), jnp.int32)],
        compiler_params=_cparams("parallel", "parallel"),
        name="merge",
    )(x3, oa3, z3, of3, ga3, gf3, na_row, pa, pf, wo, gffn_row, wr, br_row)


def _moe_body(be_ref, nv_ref, x_ref, wg_ref, bg_ref, wu_ref, bu_ref, wd_ref, bd_ref, y_ref, wg_b, wu_b, wd_b):
    i = pl.program_id(0)

    @pl.when(jnp.logical_or(i == 0, be_ref[i] != be_ref[jnp.maximum(i - 1, 0)]))
    def _():
        wg_b[...] = wg_ref[...].astype(BF16)
        wu_b[...] = wu_ref[...].astype(BF16)
        wd_b[...] = wd_ref[...].astype(BF16)

    @pl.when(i < nv_ref[0])
    def _():
        x = x_ref[...].astype(BF16)
        gate = jnp.minimum(_dot(x, wg_b[...]) + bg_ref[...], SWIGLU_LIMIT)
        up = jnp.clip(_dot(x, wu_b[...]) + bu_ref[...], -SWIGLU_LIMIT, SWIGLU_LIMIT)
        act = (up + 1.0) * gate * _sigmoid(SWIGLU_ALPHA * gate)
        y_ref[...] = _dot(act.astype(BF16), wd_b[...]) + bd_ref[...]

    @pl.when(i >= nv_ref[0])
    def _():
        y_ref[...] = jnp.zeros(y_ref.shape, y_ref.dtype)


def _moe_experts(blk_e, n_valid, xs, wg, bg, wu, bu, wd, bd):
    n_rows, d = xs.shape
    bm = MOE_BLOCK
    n_blk = n_rows // bm
    de = wg.shape[-1]
    row_idx = lambda i, be, nv: (jnp.minimum(i, nv[0] - 1), 0)
    bspec = lambda b: pl.BlockSpec((None, 1, b), lambda i, be, nv: (be[i], 0, 0))
    wspec = lambda a, b: pl.BlockSpec((None, a, b), lambda i, be, nv: (be[i], 0, 0))
    grid_spec = pltpu.PrefetchScalarGridSpec(
        num_scalar_prefetch=2,
        grid=(n_blk,),
        in_specs=[pl.BlockSpec((bm, d), row_idx),
                  wspec(d, de), bspec(de), wspec(d, de), bspec(de), wspec(de, d), bspec(d)],
        out_specs=pl.BlockSpec((bm, d), lambda i, be, nv: (i, 0)),
        scratch_shapes=[pltpu.VMEM((d, de), BF16), pltpu.VMEM((d, de), BF16), pltpu.VMEM((de, d), BF16)],
    )
    return pl.pallas_call(
        _moe_body,
        grid_spec=grid_spec,
        out_shape=jax.ShapeDtypeStruct((n_rows, d), F32),
        compiler_params=pltpu.CompilerParams(dimension_semantics=("arbitrary",), vmem_limit_bytes=MOE_VMEM_LIMIT),
        name="moe_experts",
    )(blk_e, n_valid, xs, wg, bg, wu, bu, wd, bd)


def _moe(h2, top_w, top_i, wg, bg, wu, bu, wd, bd):
    n, d = h2.shape
    bm = MOE_BLOCK
    m = n * TOP_K
    n_blk = -(-(m + N_EXP * (bm - 1)) // bm)
    flat_e = top_i.reshape(-1)
    onehot = (flat_e[:, None] == jnp.arange(N_EXP, dtype=jnp.int32)[None, :]).astype(jnp.int32)
    csum = jnp.cumsum(onehot, axis=0)
    rank = jnp.take_along_axis(csum, flat_e[:, None], axis=1)[:, 0] - 1
    counts = csum[-1]
    padded = (counts + bm - 1) // bm * bm
    pend = jnp.cumsum(padded)
    pstart = pend - padded
    dest = pstart[flat_e] + rank
    n_valid = (pend[-1] // bm).astype(jnp.int32)
    blk_ids = jnp.arange(n_blk, dtype=jnp.int32)
    expert_of = lambda blk: jnp.minimum(jnp.sum((pend[None, :] <= (blk * bm)[:, None]).astype(jnp.int32), axis=1), N_EXP - 1)
    blk_e = expert_of(jnp.minimum(blk_ids, n_valid - 1))
    shift = (m - 1).bit_length()
    assert N_EXP << shift < 2 ** 31
    pair_sorted = jnp.sort((flat_e << shift) | jnp.arange(m, dtype=jnp.int32)) & ((1 << shift) - 1)
    e_of_blk = expert_of(blk_ids)
    start = jnp.cumsum(counts) - counts
    rank = (blk_ids * bm - pstart[e_of_blk])[:, None] + jnp.arange(bm, dtype=jnp.int32)[None, :]
    src = jnp.clip(start[e_of_blk][:, None] + rank, 0, m - 1)
    row_tok = jnp.where(rank < counts[e_of_blk][:, None], pair_sorted[src.reshape(-1)].reshape(n_blk, bm) // TOP_K, n)
    row_tok = row_tok.reshape(-1)
    h_ext = jnp.concatenate([h2, jnp.zeros((1, d), h2.dtype)], axis=0)
    xs = h_ext[row_tok]
    y = _moe_experts(blk_e, n_valid.reshape(1), xs, wg, bg, wu, bu, wd, bd)
    yg = y[dest.reshape(n, TOP_K).T.reshape(-1)].reshape(TOP_K, n, d)
    return jnp.sum(yg * top_w.T[:, :, None], axis=0)


def _lane_row(vals, offset, width=LANES):
    return jnp.zeros((1, width), F32).at[0, offset:offset + vals.shape[0]].set(vals.astype(F32))


def kernel(x_prompt, x_sample, cache_k, cache_v, cache_logf, state_delta, state_conv, page_table,
           meta_tokens, g_mix, w_in, conv_w, a_log, dt_bias, norm_a, qn_g, kn_g, b_forget,
           p_a, p_f, w_o, g_ffn, w_router, b_router, w_gate, b_gate, w_up, b_up, w_down, b_down):
    depth = w_in.shape[0]
    b, seq, d = x_prompt.shape
    bd = x_sample.shape[0]
    assert x_sample.shape[1] == 1 and DK_A == LANES and DV_A == LANES
    t = N_META + seq
    lead = (-N_META) % CHUNK
    t_pad = -(-t // ATT_BLOCK) * ATT_BLOCK
    qkv_w, va_w, wf = 2 * H_A * DK_A + H_A * DV_A, H_A * DV_A, H_F * HD_F
    sizes = (qkv_w, va_w, H_A, H_A, wf, wf, wf, H_F, d, d)
    offs = [0]
    for s_ in sizes:
        offs.append(offs[-1] + s_)
    col = lambda i: slice(offs[i], offs[i + 1])
    widths = (qkv_w, va_w, wf, wf, wf, d, d)

    xp = jnp.concatenate([jnp.broadcast_to(meta_tokens.astype(x_prompt.dtype)[None], (b, N_META, d)), x_prompt], axis=1)
    xs = x_sample
    new_p = [[] for _ in range(5)]
    new_s = [[] for _ in range(5)]
    for l in range(depth):
        wl = w_in[l]
        w_main = jnp.concatenate([wl[:, col(0)], wl[:, col(1)], wl[:, col(4)], wl[:, col(5)], wl[:, col(6)],
                                  wl[:, col(8)], wl[:, col(9)]], axis=1).astype(BF16)
        w_small = jnp.concatenate([wl[:, col(2)], wl[:, col(3)], wl[:, col(7)],
                                   jnp.zeros((d, LANES - 2 * H_A - H_F), F32)], axis=1).astype(BF16)
        g_row = g_mix[l].reshape(1, d)
        a_row = _lane_row(a_log[l], H_A)
        dt_row = _lane_row(dt_bias[l], H_A)
        bf_row = _lane_row(b_forget[l], 2 * H_A)
        qg_pair = jnp.tile(qn_g[l], LANES // HD_F).reshape(1, LANES)
        kg_pair = jnp.tile(kn_g[l], LANES // HD_F).reshape(1, LANES)
        qg_full = jnp.tile(qn_g[l], H_F).reshape(1, wf)
        kg_full = jnp.tile(kn_g[l], H_F).reshape(1, wf)
        na_row = norm_a[l].reshape(1, DV_A)
        pa_b, pf_b, wo_b = p_a[l].astype(BF16), p_f[l].astype(BF16), w_o[l].astype(BF16)
        gffn_row = g_ffn[l].reshape(1, d)
        wr = jnp.pad(w_router[l], ((0, 0), (0, LANES - N_EXP))).astype(BF16)
        br_row = _lane_row(b_router[l], 0)
        merge_w = (na_row, pa_b, pf_b, wo_b, gffn_row, wr, br_row)

        qkv_p, z_p, qf_p, kf_p, vf_p, ga_p, gf_p, sm_p = _in_proj(xp.reshape(b * t, d), g_row, w_main, w_small, widths)
        r3 = lambda a: a.reshape(b, t, a.shape[-1])
        qkv3, sm3 = r3(qkv_p), r3(sm_p)
        conv_p = qkv3[:, t - (CONV_W - 1):, :]
        prep = _conv_prep(qkv3, conv_w[l], jnp.zeros((b, CONV_W - 1, qkv_w), F32))
        oa_p, s_p = _delta_scan(prep, sm3, a_row, dt_row, jnp.zeros((b, H_A, DK_A, DV_A), F32), lead)
        lf_p, c_col = _logf_cumsum(sm3, bf_row, t_pad)
        qx, kx, vb, kn_p = _fox_prep(r3(qf_p), r3(kf_p), r3(vf_p), c_col, qg_pair, kg_pair, t_pad)
        of_p = _fox_attn(qx, kx, vb)
        x1_p, h2_p, tw_p, ti_p = _merge(xp, oa_p, r3(z_p), of_p, r3(ga_p), r3(gf_p), *merge_w)

        qkv_s, z_s, qf_s, kf_s, vf_s, ga_s, gf_s, sm_s = _in_proj(xs.reshape(bd, d), g_row, w_main, w_small, widths)
        s3 = lambda a: a.reshape(bd, 1, a.shape[-1])
        oa_s, s_s, conv_s = _decode_delta(s3(qkv_s), state_conv[l], s3(sm_s), conv_w[l], a_row, dt_row, state_delta[l])
        n_pool, page = cache_k.shape[1], cache_k.shape[2]
        of_s, kn_s, lf_s = _decode_attn(page_table, s3(qf_s), s3(kf_s), s3(vf_s), s3(sm_s), qg_full, kg_full,
                                        _lane_row(b_forget[l], 2 * H_A),
                                        jnp.transpose(cache_k[l], (0, 2, 3, 1)).reshape(n_pool, wf, page),
                                        jnp.transpose(cache_v[l], (0, 2, 3, 1)).reshape(n_pool, wf, page),
                                        jnp.swapaxes(cache_logf[l], 1, 2))
        g1 = lambda a: a.reshape(1, bd, a.shape[-1])
        x1_s, h2_s, tw_s, ti_s = _merge(g1(xs), g1(oa_s), g1(z_s), g1(of_s), g1(ga_s), g1(gf_s), *merge_w)

        n_p = b * t
        h2_all = jnp.concatenate([h2_p.reshape(n_p, d), h2_s.reshape(bd, d)], axis=0)
        tw_all = jnp.concatenate([tw_p.reshape(n_p, LANES), tw_s.reshape(bd, LANES)], axis=0)[:, :TOP_K]
        ti_all = jnp.concatenate([ti_p.reshape(n_p, LANES), ti_s.reshape(bd, LANES)], axis=0)[:, :TOP_K]
        y_all = _moe(h2_all, tw_all, ti_all, w_gate[l], b_gate[l][:, None, :],
                     w_up[l], b_up[l][:, None, :], w_down[l], b_down[l][:, None, :])
        xp = x1_p + y_all[:n_p].reshape(b, t, d)
        xs = x1_s.reshape(bd, 1, d) + y_all[n_p:].reshape(bd, 1, d)

        st_p = (kn_p.reshape(b, t, H_F, HD_F), vf_p.reshape(b, t, H_F, HD_F),
                lf_p[:, :, 2 * H_A:2 * H_A + H_F], s_p, conv_p)
        st_s = (kn_s.reshape(bd, 1, H_F, HD_F), vf_s.reshape(bd, 1, H_F, HD_F),
                lf_s[:, :, 2 * H_A:2 * H_A + H_F], s_s, conv_s)
        for lst, a in zip(new_p, st_p):
            lst.append(a)
        for lst, a in zip(new_s, st_s):
            lst.append(a)
    k_p, v_p, lf_pp, d_p, c_p = (jnp.stack(a) for a in new_p)
    k_s, v_s, lf_ss, d_s, c_s = (jnp.stack(a) for a in new_s)
    return (xp[:, N_META:], xs, k_p, v_p, lf_pp, k_s, v_s, lf_ss, d_p, d_s, c_p, c_s)
```

```python
import functools

import jax
import jax.numpy as jnp
from jax import lax
from jax.experimental import pallas as pl
from jax.experimental.pallas import tpu as pltpu

F32 = jnp.float32
BF16 = jnp.bfloat16

N_META = 16
H_A = 4
DK_A = 128
DV_A = 128
CONV_W = 4
CHUNK = 64
H_F = 8
HD_F = 64
N_EXP = 32
TOP_K = 4
SWIGLU_LIMIT = 7.0
SWIGLU_ALPHA = 1.702
EPS = 1e-6
LOG2E = 1.4426950408889634

LANES = 128
F_OFF = 2 * H_A
HP = 16
ATT_BLOCK = 256
MOE_BLOCK = 512
PAGES_PER_STEP = 16
VMEM_LIMIT = 48 * 1024 * 1024
MOE_VMEM_LIMIT = 58 * 1024 * 1024


def _cparams(*sem):
    return pltpu.CompilerParams(dimension_semantics=sem, vmem_limit_bytes=VMEM_LIMIT)


def _row_tile(n, cap):
    if n <= cap:
        return n
    best = None
    for t in range(8, cap + 1, 8):
        if n % t == 0:
            best = t
    assert best is not None, n
    return best


def _dot(a, b):
    return jnp.dot(a, b, preferred_element_type=F32)


def _dot_nt(a, b):
    return lax.dot_general(a, b, (((1,), (1,)), ((), ())), preferred_element_type=F32)


def _split2(x):
    hi = x.astype(BF16)
    lo = (x - hi.astype(F32)).astype(BF16)
    return hi, lo


def _dot_x3(a, b):
    a_hi, a_lo = _split2(a)
    b_hi, b_lo = _split2(b)
    return _dot(a_hi, b_hi) + (_dot(a_hi, b_lo) + _dot(a_lo, b_hi))


def _split3(x):
    hi = x.astype(BF16)
    r = x - hi.astype(F32)
    mid = r.astype(BF16)
    lo = (r - mid.astype(F32)).astype(BF16)
    return hi, mid, lo


def _sigmoid(x):
    return 1.0 / (1.0 + jnp.exp(-x))


def _softplus(x):
    return jnp.maximum(x, 0.0) + jnp.log1p(jnp.exp(-jnp.abs(x)))


def _log_sigmoid(x):
    return jnp.minimum(x, 0.0) - jnp.log1p(jnp.exp(-jnp.abs(x)))


IN_WIDTHS = (2 * H_A * DK_A + H_A * DV_A, H_A * DV_A, H_F * HD_F, H_F * HD_F, H_F * HD_F)
COL_CHUNK = 512


def _in_proj_body(x_ref, g_ref, wm_ref, ws_ref, *out_refs, widths):
    x = x_ref[...]
    h = x * lax.rsqrt(jnp.mean(x * x, axis=-1, keepdims=True) + EPS) * g_ref[...]
    hb = h.astype(BF16)
    col = 0
    for o_ref, w in zip(out_refs[:-1], widths):
        for c0 in range(0, w, COL_CHUNK):
            o_ref[:, c0:c0 + COL_CHUNK] = _dot(hb, wm_ref[:, col + c0:col + c0 + COL_CHUNK])
        col += w
    out_refs[-1][...] = _dot(hb, ws_ref[...])


def _in_proj(x2d, g_row, w_main, w_small, widths):
    n, d = x2d.shape
    tm = _row_tile(n, 512)
    outs = [jax.ShapeDtypeStruct((n, w), F32) for w in widths] + [jax.ShapeDtypeStruct((n, LANES), F32)]
    out_specs = [pl.BlockSpec((tm, w), lambda i: (i, 0)) for w in widths] + [pl.BlockSpec((tm, LANES), lambda i: (i, 0))]
    return pl.pallas_call(
        functools.partial(_in_proj_body, widths=widths),
        grid=(n // tm,),
        in_specs=[pl.BlockSpec((tm, d), lambda i: (i, 0)),
                  pl.BlockSpec((1, d), lambda i: (0, 0)),
                  pl.BlockSpec(w_main.shape, lambda i: (0, 0), pipeline_mode=pl.Buffered(1)),
                  pl.BlockSpec(w_small.shape, lambda i: (0, 0), pipeline_mode=pl.Buffered(1))],
        out_specs=out_specs,
        out_shape=outs,
        compiler_params=_cparams("parallel"),
        name="in_proj",
    )(x2d, g_row, w_main, w_small)


def _conv_prep_body(x_ref, cw_ref, prev_ref, o_ref, buf, *, t_len, rows):
    j = pl.program_id(1)
    pad = 8
    buf[0:pad, :] = jnp.zeros((pad, LANES), F32)
    buf[pad - (CONV_W - 1):pad, :] = prev_ref[...]
    buf[pad:pad + t_len, :] = x_ref[...]
    cw = cw_ref[...]
    is_q = j < H_A
    is_qk = j < 2 * H_A
    for r0 in range(0, t_len, rows):
        y = jnp.zeros((rows, LANES), F32)
        for w in range(CONV_W):
            off = pad - (CONV_W - 1) + w + r0
            y = y + buf[off:off + rows, :] * cw[w:w + 1, :]
        y = y * _sigmoid(y)
        nrm = lax.rsqrt(jnp.sum(y * y, axis=-1, keepdims=True) + EPS)
        f = jnp.where(is_qk, nrm * jnp.where(is_q, DK_A ** -0.5, 1.0), 1.0)
        o_ref[r0:r0 + rows, :] = y * f


def _conv_prep(qkv3, conv_w, conv_prev):
    b, t, c = qkv3.shape
    rows = _row_tile(t, 512)
    return pl.pallas_call(
        functools.partial(_conv_prep_body, t_len=t, rows=rows),
        grid=(b, c // LANES),
        in_specs=[pl.BlockSpec((None, t, LANES), lambda i, j: (i, 0, j)),
                  pl.BlockSpec((CONV_W, LANES), lambda i, j: (0, j)),
                  pl.BlockSpec((None, CONV_W - 1, LANES), lambda i, j: (i, 0, j))],
        out_specs=pl.BlockSpec((None, t, LANES), lambda i, j: (i, 0, j)),
        out_shape=jax.ShapeDtypeStruct((b, t, c), F32),
        scratch_shapes=[pltpu.VMEM((8 + t, LANES), F32)],
        compiler_params=_cparams("parallel", "parallel"),
        name="conv_prep",
    )(qkv3, conv_w, conv_prev)


def _gates(sm, a_row, dt_row, h):
    lane = lax.broadcasted_iota(jnp.int32, sm.shape, 1)
    beta_all = _sigmoid(sm)
    g_all = -jnp.exp(a_row) * _softplus(sm + dt_row)
    beta = jnp.sum(jnp.where(lane == h, beta_all, 0.0), axis=-1, keepdims=True)
    g = jnp.sum(jnp.where(lane == H_A + h, g_all, 0.0), axis=-1, keepdims=True)
    return beta, g


def _decay_terms(gs):
    c = gs[0].shape[0]
    ri = lax.broadcasted_iota(jnp.int32, (c, c), 0)
    ci = lax.broadcasted_iota(jnp.int32, (c, c), 1)
    g_cum, decay = [], []
    for g in gs:
        g_row = jnp.sum(jnp.where(ri == ci, g, 0.0), axis=0, keepdims=True)
        gc = jnp.sum(jnp.where(ci <= ri, g_row, 0.0), axis=1, keepdims=True)
        gc_row = jnp.sum(jnp.where(ri <= ci, g, 0.0), axis=0, keepdims=True)
        g_cum.append(gc)
        decay.append(jnp.exp(jnp.where(ci <= ri, gc - gc_row, -jnp.inf)))
    return g_cum, decay


def _delta_step(cur, nxt, s_ref):
    o = inv_n = None
    stages = []
    if cur is not None:
        qs, ks, vs, betas, gs, inv = cur
        rng = range(len(qs))
        c = qs[0].shape[0]
        g_cum, decay = _decay_terms(gs)
        e_g = [jnp.exp(g_cum[i]) for i in rng]
        s = [s_ref[i] for i in rng]
        sb = [s[i].astype(BF16) for i in rng]
        qb = [qs[i].astype(BF16) for i in rng]
        kb = [ks[i].astype(BF16) for i in rng]
        k_s = [_dot(kb[i], sb[i]) for i in rng]
        val = {}

        def st_delta():
            rhs = [betas[i] * (vs[i] - e_g[i] * k_s[i]) for i in rng]
            val["db"] = [_dot_x3(inv[i], rhs[i]).astype(BF16) for i in rng]
            val["qk"] = [_dot_nt(qb[i], kb[i]) for i in rng]
            val["q_s"] = [_dot(qb[i], sb[i]) for i in rng]

        def st_out():
            val["o"] = [e_g[i] * val["q_s"][i] + _dot((val["qk"][i] * decay[i]).astype(BF16), val["db"][i]) for i in rng]

        def st_state():
            g_end = [g_cum[i][c - 1:c, :] for i in rng]
            k_dec = [(ks[i] * jnp.exp(g_end[i] - g_cum[i])).T.astype(BF16) for i in rng]
            for i in rng:
                s_ref[i] = jnp.exp(g_end[i]) * s[i] + _dot(k_dec[i], val["db"][i])

        stages = [st_delta, st_out, st_state]
    if nxt is not None:
        ks_n, betas_n, gs_n = nxt
        rng_n = range(len(ks_n))
        c = ks_n[0].shape[0]
        ri = lax.broadcasted_iota(jnp.int32, (c, c), 0)
        ci = lax.broadcasted_iota(jnp.int32, (c, c), 1)
        _, decay_n = _decay_terms(gs_n)
        kb_n = [ks_n[i].astype(BF16) for i in rng_n]
        kk = [_dot_nt(kb_n[i], kb_n[i]) for i in rng_n]
        a = [jnp.where(ci < ri, betas_n[i] * decay_n[i] * kk[i], 0.0) for i in rng_n]
        inv_n = [jnp.where(ri == ci, 1.0, 0.0) - a[i] for i in rng_n]
        pw = a
        for _ in range(max(1, (c - 1).bit_length() - 1)):
            if stages:
                stages.pop(0)()
            pw = [_dot_x3(pw[i], pw[i]) for i in rng_n]
            inv_n = [inv_n[i] + _dot_x3(inv_n[i], pw[i]) for i in rng_n]
    for st in stages:
        st()
    if cur is not None:
        o = val["o"]
    return o, inv_n


def _delta_scan_body(qkv_ref, sm_ref, a_ref, dt_ref, s0_ref, o_ref, s_out_ref, s_ref, *, t_len, lead):
    a_row = a_ref[...]
    dt_row = dt_ref[...]
    s_ref[...] = s0_ref[...]
    first = CHUNK - lead
    n_chunks = (lead + t_len) // CHUNK

    def gates(load, mask_lead):
        sm = load(sm_ref, 0)
        betas, gs = [], []
        for h in range(H_A):
            beta, g = _gates(sm, a_row, dt_row, h)
            if mask_lead:
                row = lax.broadcasted_iota(jnp.int32, (CHUNK, 1), 0)
                beta = jnp.where(row >= lead, beta, 0.0)
                g = jnp.where(row >= lead, g, 0.0)
            betas.append(beta)
            gs.append(g)
        return betas, gs

    def nxt_args(load, mask_lead):
        betas, gs = gates(load, mask_lead)
        return [load(qkv_ref, (H_A + h) * DK_A) for h in range(H_A)], betas, gs

    def cur_args(load, mask_lead, inv):
        betas, gs = gates(load, mask_lead)
        qs = [load(qkv_ref, h * DK_A) for h in range(H_A)]
        ks = [load(qkv_ref, (H_A + h) * DK_A) for h in range(H_A)]
        vs = [load(qkv_ref, 2 * H_A * DK_A + h * DV_A) for h in range(H_A)]
        return qs, ks, vs, betas, gs, inv

    def chunk_rows(c):
        return pl.ds(pl.multiple_of(c * CHUNK - lead, 8), CHUNK)

    def loader(c):
        sl = chunk_rows(c)
        return lambda ref, col: ref[sl, col:col + LANES]

    if lead:
        def load0(ref, col):
            return jnp.concatenate([jnp.zeros((lead, LANES), F32), ref[0:first, col:col + LANES]], axis=0)

        def store0(h, o):
            o_ref[0:first, h * DV_A:(h + 1) * DV_A] = o[lead:, :]

        _, inv0 = _delta_step(None, nxt_args(load0, True), s_ref)
        has_next = n_chunks > 1
        outs, inv1 = _delta_step(cur_args(load0, True, inv0), nxt_args(loader(1), False) if has_next else None, s_ref)
        for h, o in enumerate(outs):
            store0(h, o)
        c_start = 1
    else:
        has_next = n_chunks > 0
        _, inv1 = _delta_step(None, nxt_args(loader(0), False), s_ref) if has_next else (None, None)
        c_start = 0

    def body(c, inv):
        sl = chunk_rows(c)
        outs, inv_n = _delta_step(cur_args(loader(c), False, inv),
                                  nxt_args(loader(jnp.minimum(c + 1, n_chunks - 1)), False), s_ref)
        for h, o in enumerate(outs):
            o_ref[sl, h * DV_A:(h + 1) * DV_A] = o
        return tuple(inv_n)

    if has_next:
        lax.fori_loop(c_start, n_chunks, body, tuple(inv1))
    s_out_ref[...] = s_ref[...]


def _delta_scan(qkv3, small3, a_row, dt_row, s0, lead):
    b, t, c = qkv3.shape
    assert (lead + t) % CHUNK == 0 and lead % 8 == 0
    return pl.pallas_call(
        functools.partial(_delta_scan_body, t_len=t, lead=lead),
        grid=(b,),
        in_specs=[pl.BlockSpec((None, t, c), lambda i: (i, 0, 0)),
                  pl.BlockSpec((None, t, LANES), lambda i: (i, 0, 0)),
                  pl.BlockSpec((1, LANES), lambda i: (0, 0)),
                  pl.BlockSpec((1, LANES), lambda i: (0, 0)),
                  pl.BlockSpec((None, H_A, DK_A, DV_A), lambda i: (i, 0, 0, 0))],
        out_specs=[pl.BlockSpec((None, t, H_A * DV_A), lambda i: (i, 0, 0)),
                   pl.BlockSpec((None, H_A, DK_A, DV_A), lambda i: (i, 0, 0, 0))],
        out_shape=[jax.ShapeDtypeStruct((b, t, H_A * DV_A), F32),
                   jax.ShapeDtypeStruct((b, H_A, DK_A, DV_A), F32)],
        scratch_shapes=[pltpu.VMEM((H_A, DK_A, DV_A), F32)],
        compiler_params=_cparams("parallel"),
        name="delta_scan",
    )(qkv3, small3, a_row, dt_row, s0)


def _group_rms(x, gain, seg):
    hi, lo = _split2(x * x)
    ss = _dot(hi, seg) + _dot(lo, seg)
    return x * lax.rsqrt(ss * (1.0 / HD_F) + EPS) * gain


def _seg_matrix(n):
    r = lax.broadcasted_iota(jnp.int32, (n, n), 0) // HD_F
    c = lax.broadcasted_iota(jnp.int32, (n, n), 1) // HD_F
    return jnp.where(r == c, 1.0, 0.0).astype(BF16)


def _fox_prep_body(q_ref, k_ref, v_ref, c_ref, qg_ref, kg_ref, qx_ref, kx_ref, vt_ref, kn_ref, *, t_len, t_pad):
    p = pl.program_id(1)
    heads = LANES // HD_F

    def feature_major(x):
        if t_pad > t_len:
            x = jnp.concatenate([x, jnp.zeros((t_pad - t_len, LANES), F32)], axis=0)
        return x.T.astype(BF16)

    seg = _seg_matrix(LANES)
    qn = _group_rms(q_ref[...], qg_ref[...], seg) * (HD_F ** -0.5 * LOG2E)
    kn = _group_rms(k_ref[...], kg_ref[...], seg)
    kn_ref[...] = kn
    c_all = c_ref[...] * LOG2E
    lane = lax.broadcasted_iota(jnp.int32, (t_len, LANES), 1)
    for hh in range(heads):
        c = jnp.sum(jnp.where(lane == F_OFF + p * heads + hh, c_all, 0.0), axis=-1, keepdims=True)
        pieces = [x.astype(F32) for x in _split3(c)]
        own = (lane // HD_F) == hh
        f0 = ((hh + 1) % heads) * HD_F
        qx = jnp.where(own, qn, 0.0)
        kx = jnp.where(own, kn, 0.0)
        for n, piece in enumerate(pieces):
            qx = jnp.where(lane == f0 + n, piece, qx)
            kx = jnp.where(lane == f0 + n, 1.0, kx)
            qx = jnp.where(lane == f0 + 3 + n, 1.0, qx)
            kx = jnp.where(lane == f0 + 3 + n, -piece, kx)
        kx_ref[0:t_len, hh * LANES:(hh + 1) * LANES] = kx.astype(BF16)
        qx_ref[hh * LANES:(hh + 1) * LANES, :] = feature_major(qx)
    vt_ref[...] = feature_major(v_ref[...])
    if t_pad > t_len:
        kx_ref[t_len:, :] = jnp.zeros((t_pad - t_len, heads * LANES), BF16)


def _fox_prep(q3, k3, v3, c_col, qg_row, kg_row, t_pad):
    b, t, w = q3.shape
    heads = LANES // HD_F
    blk = pl.BlockSpec((None, t, LANES), lambda i, p: (i, 0, p))
    gblk = pl.BlockSpec((1, LANES), lambda i, p: (0, 0))
    return pl.pallas_call(
        functools.partial(_fox_prep_body, t_len=t, t_pad=t_pad),
        grid=(b, w // LANES),
        in_specs=[blk, blk, blk, pl.BlockSpec((None, t, LANES), lambda i, p: (i, 0, 0)), gblk, gblk],
        out_specs=[pl.BlockSpec((None, heads * LANES, t_pad), lambda i, p: (i, p, 0)),
                   pl.BlockSpec((None, t_pad, heads * LANES), lambda i, p: (i, 0, p)),
                   pl.BlockSpec((None, LANES, t_pad), lambda i, p: (i, p, 0)), blk],
        out_shape=[jax.ShapeDtypeStruct((b, H_F * LANES, t_pad), BF16), jax.ShapeDtypeStruct((b, t_pad, H_F * LANES), BF16),
                   jax.ShapeDtypeStruct((b, w, t_pad), BF16), jax.ShapeDtypeStruct((b, t, w), F32)],
        compiler_params=_cparams("parallel", "parallel"),
        name="fox_prep",
    )(q3, k3, v3, c_col, qg_row, kg_row)


def _logf_cumsum_body(sm_ref, bf_ref, lf_ref, ccol_ref, buf, *, t_len, t_pad):
    lf = _log_sigmoid(sm_ref[...] + bf_ref[...])
    lf_ref[...] = lf
    buf[0:t_len, :] = lf
    if t_pad > t_len:
        buf[t_len:, :] = jnp.zeros((t_pad - t_len, LANES), F32)
    blk = ATT_BLOCK
    ri = lax.broadcasted_iota(jnp.int32, (blk, blk), 0)
    ci = lax.broadcasted_iota(jnp.int32, (blk, blk), 1)
    tri = jnp.where(ci <= ri, 1.0, 0.0).astype(BF16)
    carry = jnp.zeros((1, LANES), F32)
    for i in range(t_pad // blk):
        hi, mid, lo = _split3(buf[i * blk:(i + 1) * blk, :])
        c = _dot(tri, hi) + _dot(tri, mid) + _dot(tri, lo) + carry
        ccol_ref[i * blk:(i + 1) * blk, :] = c
        carry = c[blk - 1:blk, :]


def _logf_cumsum(small3, bf_row, t_pad):
    b, t, _ = small3.shape
    return pl.pallas_call(
        functools.partial(_logf_cumsum_body, t_len=t, t_pad=t_pad),
        grid=(b,),
        in_specs=[pl.BlockSpec((None, t, LANES), lambda i: (i, 0, 0)),
                  pl.BlockSpec((1, LANES), lambda i: (0, 0))],
        out_specs=[pl.BlockSpec((None, t, LANES), lambda i: (i, 0, 0)),
                   pl.BlockSpec((None, t_pad, LANES), lambda i: (i, 0, 0))],
        out_shape=[jax.ShapeDtypeStruct((b, t, LANES), F32),
                   jax.ShapeDtypeStruct((b, t_pad, LANES), F32)],
        scratch_shapes=[pltpu.VMEM((t_pad, LANES), F32)],
        compiler_params=_cparams("parallel"),
        name="logf_cumsum",
    )(small3, bf_row)


def _fox_attn_body(qt_ref, k_ref, vt_ref, o_ref):
    i = pl.program_id(1)
    blk = ATT_BLOCK
    key_le_query = (lax.broadcasted_iota(jnp.int32, (blk, blk), 0)
                    <= lax.broadcasted_iota(jnp.int32, (blk, blk), 1))

    def step(j, carry, diagonal):
        ms, ls, accs = carry
        k0 = pl.multiple_of(j * blk, blk)
        st = []
        for h in range(H_F):
            s = _dot(k_ref[pl.ds(k0, blk), h * LANES:(h + 1) * LANES], qt_ref[h * LANES:(h + 1) * LANES, :])
            st.append(jnp.where(key_le_query, s, -jnp.inf) if diagonal else s)
        new_ms, new_ls, alphas, pts = [], [], [], []
        for h in range(H_F):
            m_new = jnp.maximum(ms[h], jnp.max(st[h], axis=0, keepdims=True))
            alpha = jnp.exp2(ms[h] - m_new)
            pt = jnp.exp2(st[h] - m_new)
            new_ls.append(alpha * ls[h] + jnp.sum(pt, axis=0, keepdims=True))
            new_ms.append(m_new)
            alphas.append(alpha)
            pts.append(pt.astype(BF16))
        new_accs = [alphas[h] * accs[h] + _dot(vt_ref[h * HD_F:(h + 1) * HD_F, pl.ds(k0, blk)], pts[h])
                    for h in range(H_F)]
        return tuple(new_ms), tuple(new_ls), tuple(new_accs)

    init = (tuple(jnp.full((1, blk), -jnp.inf, F32) for _ in range(H_F)),
            tuple(jnp.zeros((1, blk), F32) for _ in range(H_F)),
            tuple(jnp.zeros((HD_F, blk), F32) for _ in range(H_F)))
    carry = lax.fori_loop(0, i, lambda j, c: step(j, c, False), init)
    ms, ls, accs = step(i, carry, True)
    heads = LANES // HD_F
    for p in range(H_F // heads):
        ot = jnp.concatenate([accs[p * heads + hh] * (1.0 / ls[p * heads + hh]) for hh in range(heads)], axis=0)
        o_ref[:, p * LANES:(p + 1) * LANES] = ot.T.astype(o_ref.dtype)


def _fox_attn(qt, kx, vt):
    b, w, t_pad = vt.shape
    blk = ATT_BLOCK
    return pl.pallas_call(
        _fox_attn_body,
        grid=(b, t_pad // blk),
        in_specs=[pl.BlockSpec((None, H_F * LANES, blk), lambda n, i: (n, 0, i)),
                  pl.BlockSpec((None, t_pad, H_F * LANES), lambda n, i: (n, 0, 0)),
                  pl.BlockSpec((None, w, t_pad), lambda n, i: (n, 0, 0))],
        out_specs=pl.BlockSpec((None, blk, w), lambda n, i: (n, i, 0)),
        out_shape=jax.ShapeDtypeStruct((b, t_pad, w), BF16),
        compiler_params=_cparams("parallel", "arbitrary"),
        name="fox_attn",
    )(qt, kx, vt)


def _decode_delta_body(x_ref, sc_ref, sm_ref, cw_ref, a_ref, dt_ref, s_ref, o_ref, s_out_ref, conv_ref):
    x_new = x_ref[...]
    sc = sc_ref[...]
    cw = cw_ref[...]
    y = x_new * cw[CONV_W - 1:CONV_W, :]
    for w in range(CONV_W - 1):
        y = y + sc[w:w + 1, :] * cw[w:w + 1, :]
    y = y * _sigmoid(y)
    conv_ref[0:CONV_W - 2, :] = sc[1:, :]
    conv_ref[CONV_W - 2:CONV_W - 1, :] = x_new
    sm = sm_ref[...]
    lane = lax.broadcasted_iota(jnp.int32, (1, LANES), 1)
    beta_all = _sigmoid(sm)
    g_all = -jnp.exp(a_ref[...]) * _softplus(sm + dt_ref[...])
    qk_w = H_A * DK_A
    row0 = lax.broadcasted_iota(jnp.int32, (LANES, LANES), 0) == 0

    def in_row0(r):
        return jnp.where(row0, jnp.broadcast_to(r, (LANES, LANES)), 0.0)

    for h in range(H_A):
        q = y[:, h * DK_A:(h + 1) * DK_A]
        k = y[:, qk_w + h * DK_A:qk_w + (h + 1) * DK_A]
        v = y[:, 2 * qk_w + h * DV_A:2 * qk_w + (h + 1) * DV_A]
        q = q * lax.rsqrt(jnp.sum(q * q, axis=-1, keepdims=True) + EPS) * (DK_A ** -0.5)
        k = k * lax.rsqrt(jnp.sum(k * k, axis=-1, keepdims=True) + EPS)
        beta = jnp.sum(jnp.where(lane == h, beta_all, 0.0), axis=-1, keepdims=True)
        g = jnp.sum(jnp.where(lane == H_A + h, g_all, 0.0), axis=-1, keepdims=True)
        e_g = jnp.exp(g)
        s = s_ref[h]
        sb = s.astype(BF16)
        k_sq = in_row0(k)
        k_s = _dot(k_sq.astype(BF16), sb)[0:1, :]
        q_s = _dot(in_row0(q).astype(BF16), sb)[0:1, :]
        delta = beta * (v - e_g * k_s)
        qk = jnp.sum(q * k, axis=-1, keepdims=True)
        o_ref[:, h * DV_A:(h + 1) * DV_A] = e_g * q_s + qk * delta
        s_out_ref[h] = e_g * s + _dot_x3(k_sq.T, in_row0(delta))


def _decode_delta(qkv3, state_conv, small3, conv_w, a_row, dt_row, s0):
    bd, _, c = qkv3.shape
    return pl.pallas_call(
        _decode_delta_body,
        grid=(bd,),
        in_specs=[pl.BlockSpec((None, 1, c), lambda i: (i, 0, 0)),
                  pl.BlockSpec((None, CONV_W - 1, c), lambda i: (i, 0, 0)),
                  pl.BlockSpec((None, 1, LANES), lambda i: (i, 0, 0)),
                  pl.BlockSpec((CONV_W, c), lambda i: (0, 0)),
                  pl.BlockSpec((1, LANES), lambda i: (0, 0)),
                  pl.BlockSpec((1, LANES), lambda i: (0, 0)),
                  pl.BlockSpec((None, H_A, DK_A, DV_A), lambda i: (i, 0, 0, 0))],
        out_specs=[pl.BlockSpec((None, 1, H_A * DV_A), lambda i: (i, 0, 0)),
                   pl.BlockSpec((None, H_A, DK_A, DV_A), lambda i: (i, 0, 0, 0)),
                   pl.BlockSpec((None, CONV_W - 1, c), lambda i: (i, 0, 0))],
        out_shape=[jax.ShapeDtypeStruct((bd, 1, H_A * DV_A), F32),
                   jax.ShapeDtypeStruct((bd, H_A, DK_A, DV_A), F32),
                   jax.ShapeDtypeStruct((bd, CONV_W - 1, c), F32)],
        compiler_params=_cparams("parallel"),
        name="decode_delta",
    )(qkv3, state_conv, small3, conv_w, a_row, dt_row, s0)


def _decode_attn_body(pt_ref, q_ref, k_ref, v_ref, sm_ref, qg_ref, kg_ref, bf_ref, *rest, n_pg):
    page_refs = rest[:3 * n_pg]
    o_ref, kn_ref, lf_ref, qrows, vnew, snew, m_ref, l_ref, acc_ref, carry_ref, s_scr = rest[3 * n_pg:]
    step = pl.program_id(1)
    n_steps = pl.num_programs(1) // 2
    w = H_F * HD_F
    row = lax.broadcasted_iota(jnp.int32, (HP, w), 0)
    head_of_lane = lax.broadcasted_iota(jnp.int32, (HP, w), 1) // HD_F

    @pl.when(step == 0)
    def _():
        seg = _seg_matrix(w)

        def rms_rows(x_row, gain):
            xr = jnp.broadcast_to(x_row, (HP, w))
            hi, lo = _split2(xr * xr)
            ss = _dot(hi, seg) + _dot(lo, seg)
            return xr * lax.rsqrt(ss * (1.0 / HD_F) + EPS) * gain

        qn = rms_rows(q_ref[...], qg_ref[...]) * (HD_F ** -0.5)
        kn = rms_rows(k_ref[...], kg_ref[...])
        kn_ref[...] = kn[0:1, :]
        lf = _log_sigmoid(sm_ref[...] + bf_ref[...])
        lf_ref[...] = lf
        q_m = jnp.where(head_of_lane == row, qn, 0.0).astype(BF16)
        qrows[...] = q_m
        vnew[...] = jnp.broadcast_to(v_ref[...], (HP, w)).astype(BF16)
        s_new = jnp.sum(q_m.astype(F32) * kn.astype(BF16).astype(F32), axis=-1, keepdims=True)
        snew[...] = s_new
        m_ref[...] = s_new
        rr = lax.broadcasted_iota(jnp.int32, (HP, LANES), 0)
        ll = lax.broadcasted_iota(jnp.int32, (HP, LANES), 1)
        carry_ref[...] = jnp.sum(jnp.where(ll == rr + F_OFF, jnp.broadcast_to(lf, (HP, LANES)), 0.0),
                                 axis=-1, keepdims=True)

    @pl.when(step < n_steps)
    def _():
        qb = qrows[...]
        lf_all = jnp.concatenate(
            [jnp.concatenate([page_refs[3 * g + 2][...], jnp.zeros((HP - H_F, LANES), F32)], axis=0)
             for g in range(n_pg)], axis=0)
        later = (lax.broadcasted_iota(jnp.int32, (LANES, LANES), 0)
                 > lax.broadcasted_iota(jnp.int32, (LANES, LANES), 1)).astype(BF16)
        hi, mid, lo = _split3(lf_all)
        after = _dot(hi, later) + (_dot(mid, later) + _dot(lo, later))
        totals = jnp.sum(lf_all, axis=-1, keepdims=True)
        raw = [_dot(qb, page_refs[3 * g][...].astype(BF16)) for g in range(n_pg)]
        carry = carry_ref[...]
        m_new = m_ref[...]
        for g in range(n_pg):
            rows = slice(g * HP, (g + 1) * HP)
            s = raw[g] + (carry + after[rows, :])
            s_scr[step * n_pg + g] = s
            m_new = jnp.maximum(m_new, jnp.max(s, axis=-1, keepdims=True))
            carry = carry + totals[rows, :]
        carry_ref[...] = carry
        m_ref[...] = m_new

    @pl.when(step == n_steps)
    def _():
        m = m_ref[...]
        p_new = jnp.exp(snew[...] - m)

        def add(i, tot):
            return tot + jnp.exp(s_scr[i] - m)

        tot = lax.fori_loop(0, n_steps * n_pg, add, jnp.zeros((HP, LANES), F32), unroll=8)
        l = p_new + jnp.sum(tot, axis=-1, keepdims=True)
        l_ref[...] = l
        acc_ref[...] = (p_new / l).astype(BF16).astype(F32) * vnew[...].astype(F32)

    @pl.when(step >= n_steps)
    def _():
        m = m_ref[...]
        l = l_ref[...]
        probs = [(jnp.exp(s_scr[(step - n_steps) * n_pg + g] - m) / l).astype(BF16) for g in range(n_pg)]
        parts = [_dot_nt(probs[g], page_refs[3 * g + 1][...].astype(BF16)) for g in range(n_pg)]
        acc = acc_ref[...]
        for part in parts:
            acc = acc + part
        acc_ref[...] = acc

    @pl.when(step == pl.num_programs(1) - 1)
    def _():
        o_ref[...] = jnp.sum(jnp.where(head_of_lane == row, acc_ref[...], 0.0), axis=0, keepdims=True)


def _decode_attn(page_table, q3, k3, v3, small3, qg_row, kg_row, bf_row, ck, cv, clf_t):
    bd, _, w = q3.shape
    n_pages = page_table.shape[1]
    page = ck.shape[2]
    assert page == LANES and n_pages % PAGES_PER_STEP == 0 and LANES // HD_F == 2
    n_pg = PAGES_PER_STEP
    steps = n_pages // n_pg

    def page_of(i, s, g, pt):
        return pt[i, n_pages - 1 - (s * n_pg + g)]

    def key_idx(g):
        return lambda i, s, pt: (page_of(i, jnp.minimum(s, steps - 1), g, pt), 0, 0)

    def val_idx(g):
        return lambda i, s, pt: (page_of(i, jnp.maximum(s - steps, 0), g, pt), 0, 0)

    row_spec = lambda width: pl.BlockSpec((None, 1, width), lambda i, s, pt: (i, 0, 0))
    const_spec = lambda width: pl.BlockSpec((1, width), lambda i, s, pt: (0, 0))
    in_specs = [row_spec(w), row_spec(w), row_spec(w), row_spec(LANES), const_spec(w), const_spec(w), const_spec(LANES)]
    args = [q3, k3, v3, small3, qg_row, kg_row, bf_row]
    for g in range(n_pg):
        in_specs += [pl.BlockSpec((None, w, page), key_idx(g)),
                     pl.BlockSpec((None, w, page), val_idx(g)),
                     pl.BlockSpec((None, H_F, page), key_idx(g))]
        args += [ck, cv, clf_t]
    col = lambda: pltpu.VMEM((HP, 1), F32)
    grid_spec = pltpu.PrefetchScalarGridSpec(
        num_scalar_prefetch=1,
        grid=(bd, 2 * steps),
        in_specs=in_specs,
        out_specs=[row_spec(w), row_spec(w), row_spec(LANES)],
        scratch_shapes=[pltpu.VMEM((HP, w), BF16), pltpu.VMEM((HP, w), BF16), col(), col(), col(),
                        pltpu.VMEM((HP, w), F32), col(), pltpu.VMEM((n_pages, HP, LANES), F32)],
    )
    return pl.pallas_call(
        functools.partial(_decode_attn_body, n_pg=n_pg),
        grid_spec=grid_spec,
        out_shape=[jax.ShapeDtypeStruct((bd, 1, w), F32), jax.ShapeDtypeStruct((bd, 1, w), F32),
                   jax.ShapeDtypeStruct((bd, 1, LANES), F32)],
        compiler_params=_cparams("parallel", "arbitrary"),
        name="decode_attn",
    )(page_table, *args)


def _merge_body(x_ref, oa_ref, z_ref, of_ref, ga_ref, gf_ref, na_ref, pa_ref, pf_ref, wo_ref, gffn_ref,
                wr_ref, br_ref, x1_ref, h2_ref, tw_ref, ti_ref):
    o = oa_ref[...]
    z = z_ref[...]
    na = na_ref[...]
    parts = []
    for h in range(H_A):
        oh = o[:, h * DV_A:(h + 1) * DV_A]
        zh = z[:, h * DV_A:(h + 1) * DV_A]
        on = oh * lax.rsqrt(jnp.mean(oh * oh, axis=-1, keepdims=True) + EPS) * na
        parts.append((on * (zh * _sigmoid(zh))).astype(BF16))
    o_a = jnp.concatenate(parts, axis=-1)
    ya = _dot(o_a, pa_ref[...])
    yf = _dot(of_ref[...].astype(BF16), pf_ref[...])
    mixed = _sigmoid(ga_ref[...]) * ya + _sigmoid(gf_ref[...]) * yf
    x1 = x_ref[...] + _dot(mixed.astype(BF16), wo_ref[...])
    x1_ref[...] = x1
    h2 = x1 * lax.rsqrt(jnp.mean(x1 * x1, axis=-1, keepdims=True) + EPS) * gffn_ref[...]
    h2_ref[...] = h2
    logits = _dot(h2.astype(BF16), wr_ref[...]) + br_ref[...]
    lane = lax.broadcasted_iota(jnp.int32, logits.shape, 1)
    l = jnp.where(lane < N_EXP, logits, -jnp.inf)
    vals, idxs = [], []
    for _ in range(TOP_K):
        m = jnp.max(l, axis=-1, keepdims=True)
        idx = jnp.min(jnp.where(l == m, lane, LANES), axis=-1, keepdims=True)
        vals.append(m)
        idxs.append(idx)
        l = jnp.where(lane == idx, -jnp.inf, l)
    es = [jnp.exp(v - vals[0]) for v in vals]
    den = es[0]
    for e in es[1:]:
        den = den + e
    tw = jnp.zeros(logits.shape, F32)
    ti = jnp.zeros(logits.shape, jnp.int32)
    for kk in range(TOP_K):
        tw = jnp.where(lane == kk, es[kk] / den, tw)
        ti = jnp.where(lane == kk, idxs[kk], ti)
    tw_ref[...] = tw
    ti_ref[...] = ti


def _merge(x3, oa3, z3, of3, ga3, gf3, na_row, pa, pf, wo, gffn_row, wr, br_row):
    g, t, d = x3.shape
    tm = _row_tile(t, 384)
    tok = lambda c: pl.BlockSpec((None, tm, c), lambda i, j: (i, j, 0))
    const = lambda a: pl.BlockSpec(a.shape, lambda i, j: (0,) * a.ndim, pipeline_mode=pl.Buffered(1))
    wa = H_A * DV_A
    wf = H_F * HD_F
    return pl.pallas_call(
        _merge_body,
        grid=(g, t // tm),
        in_specs=[tok(d), tok(wa), tok(wa), tok(wf), tok(d), tok(d),
                  const(na_row), const(pa), const(pf), const(wo), const(gffn_row),
                  const(wr), const(br_row)],
        out_specs=[tok(d), tok(d), tok(LANES), tok(LANES)],
        out_shape=[jax.ShapeDtypeStruct((g, t, d), F32), jax.ShapeDtypeStruct((g, t, d), F32),
                   jax.ShapeDtypeStruct((g, t, LANES), F32), jax.ShapeDtypeStruct((g, t, LANES), jnp.int32)],
        compiler_params=_cparams("parallel", "parallel"),
        name="merge",
    )(x3, oa3, z3, of3, ga3, gf3, na_row, pa, pf, wo, gffn_row, wr, br_row)


def _moe_body(be_ref, nv_ref, x_ref, wg_ref, bg_ref, wu_ref, bu_ref, wd_ref, bd_ref, y_ref, wg_b, wu_b, wd_b):
    i = pl.program_id(0)

    @pl.when(jnp.logical_or(i == 0, be_ref[i] != be_ref[jnp.maximum(i - 1, 0)]))
    def _():
        wg_b[...] = wg_ref[...].astype(BF16)
        wu_b[...] = wu_ref[...].astype(BF16)
        wd_b[...] = wd_ref[...].astype(BF16)

    @pl.when(i < nv_ref[0])
    def _():
        x = x_ref[...].astype(BF16)
        gate = jnp.minimum(_dot(x, wg_b[...]) + bg_ref[...], SWIGLU_LIMIT)
        up = jnp.clip(_dot(x, wu_b[...]) + bu_ref[...], -SWIGLU_LIMIT, SWIGLU_LIMIT)
        act = (up + 1.0) * gate * _sigmoid(SWIGLU_ALPHA * gate)
        y_ref[...] = _dot(act.astype(BF16), wd_b[...]) + bd_ref[...]

    @pl.when(i >= nv_ref[0])
    def _():
        y_ref[...] = jnp.zeros(y_ref.shape, y_ref.dtype)


def _moe_experts(blk_e, n_valid, xs, wg, bg, wu, bu, wd, bd):
    n_rows, d = xs.shape
    bm = MOE_BLOCK
    n_blk = n_rows // bm
    de = wg.shape[-1]
    row_idx = lambda i, be, nv: (jnp.minimum(i, nv[0] - 1), 0)
    bspec = lambda b: pl.BlockSpec((None, 1, b), lambda i, be, nv: (be[i], 0, 0))
    wspec = lambda a, b: pl.BlockSpec((None, a, b), lambda i, be, nv: (be[i], 0, 0))
    grid_spec = pltpu.PrefetchScalarGridSpec(
        num_scalar_prefetch=2,
        grid=(n_blk,),
        in_specs=[pl.BlockSpec((bm, d), row_idx),
                  wspec(d, de), bspec(de), wspec(d, de), bspec(de), wspec(de, d), bspec(d)],
        out_specs=pl.BlockSpec((bm, d), lambda i, be, nv: (i, 0)),
        scratch_shapes=[pltpu.VMEM((d, de), BF16), pltpu.VMEM((d, de), BF16), pltpu.VMEM((de, d), BF16)],
    )
    return pl.pallas_call(
        _moe_body,
        grid_spec=grid_spec,
        out_shape=jax.ShapeDtypeStruct((n_rows, d), F32),
        compiler_params=pltpu.CompilerParams(dimension_semantics=("arbitrary",), vmem_limit_bytes=MOE_VMEM_LIMIT),
        name="moe_experts",
    )(blk_e, n_valid, xs, wg, bg, wu, bu, wd, bd)


def _moe(h2, top_i, wg, bg, wu, bu, wd, bd):
    n, d = h2.shape
    bm = MOE_BLOCK
    m = n * TOP_K
    n_blk = -(-(m + N_EXP * (bm - 1)) // bm)
    flat_e = top_i.reshape(-1)
    onehot = (flat_e[:, None] == jnp.arange(N_EXP, dtype=jnp.int32)[None, :]).astype(jnp.int32)
    csum = jnp.cumsum(onehot, axis=0)
    rank = jnp.take_along_axis(csum, flat_e[:, None], axis=1)[:, 0] - 1
    counts = csum[-1]
    padded = (counts + bm - 1) // bm * bm
    pend = jnp.cumsum(padded)
    pstart = pend - padded
    dest = pstart[flat_e] + rank
    n_valid = (pend[-1] // bm).astype(jnp.int32)
    blk_ids = jnp.arange(n_blk, dtype=jnp.int32)
    expert_of = lambda blk: jnp.minimum(jnp.sum((pend[None, :] <= (blk * bm)[:, None]).astype(jnp.int32), axis=1), N_EXP - 1)
    blk_e = expert_of(jnp.minimum(blk_ids, n_valid - 1))
    shift = (m - 1).bit_length()
    assert N_EXP << shift < 2 ** 31
    pair_sorted = jnp.sort((flat_e << shift) | jnp.arange(m, dtype=jnp.int32)) & ((1 << shift) - 1)
    e_of_blk = expert_of(blk_ids)
    start = jnp.cumsum(counts) - counts
    rank = (blk_ids * bm - pstart[e_of_blk])[:, None] + jnp.arange(bm, dtype=jnp.int32)[None, :]
    src = jnp.clip(start[e_of_blk][:, None] + rank, 0, m - 1)
    row_tok = jnp.where(rank < counts[e_of_blk][:, None], pair_sorted[src.reshape(-1)].reshape(n_blk, bm) // TOP_K, n)
    row_tok = row_tok.reshape(-1)
    h_ext = jnp.concatenate([h2, jnp.zeros((1, d), h2.dtype)], axis=0)
    xs = h_ext[row_tok]
    y = _moe_experts(blk_e, n_valid.reshape(1), xs, wg, bg, wu, bu, wd, bd)
    return y[dest.reshape(n, TOP_K).T.reshape(-1)].reshape(TOP_K, n, d)


def _combine_body(x_ref, yg_ref, tw_ref, o_ref):
    tw = tw_ref[...]
    acc = x_ref[...]
    for k in range(TOP_K):
        acc = acc + tw[:, k:k + 1] * yg_ref[k]
    o_ref[...] = acc


def _combine(x1, yg, tw, row0):
    n, d = x1.shape
    tm = _row_tile(n, 384)
    assert row0 % tm == 0
    return pl.pallas_call(
        _combine_body,
        grid=(n // tm,),
        in_specs=[pl.BlockSpec((tm, d), lambda i: (i, 0)),
                  pl.BlockSpec((TOP_K, tm, d), lambda i: (0, row0 // tm + i, 0)),
                  pl.BlockSpec((tm, LANES), lambda i: (i, 0))],
        out_specs=pl.BlockSpec((tm, d), lambda i: (i, 0)),
        out_shape=jax.ShapeDtypeStruct((n, d), F32),
        compiler_params=_cparams("parallel"),
        name="combine",
    )(x1, yg, tw)


def _lane_row(vals, offset, width=LANES):
    return jnp.zeros((1, width), F32).at[0, offset:offset + vals.shape[0]].set(vals.astype(F32))


def kernel(x_prompt, x_sample, cache_k, cache_v, cache_logf, state_delta, state_conv, page_table,
           meta_tokens, g_mix, w_in, conv_w, a_log, dt_bias, norm_a, qn_g, kn_g, b_forget,
           p_a, p_f, w_o, g_ffn, w_router, b_router, w_gate, b_gate, w_up, b_up, w_down, b_down):
    depth = w_in.shape[0]
    b, seq, d = x_prompt.shape
    bd = x_sample.shape[0]
    assert x_sample.shape[1] == 1 and DK_A == LANES and DV_A == LANES
    t = N_META + seq
    lead = (-N_META) % CHUNK
    t_pad = -(-t // ATT_BLOCK) * ATT_BLOCK
    qkv_w, va_w, wf = 2 * H_A * DK_A + H_A * DV_A, H_A * DV_A, H_F * HD_F
    sizes = (qkv_w, va_w, H_A, H_A, wf, wf, wf, H_F, d, d)
    offs = [0]
    for s_ in sizes:
        offs.append(offs[-1] + s_)
    col = lambda i: slice(offs[i], offs[i + 1])
    widths = (qkv_w, va_w, wf, wf, wf, d, d)

    xp = jnp.concatenate([jnp.broadcast_to(meta_tokens.astype(x_prompt.dtype)[None], (b, N_META, d)), x_prompt], axis=1)
    xs = x_sample
    new_p = [[] for _ in range(5)]
    new_s = [[] for _ in range(5)]
    for l in range(depth):
        wl = w_in[l]
        w_main = jnp.concatenate([wl[:, col(0)], wl[:, col(1)], wl[:, col(4)], wl[:, col(5)], wl[:, col(6)],
                                  wl[:, col(8)], wl[:, col(9)]], axis=1).astype(BF16)
        w_small = jnp.concatenate([wl[:, col(2)], wl[:, col(3)], wl[:, col(7)],
                                   jnp.zeros((d, LANES - 2 * H_A - H_F), F32)], axis=1).astype(BF16)
        g_row = g_mix[l].reshape(1, d)
        a_row = _lane_row(a_log[l], H_A)
        dt_row = _lane_row(dt_bias[l], H_A)
        bf_row = _lane_row(b_forget[l], 2 * H_A)
        qg_pair = jnp.tile(qn_g[l], LANES // HD_F).reshape(1, LANES)
        kg_pair = jnp.tile(kn_g[l], LANES // HD_F).reshape(1, LANES)
        qg_full = jnp.tile(qn_g[l], H_F).reshape(1, wf)
        kg_full = jnp.tile(kn_g[l], H_F).reshape(1, wf)
        na_row = norm_a[l].reshape(1, DV_A)
        pa_b, pf_b, wo_b = p_a[l].astype(BF16), p_f[l].astype(BF16), w_o[l].astype(BF16)
        gffn_row = g_ffn[l].reshape(1, d)
        wr = jnp.pad(w_router[l], ((0, 0), (0, LANES - N_EXP))).astype(BF16)
        br_row = _lane_row(b_router[l], 0)
        merge_w = (na_row, pa_b, pf_b, wo_b, gffn_row, wr, br_row)

        qkv_p, z_p, qf_p, kf_p, vf_p, ga_p, gf_p, sm_p = _in_proj(xp.reshape(b * t, d), g_row, w_main, w_small, widths)
        r3 = lambda a: a.reshape(b, t, a.shape[-1])
        qkv3, sm3 = r3(qkv_p), r3(sm_p)
        conv_p = qkv3[:, t - (CONV_W - 1):, :]
        prep = _conv_prep(qkv3, conv_w[l], jnp.zeros((b, CONV_W - 1, qkv_w), F32))
        oa_p, s_p = _delta_scan(prep, sm3, a_row, dt_row, jnp.zeros((b, H_A, DK_A, DV_A), F32), lead)
        lf_p, c_col = _logf_cumsum(sm3, bf_row, t_pad)
        qx, kx, vb, kn_p = _fox_prep(r3(qf_p), r3(kf_p), r3(vf_p), c_col, qg_pair, kg_pair, t_pad)
        of_p = _fox_attn(qx, kx, vb)
        x1_p, h2_p, tw_p, ti_p = _merge(xp, oa_p, r3(z_p), of_p, r3(ga_p), r3(gf_p), *merge_w)

        qkv_s, z_s, qf_s, kf_s, vf_s, ga_s, gf_s, sm_s = _in_proj(xs.reshape(bd, d), g_row, w_main, w_small, widths)
        s3 = lambda a: a.reshape(bd, 1, a.shape[-1])
        oa_s, s_s, conv_s = _decode_delta(s3(qkv_s), state_conv[l], s3(sm_s), conv_w[l], a_row, dt_row, state_delta[l])
        n_pool, page = cache_k.shape[1], cache_k.shape[2]
        of_s, kn_s, lf_s = _decode_attn(page_table, s3(qf_s), s3(kf_s), s3(vf_s), s3(sm_s), qg_full, kg_full,
                                        _lane_row(b_forget[l], 2 * H_A),
                                        jnp.transpose(cache_k[l], (0, 2, 3, 1)).reshape(n_pool, wf, page),
                                        jnp.transpose(cache_v[l], (0, 2, 3, 1)).reshape(n_pool, wf, page),
                                        jnp.swapaxes(cache_logf[l], 1, 2))
        g1 = lambda a: a.reshape(1, bd, a.shape[-1])
        x1_s, h2_s, tw_s, ti_s = _merge(g1(xs), g1(oa_s), g1(z_s), g1(of_s), g1(ga_s), g1(gf_s), *merge_w)

        n_p = b * t
        h2_all = jnp.concatenate([h2_p.reshape(n_p, d), h2_s.reshape(bd, d)], axis=0)
        ti_all = jnp.concatenate([ti_p.reshape(n_p, LANES), ti_s.reshape(bd, LANES)], axis=0)[:, :TOP_K]
        yg = _moe(h2_all, ti_all, w_gate[l], b_gate[l][:, None, :],
                  w_up[l], b_up[l][:, None, :], w_down[l], b_down[l][:, None, :])
        xp = _combine(x1_p.reshape(n_p, d), yg, tw_p.reshape(n_p, LANES), 0).reshape(b, t, d)
        xs = _combine(x1_s.reshape(bd, d), yg, tw_s.reshape(bd, LANES), n_p).reshape(bd, 1, d)

        st_p = (kn_p.reshape(b, t, H_F, HD_F), vf_p.reshape(b, t, H_F, HD_F),
                lf_p[:, :, 2 * H_A:2 * H_A + H_F], s_p, conv_p)
        st_s = (kn_s.reshape(bd, 1, H_F, HD_F), vf_s.reshape(bd, 1, H_F, HD_F),
                lf_s[:, :, 2 * H_A:2 * H_A + H_F], s_s, conv_s)
        for lst, a in zip(new_p, st_p):
            lst.append(a)
        for lst, a in zip(new_s, st_s):
            lst.append(a)
    k_p, v_p, lf_pp, d_p, c_p = (jnp.stack(a) for a in new_p)
    k_s, v_s, lf_ss, d_s, c_s = (jnp.stack(a) for a in new_s)
    return (xp[:, N_META:], xs, k_p, v_p, lf_pp, k_s, v_s, lf_ss, d_p, d_s, c_p, c_s)
```

```python
import functools

import jax
import jax.numpy as jnp
from jax import lax
from jax.experimental import pallas as pl
from jax.experimental.pallas import tpu as pltpu

F32 = jnp.float32
BF16 = jnp.bfloat16

N_META = 16
H_A = 4
DK_A = 128
DV_A = 128
CONV_W = 4
CHUNK = 64
H_F = 8
HD_F = 64
N_EXP = 32
TOP_K = 4
SWIGLU_LIMIT = 7.0
SWIGLU_ALPHA = 1.702
EPS = 1e-6
LOG2E = 1.4426950408889634

LANES = 128
F_OFF = 2 * H_A
HP = 16
ATT_BLOCK = 256
MOE_BLOCK = 512
PAGES_PER_STEP = 32
VMEM_LIMIT = 48 * 1024 * 1024
MOE_VMEM_LIMIT = 58 * 1024 * 1024


def _cparams(*sem):
    return pltpu.CompilerParams(dimension_semantics=sem, vmem_limit_bytes=VMEM_LIMIT)


def _row_tile(n, cap):
    if n <= cap:
        return n
    best = None
    for t in range(8, cap + 1, 8):
        if n % t == 0:
            best = t
    assert best is not None, n
    return best


def _dot(a, b):
    return jnp.dot(a, b, preferred_element_type=F32)


def _dot_nt(a, b):
    return lax.dot_general(a, b, (((1,), (1,)), ((), ())), preferred_element_type=F32)


def _split2(x):
    hi = x.astype(BF16)
    lo = (x - hi.astype(F32)).astype(BF16)
    return hi, lo


def _dot_x3(a, b):
    a_hi, a_lo = _split2(a)
    b_hi, b_lo = _split2(b)
    return _dot(a_hi, b_hi) + (_dot(a_hi, b_lo) + _dot(a_lo, b_hi))


def _split3(x):
    hi = x.astype(BF16)
    r = x - hi.astype(F32)
    mid = r.astype(BF16)
    lo = (r - mid.astype(F32)).astype(BF16)
    return hi, mid, lo


def _sigmoid(x):
    return 1.0 / (1.0 + jnp.exp(-x))


def _softplus(x):
    return jnp.maximum(x, 0.0) + jnp.log1p(jnp.exp(-jnp.abs(x)))


def _log_sigmoid(x):
    return jnp.minimum(x, 0.0) - jnp.log1p(jnp.exp(-jnp.abs(x)))


COL_CHUNK = 512


def _in_proj_body(x_ref, g_ref, wm_ref, ws_ref, *out_refs, widths):
    x = x_ref[...]
    h = x * lax.rsqrt(jnp.mean(x * x, axis=-1, keepdims=True) + EPS) * g_ref[...]
    hb = h.astype(BF16)
    col = 0
    for o_ref, w in zip(out_refs[:-1], widths):
        for c0 in range(0, w, COL_CHUNK):
            o_ref[:, c0:c0 + COL_CHUNK] = _dot(hb, wm_ref[:, col + c0:col + c0 + COL_CHUNK])
        col += w
    out_refs[-1][...] = _dot(hb, ws_ref[...])


def _in_proj(x2d, g_row, w_main, w_small, widths):
    n, d = x2d.shape
    tm = _row_tile(n, 512)
    outs = [jax.ShapeDtypeStruct((n, w), F32) for w in widths] + [jax.ShapeDtypeStruct((n, LANES), F32)]
    out_specs = [pl.BlockSpec((tm, w), lambda i: (i, 0)) for w in widths] + [pl.BlockSpec((tm, LANES), lambda i: (i, 0))]
    return pl.pallas_call(
        functools.partial(_in_proj_body, widths=widths),
        grid=(n // tm,),
        in_specs=[pl.BlockSpec((tm, d), lambda i: (i, 0)),
                  pl.BlockSpec((1, d), lambda i: (0, 0)),
                  pl.BlockSpec(w_main.shape, lambda i: (0, 0), pipeline_mode=pl.Buffered(1)),
                  pl.BlockSpec(w_small.shape, lambda i: (0, 0), pipeline_mode=pl.Buffered(1))],
        out_specs=out_specs,
        out_shape=outs,
        compiler_params=_cparams("parallel"),
        name="in_proj",
    )(x2d, g_row, w_main, w_small)


def _conv_prep_body(x_ref, cw_ref, prev_ref, o_ref, buf, *, t_len, rows):
    j = pl.program_id(1)
    pad = 8
    buf[0:pad, :] = jnp.zeros((pad, LANES), F32)
    buf[pad - (CONV_W - 1):pad, :] = prev_ref[...]
    buf[pad:pad + t_len, :] = x_ref[...]
    cw = cw_ref[...]
    is_q = j < H_A
    is_qk = j < 2 * H_A
    for r0 in range(0, t_len, rows):
        y = jnp.zeros((rows, LANES), F32)
        for w in range(CONV_W):
            off = pad - (CONV_W - 1) + w + r0
            y = y + buf[off:off + rows, :] * cw[w:w + 1, :]
        y = y * _sigmoid(y)
        nrm = lax.rsqrt(jnp.sum(y * y, axis=-1, keepdims=True) + EPS)
        f = jnp.where(is_qk, nrm * jnp.where(is_q, DK_A ** -0.5, 1.0), 1.0)
        o_ref[r0:r0 + rows, :] = y * f


def _conv_prep(qkv3, conv_w, conv_prev):
    b, t, c = qkv3.shape
    rows = _row_tile(t, 512)
    return pl.pallas_call(
        functools.partial(_conv_prep_body, t_len=t, rows=rows),
        grid=(b, c // LANES),
        in_specs=[pl.BlockSpec((None, t, LANES), lambda i, j: (i, 0, j)),
                  pl.BlockSpec((CONV_W, LANES), lambda i, j: (0, j)),
                  pl.BlockSpec((None, CONV_W - 1, LANES), lambda i, j: (i, 0, j))],
        out_specs=pl.BlockSpec((None, t, LANES), lambda i, j: (i, 0, j)),
        out_shape=jax.ShapeDtypeStruct((b, t, c), F32),
        scratch_shapes=[pltpu.VMEM((8 + t, LANES), F32)],
        compiler_params=_cparams("parallel", "parallel"),
        name="conv_prep",
    )(qkv3, conv_w, conv_prev)


def _gates(sm, a_row, dt_row, h):
    lane = lax.broadcasted_iota(jnp.int32, sm.shape, 1)
    beta_all = _sigmoid(sm)
    g_all = -jnp.exp(a_row) * _softplus(sm + dt_row)
    beta = jnp.sum(jnp.where(lane == h, beta_all, 0.0), axis=-1, keepdims=True)
    g = jnp.sum(jnp.where(lane == H_A + h, g_all, 0.0), axis=-1, keepdims=True)
    return beta, g


def _decay_terms(gs):
    c = gs[0].shape[0]
    ri = lax.broadcasted_iota(jnp.int32, (c, c), 0)
    ci = lax.broadcasted_iota(jnp.int32, (c, c), 1)
    g_cum, decay = [], []
    for g in gs:
        g_row = jnp.sum(jnp.where(ri == ci, g, 0.0), axis=0, keepdims=True)
        gc = jnp.sum(jnp.where(ci <= ri, g_row, 0.0), axis=1, keepdims=True)
        gc_row = jnp.sum(jnp.where(ri <= ci, g, 0.0), axis=0, keepdims=True)
        g_cum.append(gc)
        decay.append(jnp.exp(jnp.where(ci <= ri, gc - gc_row, -jnp.inf)))
    return g_cum, decay


def _delta_step(cur, nxt, s_ref):
    o = inv_n = None
    stages = []
    if cur is not None:
        qs, ks, vs, betas, gs, inv = cur
        rng = range(len(qs))
        c = qs[0].shape[0]
        g_cum, decay = _decay_terms(gs)
        e_g = [jnp.exp(g_cum[i]) for i in rng]
        s = [s_ref[i] for i in rng]
        sb = [s[i].astype(BF16) for i in rng]
        qb = [qs[i].astype(BF16) for i in rng]
        kb = [ks[i].astype(BF16) for i in rng]
        k_s = [_dot(kb[i], sb[i]) for i in rng]
        val = {}

        def st_delta():
            rhs = [betas[i] * (vs[i] - e_g[i] * k_s[i]) for i in rng]
            val["db"] = [_dot_x3(inv[i], rhs[i]).astype(BF16) for i in rng]
            val["qk"] = [_dot_nt(qb[i], kb[i]) for i in rng]
            val["q_s"] = [_dot(qb[i], sb[i]) for i in rng]

        def st_out():
            val["o"] = [e_g[i] * val["q_s"][i] + _dot((val["qk"][i] * decay[i]).astype(BF16), val["db"][i]) for i in rng]

        def st_state():
            g_end = [g_cum[i][c - 1:c, :] for i in rng]
            k_dec = [(ks[i] * jnp.exp(g_end[i] - g_cum[i])).T.astype(BF16) for i in rng]
            for i in rng:
                s_ref[i] = jnp.exp(g_end[i]) * s[i] + _dot(k_dec[i], val["db"][i])

        stages = [st_delta, st_out, st_state]
    if nxt is not None:
        ks_n, betas_n, gs_n = nxt
        rng_n = range(len(ks_n))
        c = ks_n[0].shape[0]
        ri = lax.broadcasted_iota(jnp.int32, (c, c), 0)
        ci = lax.broadcasted_iota(jnp.int32, (c, c), 1)
        _, decay_n = _decay_terms(gs_n)
        kb_n = [ks_n[i].astype(BF16) for i in rng_n]
        kk = [_dot_nt(kb_n[i], kb_n[i]) for i in rng_n]
        a = [jnp.where(ci < ri, betas_n[i] * decay_n[i] * kk[i], 0.0) for i in rng_n]
        inv_n = [jnp.where(ri == ci, 1.0, 0.0) - a[i] for i in rng_n]
        pw = a
        for _ in range(max(1, (c - 1).bit_length() - 1)):
            if stages:
                stages.pop(0)()
            pw = [_dot_x3(pw[i], pw[i]) for i in rng_n]
            inv_n = [inv_n[i] + _dot_x3(inv_n[i], pw[i]) for i in rng_n]
    for st in stages:
        st()
    if cur is not None:
        o = val["o"]
    return o, inv_n


def _delta_scan_body(qkv_ref, sm_ref, a_ref, dt_ref, s0_ref, o_ref, s_out_ref, s_ref, *, t_len, lead):
    a_row = a_ref[...]
    dt_row = dt_ref[...]
    s_ref[...] = s0_ref[...]
    first = CHUNK - lead
    n_chunks = (lead + t_len) // CHUNK

    def gates(load, mask_lead):
        sm = load(sm_ref, 0)
        betas, gs = [], []
        for h in range(H_A):
            beta, g = _gates(sm, a_row, dt_row, h)
            if mask_lead:
                row = lax.broadcasted_iota(jnp.int32, (CHUNK, 1), 0)
                beta = jnp.where(row >= lead, beta, 0.0)
                g = jnp.where(row >= lead, g, 0.0)
            betas.append(beta)
            gs.append(g)
        return betas, gs

    def nxt_args(load, mask_lead):
        betas, gs = gates(load, mask_lead)
        return [load(qkv_ref, (H_A + h) * DK_A) for h in range(H_A)], betas, gs

    def cur_args(load, mask_lead, inv):
        betas, gs = gates(load, mask_lead)
        qs = [load(qkv_ref, h * DK_A) for h in range(H_A)]
        ks = [load(qkv_ref, (H_A + h) * DK_A) for h in range(H_A)]
        vs = [load(qkv_ref, 2 * H_A * DK_A + h * DV_A) for h in range(H_A)]
        return qs, ks, vs, betas, gs, inv

    def chunk_rows(c):
        return pl.ds(pl.multiple_of(c * CHUNK - lead, 8), CHUNK)

    def loader(c):
        sl = chunk_rows(c)
        return lambda ref, col: ref[sl, col:col + LANES]

    if lead:
        def load0(ref, col):
            return jnp.concatenate([jnp.zeros((lead, LANES), F32), ref[0:first, col:col + LANES]], axis=0)

        def store0(h, o):
            o_ref[0:first, h * DV_A:(h + 1) * DV_A] = o[lead:, :]

        _, inv0 = _delta_step(None, nxt_args(load0, True), s_ref)
        has_next = n_chunks > 1
        outs, inv1 = _delta_step(cur_args(load0, True, inv0), nxt_args(loader(1), False) if has_next else None, s_ref)
        for h, o in enumerate(outs):
            store0(h, o)
        c_start = 1
    else:
        has_next = n_chunks > 0
        _, inv1 = _delta_step(None, nxt_args(loader(0), False), s_ref) if has_next else (None, None)
        c_start = 0

    def body(c, inv):
        sl = chunk_rows(c)
        outs, inv_n = _delta_step(cur_args(loader(c), False, inv),
                                  nxt_args(loader(jnp.minimum(c + 1, n_chunks - 1)), False), s_ref)
        for h, o in enumerate(outs):
            o_ref[sl, h * DV_A:(h + 1) * DV_A] = o
        return tuple(inv_n)

    if has_next:
        lax.fori_loop(c_start, n_chunks, body, tuple(inv1))
    s_out_ref[...] = s_ref[...]


def _delta_scan(qkv3, small3, a_row, dt_row, s0, lead):
    b, t, c = qkv3.shape
    assert (lead + t) % CHUNK == 0 and lead % 8 == 0
    return pl.pallas_call(
        functools.partial(_delta_scan_body, t_len=t, lead=lead),
        grid=(b,),
        in_specs=[pl.BlockSpec((None, t, c), lambda i: (i, 0, 0)),
                  pl.BlockSpec((None, t, LANES), lambda i: (i, 0, 0)),
                  pl.BlockSpec((1, LANES), lambda i: (0, 0)),
                  pl.BlockSpec((1, LANES), lambda i: (0, 0)),
                  pl.BlockSpec((None, H_A, DK_A, DV_A), lambda i: (i, 0, 0, 0))],
        out_specs=[pl.BlockSpec((None, t, H_A * DV_A), lambda i: (i, 0, 0)),
                   pl.BlockSpec((None, H_A, DK_A, DV_A), lambda i: (i, 0, 0, 0))],
        out_shape=[jax.ShapeDtypeStruct((b, t, H_A * DV_A), F32),
                   jax.ShapeDtypeStruct((b, H_A, DK_A, DV_A), F32)],
        scratch_shapes=[pltpu.VMEM((H_A, DK_A, DV_A), F32)],
        compiler_params=_cparams("parallel"),
        name="delta_scan",
    )(qkv3, small3, a_row, dt_row, s0)


def _group_rms(x, gain, seg):
    hi, lo = _split2(x * x)
    ss = _dot(hi, seg) + _dot(lo, seg)
    return x * lax.rsqrt(ss * (1.0 / HD_F) + EPS) * gain


def _seg_matrix(n):
    r = lax.broadcasted_iota(jnp.int32, (n, n), 0) // HD_F
    c = lax.broadcasted_iota(jnp.int32, (n, n), 1) // HD_F
    return jnp.where(r == c, 1.0, 0.0).astype(BF16)


def _fox_prep_body(q_ref, k_ref, v_ref, c_ref, qg_ref, kg_ref, qx_ref, kx_ref, vt_ref, kn_ref, *, t_len, t_pad):
    p = pl.program_id(1)
    heads = LANES // HD_F

    def feature_major(x):
        if t_pad > t_len:
            x = jnp.concatenate([x, jnp.zeros((t_pad - t_len, LANES), F32)], axis=0)
        return x.T.astype(BF16)

    seg = _seg_matrix(LANES)
    qn = _group_rms(q_ref[...], qg_ref[...], seg) * (HD_F ** -0.5 * LOG2E)
    kn = _group_rms(k_ref[...], kg_ref[...], seg)
    kn_ref[...] = kn
    c_all = c_ref[...] * LOG2E
    lane = lax.broadcasted_iota(jnp.int32, (t_len, LANES), 1)
    for hh in range(heads):
        c = jnp.sum(jnp.where(lane == F_OFF + p * heads + hh, c_all, 0.0), axis=-1, keepdims=True)
        pieces = [x.astype(F32) for x in _split3(c)]
        own = (lane // HD_F) == hh
        f0 = ((hh + 1) % heads) * HD_F
        qx = jnp.where(own, qn, 0.0)
        kx = jnp.where(own, kn, 0.0)
        for n, piece in enumerate(pieces):
            qx = jnp.where(lane == f0 + n, piece, qx)
            kx = jnp.where(lane == f0 + n, 1.0, kx)
            qx = jnp.where(lane == f0 + 3 + n, 1.0, qx)
            kx = jnp.where(lane == f0 + 3 + n, -piece, kx)
        kx_ref[0:t_len, hh * LANES:(hh + 1) * LANES] = kx.astype(BF16)
        qx_ref[hh * LANES:(hh + 1) * LANES, :] = feature_major(qx)
    vt_ref[...] = feature_major(v_ref[...])
    if t_pad > t_len:
        kx_ref[t_len:, :] = jnp.zeros((t_pad - t_len, heads * LANES), BF16)


def _fox_prep(q3, k3, v3, c_col, qg_row, kg_row, t_pad):
    b, t, w = q3.shape
    heads = LANES // HD_F
    blk = pl.BlockSpec((None, t, LANES), lambda i, p: (i, 0, p))
    gblk = pl.BlockSpec((1, LANES), lambda i, p: (0, 0))
    return pl.pallas_call(
        functools.partial(_fox_prep_body, t_len=t, t_pad=t_pad),
        grid=(b, w // LANES),
        in_specs=[blk, blk, blk, pl.BlockSpec((None, t, LANES), lambda i, p: (i, 0, 0)), gblk, gblk],
        out_specs=[pl.BlockSpec((None, heads * LANES, t_pad), lambda i, p: (i, p, 0)),
                   pl.BlockSpec((None, t_pad, heads * LANES), lambda i, p: (i, 0, p)),
                   pl.BlockSpec((None, LANES, t_pad), lambda i, p: (i, p, 0)), blk],
        out_shape=[jax.ShapeDtypeStruct((b, H_F * LANES, t_pad), BF16), jax.ShapeDtypeStruct((b, t_pad, H_F * LANES), BF16),
                   jax.ShapeDtypeStruct((b, w, t_pad), BF16), jax.ShapeDtypeStruct((b, t, w), F32)],
        compiler_params=_cparams("parallel", "parallel"),
        name="fox_prep",
    )(q3, k3, v3, c_col, qg_row, kg_row)


def _logf_cumsum_body(sm_ref, bf_ref, lf_ref, ccol_ref, buf, *, t_len, t_pad):
    lf = _log_sigmoid(sm_ref[...] + bf_ref[...])
    lf_ref[...] = lf
    buf[0:t_len, :] = lf
    if t_pad > t_len:
        buf[t_len:, :] = jnp.zeros((t_pad - t_len, LANES), F32)
    blk = ATT_BLOCK
    ri = lax.broadcasted_iota(jnp.int32, (blk, blk), 0)
    ci = lax.broadcasted_iota(jnp.int32, (blk, blk), 1)
    tri = jnp.where(ci <= ri, 1.0, 0.0).astype(BF16)
    carry = jnp.zeros((1, LANES), F32)
    for i in range(t_pad // blk):
        hi, mid, lo = _split3(buf[i * blk:(i + 1) * blk, :])
        c = _dot(tri, hi) + _dot(tri, mid) + _dot(tri, lo) + carry
        ccol_ref[i * blk:(i + 1) * blk, :] = c
        carry = c[blk - 1:blk, :]


def _logf_cumsum(small3, bf_row, t_pad):
    b, t, _ = small3.shape
    return pl.pallas_call(
        functools.partial(_logf_cumsum_body, t_len=t, t_pad=t_pad),
        grid=(b,),
        in_specs=[pl.BlockSpec((None, t, LANES), lambda i: (i, 0, 0)),
                  pl.BlockSpec((1, LANES), lambda i: (0, 0))],
        out_specs=[pl.BlockSpec((None, t, LANES), lambda i: (i, 0, 0)),
                   pl.BlockSpec((None, t_pad, LANES), lambda i: (i, 0, 0))],
        out_shape=[jax.ShapeDtypeStruct((b, t, LANES), F32),
                   jax.ShapeDtypeStruct((b, t_pad, LANES), F32)],
        scratch_shapes=[pltpu.VMEM((t_pad, LANES), F32)],
        compiler_params=_cparams("parallel"),
        name="logf_cumsum",
    )(small3, bf_row)


def _fox_attn_body(qt_ref, k_ref, vt_ref, o_ref):
    i = pl.program_id(1)
    blk = ATT_BLOCK
    key_le_query = (lax.broadcasted_iota(jnp.int32, (blk, blk), 0)
                    <= lax.broadcasted_iota(jnp.int32, (blk, blk), 1))

    def step(j, carry, diagonal):
        ms, ls, accs = carry
        k0 = pl.multiple_of(j * blk, blk)
        st = []
        for h in range(H_F):
            s = _dot(k_ref[pl.ds(k0, blk), h * LANES:(h + 1) * LANES], qt_ref[h * LANES:(h + 1) * LANES, :])
            st.append(jnp.where(key_le_query, s, -jnp.inf) if diagonal else s)
        new_ms, new_ls, alphas, pts = [], [], [], []
        for h in range(H_F):
            m_new = jnp.maximum(ms[h], jnp.max(st[h], axis=0, keepdims=True))
            alpha = jnp.exp2(ms[h] - m_new)
            pt = jnp.exp2(st[h] - m_new)
            new_ls.append(alpha * ls[h] + jnp.sum(pt, axis=0, keepdims=True))
            new_ms.append(m_new)
            alphas.append(alpha)
            pts.append(pt.astype(BF16))
        new_accs = [alphas[h] * accs[h] + _dot(vt_ref[h * HD_F:(h + 1) * HD_F, pl.ds(k0, blk)], pts[h])
                    for h in range(H_F)]
        return tuple(new_ms), tuple(new_ls), tuple(new_accs)

    init = (tuple(jnp.full((1, blk), -jnp.inf, F32) for _ in range(H_F)),
            tuple(jnp.zeros((1, blk), F32) for _ in range(H_F)),
            tuple(jnp.zeros((HD_F, blk), F32) for _ in range(H_F)))
    carry = lax.fori_loop(0, i, lambda j, c: step(j, c, False), init)
    ms, ls, accs = step(i, carry, True)
    heads = LANES // HD_F
    for p in range(H_F // heads):
        ot = jnp.concatenate([accs[p * heads + hh] * (1.0 / ls[p * heads + hh]) for hh in range(heads)], axis=0)
        o_ref[:, p * LANES:(p + 1) * LANES] = ot.T.astype(o_ref.dtype)


def _fox_attn(qt, kx, vt):
    b, w, t_pad = vt.shape
    blk = ATT_BLOCK
    return pl.pallas_call(
        _fox_attn_body,
        grid=(b, t_pad // blk),
        in_specs=[pl.BlockSpec((None, H_F * LANES, blk), lambda n, i: (n, 0, i)),
                  pl.BlockSpec((None, t_pad, H_F * LANES), lambda n, i: (n, 0, 0)),
                  pl.BlockSpec((None, w, t_pad), lambda n, i: (n, 0, 0))],
        out_specs=pl.BlockSpec((None, blk, w), lambda n, i: (n, i, 0)),
        out_shape=jax.ShapeDtypeStruct((b, t_pad, w), BF16),
        compiler_params=_cparams("parallel", "arbitrary"),
        name="fox_attn",
    )(qt, kx, vt)


def _decode_delta_body(x_ref, sc_ref, sm_ref, cw_ref, a_ref, dt_ref, s_ref, o_ref, s_out_ref, conv_ref):
    x_new = x_ref[...]
    sc = sc_ref[...]
    cw = cw_ref[...]
    y = x_new * cw[CONV_W - 1:CONV_W, :]
    for w in range(CONV_W - 1):
        y = y + sc[w:w + 1, :] * cw[w:w + 1, :]
    y = y * _sigmoid(y)
    conv_ref[0:CONV_W - 2, :] = sc[1:, :]
    conv_ref[CONV_W - 2:CONV_W - 1, :] = x_new
    sm = sm_ref[...]
    lane = lax.broadcasted_iota(jnp.int32, (1, LANES), 1)
    beta_all = _sigmoid(sm)
    g_all = -jnp.exp(a_ref[...]) * _softplus(sm + dt_ref[...])
    qk_w = H_A * DK_A
    row0 = lax.broadcasted_iota(jnp.int32, (LANES, LANES), 0) == 0

    def in_row0(r):
        return jnp.where(row0, jnp.broadcast_to(r, (LANES, LANES)), 0.0)

    for h in range(H_A):
        q = y[:, h * DK_A:(h + 1) * DK_A]
        k = y[:, qk_w + h * DK_A:qk_w + (h + 1) * DK_A]
        v = y[:, 2 * qk_w + h * DV_A:2 * qk_w + (h + 1) * DV_A]
        q = q * lax.rsqrt(jnp.sum(q * q, axis=-1, keepdims=True) + EPS) * (DK_A ** -0.5)
        k = k * lax.rsqrt(jnp.sum(k * k, axis=-1, keepdims=True) + EPS)
        beta = jnp.sum(jnp.where(lane == h, beta_all, 0.0), axis=-1, keepdims=True)
        g = jnp.sum(jnp.where(lane == H_A + h, g_all, 0.0), axis=-1, keepdims=True)
        e_g = jnp.exp(g)
        s = s_ref[h]
        sb = s.astype(BF16)
        k_sq = in_row0(k)
        k_s = _dot(k_sq.astype(BF16), sb)[0:1, :]
        q_s = _dot(in_row0(q).astype(BF16), sb)[0:1, :]
        delta = beta * (v - e_g * k_s)
        qk = jnp.sum(q * k, axis=-1, keepdims=True)
        o_ref[:, h * DV_A:(h + 1) * DV_A] = e_g * q_s + qk * delta
        s_out_ref[h] = e_g * s + _dot_x3(k_sq.T, in_row0(delta))


def _decode_delta(qkv3, state_conv, small3, conv_w, a_row, dt_row, s0):
    bd, _, c = qkv3.shape
    return pl.pallas_call(
        _decode_delta_body,
        grid=(bd,),
        in_specs=[pl.BlockSpec((None, 1, c), lambda i: (i, 0, 0)),
                  pl.BlockSpec((None, CONV_W - 1, c), lambda i: (i, 0, 0)),
                  pl.BlockSpec((None, 1, LANES), lambda i: (i, 0, 0)),
                  pl.BlockSpec((CONV_W, c), lambda i: (0, 0)),
                  pl.BlockSpec((1, LANES), lambda i: (0, 0)),
                  pl.BlockSpec((1, LANES), lambda i: (0, 0)),
                  pl.BlockSpec((None, H_A, DK_A, DV_A), lambda i: (i, 0, 0, 0))],
        out_specs=[pl.BlockSpec((None, 1, H_A * DV_A), lambda i: (i, 0, 0)),
                   pl.BlockSpec((None, H_A, DK_A, DV_A), lambda i: (i, 0, 0, 0)),
                   pl.BlockSpec((None, CONV_W - 1, c), lambda i: (i, 0, 0))],
        out_shape=[jax.ShapeDtypeStruct((bd, 1, H_A * DV_A), F32),
                   jax.ShapeDtypeStruct((bd, H_A, DK_A, DV_A), F32),
                   jax.ShapeDtypeStruct((bd, CONV_W - 1, c), F32)],
        compiler_params=_cparams("parallel"),
        name="decode_delta",
    )(qkv3, state_conv, small3, conv_w, a_row, dt_row, s0)


def _decode_attn_body(pt_ref, q_ref, k_ref, v_ref, sm_ref, qg_ref, kg_ref, bf_ref, *rest, n_pg):
    page_refs = rest[:3 * n_pg]
    o_ref, kn_ref, lf_ref, qrows, vnew, snew, m_ref, l_ref, acc_ref, carry_ref, s_scr = rest[3 * n_pg:]
    step = pl.program_id(1)
    n_steps = pl.num_programs(1) // 2
    w = H_F * HD_F
    row = lax.broadcasted_iota(jnp.int32, (HP, w), 0)
    head_of_lane = lax.broadcasted_iota(jnp.int32, (HP, w), 1) // HD_F

    @pl.when(step == 0)
    def _():
        seg = _seg_matrix(w)

        def rms_rows(x_row, gain):
            xr = jnp.broadcast_to(x_row, (HP, w))
            hi, lo = _split2(xr * xr)
            ss = _dot(hi, seg) + _dot(lo, seg)
            return xr * lax.rsqrt(ss * (1.0 / HD_F) + EPS) * gain

        qn = rms_rows(q_ref[...], qg_ref[...]) * (HD_F ** -0.5)
        kn = rms_rows(k_ref[...], kg_ref[...])
        kn_ref[...] = kn[0:1, :]
        lf = _log_sigmoid(sm_ref[...] + bf_ref[...])
        lf_ref[...] = lf
        q_m = jnp.where(head_of_lane == row, qn, 0.0).astype(BF16)
        qrows[...] = q_m
        vnew[...] = jnp.broadcast_to(v_ref[...], (HP, w)).astype(BF16)
        s_new = jnp.sum(q_m.astype(F32) * kn.astype(BF16).astype(F32), axis=-1, keepdims=True)
        snew[...] = s_new
        m_ref[...] = s_new
        rr = lax.broadcasted_iota(jnp.int32, (HP, LANES), 0)
        ll = lax.broadcasted_iota(jnp.int32, (HP, LANES), 1)
        carry_ref[...] = jnp.sum(jnp.where(ll == rr + F_OFF, jnp.broadcast_to(lf, (HP, LANES)), 0.0),
                                 axis=-1, keepdims=True)

    @pl.when(step < n_steps)
    def _():
        qb = qrows[...]
        lf_all = jnp.concatenate(
            [jnp.concatenate([page_refs[3 * g + 2][...], jnp.zeros((HP - H_F, LANES), F32)], axis=0)
             for g in range(n_pg)], axis=0)
        later = (lax.broadcasted_iota(jnp.int32, (LANES, LANES), 0)
                 > lax.broadcasted_iota(jnp.int32, (LANES, LANES), 1)).astype(BF16)
        hi, mid, lo = _split3(lf_all)
        after = _dot(hi, later) + (_dot(mid, later) + _dot(lo, later))
        totals = jnp.sum(lf_all, axis=-1, keepdims=True)
        raw = [_dot(qb, page_refs[3 * g][...].astype(BF16)) for g in range(n_pg)]
        carry = carry_ref[...]
        m_new = m_ref[...]
        for g in range(n_pg):
            rows = slice(g * HP, (g + 1) * HP)
            s = raw[g] + (carry + after[rows, :])
            s_scr[step * n_pg + g] = s
            m_new = jnp.maximum(m_new, jnp.max(s, axis=-1, keepdims=True))
            carry = carry + totals[rows, :]
        carry_ref[...] = carry
        m_ref[...] = m_new

    @pl.when(step == n_steps)
    def _():
        m = m_ref[...]
        p_new = jnp.exp(snew[...] - m)

        def add(i, tot):
            return tot + jnp.exp(s_scr[i] - m)

        tot = lax.fori_loop(0, n_steps * n_pg, add, jnp.zeros((HP, LANES), F32), unroll=8)
        l = p_new + jnp.sum(tot, axis=-1, keepdims=True)
        l_ref[...] = l
        acc_ref[...] = (p_new / l).astype(BF16).astype(F32) * vnew[...].astype(F32)

    @pl.when(step >= n_steps)
    def _():
        m = m_ref[...]
        l = l_ref[...]
        probs = [(jnp.exp(s_scr[(step - n_steps) * n_pg + g] - m) / l).astype(BF16) for g in range(n_pg)]
        parts = [_dot_nt(probs[g], page_refs[3 * g + 1][...].astype(BF16)) for g in range(n_pg)]
        acc = acc_ref[...]
        for part in parts:
            acc = acc + part
        acc_ref[...] = acc

    @pl.when(step == pl.num_programs(1) - 1)
    def _():
        o_ref[...] = jnp.sum(jnp.where(head_of_lane == row, acc_ref[...], 0.0), axis=0, keepdims=True)


def _decode_attn(page_table, q3, k3, v3, small3, qg_row, kg_row, bf_row, ck, cv, clf_t):
    bd, _, w = q3.shape
    n_pages = page_table.shape[1]
    page = ck.shape[2]
    assert page == LANES and n_pages % PAGES_PER_STEP == 0 and LANES // HD_F == 2
    n_pg = PAGES_PER_STEP
    steps = n_pages // n_pg

    def page_of(i, s, g, pt):
        return pt[i, n_pages - 1 - (s * n_pg + g)]

    def key_idx(g):
        return lambda i, s, pt: (page_of(i, jnp.minimum(s, steps - 1), g, pt), 0, 0)

    def val_idx(g):
        return lambda i, s, pt: (page_of(i, jnp.maximum(s - steps, 0), g, pt), 0, 0)

    row_spec = lambda width: pl.BlockSpec((None, 1, width), lambda i, s, pt: (i, 0, 0))
    const_spec = lambda width: pl.BlockSpec((1, width), lambda i, s, pt: (0, 0))
    in_specs = [row_spec(w), row_spec(w), row_spec(w), row_spec(LANES), const_spec(w), const_spec(w), const_spec(LANES)]
    args = [q3, k3, v3, small3, qg_row, kg_row, bf_row]
    for g in range(n_pg):
        in_specs += [pl.BlockSpec((None, w, page), key_idx(g)),
                     pl.BlockSpec((None, w, page), val_idx(g)),
                     pl.BlockSpec((None, H_F, page), key_idx(g))]
        args += [ck, cv, clf_t]
    col = lambda: pltpu.VMEM((HP, 1), F32)
    grid_spec = pltpu.PrefetchScalarGridSpec(
        num_scalar_prefetch=1,
        grid=(bd, 2 * steps),
        in_specs=in_specs,
        out_specs=[row_spec(w), row_spec(w), row_spec(LANES)],
        scratch_shapes=[pltpu.VMEM((HP, w), BF16), pltpu.VMEM((HP, w), BF16), col(), col(), col(),
                        pltpu.VMEM((HP, w), F32), col(), pltpu.VMEM((n_pages, HP, LANES), F32)],
    )
    return pl.pallas_call(
        functools.partial(_decode_attn_body, n_pg=n_pg),
        grid_spec=grid_spec,
        out_shape=[jax.ShapeDtypeStruct((bd, 1, w), F32), jax.ShapeDtypeStruct((bd, 1, w), F32),
                   jax.ShapeDtypeStruct((bd, 1, LANES), F32)],
        compiler_params=_cparams("parallel", "arbitrary"),
        name="decode_attn",
    )(page_table, *args)


def _merge_body(x_ref, oa_ref, z_ref, of_ref, ga_ref, gf_ref, na_ref, pa_ref, pf_ref, wo_ref, gffn_ref,
                wr_ref, br_ref, x1_ref, h2_ref, tw_ref, ti_ref):
    o = oa_ref[...]
    z = z_ref[...]
    na = na_ref[...]
    parts = []
    for h in range(H_A):
        oh = o[:, h * DV_A:(h + 1) * DV_A]
        zh = z[:, h * DV_A:(h + 1) * DV_A]
        on = oh * lax.rsqrt(jnp.mean(oh * oh, axis=-1, keepdims=True) + EPS) * na
        parts.append((on * (zh * _sigmoid(zh))).astype(BF16))
    o_a = jnp.concatenate(parts, axis=-1)
    ya = _dot(o_a, pa_ref[...])
    yf = _dot(of_ref[...].astype(BF16), pf_ref[...])
    mixed = _sigmoid(ga_ref[...]) * ya + _sigmoid(gf_ref[...]) * yf
    x1 = x_ref[...] + _dot(mixed.astype(BF16), wo_ref[...])
    x1_ref[...] = x1
    h2 = x1 * lax.rsqrt(jnp.mean(x1 * x1, axis=-1, keepdims=True) + EPS) * gffn_ref[...]
    h2_ref[...] = h2
    logits = _dot(h2.astype(BF16), wr_ref[...]) + br_ref[...]
    lane = lax.broadcasted_iota(jnp.int32, logits.shape, 1)
    l = jnp.where(lane < N_EXP, logits, -jnp.inf)
    vals, idxs = [], []
    for _ in range(TOP_K):
        m = jnp.max(l, axis=-1, keepdims=True)
        idx = jnp.min(jnp.where(l == m, lane, LANES), axis=-1, keepdims=True)
        vals.append(m)
        idxs.append(idx)
        l = jnp.where(lane == idx, -jnp.inf, l)
    es = [jnp.exp(v - vals[0]) for v in vals]
    den = es[0]
    for e in es[1:]:
        den = den + e
    tw = jnp.zeros(logits.shape, F32)
    ti = jnp.zeros(logits.shape, jnp.int32)
    for kk in range(TOP_K):
        tw = jnp.where(lane == kk, es[kk] / den, tw)
        ti = jnp.where(lane == kk, idxs[kk], ti)
    tw_ref[...] = tw
    ti_ref[...] = ti


def _merge(x3, oa3, z3, of3, ga3, gf3, na_row, pa, pf, wo, gffn_row, wr, br_row):
    g, t, d = x3.shape
    tm = _row_tile(t, 384)
    tok = lambda c: pl.BlockSpec((None, tm, c), lambda i, j: (i, j, 0))
    const = lambda a: pl.BlockSpec(a.shape, lambda i, j: (0,) * a.ndim, pipeline_mode=pl.Buffered(1))
    wa = H_A * DV_A
    wf = H_F * HD_F
    return pl.pallas_call(
        _merge_body,
        grid=(g, t // tm),
        in_specs=[tok(d), tok(wa), tok(wa), tok(wf), tok(d), tok(d),
                  const(na_row), const(pa), const(pf), const(wo), const(gffn_row),
                  const(wr), const(br_row)],
        out_specs=[tok(d), tok(d), tok(LANES), tok(LANES)],
        out_shape=[jax.ShapeDtypeStruct((g, t, d), F32), jax.ShapeDtypeStruct((g, t, d), F32),
                   jax.ShapeDtypeStruct((g, t, LANES), F32), jax.ShapeDtypeStruct((g, t, LANES), jnp.int32)],
        compiler_params=_cparams("parallel", "parallel"),
        name="merge",
    )(x3, oa3, z3, of3, ga3, gf3, na_row, pa, pf, wo, gffn_row, wr, br_row)


def _moe_body(be_ref, nv_ref, x_ref, wg_ref, bg_ref, wu_ref, bu_ref, wd_ref, bd_ref, y_ref, wg_b, wu_b, wd_b):
    i = pl.program_id(0)

    @pl.when(jnp.logical_or(i == 0, be_ref[i] != be_ref[jnp.maximum(i - 1, 0)]))
    def _():
        wg_b[...] = wg_ref[...].astype(BF16)
        wu_b[...] = wu_ref[...].astype(BF16)
        wd_b[...] = wd_ref[...].astype(BF16)

    @pl.when(i < nv_ref[0])
    def _():
        x = x_ref[...].astype(BF16)
        gate = jnp.minimum(_dot(x, wg_b[...]) + bg_ref[...], SWIGLU_LIMIT)
        up = jnp.clip(_dot(x, wu_b[...]) + bu_ref[...], -SWIGLU_LIMIT, SWIGLU_LIMIT)
        act = (up + 1.0) * gate * _sigmoid(SWIGLU_ALPHA * gate)
        y_ref[...] = _dot(act.astype(BF16), wd_b[...]) + bd_ref[...]

    @pl.when(i >= nv_ref[0])
    def _():
        y_ref[...] = jnp.zeros(y_ref.shape, y_ref.dtype)


def _moe_experts(blk_e, n_valid, xs, wg, bg, wu, bu, wd, bd):
    n_rows, d = xs.shape
    bm = MOE_BLOCK
    n_blk = n_rows // bm
    de = wg.shape[-1]
    row_idx = lambda i, be, nv: (jnp.minimum(i, nv[0] - 1), 0)
    bspec = lambda b: pl.BlockSpec((None, 1, b), lambda i, be, nv: (be[i], 0, 0))
    wspec = lambda a, b: pl.BlockSpec((None, a, b), lambda i, be, nv: (be[i], 0, 0))
    grid_spec = pltpu.PrefetchScalarGridSpec(
        num_scalar_prefetch=2,
        grid=(n_blk,),
        in_specs=[pl.BlockSpec((bm, d), row_idx),
                  wspec(d, de), bspec(de), wspec(d, de), bspec(de), wspec(de, d), bspec(d)],
        out_specs=pl.BlockSpec((bm, d), lambda i, be, nv: (i, 0)),
        scratch_shapes=[pltpu.VMEM((d, de), BF16), pltpu.VMEM((d, de), BF16), pltpu.VMEM((de, d), BF16)],
    )
    return pl.pallas_call(
        _moe_body,
        grid_spec=grid_spec,
        out_shape=jax.ShapeDtypeStruct((n_rows, d), F32),
        compiler_params=pltpu.CompilerParams(dimension_semantics=("arbitrary",), vmem_limit_bytes=MOE_VMEM_LIMIT),
        name="moe_experts",
    )(blk_e, n_valid, xs, wg, bg, wu, bu, wd, bd)


def _moe(h2, top_i, wg, bg, wu, bu, wd, bd):
    n, d = h2.shape
    bm = MOE_BLOCK
    m = n * TOP_K
    n_blk = -(-(m + N_EXP * (bm - 1)) // bm)
    flat_e = top_i.reshape(-1)
    onehot = (flat_e[:, None] == jnp.arange(N_EXP, dtype=jnp.int32)[None, :]).astype(jnp.int32)
    csum = jnp.cumsum(onehot, axis=0)
    rank = jnp.take_along_axis(csum, flat_e[:, None], axis=1)[:, 0] - 1
    counts = csum[-1]
    padded = (counts + bm - 1) // bm * bm
    pend = jnp.cumsum(padded)
    pstart = pend - padded
    dest = pstart[flat_e] + rank
    n_valid = (pend[-1] // bm).astype(jnp.int32)
    blk_ids = jnp.arange(n_blk, dtype=jnp.int32)
    expert_of = lambda blk: jnp.minimum(jnp.sum((pend[None, :] <= (blk * bm)[:, None]).astype(jnp.int32), axis=1), N_EXP - 1)
    blk_e = expert_of(jnp.minimum(blk_ids, n_valid - 1))
    shift = (m - 1).bit_length()
    assert N_EXP << shift < 2 ** 31
    pair_sorted = jnp.sort((flat_e << shift) | jnp.arange(m, dtype=jnp.int32)) & ((1 << shift) - 1)
    e_of_blk = expert_of(blk_ids)
    start = jnp.cumsum(counts) - counts
    slot = (blk_ids * bm - pstart[e_of_blk])[:, None] + jnp.arange(bm, dtype=jnp.int32)[None, :]
    src = jnp.clip(start[e_of_blk][:, None] + slot, 0, m - 1)
    row_tok = jnp.where(slot < counts[e_of_blk][:, None], pair_sorted[src.reshape(-1)].reshape(n_blk, bm) // TOP_K, n)
    row_tok = row_tok.reshape(-1)
    h_ext = jnp.concatenate([h2, jnp.zeros((1, d), h2.dtype)], axis=0)
    xs = h_ext[row_tok]
    y = _moe_experts(blk_e, n_valid.reshape(1), xs, wg, bg, wu, bu, wd, bd)
    return y[dest.reshape(n, TOP_K).T.reshape(-1)].reshape(TOP_K, n, d)


def _combine_body(x_ref, yg_ref, tw_ref, o_ref):
    tw = tw_ref[...]
    acc = x_ref[...]
    for k in range(TOP_K):
        acc = acc + tw[:, k:k + 1] * yg_ref[k]
    o_ref[...] = acc


def _combine(x1, yg, tw, row0):
    n, d = x1.shape
    tm = _row_tile(n, 384)
    assert row0 % tm == 0
    return pl.pallas_call(
        _combine_body,
        grid=(n // tm,),
        in_specs=[pl.BlockSpec((tm, d), lambda i: (i, 0)),
                  pl.BlockSpec((TOP_K, tm, d), lambda i: (0, row0 // tm + i, 0)),
                  pl.BlockSpec((tm, LANES), lambda i: (i, 0))],
        out_specs=pl.BlockSpec((tm, d), lambda i: (i, 0)),
        out_shape=jax.ShapeDtypeStruct((n, d), F32),
        compiler_params=_cparams("parallel"),
        name="combine",
    )(x1, yg, tw)


def _lane_row(vals, offset, width=LANES):
    return jnp.zeros((1, width), F32).at[0, offset:offset + vals.shape[0]].set(vals.astype(F32))


def kernel(x_prompt, x_sample, cache_k, cache_v, cache_logf, state_delta, state_conv, page_table,
           meta_tokens, g_mix, w_in, conv_w, a_log, dt_bias, norm_a, qn_g, kn_g, b_forget,
           p_a, p_f, w_o, g_ffn, w_router, b_router, w_gate, b_gate, w_up, b_up, w_down, b_down):
    depth = w_in.shape[0]
    b, seq, d = x_prompt.shape
    bd = x_sample.shape[0]
    assert x_sample.shape[1] == 1 and DK_A == LANES and DV_A == LANES
    t = N_META + seq
    lead = (-N_META) % CHUNK
    t_pad = -(-t // ATT_BLOCK) * ATT_BLOCK
    qkv_w, va_w, wf = 2 * H_A * DK_A + H_A * DV_A, H_A * DV_A, H_F * HD_F
    sizes = (qkv_w, va_w, H_A, H_A, wf, wf, wf, H_F, d, d)
    offs = [0]
    for s_ in sizes:
        offs.append(offs[-1] + s_)
    col = lambda i: slice(offs[i], offs[i + 1])
    widths = (qkv_w, va_w, wf, wf, wf, d, d)

    xp = jnp.concatenate([jnp.broadcast_to(meta_tokens.astype(x_prompt.dtype)[None], (b, N_META, d)), x_prompt], axis=1)
    xs = x_sample
    new_p = [[] for _ in range(5)]
    new_s = [[] for _ in range(5)]
    for l in range(depth):
        wl = w_in[l]
        w_main = jnp.concatenate([wl[:, col(0)], wl[:, col(1)], wl[:, col(4)], wl[:, col(5)], wl[:, col(6)],
                                  wl[:, col(8)], wl[:, col(9)]], axis=1).astype(BF16)
        w_small = jnp.concatenate([wl[:, col(2)], wl[:, col(3)], wl[:, col(7)],
                                   jnp.zeros((d, LANES - 2 * H_A - H_F), F32)], axis=1).astype(BF16)
        g_row = g_mix[l].reshape(1, d)
        a_row = _lane_row(a_log[l], H_A)
        dt_row = _lane_row(dt_bias[l], H_A)
        bf_row = _lane_row(b_forget[l], 2 * H_A)
        qg_pair = jnp.tile(qn_g[l], LANES // HD_F).reshape(1, LANES)
        kg_pair = jnp.tile(kn_g[l], LANES // HD_F).reshape(1, LANES)
        qg_full = jnp.tile(qn_g[l], H_F).reshape(1, wf)
        kg_full = jnp.tile(kn_g[l], H_F).reshape(1, wf)
        na_row = norm_a[l].reshape(1, DV_A)
        pa_b, pf_b, wo_b = p_a[l].astype(BF16), p_f[l].astype(BF16), w_o[l].astype(BF16)
        gffn_row = g_ffn[l].reshape(1, d)
        wr = jnp.pad(w_router[l], ((0, 0), (0, LANES - N_EXP))).astype(BF16)
        br_row = _lane_row(b_router[l], 0)
        merge_w = (na_row, pa_b, pf_b, wo_b, gffn_row, wr, br_row)

        qkv_p, z_p, qf_p, kf_p, vf_p, ga_p, gf_p, sm_p = _in_proj(xp.reshape(b * t, d), g_row, w_main, w_small, widths)
        r3 = lambda a: a.reshape(b, t, a.shape[-1])
        qkv3, sm3 = r3(qkv_p), r3(sm_p)
        conv_p = qkv3[:, t - (CONV_W - 1):, :]
        prep = _conv_prep(qkv3, conv_w[l], jnp.zeros((b, CONV_W - 1, qkv_w), F32))
        oa_p, s_p = _delta_scan(prep, sm3, a_row, dt_row, jnp.zeros((b, H_A, DK_A, DV_A), F32), lead)
        lf_p, c_col = _logf_cumsum(sm3, bf_row, t_pad)
        qx, kx, vb, kn_p = _fox_prep(r3(qf_p), r3(kf_p), r3(vf_p), c_col, qg_pair, kg_pair, t_pad)
        of_p = _fox_attn(qx, kx, vb)
        x1_p, h2_p, tw_p, ti_p = _merge(xp, oa_p, r3(z_p), of_p, r3(ga_p), r3(gf_p), *merge_w)

        qkv_s, z_s, qf_s, kf_s, vf_s, ga_s, gf_s, sm_s = _in_proj(xs.reshape(bd, d), g_row, w_main, w_small, widths)
        s3 = lambda a: a.reshape(bd, 1, a.shape[-1])
        oa_s, s_s, conv_s = _decode_delta(s3(qkv_s), state_conv[l], s3(sm_s), conv_w[l], a_row, dt_row, state_delta[l])
        n_pool, page = cache_k.shape[1], cache_k.shape[2]
        of_s, kn_s, lf_s = _decode_attn(page_table, s3(qf_s), s3(kf_s), s3(vf_s), s3(sm_s), qg_full, kg_full,
                                        _lane_row(b_forget[l], 2 * H_A),
                                        jnp.transpose(cache_k[l], (0, 2, 3, 1)).reshape(n_pool, wf, page),
                                        jnp.transpose(cache_v[l], (0, 2, 3, 1)).reshape(n_pool, wf, page),
                                        jnp.swapaxes(cache_logf[l], 1, 2))
        g1 = lambda a: a.reshape(1, bd, a.shape[-1])
        x1_s, h2_s, tw_s, ti_s = _merge(g1(xs), g1(oa_s), g1(z_s), g1(of_s), g1(ga_s), g1(gf_s), *merge_w)

        n_p = b * t
        h2_all = jnp.concatenate([h2_p.reshape(n_p, d), h2_s.reshape(bd, d)], axis=0)
        ti_all = jnp.concatenate([ti_p.reshape(n_p, LANES), ti_s.reshape(bd, LANES)], axis=0)[:, :TOP_K]
        yg = _moe(h2_all, ti_all, w_gate[l], b_gate[l][:, None, :],
                  w_up[l], b_up[l][:, None, :], w_down[l], b_down[l][:, None, :])
        xp = _combine(x1_p.reshape(n_p, d), yg, tw_p.reshape(n_p, LANES), 0).reshape(b, t, d)
        xs = _combine(x1_s.reshape(bd, d), yg, tw_s.reshape(bd, LANES), n_p).reshape(bd, 1, d)

        st_p = (kn_p.reshape(b, t, H_F, HD_F), vf_p.reshape(b, t, H_F, HD_F),
                lf_p[:, :, 2 * H_A:2 * H_A + H_F], s_p, conv_p)
        st_s = (kn_s.reshape(bd, 1, H_F, HD_F), vf_s.reshape(bd, 1, H_F, HD_F),
                lf_s[:, :, 2 * H_A:2 * H_A + H_F], s_s, conv_s)
        for lst, a in zip(new_p, st_p):
            lst.append(a)
        for lst, a in zip(new_s, st_s):
            lst.append(a)
    k_p, v_p, lf_pp, d_p, c_p = (jnp.stack(a) for a in new_p)
    k_s, v_s, lf_ss, d_s, c_s = (jnp.stack(a) for a in new_s)
    return (xp[:, N_META:], xs, k_p, v_p, lf_pp, k_s, v_s, lf_ss, d_p, d_s, c_p, c_s)
```

```python
import functools

import jax
import jax.numpy as jnp
from jax import lax
from jax.experimental import pallas as pl
from jax.experimental.pallas import tpu as pltpu

F32 = jnp.float32
BF16 = jnp.bfloat16

N_META = 16
H_A = 4
DK_A = 128
DV_A = 128
CONV_W = 4
CHUNK = 64
H_F = 8
HD_F = 64
N_EXP = 32
TOP_K = 4
SWIGLU_LIMIT = 7.0
SWIGLU_ALPHA = 1.702
EPS = 1e-6
LOG2E = 1.4426950408889634

LANES = 128
F_OFF = 2 * H_A
HP = 16
ATT_BLOCK = 256
MOE_BLOCK = 512
PAGES_PER_STEP = 32
VMEM_LIMIT = 48 * 1024 * 1024
BIG_VMEM_LIMIT = 58 * 1024 * 1024


def _cparams(*sem):
    return pltpu.CompilerParams(dimension_semantics=sem, vmem_limit_bytes=VMEM_LIMIT)


def _row_tile(n, cap):
    if n <= cap:
        return n
    best = None
    for t in range(8, cap + 1, 8):
        if n % t == 0:
            best = t
    assert best is not None, n
    return best


def _dot(a, b):
    return jnp.dot(a, b, preferred_element_type=F32)


def _dot_nt(a, b):
    return lax.dot_general(a, b, (((1,), (1,)), ((), ())), preferred_element_type=F32)


def _split2(x):
    hi = x.astype(BF16)
    lo = (x - hi.astype(F32)).astype(BF16)
    return hi, lo


def _dot_x3(a, b):
    a_hi, a_lo = _split2(a)
    b_hi, b_lo = _split2(b)
    return _dot(a_hi, b_hi) + (_dot(a_hi, b_lo) + _dot(a_lo, b_hi))


def _split3(x):
    hi = x.astype(BF16)
    r = x - hi.astype(F32)
    mid = r.astype(BF16)
    lo = (r - mid.astype(F32)).astype(BF16)
    return hi, mid, lo


def _sigmoid(x):
    return 1.0 / (1.0 + jnp.exp(-x))


def _softplus(x):
    return jnp.maximum(x, 0.0) + jnp.log1p(jnp.exp(-jnp.abs(x)))


def _log_sigmoid(x):
    return jnp.minimum(x, 0.0) - jnp.log1p(jnp.exp(-jnp.abs(x)))


COL_CHUNK = 512


def _in_proj_body(x_ref, g_ref, wm_ref, ws_ref, *out_refs, widths):
    x = x_ref[...]
    h = x * lax.rsqrt(jnp.mean(x * x, axis=-1, keepdims=True) + EPS) * g_ref[...]
    hb = h.astype(BF16)
    col = 0
    for o_ref, w in zip(out_refs[:-1], widths):
        for c0 in range(0, w, COL_CHUNK):
            o_ref[:, c0:c0 + COL_CHUNK] = _dot(hb, wm_ref[:, col + c0:col + c0 + COL_CHUNK])
        col += w
    out_refs[-1][...] = _dot(hb, ws_ref[...])


def _in_proj(x2d, g_row, w_main, w_small, widths):
    n, d = x2d.shape
    tm = _row_tile(n, 512)
    outs = [jax.ShapeDtypeStruct((n, w), F32) for w in widths] + [jax.ShapeDtypeStruct((n, LANES), F32)]
    out_specs = [pl.BlockSpec((tm, w), lambda i: (i, 0)) for w in widths] + [pl.BlockSpec((tm, LANES), lambda i: (i, 0))]
    return pl.pallas_call(
        functools.partial(_in_proj_body, widths=widths),
        grid=(n // tm,),
        in_specs=[pl.BlockSpec((tm, d), lambda i: (i, 0)),
                  pl.BlockSpec((1, d), lambda i: (0, 0)),
                  pl.BlockSpec(w_main.shape, lambda i: (0, 0), pipeline_mode=pl.Buffered(1)),
                  pl.BlockSpec(w_small.shape, lambda i: (0, 0), pipeline_mode=pl.Buffered(1))],
        out_specs=out_specs,
        out_shape=outs,
        compiler_params=_cparams("parallel"),
        name="in_proj",
    )(x2d, g_row, w_main, w_small)


def _conv_prep_body(x_ref, cw_ref, prev_ref, o_ref, buf, *, t_len, rows):
    j = pl.program_id(1)
    pad = 8
    buf[0:pad, :] = jnp.zeros((pad, LANES), F32)
    buf[pad - (CONV_W - 1):pad, :] = prev_ref[...]
    buf[pad:pad + t_len, :] = x_ref[...]
    cw = cw_ref[...]
    is_q = j < H_A
    is_qk = j < 2 * H_A
    for r0 in range(0, t_len, rows):
        y = jnp.zeros((rows, LANES), F32)
        for w in range(CONV_W):
            off = pad - (CONV_W - 1) + w + r0
            y = y + buf[off:off + rows, :] * cw[w:w + 1, :]
        y = y * _sigmoid(y)
        nrm = lax.rsqrt(jnp.sum(y * y, axis=-1, keepdims=True) + EPS)
        f = jnp.where(is_qk, nrm * jnp.where(is_q, DK_A ** -0.5, 1.0), 1.0)
        o_ref[r0:r0 + rows, :] = y * f


def _conv_prep(qkv3, conv_w, conv_prev):
    b, t, c = qkv3.shape
    rows = _row_tile(t, 512)
    return pl.pallas_call(
        functools.partial(_conv_prep_body, t_len=t, rows=rows),
        grid=(b, c // LANES),
        in_specs=[pl.BlockSpec((None, t, LANES), lambda i, j: (i, 0, j)),
                  pl.BlockSpec((CONV_W, LANES), lambda i, j: (0, j)),
                  pl.BlockSpec((None, CONV_W - 1, LANES), lambda i, j: (i, 0, j))],
        out_specs=pl.BlockSpec((None, t, LANES), lambda i, j: (i, 0, j)),
        out_shape=jax.ShapeDtypeStruct((b, t, c), F32),
        scratch_shapes=[pltpu.VMEM((8 + t, LANES), F32)],
        compiler_params=_cparams("parallel", "parallel"),
        name="conv_prep",
    )(qkv3, conv_w, conv_prev)


def _gates(sm, a_row, dt_row, h):
    lane = lax.broadcasted_iota(jnp.int32, sm.shape, 1)
    beta_all = _sigmoid(sm)
    g_all = -jnp.exp(a_row) * _softplus(sm + dt_row)
    beta = jnp.sum(jnp.where(lane == h, beta_all, 0.0), axis=-1, keepdims=True)
    g = jnp.sum(jnp.where(lane == H_A + h, g_all, 0.0), axis=-1, keepdims=True)
    return beta, g


def _decay_terms(gs):
    c = gs[0].shape[0]
    ri = lax.broadcasted_iota(jnp.int32, (c, c), 0)
    ci = lax.broadcasted_iota(jnp.int32, (c, c), 1)
    g_cum, decay = [], []
    for g in gs:
        g_row = jnp.sum(jnp.where(ri == ci, g, 0.0), axis=0, keepdims=True)
        gc = jnp.sum(jnp.where(ci <= ri, g_row, 0.0), axis=1, keepdims=True)
        gc_row = jnp.sum(jnp.where(ri <= ci, g, 0.0), axis=0, keepdims=True)
        g_cum.append(gc)
        decay.append(jnp.exp(jnp.where(ci <= ri, gc - gc_row, -jnp.inf)))
    return g_cum, decay


def _delta_step(cur, nxt, s_ref):
    o = inv_n = None
    stages = []
    if cur is not None:
        qs, ks, vs, betas, gs, inv = cur
        rng = range(len(qs))
        c = qs[0].shape[0]
        g_cum, decay = _decay_terms(gs)
        e_g = [jnp.exp(g_cum[i]) for i in rng]
        s = [s_ref[i] for i in rng]
        sb = [s[i].astype(BF16) for i in rng]
        qb = [qs[i].astype(BF16) for i in rng]
        kb = [ks[i].astype(BF16) for i in rng]
        k_s = [_dot(kb[i], sb[i]) for i in rng]
        val = {}

        def st_delta():
            rhs = [betas[i] * (vs[i] - e_g[i] * k_s[i]) for i in rng]
            val["db"] = [_dot_x3(inv[i], rhs[i]).astype(BF16) for i in rng]
            val["qk"] = [_dot_nt(qb[i], kb[i]) for i in rng]
            val["q_s"] = [_dot(qb[i], sb[i]) for i in rng]

        def st_out():
            val["o"] = [e_g[i] * val["q_s"][i] + _dot((val["qk"][i] * decay[i]).astype(BF16), val["db"][i]) for i in rng]

        def st_state():
            g_end = [g_cum[i][c - 1:c, :] for i in rng]
            k_dec = [(ks[i] * jnp.exp(g_end[i] - g_cum[i])).T.astype(BF16) for i in rng]
            for i in rng:
                s_ref[i] = jnp.exp(g_end[i]) * s[i] + _dot(k_dec[i], val["db"][i])

        stages = [st_delta, st_out, st_state]
    if nxt is not None:
        ks_n, betas_n, gs_n = nxt
        rng_n = range(len(ks_n))
        c = ks_n[0].shape[0]
        ri = lax.broadcasted_iota(jnp.int32, (c, c), 0)
        ci = lax.broadcasted_iota(jnp.int32, (c, c), 1)
        _, decay_n = _decay_terms(gs_n)
        kb_n = [ks_n[i].astype(BF16) for i in rng_n]
        kk = [_dot_nt(kb_n[i], kb_n[i]) for i in rng_n]
        a = [jnp.where(ci < ri, betas_n[i] * decay_n[i] * kk[i], 0.0) for i in rng_n]
        inv_n = [jnp.where(ri == ci, 1.0, 0.0) - a[i] for i in rng_n]
        pw = a
        for _ in range(max(1, (c - 1).bit_length() - 1)):
            if stages:
                stages.pop(0)()
            pw = [_dot_x3(pw[i], pw[i]) for i in rng_n]
            inv_n = [inv_n[i] + _dot_x3(inv_n[i], pw[i]) for i in rng_n]
    for st in stages:
        st()
    if cur is not None:
        o = val["o"]
    return o, inv_n


def _delta_scan_body(qkv_ref, sm_ref, a_ref, dt_ref, s0_ref, o_ref, s_out_ref, s_ref, *, t_len, lead):
    nb = qkv_ref.shape[0]
    pairs = [(bb, h) for bb in range(nb) for h in range(H_A)]
    a_row = a_ref[...]
    dt_row = dt_ref[...]
    s_ref[...] = s0_ref[...].reshape(nb * H_A, DK_A, DV_A)
    first = CHUNK - lead
    n_chunks = (lead + t_len) // CHUNK

    def gates(load, mask_lead):
        betas, gs = [], []
        for bb in range(nb):
            sm = load(sm_ref, bb, 0)
            for h in range(H_A):
                beta, g = _gates(sm, a_row, dt_row, h)
                if mask_lead:
                    row = lax.broadcasted_iota(jnp.int32, (CHUNK, 1), 0)
                    beta = jnp.where(row >= lead, beta, 0.0)
                    g = jnp.where(row >= lead, g, 0.0)
                betas.append(beta)
                gs.append(g)
        return betas, gs

    def nxt_args(load, mask_lead):
        betas, gs = gates(load, mask_lead)
        return [load(qkv_ref, bb, (H_A + h) * DK_A) for bb, h in pairs], betas, gs

    def cur_args(load, mask_lead, inv):
        betas, gs = gates(load, mask_lead)
        qs = [load(qkv_ref, bb, h * DK_A) for bb, h in pairs]
        ks = [load(qkv_ref, bb, (H_A + h) * DK_A) for bb, h in pairs]
        vs = [load(qkv_ref, bb, 2 * H_A * DK_A + h * DV_A) for bb, h in pairs]
        return qs, ks, vs, betas, gs, inv

    def chunk_rows(c):
        return pl.ds(pl.multiple_of(c * CHUNK - lead, 8), CHUNK)

    def loader(c):
        sl = chunk_rows(c)
        return lambda ref, bb, col: ref[bb, sl, col:col + LANES]

    if lead:
        def load0(ref, bb, col):
            return jnp.concatenate([jnp.zeros((lead, LANES), F32), ref[bb, 0:first, col:col + LANES]], axis=0)

        _, inv0 = _delta_step(None, nxt_args(load0, True), s_ref)
        has_next = n_chunks > 1
        outs, inv1 = _delta_step(cur_args(load0, True, inv0), nxt_args(loader(1), False) if has_next else None, s_ref)
        for (bb, h), o in zip(pairs, outs):
            o_ref[bb, 0:first, h * DV_A:(h + 1) * DV_A] = o[lead:, :]
        c_start = 1
    else:
        has_next = n_chunks > 0
        _, inv1 = _delta_step(None, nxt_args(loader(0), False), s_ref) if has_next else (None, None)
        c_start = 0

    def body(c, inv):
        sl = chunk_rows(c)
        outs, inv_n = _delta_step(cur_args(loader(c), False, inv),
                                  nxt_args(loader(jnp.minimum(c + 1, n_chunks - 1)), False), s_ref)
        for (bb, h), o in zip(pairs, outs):
            o_ref[bb, sl, h * DV_A:(h + 1) * DV_A] = o
        return tuple(inv_n)

    if has_next:
        lax.fori_loop(c_start, n_chunks, body, tuple(inv1))
    s_out_ref[...] = s_ref[...].reshape(nb, H_A, DK_A, DV_A)


def _delta_scan(qkv3, small3, a_row, dt_row, s0, lead):
    b, t, c = qkv3.shape
    assert (lead + t) % CHUNK == 0 and lead % 8 == 0
    nb = 2 if b % 2 == 0 else 1
    return pl.pallas_call(
        functools.partial(_delta_scan_body, t_len=t, lead=lead),
        grid=(b // nb,),
        in_specs=[pl.BlockSpec((nb, t, c), lambda i: (i, 0, 0), pipeline_mode=pl.Buffered(1)),
                  pl.BlockSpec((nb, t, LANES), lambda i: (i, 0, 0)),
                  pl.BlockSpec((1, LANES), lambda i: (0, 0)),
                  pl.BlockSpec((1, LANES), lambda i: (0, 0)),
                  pl.BlockSpec((nb, H_A, DK_A, DV_A), lambda i: (i, 0, 0, 0))],
        out_specs=[pl.BlockSpec((nb, t, H_A * DV_A), lambda i: (i, 0, 0)),
                   pl.BlockSpec((nb, H_A, DK_A, DV_A), lambda i: (i, 0, 0, 0))],
        out_shape=[jax.ShapeDtypeStruct((b, t, H_A * DV_A), F32),
                   jax.ShapeDtypeStruct((b, H_A, DK_A, DV_A), F32)],
        scratch_shapes=[pltpu.VMEM((nb * H_A, DK_A, DV_A), F32)],
        compiler_params=pltpu.CompilerParams(dimension_semantics=("parallel",), vmem_limit_bytes=BIG_VMEM_LIMIT),
        name="delta_scan",
    )(qkv3, small3, a_row, dt_row, s0)


def _group_rms(x, gain, seg):
    hi, lo = _split2(x * x)
    ss = _dot(hi, seg) + _dot(lo, seg)
    return x * lax.rsqrt(ss * (1.0 / HD_F) + EPS) * gain


def _seg_matrix(n):
    r = lax.broadcasted_iota(jnp.int32, (n, n), 0) // HD_F
    c = lax.broadcasted_iota(jnp.int32, (n, n), 1) // HD_F
    return jnp.where(r == c, 1.0, 0.0).astype(BF16)


def _fox_prep_body(q_ref, k_ref, v_ref, c_ref, qg_ref, kg_ref, qx_ref, kx_ref, vt_ref, kn_ref, *, t_len, t_pad):
    p = pl.program_id(1)
    heads = LANES // HD_F

    def feature_major(x):
        if t_pad > t_len:
            x = jnp.concatenate([x, jnp.zeros((t_pad - t_len, LANES), F32)], axis=0)
        return x.T.astype(BF16)

    seg = _seg_matrix(LANES)
    qn = _group_rms(q_ref[...], qg_ref[...], seg) * (HD_F ** -0.5 * LOG2E)
    kn = _group_rms(k_ref[...], kg_ref[...], seg)
    kn_ref[...] = kn
    c_all = c_ref[...] * LOG2E
    lane = lax.broadcasted_iota(jnp.int32, (t_len, LANES), 1)
    for hh in range(heads):
        c = jnp.sum(jnp.where(lane == F_OFF + p * heads + hh, c_all, 0.0), axis=-1, keepdims=True)
        pieces = [x.astype(F32) for x in _split3(c)]
        own = (lane // HD_F) == hh
        f0 = ((hh + 1) % heads) * HD_F
        qx = jnp.where(own, qn, 0.0)
        kx = jnp.where(own, kn, 0.0)
        for n, piece in enumerate(pieces):
            qx = jnp.where(lane == f0 + n, piece, qx)
            kx = jnp.where(lane == f0 + n, 1.0, kx)
            qx = jnp.where(lane == f0 + 3 + n, 1.0, qx)
            kx = jnp.where(lane == f0 + 3 + n, -piece, kx)
        kx_ref[0:t_len, hh * LANES:(hh + 1) * LANES] = kx.astype(BF16)
        qx_ref[hh * LANES:(hh + 1) * LANES, :] = feature_major(qx)
    vt_ref[...] = feature_major(v_ref[...])
    if t_pad > t_len:
        kx_ref[t_len:, :] = jnp.zeros((t_pad - t_len, heads * LANES), BF16)


def _fox_prep(q3, k3, v3, c_col, qg_row, kg_row, t_pad):
    b, t, w = q3.shape
    heads = LANES // HD_F
    blk = pl.BlockSpec((None, t, LANES), lambda i, p: (i, 0, p))
    gblk = pl.BlockSpec((1, LANES), lambda i, p: (0, 0))
    return pl.pallas_call(
        functools.partial(_fox_prep_body, t_len=t, t_pad=t_pad),
        grid=(b, w // LANES),
        in_specs=[blk, blk, blk, pl.BlockSpec((None, t, LANES), lambda i, p: (i, 0, 0)), gblk, gblk],
        out_specs=[pl.BlockSpec((None, heads * LANES, t_pad), lambda i, p: (i, p, 0)),
                   pl.BlockSpec((None, t_pad, heads * LANES), lambda i, p: (i, 0, p)),
                   pl.BlockSpec((None, LANES, t_pad), lambda i, p: (i, p, 0)), blk],
        out_shape=[jax.ShapeDtypeStruct((b, H_F * LANES, t_pad), BF16), jax.ShapeDtypeStruct((b, t_pad, H_F * LANES), BF16),
                   jax.ShapeDtypeStruct((b, w, t_pad), BF16), jax.ShapeDtypeStruct((b, t, w), F32)],
        compiler_params=_cparams("parallel", "parallel"),
        name="fox_prep",
    )(q3, k3, v3, c_col, qg_row, kg_row)


def _logf_cumsum_body(sm_ref, bf_ref, lf_ref, ccol_ref, buf, *, t_len, t_pad):
    lf = _log_sigmoid(sm_ref[...] + bf_ref[...])
    lf_ref[...] = lf
    buf[0:t_len, :] = lf
    if t_pad > t_len:
        buf[t_len:, :] = jnp.zeros((t_pad - t_len, LANES), F32)
    blk = ATT_BLOCK
    ri = lax.broadcasted_iota(jnp.int32, (blk, blk), 0)
    ci = lax.broadcasted_iota(jnp.int32, (blk, blk), 1)
    tri = jnp.where(ci <= ri, 1.0, 0.0).astype(BF16)
    carry = jnp.zeros((1, LANES), F32)
    for i in range(t_pad // blk):
        hi, mid, lo = _split3(buf[i * blk:(i + 1) * blk, :])
        c = _dot(tri, hi) + _dot(tri, mid) + _dot(tri, lo) + carry
        ccol_ref[i * blk:(i + 1) * blk, :] = c
        carry = c[blk - 1:blk, :]


def _logf_cumsum(small3, bf_row, t_pad):
    b, t, _ = small3.shape
    return pl.pallas_call(
        functools.partial(_logf_cumsum_body, t_len=t, t_pad=t_pad),
        grid=(b,),
        in_specs=[pl.BlockSpec((None, t, LANES), lambda i: (i, 0, 0)),
                  pl.BlockSpec((1, LANES), lambda i: (0, 0))],
        out_specs=[pl.BlockSpec((None, t, LANES), lambda i: (i, 0, 0)),
                   pl.BlockSpec((None, t_pad, LANES), lambda i: (i, 0, 0))],
        out_shape=[jax.ShapeDtypeStruct((b, t, LANES), F32),
                   jax.ShapeDtypeStruct((b, t_pad, LANES), F32)],
        scratch_shapes=[pltpu.VMEM((t_pad, LANES), F32)],
        compiler_params=_cparams("parallel"),
        name="logf_cumsum",
    )(small3, bf_row)


def _fox_attn_body(qt_ref, k_ref, vt_ref, o_ref):
    i = pl.program_id(1)
    blk = ATT_BLOCK
    key_le_query = (lax.broadcasted_iota(jnp.int32, (blk, blk), 0)
                    <= lax.broadcasted_iota(jnp.int32, (blk, blk), 1))

    def step(j, carry, diagonal):
        ms, ls, accs = carry
        k0 = pl.multiple_of(j * blk, blk)
        st = []
        for h in range(H_F):
            s = _dot(k_ref[pl.ds(k0, blk), h * LANES:(h + 1) * LANES], qt_ref[h * LANES:(h + 1) * LANES, :])
            st.append(jnp.where(key_le_query, s, -jnp.inf) if diagonal else s)
        new_ms, new_ls, alphas, pts = [], [], [], []
        for h in range(H_F):
            m_new = jnp.maximum(ms[h], jnp.max(st[h], axis=0, keepdims=True))
            alpha = jnp.exp2(ms[h] - m_new)
            pt = jnp.exp2(st[h] - m_new)
            new_ls.append(alpha * ls[h] + jnp.sum(pt, axis=0, keepdims=True))
            new_ms.append(m_new)
            alphas.append(alpha)
            pts.append(pt.astype(BF16))
        new_accs = [alphas[h] * accs[h] + _dot(vt_ref[h * HD_F:(h + 1) * HD_F, pl.ds(k0, blk)], pts[h])
                    for h in range(H_F)]
        return tuple(new_ms), tuple(new_ls), tuple(new_accs)

    init = (tuple(jnp.full((1, blk), -jnp.inf, F32) for _ in range(H_F)),
            tuple(jnp.zeros((1, blk), F32) for _ in range(H_F)),
            tuple(jnp.zeros((HD_F, blk), F32) for _ in range(H_F)))
    carry = lax.fori_loop(0, i, lambda j, c: step(j, c, False), init)
    ms, ls, accs = step(i, carry, True)
    heads = LANES // HD_F
    for p in range(H_F // heads):
        ot = jnp.concatenate([accs[p * heads + hh] * (1.0 / ls[p * heads + hh]) for hh in range(heads)], axis=0)
        o_ref[:, p * LANES:(p + 1) * LANES] = ot.T.astype(o_ref.dtype)


def _fox_attn(qt, kx, vt):
    b, w, t_pad = vt.shape
    blk = ATT_BLOCK
    return pl.pallas_call(
        _fox_attn_body,
        grid=(b, t_pad // blk),
        in_specs=[pl.BlockSpec((None, H_F * LANES, blk), lambda n, i: (n, 0, i)),
                  pl.BlockSpec((None, t_pad, H_F * LANES), lambda n, i: (n, 0, 0)),
                  pl.BlockSpec((None, w, t_pad), lambda n, i: (n, 0, 0))],
        out_specs=pl.BlockSpec((None, blk, w), lambda n, i: (n, i, 0)),
        out_shape=jax.ShapeDtypeStruct((b, t_pad, w), BF16),
        compiler_params=_cparams("parallel", "arbitrary"),
        name="fox_attn",
    )(qt, kx, vt)


def _decode_delta_body(x_ref, sc_ref, sm_ref, cw_ref, a_ref, dt_ref, s_ref, o_ref, s_out_ref, conv_ref):
    x_new = x_ref[...]
    sc = sc_ref[...]
    cw = cw_ref[...]
    y = x_new * cw[CONV_W - 1:CONV_W, :]
    for w in range(CONV_W - 1):
        y = y + sc[w:w + 1, :] * cw[w:w + 1, :]
    y = y * _sigmoid(y)
    conv_ref[0:CONV_W - 2, :] = sc[1:, :]
    conv_ref[CONV_W - 2:CONV_W - 1, :] = x_new
    sm = sm_ref[...]
    lane = lax.broadcasted_iota(jnp.int32, (1, LANES), 1)
    beta_all = _sigmoid(sm)
    g_all = -jnp.exp(a_ref[...]) * _softplus(sm + dt_ref[...])
    qk_w = H_A * DK_A
    row0 = lax.broadcasted_iota(jnp.int32, (LANES, LANES), 0) == 0

    def in_row0(r):
        return jnp.where(row0, jnp.broadcast_to(r, (LANES, LANES)), 0.0)

    for h in range(H_A):
        q = y[:, h * DK_A:(h + 1) * DK_A]
        k = y[:, qk_w + h * DK_A:qk_w + (h + 1) * DK_A]
        v = y[:, 2 * qk_w + h * DV_A:2 * qk_w + (h + 1) * DV_A]
        q = q * lax.rsqrt(jnp.sum(q * q, axis=-1, keepdims=True) + EPS) * (DK_A ** -0.5)
        k = k * lax.rsqrt(jnp.sum(k * k, axis=-1, keepdims=True) + EPS)
        beta = jnp.sum(jnp.where(lane == h, beta_all, 0.0), axis=-1, keepdims=True)
        g = jnp.sum(jnp.where(lane == H_A + h, g_all, 0.0), axis=-1, keepdims=True)
        e_g = jnp.exp(g)
        s = s_ref[h]
        sb = s.astype(BF16)
        k_sq = in_row0(k)
        k_s = _dot(k_sq.astype(BF16), sb)[0:1, :]
        q_s = _dot(in_row0(q).astype(BF16), sb)[0:1, :]
        delta = beta * (v - e_g * k_s)
        qk = jnp.sum(q * k, axis=-1, keepdims=True)
        o_ref[:, h * DV_A:(h + 1) * DV_A] = e_g * q_s + qk * delta
        s_out_ref[h] = e_g * s + _dot_x3(k_sq.T, in_row0(delta))


def _decode_delta(qkv3, state_conv, small3, conv_w, a_row, dt_row, s0):
    bd, _, c = qkv3.shape
    return pl.pallas_call(
        _decode_delta_body,
        grid=(bd,),
        in_specs=[pl.BlockSpec((None, 1, c), lambda i: (i, 0, 0)),
                  pl.BlockSpec((None, CONV_W - 1, c), lambda i: (i, 0, 0)),
                  pl.BlockSpec((None, 1, LANES), lambda i: (i, 0, 0)),
                  pl.BlockSpec((CONV_W, c), lambda i: (0, 0)),
                  pl.BlockSpec((1, LANES), lambda i: (0, 0)),
                  pl.BlockSpec((1, LANES), lambda i: (0, 0)),
                  pl.BlockSpec((None, H_A, DK_A, DV_A), lambda i: (i, 0, 0, 0))],
        out_specs=[pl.BlockSpec((None, 1, H_A * DV_A), lambda i: (i, 0, 0)),
                   pl.BlockSpec((None, H_A, DK_A, DV_A), lambda i: (i, 0, 0, 0)),
                   pl.BlockSpec((None, CONV_W - 1, c), lambda i: (i, 0, 0))],
        out_shape=[jax.ShapeDtypeStruct((bd, 1, H_A * DV_A), F32),
                   jax.ShapeDtypeStruct((bd, H_A, DK_A, DV_A), F32),
                   jax.ShapeDtypeStruct((bd, CONV_W - 1, c), F32)],
        compiler_params=_cparams("parallel"),
        name="decode_delta",
    )(qkv3, state_conv, small3, conv_w, a_row, dt_row, s0)


def _decode_attn_body(pt_ref, q_ref, k_ref, v_ref, sm_ref, qg_ref, kg_ref, bf_ref, *rest, n_pg):
    page_refs = rest[:3 * n_pg]
    o_ref, kn_ref, lf_ref, qrows, vnew, snew, m_ref, l_ref, acc_ref, carry_ref, s_scr = rest[3 * n_pg:]
    step = pl.program_id(1)
    n_steps = pl.num_programs(1) // 2
    w = H_F * HD_F
    row = lax.broadcasted_iota(jnp.int32, (HP, w), 0)
    head_of_lane = lax.broadcasted_iota(jnp.int32, (HP, w), 1) // HD_F

    @pl.when(step == 0)
    def _():
        seg = _seg_matrix(w)

        def rms_rows(x_row, gain):
            xr = jnp.broadcast_to(x_row, (HP, w))
            hi, lo = _split2(xr * xr)
            ss = _dot(hi, seg) + _dot(lo, seg)
            return xr * lax.rsqrt(ss * (1.0 / HD_F) + EPS) * gain

        qn = rms_rows(q_ref[...], qg_ref[...]) * (HD_F ** -0.5)
        kn = rms_rows(k_ref[...], kg_ref[...])
        kn_ref[...] = kn[0:1, :]
        lf = _log_sigmoid(sm_ref[...] + bf_ref[...])
        lf_ref[...] = lf
        q_m = jnp.where(head_of_lane == row, qn, 0.0).astype(BF16)
        qrows[...] = q_m
        vnew[...] = jnp.broadcast_to(v_ref[...], (HP, w)).astype(BF16)
        s_new = jnp.sum(q_m.astype(F32) * kn.astype(BF16).astype(F32), axis=-1, keepdims=True)
        snew[...] = s_new
        m_ref[...] = s_new
        rr = lax.broadcasted_iota(jnp.int32, (HP, LANES), 0)
        ll = lax.broadcasted_iota(jnp.int32, (HP, LANES), 1)
        carry_ref[...] = jnp.sum(jnp.where(ll == rr + F_OFF, jnp.broadcast_to(lf, (HP, LANES)), 0.0),
                                 axis=-1, keepdims=True)

    @pl.when(step < n_steps)
    def _():
        qb = qrows[...]
        lf_all = jnp.concatenate(
            [jnp.concatenate([page_refs[3 * g + 2][...], jnp.zeros((HP - H_F, LANES), F32)], axis=0)
             for g in range(n_pg)], axis=0)
        later = (lax.broadcasted_iota(jnp.int32, (LANES, LANES), 0)
                 > lax.broadcasted_iota(jnp.int32, (LANES, LANES), 1)).astype(BF16)
        hi, mid, lo = _split3(lf_all)
        after = _dot(hi, later) + (_dot(mid, later) + _dot(lo, later))
        totals = jnp.sum(lf_all, axis=-1, keepdims=True)
        raw = [_dot(qb, page_refs[3 * g][...].astype(BF16)) for g in range(n_pg)]
        carry = carry_ref[...]
        m_new = m_ref[...]
        for g in range(n_pg):
            rows = slice(g * HP, (g + 1) * HP)
            s = raw[g] + (carry + after[rows, :])
            s_scr[step * n_pg + g] = s
            m_new = jnp.maximum(m_new, jnp.max(s, axis=-1, keepdims=True))
            carry = carry + totals[rows, :]
        carry_ref[...] = carry
        m_ref[...] = m_new

    @pl.when(step == n_steps)
    def _():
        m = m_ref[...]
        p_new = jnp.exp(snew[...] - m)

        def add(i, tot):
            return tot + jnp.exp(s_scr[i] - m)

        tot = lax.fori_loop(0, n_steps * n_pg, add, jnp.zeros((HP, LANES), F32), unroll=8)
        l = p_new + jnp.sum(tot, axis=-1, keepdims=True)
        l_ref[...] = l
        acc_ref[...] = (p_new / l).astype(BF16).astype(F32) * vnew[...].astype(F32)

    @pl.when(step >= n_steps)
    def _():
        m = m_ref[...]
        l = l_ref[...]
        probs = [(jnp.exp(s_scr[(step - n_steps) * n_pg + g] - m) / l).astype(BF16) for g in range(n_pg)]
        parts = [_dot_nt(probs[g], page_refs[3 * g + 1][...].astype(BF16)) for g in range(n_pg)]
        acc = acc_ref[...]
        for part in parts:
            acc = acc + part
        acc_ref[...] = acc

    @pl.when(step == pl.num_programs(1) - 1)
    def _():
        o_ref[...] = jnp.sum(jnp.where(head_of_lane == row, acc_ref[...], 0.0), axis=0, keepdims=True)


def _decode_attn(page_table, q3, k3, v3, small3, qg_row, kg_row, bf_row, ck, cv, clf_t):
    bd, _, w = q3.shape
    n_pages = page_table.shape[1]
    page = ck.shape[2]
    assert page == LANES and n_pages % PAGES_PER_STEP == 0 and LANES // HD_F == 2
    n_pg = PAGES_PER_STEP
    steps = n_pages // n_pg

    def page_of(i, s, g, pt):
        return pt[i, n_pages - 1 - (s * n_pg + g)]

    def key_idx(g):
        return lambda i, s, pt: (page_of(i, jnp.minimum(s, steps - 1), g, pt), 0, 0)

    def val_idx(g):
        return lambda i, s, pt: (page_of(i, jnp.maximum(s - steps, 0), g, pt), 0, 0)

    row_spec = lambda width: pl.BlockSpec((None, 1, width), lambda i, s, pt: (i, 0, 0))
    const_spec = lambda width: pl.BlockSpec((1, width), lambda i, s, pt: (0, 0))
    in_specs = [row_spec(w), row_spec(w), row_spec(w), row_spec(LANES), const_spec(w), const_spec(w), const_spec(LANES)]
    args = [q3, k3, v3, small3, qg_row, kg_row, bf_row]
    for g in range(n_pg):
        in_specs += [pl.BlockSpec((None, w, page), key_idx(g)),
                     pl.BlockSpec((None, w, page), val_idx(g)),
                     pl.BlockSpec((None, H_F, page), key_idx(g))]
        args += [ck, cv, clf_t]
    col = lambda: pltpu.VMEM((HP, 1), F32)
    grid_spec = pltpu.PrefetchScalarGridSpec(
        num_scalar_prefetch=1,
        grid=(bd, 2 * steps),
        in_specs=in_specs,
        out_specs=[row_spec(w), row_spec(w), row_spec(LANES)],
        scratch_shapes=[pltpu.VMEM((HP, w), BF16), pltpu.VMEM((HP, w), BF16), col(), col(), col(),
                        pltpu.VMEM((HP, w), F32), col(), pltpu.VMEM((n_pages, HP, LANES), F32)],
    )
    return pl.pallas_call(
        functools.partial(_decode_attn_body, n_pg=n_pg),
        grid_spec=grid_spec,
        out_shape=[jax.ShapeDtypeStruct((bd, 1, w), F32), jax.ShapeDtypeStruct((bd, 1, w), F32),
                   jax.ShapeDtypeStruct((bd, 1, LANES), F32)],
        compiler_params=_cparams("parallel", "arbitrary"),
        name="decode_attn",
    )(page_table, *args)


def _merge_body(x_ref, oa_ref, z_ref, of_ref, ga_ref, gf_ref, na_ref, pa_ref, pf_ref, wo_ref, gffn_ref,
                wr_ref, br_ref, x1_ref, h2_ref, tw_ref, ti_ref):
    o = oa_ref[...]
    z = z_ref[...]
    na = na_ref[...]
    parts = []
    for h in range(H_A):
        oh = o[:, h * DV_A:(h + 1) * DV_A]
        zh = z[:, h * DV_A:(h + 1) * DV_A]
        on = oh * lax.rsqrt(jnp.mean(oh * oh, axis=-1, keepdims=True) + EPS) * na
        parts.append((on * (zh * _sigmoid(zh))).astype(BF16))
    o_a = jnp.concatenate(parts, axis=-1)
    ya = _dot(o_a, pa_ref[...])
    yf = _dot(of_ref[...].astype(BF16), pf_ref[...])
    mixed = _sigmoid(ga_ref[...]) * ya + _sigmoid(gf_ref[...]) * yf
    x1 = x_ref[...] + _dot(mixed.astype(BF16), wo_ref[...])
    x1_ref[...] = x1
    h2 = x1 * lax.rsqrt(jnp.mean(x1 * x1, axis=-1, keepdims=True) + EPS) * gffn_ref[...]
    h2_ref[...] = h2
    logits = _dot(h2.astype(BF16), wr_ref[...]) + br_ref[...]
    lane = lax.broadcasted_iota(jnp.int32, logits.shape, 1)
    l = jnp.where(lane < N_EXP, logits, -jnp.inf)
    vals, idxs = [], []
    for _ in range(TOP_K):
        m = jnp.max(l, axis=-1, keepdims=True)
        idx = jnp.min(jnp.where(l == m, lane, LANES), axis=-1, keepdims=True)
        vals.append(m)
        idxs.append(idx)
        l = jnp.where(lane == idx, -jnp.inf, l)
    es = [jnp.exp(v - vals[0]) for v in vals]
    den = es[0]
    for e in es[1:]:
        den = den + e
    tw = jnp.zeros(logits.shape, F32)
    ti = jnp.zeros(logits.shape, jnp.int32)
    for kk in range(TOP_K):
        tw = jnp.where(lane == kk, es[kk] / den, tw)
        ti = jnp.where(lane == kk, idxs[kk], ti)
    tw_ref[...] = tw
    ti_ref[...] = ti


def _merge(x3, oa3, z3, of3, ga3, gf3, na_row, pa, pf, wo, gffn_row, wr, br_row):
    g, t, d = x3.shape
    tm = _row_tile(t, 384)
    tok = lambda c: pl.BlockSpec((None, tm, c), lambda i, j: (i, j, 0))
    const = lambda a: pl.BlockSpec(a.shape, lambda i, j: (0,) * a.ndim, pipeline_mode=pl.Buffered(1))
    wa = H_A * DV_A
    wf = H_F * HD_F
    return pl.pallas_call(
        _merge_body,
        grid=(g, t // tm),
        in_specs=[tok(d), tok(wa), tok(wa), tok(wf), tok(d), tok(d),
                  const(na_row), const(pa), const(pf), const(wo), const(gffn_row),
                  const(wr), const(br_row)],
        out_specs=[tok(d), tok(d), tok(LANES), tok(LANES)],
        out_shape=[jax.ShapeDtypeStruct((g, t, d), F32), jax.ShapeDtypeStruct((g, t, d), F32),
                   jax.ShapeDtypeStruct((g, t, LANES), F32), jax.ShapeDtypeStruct((g, t, LANES), jnp.int32)],
        compiler_params=_cparams("parallel", "parallel"),
        name="merge",
    )(x3, oa3, z3, of3, ga3, gf3, na_row, pa, pf, wo, gffn_row, wr, br_row)


def _moe_body(be_ref, nv_ref, x_ref, wg_ref, bg_ref, wu_ref, bu_ref, wd_ref, bd_ref, y_ref, wg_b, wu_b, wd_b):
    i = pl.program_id(0)

    @pl.when(jnp.logical_or(i == 0, be_ref[i] != be_ref[jnp.maximum(i - 1, 0)]))
    def _():
        wg_b[...] = wg_ref[...].astype(BF16)
        wu_b[...] = wu_ref[...].astype(BF16)
        wd_b[...] = wd_ref[...].astype(BF16)

    @pl.when(i < nv_ref[0])
    def _():
        x = x_ref[...].astype(BF16)
        gate = jnp.minimum(_dot(x, wg_b[...]) + bg_ref[...], SWIGLU_LIMIT)
        up = jnp.clip(_dot(x, wu_b[...]) + bu_ref[...], -SWIGLU_LIMIT, SWIGLU_LIMIT)
        act = (up + 1.0) * gate * _sigmoid(SWIGLU_ALPHA * gate)
        y_ref[...] = _dot(act.astype(BF16), wd_b[...]) + bd_ref[...]

    @pl.when(i >= nv_ref[0])
    def _():
        y_ref[...] = jnp.zeros(y_ref.shape, y_ref.dtype)


def _moe_experts(blk_e, n_valid, xs, wg, bg, wu, bu, wd, bd):
    n_rows, d = xs.shape
    bm = MOE_BLOCK
    n_blk = n_rows // bm
    de = wg.shape[-1]
    row_idx = lambda i, be, nv: (jnp.minimum(i, nv[0] - 1), 0)
    bspec = lambda b: pl.BlockSpec((None, 1, b), lambda i, be, nv: (be[i], 0, 0))
    wspec = lambda a, b: pl.BlockSpec((None, a, b), lambda i, be, nv: (be[i], 0, 0))
    grid_spec = pltpu.PrefetchScalarGridSpec(
        num_scalar_prefetch=2,
        grid=(n_blk,),
        in_specs=[pl.BlockSpec((bm, d), row_idx),
                  wspec(d, de), bspec(de), wspec(d, de), bspec(de), wspec(de, d), bspec(d)],
        out_specs=pl.BlockSpec((bm, d), lambda i, be, nv: (i, 0)),
        scratch_shapes=[pltpu.VMEM((d, de), BF16), pltpu.VMEM((d, de), BF16), pltpu.VMEM((de, d), BF16)],
    )
    return pl.pallas_call(
        _moe_body,
        grid_spec=grid_spec,
        out_shape=jax.ShapeDtypeStruct((n_rows, d), F32),
        compiler_params=pltpu.CompilerParams(dimension_semantics=("arbitrary",), vmem_limit_bytes=BIG_VMEM_LIMIT),
        name="moe_experts",
    )(blk_e, n_valid, xs, wg, bg, wu, bu, wd, bd)


def _moe(h2, top_i, wg, bg, wu, bu, wd, bd):
    n, d = h2.shape
    bm = MOE_BLOCK
    m = n * TOP_K
    n_blk = -(-(m + N_EXP * (bm - 1)) // bm)
    flat_e = top_i.reshape(-1)
    onehot = (flat_e[:, None] == jnp.arange(N_EXP, dtype=jnp.int32)[None, :]).astype(jnp.int32)
    csum = jnp.cumsum(onehot, axis=0)
    rank = jnp.take_along_axis(csum, flat_e[:, None], axis=1)[:, 0] - 1
    counts = csum[-1]
    padded = (counts + bm - 1) // bm * bm
    pend = jnp.cumsum(padded)
    pstart = pend - padded
    dest = pstart[flat_e] + rank
    n_valid = (pend[-1] // bm).astype(jnp.int32)
    blk_ids = jnp.arange(n_blk, dtype=jnp.int32)
    expert_of = lambda blk: jnp.minimum(jnp.sum((pend[None, :] <= (blk * bm)[:, None]).astype(jnp.int32), axis=1), N_EXP - 1)
    blk_e = expert_of(jnp.minimum(blk_ids, n_valid - 1))
    shift = (m - 1).bit_length()
    assert N_EXP << shift < 2 ** 31
    pair_sorted = jnp.sort((flat_e << shift) | jnp.arange(m, dtype=jnp.int32)) & ((1 << shift) - 1)
    e_of_blk = expert_of(blk_ids)
    start = jnp.cumsum(counts) - counts
    slot = (blk_ids * bm - pstart[e_of_blk])[:, None] + jnp.arange(bm, dtype=jnp.int32)[None, :]
    src = jnp.clip(start[e_of_blk][:, None] + slot, 0, m - 1)
    row_tok = jnp.where(slot < counts[e_of_blk][:, None], pair_sorted[src.reshape(-1)].reshape(n_blk, bm) // TOP_K, n)
    row_tok = row_tok.reshape(-1)
    h_ext = jnp.concatenate([h2, jnp.zeros((1, d), h2.dtype)], axis=0)
    xs = h_ext[row_tok]
    y = _moe_experts(blk_e, n_valid.reshape(1), xs, wg, bg, wu, bu, wd, bd)
    return y[dest.reshape(n, TOP_K).T.reshape(-1)].reshape(TOP_K, n, d)


def _combine_body(x_ref, yg_ref, tw_ref, o_ref):
    tw = tw_ref[...]
    acc = x_ref[...]
    for k in range(TOP_K):
        acc = acc + tw[:, k:k + 1] * yg_ref[k]
    o_ref[...] = acc


def _combine(x1, yg, tw, row0):
    n, d = x1.shape
    tm = _row_tile(n, 384)
    assert row0 % tm == 0
    return pl.pallas_call(
        _combine_body,
        grid=(n // tm,),
        in_specs=[pl.BlockSpec((tm, d), lambda i: (i, 0)),
                  pl.BlockSpec((TOP_K, tm, d), lambda i: (0, row0 // tm + i, 0)),
                  pl.BlockSpec((tm, LANES), lambda i: (i, 0))],
        out_specs=pl.BlockSpec((tm, d), lambda i: (i, 0)),
        out_shape=jax.ShapeDtypeStruct((n, d), F32),
        compiler_params=_cparams("parallel"),
        name="combine",
    )(x1, yg, tw)


def _lane_row(vals, offset, width=LANES):
    return jnp.zeros((1, width), F32).at[0, offset:offset + vals.shape[0]].set(vals.astype(F32))


def kernel(x_prompt, x_sample, cache_k, cache_v, cache_logf, state_delta, state_conv, page_table,
           meta_tokens, g_mix, w_in, conv_w, a_log, dt_bias, norm_a, qn_g, kn_g, b_forget,
           p_a, p_f, w_o, g_ffn, w_router, b_router, w_gate, b_gate, w_up, b_up, w_down, b_down):
    depth = w_in.shape[0]
    b, seq, d = x_prompt.shape
    bd = x_sample.shape[0]
    assert x_sample.shape[1] == 1 and DK_A == LANES and DV_A == LANES
    t = N_META + seq
    lead = (-N_META) % CHUNK
    t_pad = -(-t // ATT_BLOCK) * ATT_BLOCK
    qkv_w, va_w, wf = 2 * H_A * DK_A + H_A * DV_A, H_A * DV_A, H_F * HD_F
    sizes = (qkv_w, va_w, H_A, H_A, wf, wf, wf, H_F, d, d)
    offs = [0]
    for s_ in sizes:
        offs.append(offs[-1] + s_)
    col = lambda i: slice(offs[i], offs[i + 1])
    widths = (qkv_w, va_w, wf, wf, wf, d, d)

    xp = jnp.concatenate([jnp.broadcast_to(meta_tokens.astype(x_prompt.dtype)[None], (b, N_META, d)), x_prompt], axis=1)
    xs = x_sample
    new_p = [[] for _ in range(5)]
    new_s = [[] for _ in range(5)]
    for l in range(depth):
        wl = w_in[l]
        w_main = jnp.concatenate([wl[:, col(0)], wl[:, col(1)], wl[:, col(4)], wl[:, col(5)], wl[:, col(6)],
                                  wl[:, col(8)], wl[:, col(9)]], axis=1).astype(BF16)
        w_small = jnp.concatenate([wl[:, col(2)], wl[:, col(3)], wl[:, col(7)],
                                   jnp.zeros((d, LANES - 2 * H_A - H_F), F32)], axis=1).astype(BF16)
        g_row = g_mix[l].reshape(1, d)
        a_row = _lane_row(a_log[l], H_A)
        dt_row = _lane_row(dt_bias[l], H_A)
        bf_row = _lane_row(b_forget[l], 2 * H_A)
        qg_pair = jnp.tile(qn_g[l], LANES // HD_F).reshape(1, LANES)
        kg_pair = jnp.tile(kn_g[l], LANES // HD_F).reshape(1, LANES)
        qg_full = jnp.tile(qn_g[l], H_F).reshape(1, wf)
        kg_full = jnp.tile(kn_g[l], H_F).reshape(1, wf)
        na_row = norm_a[l].reshape(1, DV_A)
        pa_b, pf_b, wo_b = p_a[l].astype(BF16), p_f[l].astype(BF16), w_o[l].astype(BF16)
        gffn_row = g_ffn[l].reshape(1, d)
        wr = jnp.pad(w_router[l], ((0, 0), (0, LANES - N_EXP))).astype(BF16)
        br_row = _lane_row(b_router[l], 0)
        merge_w = (na_row, pa_b, pf_b, wo_b, gffn_row, wr, br_row)

        qkv_p, z_p, qf_p, kf_p, vf_p, ga_p, gf_p, sm_p = _in_proj(xp.reshape(b * t, d), g_row, w_main, w_small, widths)
        r3 = lambda a: a.reshape(b, t, a.shape[-1])
        qkv3, sm3 = r3(qkv_p), r3(sm_p)
        conv_p = qkv3[:, t - (CONV_W - 1):, :]
        prep = _conv_prep(qkv3, conv_w[l], jnp.zeros((b, CONV_W - 1, qkv_w), F32))
        oa_p, s_p = _delta_scan(prep, sm3, a_row, dt_row, jnp.zeros((b, H_A, DK_A, DV_A), F32), lead)
        lf_p, c_col = _logf_cumsum(sm3, bf_row, t_pad)
        qx, kx, vb, kn_p = _fox_prep(r3(qf_p), r3(kf_p), r3(vf_p), c_col, qg_pair, kg_pair, t_pad)
        of_p = _fox_attn(qx, kx, vb)
        x1_p, h2_p, tw_p, ti_p = _merge(xp, oa_p, r3(z_p), of_p, r3(ga_p), r3(gf_p), *merge_w)

        qkv_s, z_s, qf_s, kf_s, vf_s, ga_s, gf_s, sm_s = _in_proj(xs.reshape(bd, d), g_row, w_main, w_small, widths)
        s3 = lambda a: a.reshape(bd, 1, a.shape[-1])
        oa_s, s_s, conv_s = _decode_delta(s3(qkv_s), state_conv[l], s3(sm_s), conv_w[l], a_row, dt_row, state_delta[l])
        n_pool, page = cache_k.shape[1], cache_k.shape[2]
        of_s, kn_s, lf_s = _decode_attn(page_table, s3(qf_s), s3(kf_s), s3(vf_s), s3(sm_s), qg_full, kg_full,
                                        _lane_row(b_forget[l], 2 * H_A),
                                        jnp.transpose(cache_k[l], (0, 2, 3, 1)).reshape(n_pool, wf, page),
                                        jnp.transpose(cache_v[l], (0, 2, 3, 1)).reshape(n_pool, wf, page),
                                        jnp.swapaxes(cache_logf[l], 1, 2))
        g1 = lambda a: a.reshape(1, bd, a.shape[-1])
        x1_s, h2_s, tw_s, ti_s = _merge(g1(xs), g1(oa_s), g1(z_s), g1(of_s), g1(ga_s), g1(gf_s), *merge_w)

        n_p = b * t
        h2_all = jnp.concatenate([h2_p.reshape(n_p, d), h2_s.reshape(bd, d)], axis=0)
        ti_all = jnp.concatenate([ti_p.reshape(n_p, LANES), ti_s.reshape(bd, LANES)], axis=0)[:, :TOP_K]
        yg = _moe(h2_all, ti_all, w_gate[l], b_gate[l][:, None, :],
                  w_up[l], b_up[l][:, None, :], w_down[l], b_down[l][:, None, :])
        xp = _combine(x1_p.reshape(n_p, d), yg, tw_p.reshape(n_p, LANES), 0).reshape(b, t, d)
        xs = _combine(x1_s.reshape(bd, d), yg, tw_s.reshape(bd, LANES), n_p).reshape(bd, 1, d)

        st_p = (kn_p.reshape(b, t, H_F, HD_F), vf_p.reshape(b, t, H_F, HD_F),
                lf_p[:, :, 2 * H_A:2 * H_A + H_F], s_p, conv_p)
        st_s = (kn_s.reshape(bd, 1, H_F, HD_F), vf_s.reshape(bd, 1, H_F, HD_F),
                lf_s[:, :, 2 * H_A:2 * H_A + H_F], s_s, conv_s)
        for lst, a in zip(new_p, st_p):
            lst.append(a)
        for lst, a in zip(new_s, st_s):
            lst.append(a)
    k_p, v_p, lf_pp, d_p, c_p = (jnp.stack(a) for a in new_p)
    k_s, v_s, lf_ss, d_s, c_s = (jnp.stack(a) for a in new_s)
    return (xp[:, N_META:], xs, k_p, v_p, lf_pp, k_s, v_s, lf_ss, d_p, d_s, c_p, c_s)
```

```python
import functools

import jax
import jax.numpy as jnp
from jax import lax
from jax.experimental import pallas as pl
from jax.experimental.pallas import tpu as pltpu

F32 = jnp.float32
BF16 = jnp.bfloat16

N_META = 16
H_A = 4
DK_A = 128
DV_A = 128
CONV_W = 4
CHUNK = 64
H_F = 8
HD_F = 64
N_EXP = 32
TOP_K = 4
SWIGLU_LIMIT = 7.0
SWIGLU_ALPHA = 1.702
EPS = 1e-6
LOG2E = 1.4426950408889634

LANES = 128
F_OFF = 2 * H_A
HP = 16
ATT_BLOCK = 256
MOE_BLOCK = 512
PAGES_PER_STEP = 32
VMEM_LIMIT = 48 * 1024 * 1024
BIG_VMEM_LIMIT = 58 * 1024 * 1024


def _cparams(*sem):
    return pltpu.CompilerParams(dimension_semantics=sem, vmem_limit_bytes=VMEM_LIMIT)


def _row_tile(n, cap):
    if n <= cap:
        return n
    best = None
    for t in range(8, cap + 1, 8):
        if n % t == 0:
            best = t
    assert best is not None, n
    return best


def _dot(a, b):
    return jnp.dot(a, b, preferred_element_type=F32)


def _dot_nt(a, b):
    return lax.dot_general(a, b, (((1,), (1,)), ((), ())), preferred_element_type=F32)


def _split2(x):
    hi = x.astype(BF16)
    lo = (x - hi.astype(F32)).astype(BF16)
    return hi, lo


def _dot_x3(a, b):
    a_hi, a_lo = _split2(a)
    b_hi, b_lo = _split2(b)
    return _dot(a_hi, b_hi) + (_dot(a_hi, b_lo) + _dot(a_lo, b_hi))


def _split3(x):
    hi = x.astype(BF16)
    r = x - hi.astype(F32)
    mid = r.astype(BF16)
    lo = (r - mid.astype(F32)).astype(BF16)
    return hi, mid, lo


def _sigmoid(x):
    return 1.0 / (1.0 + jnp.exp(-x))


def _softplus(x):
    return jnp.maximum(x, 0.0) + jnp.log1p(jnp.exp(-jnp.abs(x)))


def _log_sigmoid(x):
    return jnp.minimum(x, 0.0) - jnp.log1p(jnp.exp(-jnp.abs(x)))


COL_CHUNK = 512


def _in_proj_body(x_ref, g_ref, wm_ref, ws_ref, *out_refs, widths):
    x = x_ref[...]
    h = x * lax.rsqrt(jnp.mean(x * x, axis=-1, keepdims=True) + EPS) * g_ref[...]
    hb = h.astype(BF16)
    col = 0
    for o_ref, w in zip(out_refs[:-1], widths):
        for c0 in range(0, w, COL_CHUNK):
            o_ref[:, c0:c0 + COL_CHUNK] = _dot(hb, wm_ref[:, col + c0:col + c0 + COL_CHUNK])
        col += w
    out_refs[-1][...] = _dot(hb, ws_ref[...])


def _in_proj(x2d, g_row, w_main, w_small, widths):
    n, d = x2d.shape
    tm = _row_tile(n, 512)
    outs = [jax.ShapeDtypeStruct((n, w), F32) for w in widths] + [jax.ShapeDtypeStruct((n, LANES), F32)]
    out_specs = [pl.BlockSpec((tm, w), lambda i: (i, 0)) for w in widths] + [pl.BlockSpec((tm, LANES), lambda i: (i, 0))]
    return pl.pallas_call(
        functools.partial(_in_proj_body, widths=widths),
        grid=(n // tm,),
        in_specs=[pl.BlockSpec((tm, d), lambda i: (i, 0)),
                  pl.BlockSpec((1, d), lambda i: (0, 0)),
                  pl.BlockSpec(w_main.shape, lambda i: (0, 0), pipeline_mode=pl.Buffered(1)),
                  pl.BlockSpec(w_small.shape, lambda i: (0, 0), pipeline_mode=pl.Buffered(1))],
        out_specs=out_specs,
        out_shape=outs,
        compiler_params=_cparams("parallel"),
        name="in_proj",
    )(x2d, g_row, w_main, w_small)


def _conv_prep_body(x_ref, cw_ref, prev_ref, o_ref, buf, *, t_len, rows):
    j = pl.program_id(1)
    pad = 8
    buf[0:pad, :] = jnp.zeros((pad, LANES), F32)
    buf[pad - (CONV_W - 1):pad, :] = prev_ref[...]
    buf[pad:pad + t_len, :] = x_ref[...]
    cw = cw_ref[...]
    is_q = j < H_A
    is_qk = j < 2 * H_A
    for r0 in range(0, t_len, rows):
        y = jnp.zeros((rows, LANES), F32)
        for w in range(CONV_W):
            off = pad - (CONV_W - 1) + w + r0
            y = y + buf[off:off + rows, :] * cw[w:w + 1, :]
        y = y * _sigmoid(y)
        nrm = lax.rsqrt(jnp.sum(y * y, axis=-1, keepdims=True) + EPS)
        f = jnp.where(is_qk, nrm * jnp.where(is_q, DK_A ** -0.5, 1.0), 1.0)
        o_ref[r0:r0 + rows, :] = y * f


def _conv_prep(qkv3, conv_w, conv_prev):
    b, t, c = qkv3.shape
    rows = _row_tile(t, 512)
    return pl.pallas_call(
        functools.partial(_conv_prep_body, t_len=t, rows=rows),
        grid=(b, c // LANES),
        in_specs=[pl.BlockSpec((None, t, LANES), lambda i, j: (i, 0, j)),
                  pl.BlockSpec((CONV_W, LANES), lambda i, j: (0, j)),
                  pl.BlockSpec((None, CONV_W - 1, LANES), lambda i, j: (i, 0, j))],
        out_specs=pl.BlockSpec((None, t, LANES), lambda i, j: (i, 0, j)),
        out_shape=jax.ShapeDtypeStruct((b, t, c), F32),
        scratch_shapes=[pltpu.VMEM((8 + t, LANES), F32)],
        compiler_params=_cparams("parallel", "parallel"),
        name="conv_prep",
    )(qkv3, conv_w, conv_prev)


def _gates(sm, a_row, dt_row, h):
    lane = lax.broadcasted_iota(jnp.int32, sm.shape, 1)
    beta_all = _sigmoid(sm)
    g_all = -jnp.exp(a_row) * _softplus(sm + dt_row)
    beta = jnp.sum(jnp.where(lane == h, beta_all, 0.0), axis=-1, keepdims=True)
    g = jnp.sum(jnp.where(lane == H_A + h, g_all, 0.0), axis=-1, keepdims=True)
    return beta, g


def _decay_terms(gs):
    c = gs[0].shape[0]
    ri = lax.broadcasted_iota(jnp.int32, (c, c), 0)
    ci = lax.broadcasted_iota(jnp.int32, (c, c), 1)
    g_cum, decay = [], []
    for g in gs:
        g_row = jnp.sum(jnp.where(ri == ci, g, 0.0), axis=0, keepdims=True)
        gc = jnp.sum(jnp.where(ci <= ri, g_row, 0.0), axis=1, keepdims=True)
        gc_row = jnp.sum(jnp.where(ri <= ci, g, 0.0), axis=0, keepdims=True)
        g_cum.append(gc)
        decay.append(jnp.exp(jnp.where(ci <= ri, gc - gc_row, -jnp.inf)))
    return g_cum, decay


def _delta_step(cur, nxt, s_ref):
    o = inv_n = None
    stages = []
    if cur is not None:
        qs, ks, vs, betas, gs, inv = cur
        rng = range(len(qs))
        c = qs[0].shape[0]
        g_cum, decay = _decay_terms(gs)
        e_g = [jnp.exp(g_cum[i]) for i in rng]
        s = [s_ref[i] for i in rng]
        sb = [s[i].astype(BF16) for i in rng]
        qb = [qs[i].astype(BF16) for i in rng]
        kb = [ks[i].astype(BF16) for i in rng]
        k_s = [_dot(kb[i], sb[i]) for i in rng]
        val = {}

        def st_delta():
            rhs = [betas[i] * (vs[i] - e_g[i] * k_s[i]) for i in rng]
            val["db"] = [_dot_x3(inv[i], rhs[i]).astype(BF16) for i in rng]
            val["qk"] = [_dot_nt(qb[i], kb[i]) for i in rng]
            val["q_s"] = [_dot(qb[i], sb[i]) for i in rng]

        def st_out():
            val["o"] = [e_g[i] * val["q_s"][i] + _dot((val["qk"][i] * decay[i]).astype(BF16), val["db"][i]) for i in rng]

        def st_state():
            g_end = [g_cum[i][c - 1:c, :] for i in rng]
            k_dec = [(ks[i] * jnp.exp(g_end[i] - g_cum[i])).T.astype(BF16) for i in rng]
            for i in rng:
                s_ref[i] = jnp.exp(g_end[i]) * s[i] + _dot(k_dec[i], val["db"][i])

        stages = [st_delta, st_out, st_state]
    if nxt is not None:
        ks_n, betas_n, gs_n = nxt
        rng_n = range(len(ks_n))
        c = ks_n[0].shape[0]
        ri = lax.broadcasted_iota(jnp.int32, (c, c), 0)
        ci = lax.broadcasted_iota(jnp.int32, (c, c), 1)
        _, decay_n = _decay_terms(gs_n)
        kb_n = [ks_n[i].astype(BF16) for i in rng_n]
        kk = [_dot_nt(kb_n[i], kb_n[i]) for i in rng_n]
        a = [jnp.where(ci < ri, betas_n[i] * decay_n[i] * kk[i], 0.0) for i in rng_n]
        inv_n = [jnp.where(ri == ci, 1.0, 0.0) - a[i] for i in rng_n]
        pw = a
        for _ in range(max(1, (c - 1).bit_length() - 1)):
            if stages:
                stages.pop(0)()
            pw = [_dot_x3(pw[i], pw[i]) for i in rng_n]
            inv_n = [inv_n[i] + _dot_x3(inv_n[i], pw[i]) for i in rng_n]
    for st in stages:
        st()
    if cur is not None:
        o = val["o"]
    return o, inv_n


def _delta_scan_body(qkv_ref, sm_ref, a_ref, dt_ref, s0_ref, o_ref, s_out_ref, s_ref, *, t_len, lead):
    nb = qkv_ref.shape[0]
    pairs = [(bb, h) for bb in range(nb) for h in range(H_A)]
    a_row = a_ref[...]
    dt_row = dt_ref[...]
    s_ref[...] = s0_ref[...].reshape(nb * H_A, DK_A, DV_A)
    first = CHUNK - lead
    n_chunks = (lead + t_len) // CHUNK

    def gates(load, mask_lead):
        betas, gs = [], []
        for bb in range(nb):
            sm = load(sm_ref, bb, 0)
            for h in range(H_A):
                beta, g = _gates(sm, a_row, dt_row, h)
                if mask_lead:
                    row = lax.broadcasted_iota(jnp.int32, (CHUNK, 1), 0)
                    beta = jnp.where(row >= lead, beta, 0.0)
                    g = jnp.where(row >= lead, g, 0.0)
                betas.append(beta)
                gs.append(g)
        return betas, gs

    def nxt_args(load, mask_lead):
        betas, gs = gates(load, mask_lead)
        return [load(qkv_ref, bb, (H_A + h) * DK_A) for bb, h in pairs], betas, gs

    def cur_args(load, mask_lead, inv):
        betas, gs = gates(load, mask_lead)
        qs = [load(qkv_ref, bb, h * DK_A) for bb, h in pairs]
        ks = [load(qkv_ref, bb, (H_A + h) * DK_A) for bb, h in pairs]
        vs = [load(qkv_ref, bb, 2 * H_A * DK_A + h * DV_A) for bb, h in pairs]
        return qs, ks, vs, betas, gs, inv

    def chunk_rows(c):
        return pl.ds(pl.multiple_of(c * CHUNK - lead, 8), CHUNK)

    def loader(c):
        sl = chunk_rows(c)
        return lambda ref, bb, col: ref[bb, sl, col:col + LANES]

    if lead:
        def load0(ref, bb, col):
            return jnp.concatenate([jnp.zeros((lead, LANES), F32), ref[bb, 0:first, col:col + LANES]], axis=0)

        _, inv0 = _delta_step(None, nxt_args(load0, True), s_ref)
        has_next = n_chunks > 1
        outs, inv1 = _delta_step(cur_args(load0, True, inv0), nxt_args(loader(1), False) if has_next else None, s_ref)
        for (bb, h), o in zip(pairs, outs):
            o_ref[bb, 0:first, h * DV_A:(h + 1) * DV_A] = o[lead:, :]
        c_start = 1
    else:
        has_next = n_chunks > 0
        _, inv1 = _delta_step(None, nxt_args(loader(0), False), s_ref) if has_next else (None, None)
        c_start = 0

    def body(c, inv):
        sl = chunk_rows(c)
        outs, inv_n = _delta_step(cur_args(loader(c), False, inv),
                                  nxt_args(loader(jnp.minimum(c + 1, n_chunks - 1)), False), s_ref)
        for (bb, h), o in zip(pairs, outs):
            o_ref[bb, sl, h * DV_A:(h + 1) * DV_A] = o
        return tuple(inv_n)

    if has_next:
        lax.fori_loop(c_start, n_chunks, body, tuple(inv1))
    s_out_ref[...] = s_ref[...].reshape(nb, H_A, DK_A, DV_A)


def _delta_scan(qkv3, small3, a_row, dt_row, s0, lead):
    b, t, c = qkv3.shape
    assert (lead + t) % CHUNK == 0 and lead % 8 == 0
    nb = 2 if b % 2 == 0 else 1
    return pl.pallas_call(
        functools.partial(_delta_scan_body, t_len=t, lead=lead),
        grid=(b // nb,),
        in_specs=[pl.BlockSpec((nb, t, c), lambda i: (i, 0, 0), pipeline_mode=pl.Buffered(1)),
                  pl.BlockSpec((nb, t, LANES), lambda i: (i, 0, 0)),
                  pl.BlockSpec((1, LANES), lambda i: (0, 0)),
                  pl.BlockSpec((1, LANES), lambda i: (0, 0)),
                  pl.BlockSpec((nb, H_A, DK_A, DV_A), lambda i: (i, 0, 0, 0))],
        out_specs=[pl.BlockSpec((nb, t, H_A * DV_A), lambda i: (i, 0, 0)),
                   pl.BlockSpec((nb, H_A, DK_A, DV_A), lambda i: (i, 0, 0, 0))],
        out_shape=[jax.ShapeDtypeStruct((b, t, H_A * DV_A), F32),
                   jax.ShapeDtypeStruct((b, H_A, DK_A, DV_A), F32)],
        scratch_shapes=[pltpu.VMEM((nb * H_A, DK_A, DV_A), F32)],
        compiler_params=pltpu.CompilerParams(dimension_semantics=("parallel",), vmem_limit_bytes=BIG_VMEM_LIMIT),
        name="delta_scan",
    )(qkv3, small3, a_row, dt_row, s0)


def _group_rms(x, gain, seg):
    hi, lo = _split2(x * x)
    ss = _dot(hi, seg) + _dot(lo, seg)
    return x * lax.rsqrt(ss * (1.0 / HD_F) + EPS) * gain


def _seg_matrix(n):
    r = lax.broadcasted_iota(jnp.int32, (n, n), 0) // HD_F
    c = lax.broadcasted_iota(jnp.int32, (n, n), 1) // HD_F
    return jnp.where(r == c, 1.0, 0.0).astype(BF16)


def _fox_prep_body(q_ref, k_ref, v_ref, c_ref, qg_ref, kg_ref, qx_ref, kx_ref, vt_ref, kn_ref, *, t_len, t_pad):
    p = pl.program_id(1)
    heads = LANES // HD_F

    def feature_major(x):
        if t_pad > t_len:
            x = jnp.concatenate([x, jnp.zeros((t_pad - t_len, LANES), F32)], axis=0)
        return x.T.astype(BF16)

    seg = _seg_matrix(LANES)
    qn = _group_rms(q_ref[...], qg_ref[...], seg) * (HD_F ** -0.5 * LOG2E)
    kn = _group_rms(k_ref[...], kg_ref[...], seg)
    kn_ref[...] = kn
    c_all = c_ref[...] * LOG2E
    lane = lax.broadcasted_iota(jnp.int32, (t_len, LANES), 1)
    for hh in range(heads):
        c = jnp.sum(jnp.where(lane == F_OFF + p * heads + hh, c_all, 0.0), axis=-1, keepdims=True)
        pieces = [x.astype(F32) for x in _split3(c)]
        own = (lane // HD_F) == hh
        f0 = ((hh + 1) % heads) * HD_F
        qx = jnp.where(own, qn, 0.0)
        kx = jnp.where(own, kn, 0.0)
        for n, piece in enumerate(pieces):
            qx = jnp.where(lane == f0 + n, piece, qx)
            kx = jnp.where(lane == f0 + n, 1.0, kx)
            qx = jnp.where(lane == f0 + 3 + n, 1.0, qx)
            kx = jnp.where(lane == f0 + 3 + n, -piece, kx)
        kx_ref[0:t_len, hh * LANES:(hh + 1) * LANES] = kx.astype(BF16)
        qx_ref[hh * LANES:(hh + 1) * LANES, :] = feature_major(qx)
    vt_ref[...] = feature_major(v_ref[...])
    if t_pad > t_len:
        kx_ref[t_len:, :] = jnp.zeros((t_pad - t_len, heads * LANES), BF16)


def _fox_prep(q3, k3, v3, c_col, qg_row, kg_row, t_pad):
    b, t, w = q3.shape
    heads = LANES // HD_F
    blk = pl.BlockSpec((None, t, LANES), lambda i, p: (i, 0, p))
    gblk = pl.BlockSpec((1, LANES), lambda i, p: (0, 0))
    return pl.pallas_call(
        functools.partial(_fox_prep_body, t_len=t, t_pad=t_pad),
        grid=(b, w // LANES),
        in_specs=[blk, blk, blk, pl.BlockSpec((None, t, LANES), lambda i, p: (i, 0, 0)), gblk, gblk],
        out_specs=[pl.BlockSpec((None, heads * LANES, t_pad), lambda i, p: (i, p, 0)),
                   pl.BlockSpec((None, t_pad, heads * LANES), lambda i, p: (i, 0, p)),
                   pl.BlockSpec((None, LANES, t_pad), lambda i, p: (i, p, 0)), blk],
        out_shape=[jax.ShapeDtypeStruct((b, H_F * LANES, t_pad), BF16), jax.ShapeDtypeStruct((b, t_pad, H_F * LANES), BF16),
                   jax.ShapeDtypeStruct((b, w, t_pad), BF16), jax.ShapeDtypeStruct((b, t, w), F32)],
        compiler_params=_cparams("parallel", "parallel"),
        name="fox_prep",
    )(q3, k3, v3, c_col, qg_row, kg_row)


def _logf_cumsum_body(sm_ref, bf_ref, lf_ref, ccol_ref, buf, *, t_len, t_pad):
    lf = _log_sigmoid(sm_ref[...] + bf_ref[...])
    lf_ref[...] = lf
    buf[0:t_len, :] = lf
    if t_pad > t_len:
        buf[t_len:, :] = jnp.zeros((t_pad - t_len, LANES), F32)
    blk = ATT_BLOCK
    ri = lax.broadcasted_iota(jnp.int32, (blk, blk), 0)
    ci = lax.broadcasted_iota(jnp.int32, (blk, blk), 1)
    tri = jnp.where(ci <= ri, 1.0, 0.0).astype(BF16)
    carry = jnp.zeros((1, LANES), F32)
    for i in range(t_pad // blk):
        hi, mid, lo = _split3(buf[i * blk:(i + 1) * blk, :])
        c = _dot(tri, hi) + _dot(tri, mid) + _dot(tri, lo) + carry
        ccol_ref[i * blk:(i + 1) * blk, :] = c
        carry = c[blk - 1:blk, :]


def _logf_cumsum(small3, bf_row, t_pad):
    b, t, _ = small3.shape
    return pl.pallas_call(
        functools.partial(_logf_cumsum_body, t_len=t, t_pad=t_pad),
        grid=(b,),
        in_specs=[pl.BlockSpec((None, t, LANES), lambda i: (i, 0, 0)),
                  pl.BlockSpec((1, LANES), lambda i: (0, 0))],
        out_specs=[pl.BlockSpec((None, t, LANES), lambda i: (i, 0, 0)),
                   pl.BlockSpec((None, t_pad, LANES), lambda i: (i, 0, 0))],
        out_shape=[jax.ShapeDtypeStruct((b, t, LANES), F32),
                   jax.ShapeDtypeStruct((b, t_pad, LANES), F32)],
        scratch_shapes=[pltpu.VMEM((t_pad, LANES), F32)],
        compiler_params=_cparams("parallel"),
        name="logf_cumsum",
    )(small3, bf_row)


def _fox_attn_body(qt_ref, k_ref, vt_ref, o_ref):
    i = pl.program_id(1)
    blk = ATT_BLOCK
    key_le_query = (lax.broadcasted_iota(jnp.int32, (blk, blk), 0)
                    <= lax.broadcasted_iota(jnp.int32, (blk, blk), 1))

    def step(j, carry, diagonal):
        ms, ls, accs = carry
        k0 = pl.multiple_of(j * blk, blk)
        st = []
        for h in range(H_F):
            s = _dot(k_ref[pl.ds(k0, blk), h * LANES:(h + 1) * LANES], qt_ref[h * LANES:(h + 1) * LANES, :])
            st.append(jnp.where(key_le_query, s, -jnp.inf) if diagonal else s)
        new_ms, new_ls, alphas, pts = [], [], [], []
        for h in range(H_F):
            m_new = jnp.maximum(ms[h], jnp.max(st[h], axis=0, keepdims=True))
            alpha = jnp.exp2(ms[h] - m_new)
            pt = jnp.exp2(st[h] - m_new)
            new_ls.append(alpha * ls[h] + jnp.sum(pt, axis=0, keepdims=True))
            new_ms.append(m_new)
            alphas.append(alpha)
            pts.append(pt.astype(BF16))
        new_accs = [alphas[h] * accs[h] + _dot(vt_ref[h * HD_F:(h + 1) * HD_F, pl.ds(k0, blk)], pts[h])
                    for h in range(H_F)]
        return tuple(new_ms), tuple(new_ls), tuple(new_accs)

    init = (tuple(jnp.full((1, blk), -jnp.inf, F32) for _ in range(H_F)),
            tuple(jnp.zeros((1, blk), F32) for _ in range(H_F)),
            tuple(jnp.zeros((HD_F, blk), F32) for _ in range(H_F)))
    carry = lax.fori_loop(0, i, lambda j, c: step(j, c, False), init)
    ms, ls, accs = step(i, carry, True)
    heads = LANES // HD_F
    for p in range(H_F // heads):
        ot = jnp.concatenate([accs[p * heads + hh] * (1.0 / ls[p * heads + hh]) for hh in range(heads)], axis=0)
        o_ref[:, p * LANES:(p + 1) * LANES] = ot.T.astype(o_ref.dtype)


def _fox_attn(qt, kx, vt):
    b, w, t_pad = vt.shape
    blk = ATT_BLOCK
    return pl.pallas_call(
        _fox_attn_body,
        grid=(b, t_pad // blk),
        in_specs=[pl.BlockSpec((None, H_F * LANES, blk), lambda n, i: (n, 0, i)),
                  pl.BlockSpec((None, t_pad, H_F * LANES), lambda n, i: (n, 0, 0)),
                  pl.BlockSpec((None, w, t_pad), lambda n, i: (n, 0, 0))],
        out_specs=pl.BlockSpec((None, blk, w), lambda n, i: (n, i, 0)),
        out_shape=jax.ShapeDtypeStruct((b, t_pad, w), BF16),
        compiler_params=_cparams("parallel", "arbitrary"),
        name="fox_attn",
    )(qt, kx, vt)


def _decode_delta_body(x_ref, sc_ref, sm_ref, cw_ref, a_ref, dt_ref, s_ref, o_ref, s_out_ref, conv_ref):
    x_new = x_ref[...]
    sc = sc_ref[...]
    cw = cw_ref[...]
    y = x_new * cw[CONV_W - 1:CONV_W, :]
    for w in range(CONV_W - 1):
        y = y + sc[w:w + 1, :] * cw[w:w + 1, :]
    y = y * _sigmoid(y)
    conv_ref[0:CONV_W - 2, :] = sc[1:, :]
    conv_ref[CONV_W - 2:CONV_W - 1, :] = x_new
    sm = sm_ref[...]
    lane = lax.broadcasted_iota(jnp.int32, (1, LANES), 1)
    beta_all = _sigmoid(sm)
    g_all = -jnp.exp(a_ref[...]) * _softplus(sm + dt_ref[...])
    qk_w = H_A * DK_A
    row0 = lax.broadcasted_iota(jnp.int32, (LANES, LANES), 0) == 0

    def in_row0(r):
        return jnp.where(row0, jnp.broadcast_to(r, (LANES, LANES)), 0.0)

    for h in range(H_A):
        q = y[:, h * DK_A:(h + 1) * DK_A]
        k = y[:, qk_w + h * DK_A:qk_w + (h + 1) * DK_A]
        v = y[:, 2 * qk_w + h * DV_A:2 * qk_w + (h + 1) * DV_A]
        q = q * lax.rsqrt(jnp.sum(q * q, axis=-1, keepdims=True) + EPS) * (DK_A ** -0.5)
        k = k * lax.rsqrt(jnp.sum(k * k, axis=-1, keepdims=True) + EPS)
        beta = jnp.sum(jnp.where(lane == h, beta_all, 0.0), axis=-1, keepdims=True)
        g = jnp.sum(jnp.where(lane == H_A + h, g_all, 0.0), axis=-1, keepdims=True)
        e_g = jnp.exp(g)
        s = s_ref[h]
        sb = s.astype(BF16)
        k_sq = in_row0(k)
        k_s = _dot(k_sq.astype(BF16), sb)[0:1, :]
        q_s = _dot(in_row0(q).astype(BF16), sb)[0:1, :]
        delta = beta * (v - e_g * k_s)
        qk = jnp.sum(q * k, axis=-1, keepdims=True)
        o_ref[:, h * DV_A:(h + 1) * DV_A] = e_g * q_s + qk * delta
        s_out_ref[h] = e_g * s + _dot_x3(k_sq.T, in_row0(delta))


def _decode_delta(qkv3, state_conv, small3, conv_w, a_row, dt_row, s0):
    bd, _, c = qkv3.shape
    return pl.pallas_call(
        _decode_delta_body,
        grid=(bd,),
        in_specs=[pl.BlockSpec((None, 1, c), lambda i: (i, 0, 0)),
                  pl.BlockSpec((None, CONV_W - 1, c), lambda i: (i, 0, 0)),
                  pl.BlockSpec((None, 1, LANES), lambda i: (i, 0, 0)),
                  pl.BlockSpec((CONV_W, c), lambda i: (0, 0)),
                  pl.BlockSpec((1, LANES), lambda i: (0, 0)),
                  pl.BlockSpec((1, LANES), lambda i: (0, 0)),
                  pl.BlockSpec((None, H_A, DK_A, DV_A), lambda i: (i, 0, 0, 0))],
        out_specs=[pl.BlockSpec((None, 1, H_A * DV_A), lambda i: (i, 0, 0)),
                   pl.BlockSpec((None, H_A, DK_A, DV_A), lambda i: (i, 0, 0, 0)),
                   pl.BlockSpec((None, CONV_W - 1, c), lambda i: (i, 0, 0))],
        out_shape=[jax.ShapeDtypeStruct((bd, 1, H_A * DV_A), F32),
                   jax.ShapeDtypeStruct((bd, H_A, DK_A, DV_A), F32),
                   jax.ShapeDtypeStruct((bd, CONV_W - 1, c), F32)],
        compiler_params=_cparams("parallel"),
        name="decode_delta",
    )(qkv3, state_conv, small3, conv_w, a_row, dt_row, s0)


def _decode_attn_body(pt_ref, q_ref, k_ref, v_ref, sm_ref, qg_ref, kg_ref, bf_ref, *rest, n_pg):
    page_refs = rest[:3 * n_pg]
    o_ref, kn_ref, lf_ref, qrows, vnew, snew, m_ref, l_ref, acc_ref, carry_ref, s_scr = rest[3 * n_pg:]
    step = pl.program_id(1)
    n_steps = pl.num_programs(1) // 2
    w = H_F * HD_F
    row = lax.broadcasted_iota(jnp.int32, (HP, w), 0)
    head_of_lane = lax.broadcasted_iota(jnp.int32, (HP, w), 1) // HD_F

    @pl.when(step == 0)
    def _():
        seg = _seg_matrix(w)

        def rms_rows(x_row, gain):
            xr = jnp.broadcast_to(x_row, (HP, w))
            hi, lo = _split2(xr * xr)
            ss = _dot(hi, seg) + _dot(lo, seg)
            return xr * lax.rsqrt(ss * (1.0 / HD_F) + EPS) * gain

        qn = rms_rows(q_ref[...], qg_ref[...]) * (HD_F ** -0.5)
        kn = rms_rows(k_ref[...], kg_ref[...])
        kn_ref[...] = kn[0:1, :]
        lf = _log_sigmoid(sm_ref[...] + bf_ref[...])
        lf_ref[...] = lf
        q_m = jnp.where(head_of_lane == row, qn, 0.0).astype(BF16)
        qrows[...] = q_m
        vnew[...] = jnp.broadcast_to(v_ref[...], (HP, w)).astype(BF16)
        s_new = jnp.sum(q_m.astype(F32) * kn.astype(BF16).astype(F32), axis=-1, keepdims=True)
        snew[...] = s_new
        m_ref[...] = s_new
        rr = lax.broadcasted_iota(jnp.int32, (HP, LANES), 0)
        ll = lax.broadcasted_iota(jnp.int32, (HP, LANES), 1)
        carry_ref[...] = jnp.sum(jnp.where(ll == rr + F_OFF, jnp.broadcast_to(lf, (HP, LANES)), 0.0),
                                 axis=-1, keepdims=True)

    @pl.when(step < n_steps)
    def _():
        qb = qrows[...]
        lf_all = jnp.concatenate(
            [jnp.concatenate([page_refs[3 * g + 2][...], jnp.zeros((HP - H_F, LANES), F32)], axis=0)
             for g in range(n_pg)], axis=0)
        later = (lax.broadcasted_iota(jnp.int32, (LANES, LANES), 0)
                 > lax.broadcasted_iota(jnp.int32, (LANES, LANES), 1)).astype(BF16)
        hi, mid, lo = _split3(lf_all)
        after = _dot(hi, later) + (_dot(mid, later) + _dot(lo, later))
        totals = jnp.sum(lf_all, axis=-1, keepdims=True)
        raw = [_dot(qb, page_refs[3 * g][...].astype(BF16)) for g in range(n_pg)]
        carry = carry_ref[...]
        m_new = m_ref[...]
        for g in range(n_pg):
            rows = slice(g * HP, (g + 1) * HP)
            s = raw[g] + (carry + after[rows, :])
            s_scr[step * n_pg + g] = s
            m_new = jnp.maximum(m_new, jnp.max(s, axis=-1, keepdims=True))
            carry = carry + totals[rows, :]
        carry_ref[...] = carry
        m_ref[...] = m_new

    @pl.when(step == n_steps)
    def _():
        m = m_ref[...]
        p_new = jnp.exp(snew[...] - m)

        def add(i, tot):
            return tot + jnp.exp(s_scr[i] - m)

        tot = lax.fori_loop(0, n_steps * n_pg, add, jnp.zeros((HP, LANES), F32), unroll=8)
        l = p_new + jnp.sum(tot, axis=-1, keepdims=True)
        l_ref[...] = l
        acc_ref[...] = (p_new / l).astype(BF16).astype(F32) * vnew[...].astype(F32)

    @pl.when(step >= n_steps)
    def _():
        m = m_ref[...]
        l = l_ref[...]
        probs = [(jnp.exp(s_scr[(step - n_steps) * n_pg + g] - m) / l).astype(BF16) for g in range(n_pg)]
        parts = [_dot_nt(probs[g], page_refs[3 * g + 1][...].astype(BF16)) for g in range(n_pg)]
        acc = acc_ref[...]
        for part in parts:
            acc = acc + part
        acc_ref[...] = acc

    @pl.when(step == pl.num_programs(1) - 1)
    def _():
        o_ref[...] = jnp.sum(jnp.where(head_of_lane == row, acc_ref[...], 0.0), axis=0, keepdims=True)


def _decode_attn(page_table, q3, k3, v3, small3, qg_row, kg_row, bf_row, ck, cv, clf_t):
    bd, _, w = q3.shape
    n_pages = page_table.shape[1]
    page = ck.shape[2]
    assert page == LANES and n_pages % PAGES_PER_STEP == 0 and LANES // HD_F == 2
    n_pg = PAGES_PER_STEP
    steps = n_pages // n_pg

    def page_of(i, s, g, pt):
        return pt[i, n_pages - 1 - (s * n_pg + g)]

    def key_idx(g):
        return lambda i, s, pt: (page_of(i, jnp.minimum(s, steps - 1), g, pt), 0, 0)

    def val_idx(g):
        return lambda i, s, pt: (page_of(i, jnp.maximum(s - steps, 0), g, pt), 0, 0)

    row_spec = lambda width: pl.BlockSpec((None, 1, width), lambda i, s, pt: (i, 0, 0))
    const_spec = lambda width: pl.BlockSpec((1, width), lambda i, s, pt: (0, 0))
    in_specs = [row_spec(w), row_spec(w), row_spec(w), row_spec(LANES), const_spec(w), const_spec(w), const_spec(LANES)]
    args = [q3, k3, v3, small3, qg_row, kg_row, bf_row]
    for g in range(n_pg):
        in_specs += [pl.BlockSpec((None, w, page), key_idx(g)),
                     pl.BlockSpec((None, w, page), val_idx(g)),
                     pl.BlockSpec((None, H_F, page), key_idx(g))]
        args += [ck, cv, clf_t]
    col = lambda: pltpu.VMEM((HP, 1), F32)
    grid_spec = pltpu.PrefetchScalarGridSpec(
        num_scalar_prefetch=1,
        grid=(bd, 2 * steps),
        in_specs=in_specs,
        out_specs=[row_spec(w), row_spec(w), row_spec(LANES)],
        scratch_shapes=[pltpu.VMEM((HP, w), BF16), pltpu.VMEM((HP, w), BF16), col(), col(), col(),
                        pltpu.VMEM((HP, w), F32), col(), pltpu.VMEM((n_pages, HP, LANES), F32)],
    )
    return pl.pallas_call(
        functools.partial(_decode_attn_body, n_pg=n_pg),
        grid_spec=grid_spec,
        out_shape=[jax.ShapeDtypeStruct((bd, 1, w), F32), jax.ShapeDtypeStruct((bd, 1, w), F32),
                   jax.ShapeDtypeStruct((bd, 1, LANES), F32)],
        compiler_params=_cparams("parallel", "arbitrary"),
        name="decode_attn",
    )(page_table, *args)


def _merge_body(x_ref, oa_ref, z_ref, of_ref, ga_ref, gf_ref, na_ref, pa_ref, pf_ref, wo_ref, gffn_ref,
                wr_ref, br_ref, x1_ref, h2_ref, tw_ref, ti_ref):
    o = oa_ref[...]
    z = z_ref[...]
    na = na_ref[...]
    parts = []
    for h in range(H_A):
        oh = o[:, h * DV_A:(h + 1) * DV_A]
        zh = z[:, h * DV_A:(h + 1) * DV_A]
        on = oh * lax.rsqrt(jnp.mean(oh * oh, axis=-1, keepdims=True) + EPS) * na
        parts.append((on * (zh * _sigmoid(zh))).astype(BF16))
    o_a = jnp.concatenate(parts, axis=-1)
    ya = _dot(o_a, pa_ref[...])
    yf = _dot(of_ref[...].astype(BF16), pf_ref[...])
    mixed = _sigmoid(ga_ref[...]) * ya + _sigmoid(gf_ref[...]) * yf
    x1 = x_ref[...] + _dot(mixed.astype(BF16), wo_ref[...])
    x1_ref[...] = x1
    h2 = x1 * lax.rsqrt(jnp.mean(x1 * x1, axis=-1, keepdims=True) + EPS) * gffn_ref[...]
    h2_ref[...] = h2
    logits = _dot(h2.astype(BF16), wr_ref[...]) + br_ref[...]
    lane = lax.broadcasted_iota(jnp.int32, logits.shape, 1)
    l = jnp.where(lane < N_EXP, logits, -jnp.inf)
    vals, idxs = [], []
    for _ in range(TOP_K):
        m = jnp.max(l, axis=-1, keepdims=True)
        idx = jnp.min(jnp.where(l == m, lane, LANES), axis=-1, keepdims=True)
        vals.append(m)
        idxs.append(idx)
        l = jnp.where(lane == idx, -jnp.inf, l)
    es = [jnp.exp(v - vals[0]) for v in vals]
    den = es[0]
    for e in es[1:]:
        den = den + e
    tw = jnp.zeros(logits.shape, F32)
    ti = jnp.zeros(logits.shape, jnp.int32)
    for kk in range(TOP_K):
        tw = jnp.where(lane == kk, es[kk] / den, tw)
        ti = jnp.where(lane == kk, idxs[kk], ti)
    tw_ref[...] = tw
    ti_ref[...] = ti


def _merge(x3, oa3, z3, of3, ga3, gf3, na_row, pa, pf, wo, gffn_row, wr, br_row):
    g, t, d = x3.shape
    tm = _row_tile(t, 384)
    tok = lambda c: pl.BlockSpec((None, tm, c), lambda i, j: (i, j, 0))
    const = lambda a: pl.BlockSpec(a.shape, lambda i, j: (0,) * a.ndim, pipeline_mode=pl.Buffered(1))
    wa = H_A * DV_A
    wf = H_F * HD_F
    return pl.pallas_call(
        _merge_body,
        grid=(g, t // tm),
        in_specs=[tok(d), tok(wa), tok(wa), tok(wf), tok(d), tok(d),
                  const(na_row), const(pa), const(pf), const(wo), const(gffn_row),
                  const(wr), const(br_row)],
        out_specs=[tok(d), tok(d), tok(LANES), tok(LANES)],
        out_shape=[jax.ShapeDtypeStruct((g, t, d), F32), jax.ShapeDtypeStruct((g, t, d), F32),
                   jax.ShapeDtypeStruct((g, t, LANES), F32), jax.ShapeDtypeStruct((g, t, LANES), jnp.int32)],
        compiler_params=_cparams("parallel", "parallel"),
        name="merge",
    )(x3, oa3, z3, of3, ga3, gf3, na_row, pa, pf, wo, gffn_row, wr, br_row)


def _moe_body(be_ref, nv_ref, x_ref, wg_ref, bg_ref, wu_ref, bu_ref, wd_ref, bd_ref, y_ref, wg_b, wu_b, wd_b):
    i = pl.program_id(0)

    @pl.when(jnp.logical_or(i == 0, be_ref[i] != be_ref[jnp.maximum(i - 1, 0)]))
    def _():
        wg_b[...] = wg_ref[...].astype(BF16)
        wu_b[...] = wu_ref[...].astype(BF16)
        wd_b[...] = wd_ref[...].astype(BF16)

    @pl.when(i < nv_ref[0])
    def _():
        x = x_ref[...].astype(BF16)
        gate = jnp.minimum(_dot(x, wg_b[...]) + bg_ref[...], SWIGLU_LIMIT)
        up = jnp.clip(_dot(x, wu_b[...]) + bu_ref[...], -SWIGLU_LIMIT, SWIGLU_LIMIT)
        act = (up + 1.0) * gate * _sigmoid(SWIGLU_ALPHA * gate)
        y_ref[...] = _dot(act.astype(BF16), wd_b[...]) + bd_ref[...]

    @pl.when(i >= nv_ref[0])
    def _():
        y_ref[...] = jnp.zeros(y_ref.shape, y_ref.dtype)


def _moe_pipe_body(be_ref, nv_ref, xs_hbm, wg_hbm, bg_hbm, wu_hbm, bu_hbm, wd_hbm, bd_hbm, y_hbm,
                   wg_b, wu_b, wd_b, step_ref, *, bm):
    n_blk = xs_hbm.shape[0] // bm
    d = xs_hbm.shape[1]
    de = wg_hbm.shape[-1]
    step_ref[0] = 0

    def block(x_ref, wg_ref, bg_ref, wu_ref, bu_ref, wd_ref, bd_ref, y_ref):
        i = step_ref[0]

        @pl.when(jnp.logical_or(i == 0, be_ref[i] != be_ref[jnp.maximum(i - 1, 0)]))
        def _():
            wg_b[...] = wg_ref[0].astype(BF16)
            wu_b[...] = wu_ref[0].astype(BF16)
            wd_b[...] = wd_ref[0].astype(BF16)

        @pl.when(i < nv_ref[0])
        def _():
            x = x_ref[...].astype(BF16)
            gate = jnp.minimum(_dot(x, wg_b[...]) + bg_ref[0], SWIGLU_LIMIT)
            up = jnp.clip(_dot(x, wu_b[...]) + bu_ref[0], -SWIGLU_LIMIT, SWIGLU_LIMIT)
            act = (up + 1.0) * gate * _sigmoid(SWIGLU_ALPHA * gate)
            y_ref[...] = _dot(act.astype(BF16), wd_b[...]) + bd_ref[0]

        @pl.when(i >= nv_ref[0])
        def _():
            y_ref[...] = jnp.zeros(y_ref.shape, y_ref.dtype)

        step_ref[0] = i + 1

    ahead = pl.Buffered(2, use_lookahead=True)
    wspec = lambda a, b: pl.BlockSpec((1, a, b), lambda i: (be_ref[i], 0, 0), pipeline_mode=ahead)
    bspec = lambda b: pl.BlockSpec((1, 1, b), lambda i: (be_ref[i], 0, 0))
    pltpu.emit_pipeline(
        block,
        grid=(n_blk,),
        in_specs=[pl.BlockSpec((bm, d), lambda i: (jnp.minimum(i, nv_ref[0] - 1), 0)),
                  wspec(d, de), bspec(de), wspec(d, de), bspec(de), wspec(de, d), bspec(d)],
        out_specs=[pl.BlockSpec((bm, d), lambda i: (i, 0))],
    )(xs_hbm, wg_hbm, bg_hbm, wu_hbm, bu_hbm, wd_hbm, bd_hbm, y_hbm)


def _moe_experts_pipelined(blk_e, n_valid, xs, wg, bg, wu, bu, wd, bd):
    n_rows, d = xs.shape
    de = wg.shape[-1]
    any_spec = pl.BlockSpec(memory_space=pl.ANY)
    grid_spec = pltpu.PrefetchScalarGridSpec(
        num_scalar_prefetch=2,
        grid=(1,),
        in_specs=[any_spec] * 7,
        out_specs=any_spec,
        scratch_shapes=[pltpu.VMEM((d, de), BF16), pltpu.VMEM((d, de), BF16), pltpu.VMEM((de, d), BF16),
                        pltpu.SMEM((1,), jnp.int32)],
    )
    return pl.pallas_call(
        functools.partial(_moe_pipe_body, bm=MOE_BLOCK),
        grid_spec=grid_spec,
        out_shape=jax.ShapeDtypeStruct((n_rows, d), F32),
        compiler_params=pltpu.CompilerParams(dimension_semantics=("arbitrary",), vmem_limit_bytes=BIG_VMEM_LIMIT),
        name="moe_experts",
    )(blk_e, n_valid, xs, wg, bg, wu, bu, wd, bd)


def _moe_experts(blk_e, n_valid, xs, wg, bg, wu, bu, wd, bd):
    n_rows, d = xs.shape
    bm = MOE_BLOCK
    n_blk = n_rows // bm
    de = wg.shape[-1]
    row_idx = lambda i, be, nv: (jnp.minimum(i, nv[0] - 1), 0)
    bspec = lambda b: pl.BlockSpec((None, 1, b), lambda i, be, nv: (be[i], 0, 0))
    wspec = lambda a, b: pl.BlockSpec((None, a, b), lambda i, be, nv: (be[i], 0, 0))
    grid_spec = pltpu.PrefetchScalarGridSpec(
        num_scalar_prefetch=2,
        grid=(n_blk,),
        in_specs=[pl.BlockSpec((bm, d), row_idx),
                  wspec(d, de), bspec(de), wspec(d, de), bspec(de), wspec(de, d), bspec(d)],
        out_specs=pl.BlockSpec((bm, d), lambda i, be, nv: (i, 0)),
        scratch_shapes=[pltpu.VMEM((d, de), BF16), pltpu.VMEM((d, de), BF16), pltpu.VMEM((de, d), BF16)],
    )
    return pl.pallas_call(
        _moe_body,
        grid_spec=grid_spec,
        out_shape=jax.ShapeDtypeStruct((n_rows, d), F32),
        compiler_params=pltpu.CompilerParams(dimension_semantics=("arbitrary",), vmem_limit_bytes=BIG_VMEM_LIMIT),
        name="moe_experts",
    )(blk_e, n_valid, xs, wg, bg, wu, bu, wd, bd)


def _moe(h2, top_i, wg, bg, wu, bu, wd, bd):
    n, d = h2.shape
    bm = MOE_BLOCK
    m = n * TOP_K
    n_blk = -(-(m + N_EXP * (bm - 1)) // bm)
    flat_e = top_i.reshape(-1)
    onehot = (flat_e[:, None] == jnp.arange(N_EXP, dtype=jnp.int32)[None, :]).astype(jnp.int32)
    csum = jnp.cumsum(onehot, axis=0)
    rank = jnp.take_along_axis(csum, flat_e[:, None], axis=1)[:, 0] - 1
    counts = csum[-1]
    padded = (counts + bm - 1) // bm * bm
    pend = jnp.cumsum(padded)
    pstart = pend - padded
    dest = pstart[flat_e] + rank
    n_valid = (pend[-1] // bm).astype(jnp.int32)
    blk_ids = jnp.arange(n_blk, dtype=jnp.int32)
    expert_of = lambda blk: jnp.minimum(jnp.sum((pend[None, :] <= (blk * bm)[:, None]).astype(jnp.int32), axis=1), N_EXP - 1)
    blk_e = expert_of(jnp.minimum(blk_ids, n_valid - 1))
    shift = (m - 1).bit_length()
    assert N_EXP << shift < 2 ** 31
    pair_sorted = jnp.sort((flat_e << shift) | jnp.arange(m, dtype=jnp.int32)) & ((1 << shift) - 1)
    e_of_blk = expert_of(blk_ids)
    start = jnp.cumsum(counts) - counts
    slot = (blk_ids * bm - pstart[e_of_blk])[:, None] + jnp.arange(bm, dtype=jnp.int32)[None, :]
    src = jnp.clip(start[e_of_blk][:, None] + slot, 0, m - 1)
    row_tok = jnp.where(slot < counts[e_of_blk][:, None], pair_sorted[src.reshape(-1)].reshape(n_blk, bm) // TOP_K, n)
    row_tok = row_tok.reshape(-1)
    h_ext = jnp.concatenate([h2, jnp.zeros((1, d), h2.dtype)], axis=0)
    xs = h_ext[row_tok]
    y = _moe_experts_pipelined(blk_e, n_valid.reshape(1), xs, wg, bg, wu, bu, wd, bd)
    return y[dest.reshape(n, TOP_K).T.reshape(-1)].reshape(TOP_K, n, d)


def _combine_body(x_ref, yg_ref, tw_ref, o_ref):
    tw = tw_ref[...]
    acc = x_ref[...]
    for k in range(TOP_K):
        acc = acc + tw[:, k:k + 1] * yg_ref[k]
    o_ref[...] = acc


def _combine(x1, yg, tw, row0):
    n, d = x1.shape
    tm = _row_tile(n, 384)
    assert row0 % tm == 0
    return pl.pallas_call(
        _combine_body,
        grid=(n // tm,),
        in_specs=[pl.BlockSpec((tm, d), lambda i: (i, 0)),
                  pl.BlockSpec((TOP_K, tm, d), lambda i: (0, row0 // tm + i, 0)),
                  pl.BlockSpec((tm, LANES), lambda i: (i, 0))],
        out_specs=pl.BlockSpec((tm, d), lambda i: (i, 0)),
        out_shape=jax.ShapeDtypeStruct((n, d), F32),
        compiler_params=_cparams("parallel"),
        name="combine",
    )(x1, yg, tw)


def _lane_row(vals, offset, width=LANES):
    return jnp.zeros((1, width), F32).at[0, offset:offset + vals.shape[0]].set(vals.astype(F32))


def kernel(x_prompt, x_sample, cache_k, cache_v, cache_logf, state_delta, state_conv, page_table,
           meta_tokens, g_mix, w_in, conv_w, a_log, dt_bias, norm_a, qn_g, kn_g, b_forget,
           p_a, p_f, w_o, g_ffn, w_router, b_router, w_gate, b_gate, w_up, b_up, w_down, b_down):
    depth = w_in.shape[0]
    b, seq, d = x_prompt.shape
    bd = x_sample.shape[0]
    assert x_sample.shape[1] == 1 and DK_A == LANES and DV_A == LANES
    t = N_META + seq
    lead = (-N_META) % CHUNK
    t_pad = -(-t // ATT_BLOCK) * ATT_BLOCK
    qkv_w, va_w, wf = 2 * H_A * DK_A + H_A * DV_A, H_A * DV_A, H_F * HD_F
    sizes = (qkv_w, va_w, H_A, H_A, wf, wf, wf, H_F, d, d)
    offs = [0]
    for s_ in sizes:
        offs.append(offs[-1] + s_)
    col = lambda i: slice(offs[i], offs[i + 1])
    widths = (qkv_w, va_w, wf, wf, wf, d, d)

    xp = jnp.concatenate([jnp.broadcast_to(meta_tokens.astype(x_prompt.dtype)[None], (b, N_META, d)), x_prompt], axis=1)
    xs = x_sample
    new_p = [[] for _ in range(5)]
    new_s = [[] for _ in range(5)]
    for l in range(depth):
        wl = w_in[l]
        w_main = jnp.concatenate([wl[:, col(0)], wl[:, col(1)], wl[:, col(4)], wl[:, col(5)], wl[:, col(6)],
                                  wl[:, col(8)], wl[:, col(9)]], axis=1).astype(BF16)
        w_small = jnp.concatenate([wl[:, col(2)], wl[:, col(3)], wl[:, col(7)],
                                   jnp.zeros((d, LANES - 2 * H_A - H_F), F32)], axis=1).astype(BF16)
        g_row = g_mix[l].reshape(1, d)
        a_row = _lane_row(a_log[l], H_A)
        dt_row = _lane_row(dt_bias[l], H_A)
        bf_row = _lane_row(b_forget[l], 2 * H_A)
        qg_pair = jnp.tile(qn_g[l], LANES // HD_F).reshape(1, LANES)
        kg_pair = jnp.tile(kn_g[l], LANES // HD_F).reshape(1, LANES)
        qg_full = jnp.tile(qn_g[l], H_F).reshape(1, wf)
        kg_full = jnp.tile(kn_g[l], H_F).reshape(1, wf)
        na_row = norm_a[l].reshape(1, DV_A)
        pa_b, pf_b, wo_b = p_a[l].astype(BF16), p_f[l].astype(BF16), w_o[l].astype(BF16)
        gffn_row = g_ffn[l].reshape(1, d)
        wr = jnp.pad(w_router[l], ((0, 0), (0, LANES - N_EXP))).astype(BF16)
        br_row = _lane_row(b_router[l], 0)
        merge_w = (na_row, pa_b, pf_b, wo_b, gffn_row, wr, br_row)

        qkv_p, z_p, qf_p, kf_p, vf_p, ga_p, gf_p, sm_p = _in_proj(xp.reshape(b * t, d), g_row, w_main, w_small, widths)
        r3 = lambda a: a.reshape(b, t, a.shape[-1])
        qkv3, sm3 = r3(qkv_p), r3(sm_p)
        conv_p = qkv3[:, t - (CONV_W - 1):, :]
        prep = _conv_prep(qkv3, conv_w[l], jnp.zeros((b, CONV_W - 1, qkv_w), F32))
        oa_p, s_p = _delta_scan(prep, sm3, a_row, dt_row, jnp.zeros((b, H_A, DK_A, DV_A), F32), lead)
        lf_p, c_col = _logf_cumsum(sm3, bf_row, t_pad)
        qx, kx, vb, kn_p = _fox_prep(r3(qf_p), r3(kf_p), r3(vf_p), c_col, qg_pair, kg_pair, t_pad)
        of_p = _fox_attn(qx, kx, vb)
        x1_p, h2_p, tw_p, ti_p = _merge(xp, oa_p, r3(z_p), of_p, r3(ga_p), r3(gf_p), *merge_w)

        qkv_s, z_s, qf_s, kf_s, vf_s, ga_s, gf_s, sm_s = _in_proj(xs.reshape(bd, d), g_row, w_main, w_small, widths)
        s3 = lambda a: a.reshape(bd, 1, a.shape[-1])
        oa_s, s_s, conv_s = _decode_delta(s3(qkv_s), state_conv[l], s3(sm_s), conv_w[l], a_row, dt_row, state_delta[l])
        n_pool, page = cache_k.shape[1], cache_k.shape[2]
        of_s, kn_s, lf_s = _decode_attn(page_table, s3(qf_s), s3(kf_s), s3(vf_s), s3(sm_s), qg_full, kg_full,
                                        _lane_row(b_forget[l], 2 * H_A),
                                        jnp.transpose(cache_k[l], (0, 2, 3, 1)).reshape(n_pool, wf, page),
                                        jnp.transpose(cache_v[l], (0, 2, 3, 1)).reshape(n_pool, wf, page),
                                        jnp.swapaxes(cache_logf[l], 1, 2))
        g1 = lambda a: a.reshape(1, bd, a.shape[-1])
        x1_s, h2_s, tw_s, ti_s = _merge(g1(xs), g1(oa_s), g1(z_s), g1(of_s), g1(ga_s), g1(gf_s), *merge_w)

        n_p = b * t
        h2_all = jnp.concatenate([h2_p.reshape(n_p, d), h2_s.reshape(bd, d)], axis=0)
        ti_all = jnp.concatenate([ti_p.reshape(n_p, LANES), ti_s.reshape(bd, LANES)], axis=0)[:, :TOP_K]
        yg = _moe(h2_all, ti_all, w_gate[l], b_gate[l][:, None, :],
                  w_up[l], b_up[l][:, None, :], w_down[l], b_down[l][:, None, :])
        xp = _combine(x1_p.reshape(n_p, d), yg, tw_p.reshape(n_p, LANES), 0).reshape(b, t, d)
        xs = _combine(x1_s.reshape(bd, d), yg, tw_s.reshape(bd, LANES), n_p).reshape(bd, 1, d)

        st_p = (kn_p.reshape(b, t, H_F, HD_F), vf_p.reshape(b, t, H_F, HD_F),
                lf_p[:, :, 2 * H_A:2 * H_A + H_F], s_p, conv_p)
        st_s = (kn_s.reshape(bd, 1, H_F, HD_F), vf_s.reshape(bd, 1, H_F, HD_F),
                lf_s[:, :, 2 * H_A:2 * H_A + H_F], s_s, conv_s)
        for lst, a in zip(new_p, st_p):
            lst.append(a)
        for lst, a in zip(new_s, st_s):
            lst.append(a)
    k_p, v_p, lf_pp, d_p, c_p = (jnp.stack(a) for a in new_p)
    k_s, v_s, lf_ss, d_s, c_s = (jnp.stack(a) for a in new_s)
    return (xp[:, N_META:], xs, k_p, v_p, lf_pp, k_s, v_s, lf_ss, d_p, d_s, c_p, c_s)
```
